```python
import math
import jax, jax.numpy as jnp
from jax import lax
import numpy as np

D_MODEL = 2048
BATCH = 2
SEQ = 4096
DEPTH = 2

D_MIX = D_MODEL
HEAD_DIM = 128
GROUP_W = D_MIX // 4
FOX_HEADS = GROUP_W // HEAD_DIM
GDN_HEADS = GROUP_W // HEAD_DIM
GDN_CHUNK = 64
CONV_W = 4
LRU_W = GROUP_W
LRU_BLOCKS = 8
LRU_C = 8.0
NSA_HEADS = GROUP_W // HEAD_DIM
CMP_LEN = 32
CMP_STRIDE = 16
SLC_LEN = 64
N_SELECT = 16
WINDOW = 512
FORCED_SCORE = 1.0e6
N_BUCKETS = 32
MAX_EXACT = 16
MAX_DIST = 1024
D_FF = 5632
Q_BLOCK = 128
EPS = 1e-6

IN_SPLITS = (
    ("fox_q", GROUP_W), ("fox_k", GROUP_W), ("fox_v", GROUP_W), ("fox_f", FOX_HEADS),
    ("gdn_q", GROUP_W), ("gdn_k", GROUP_W), ("gdn_v", GROUP_W), ("gdn_a", GDN_HEADS), ("gdn_b", GDN_HEADS), ("gdn_z", GROUP_W),
    ("lru_x", LRU_W), ("lru_gate", LRU_W),
    ("nsa_q", GROUP_W), ("nsa_kc", HEAD_DIM), ("nsa_vc", HEAD_DIM), ("nsa_ks", HEAD_DIM), ("nsa_vs", HEAD_DIM),
    ("nsa_kw", HEAD_DIM), ("nsa_vw", HEAD_DIM), ("nsa_g", 3 * NSA_HEADS),
)
D_IN = sum(w for _, w in IN_SPLITS)

kernel_name = "hymba_fox_gdn_rglru_nsa_macaron"


def rmsnorm(x, g):
    xf = x.astype(jnp.float32)
    y = xf * lax.rsqrt(jnp.mean(xf * xf, axis=-1, keepdims=True) + EPS)
    return (y * g.astype(jnp.float32)).astype(x.dtype)


def l2norm(x):
    return x * lax.rsqrt(jnp.sum(x * x, axis=-1, keepdims=True) + EPS)


def swiglu(h, w_g, w_u, w_d):
    return (jax.nn.silu(h @ w_g) * (h @ w_u)) @ w_d


def split_cols(z):
    out = []
    off = 0
    for _, w in IN_SPLITS:
        out.append(z[..., off:off + w])
        off += w
    return out


def causal_dwconv(x, w):
    K, C = w.shape
    xp = jnp.pad(x, ((0, 0), (K - 1, 0), (0, 0)))
    return lax.conv_general_dilated(xp, w[:, None, :].astype(x.dtype), window_strides=(1,), padding="VALID",
                                    dimension_numbers=("NWC", "WIO", "NWC"), feature_group_count=C)


def t5_bucket(dist):
    n = jnp.maximum(dist, 0)
    nf = jnp.maximum(n, 1).astype(jnp.float32)
    large = MAX_EXACT + (jnp.log(nf / MAX_EXACT) / math.log(MAX_DIST / MAX_EXACT)
                         * (N_BUCKETS - MAX_EXACT)).astype(jnp.int32)
    large = jnp.minimum(large, N_BUCKETS - 1)
    return jnp.where(n < MAX_EXACT, n, large)


def masked_softmax(s, mask):
    s = jnp.where(mask, s, -jnp.inf)
    m = jnp.max(s, axis=-1, keepdims=True)
    m = jnp.where(jnp.isfinite(m), m, 0.0)
    e = jnp.where(mask, jnp.exp(s - m), 0.0)
    return e / jnp.maximum(jnp.sum(e, axis=-1, keepdims=True), 1e-30)


def fox_attention(q, k, v, log_f):
    B, T, H, D = q.shape
    nb = T // Q_BLOCK
    c = jnp.cumsum(log_f, axis=1)
    c_k = c.transpose(0, 2, 1)[:, :, None, :]
    kpos = jnp.arange(T)
    scale = D ** -0.5
    qb = q.reshape(B, nb, Q_BLOCK, H, D).transpose(1, 0, 2, 3, 4)
    cb = c.reshape(B, nb, Q_BLOCK, H).transpose(1, 0, 2, 3)
    starts = jnp.arange(nb) * Q_BLOCK

    def block(args):
        q_blk, c_blk, s0 = args
        s = jnp.einsum('bqhd,bkhd->bhqk', q_blk, k, preferred_element_type=jnp.float32) * scale
        s = s + c_blk.transpose(0, 2, 1)[..., None] - c_k
        qpos = s0 + jnp.arange(Q_BLOCK)
        mask = kpos[None, :] <= qpos[:, None]
        p = jax.nn.softmax(jnp.where(mask, s, -jnp.inf), axis=-1)
        return jnp.einsum('bhqk,bkhd->bqhd', p.astype(v.dtype), v)

    o = lax.map(block, (qb, cb, starts))
    return o.transpose(1, 0, 2, 3, 4).reshape(B, T, H, D)


def gated_delta_rule(q, k, v, g, beta):
    B, H, T, Dk = q.shape
    Dv = v.shape[-1]
    C = GDN_CHUNK
    N = T // C
    q = q * Dk ** -0.5
    rs = lambda t: t.reshape(B, H, N, C, *t.shape[3:])
    q, k, v, g, beta = rs(q), rs(k), rs(v), rs(g), rs(beta)
    g = jnp.cumsum(g, axis=-1)
    idx = jnp.arange(C)
    tril = idx[:, None] >= idx[None, :]
    strict = idx[:, None] > idx[None, :]
    decay = jnp.where(tril, jnp.exp(jnp.where(tril, g[..., :, None] - g[..., None, :], 0.0)), 0.0)
    kb = k * beta[..., None]
    A = jnp.where(strict, jnp.einsum('bhnid,bhnjd->bhnij', kb, k) * decay, 0.0)
    eye = jnp.eye(C, dtype=jnp.float32)
    Tm = lax.linalg.triangular_solve(eye + A, jnp.broadcast_to(eye, A.shape), left_side=True,
                                     lower=True, unit_diagonal=True)
    U = Tm @ (v * beta[..., None])
    W = Tm @ (kb * jnp.exp(g)[..., None])
    qk = jnp.einsum('bhnid,bhnjd->bhnij', q, k) * decay
    g_last = g[..., -1]

    def step(S, xs):
        q_c, k_c, U_c, W_c, qk_c, g_c, gl_c = xs
        v_new = U_c - W_c @ S
        o = (q_c * jnp.exp(g_c)[..., None]) @ S + qk_c @ v_new
        k_dec = k_c * jnp.exp(gl_c[..., None] - g_c)[..., None]
        S = S * jnp.exp(gl_c)[..., None, None] + jnp.swapaxes(k_dec, -1, -2) @ v_new
        return S, o

    mv = lambda t: jnp.moveaxis(t, 2, 0)
    xs = (mv(q), mv(k), mv(U), mv(W), mv(qk), mv(g), mv(g_last))
    S0 = jnp.zeros((B, H, Dk, Dv), jnp.float32)
    _, o = lax.scan(step, S0, xs)
    return jnp.moveaxis(o, 0, 2).reshape(B, H, T, Dv)


def rg_lru(x, w_a, b_a, w_x, b_x, lam):
    B, T, W = x.shape
    xr = x.reshape(B, T, LRU_BLOCKS, W // LRU_BLOCKS)
    r = jax.nn.sigmoid(jnp.einsum('btnd,nde->btne', xr, w_a).reshape(B, T, W) + b_a)
    i = jax.nn.sigmoid(jnp.einsum('btnd,nde->btne', xr, w_x).reshape(B, T, W) + b_x)
    log_a = -LRU_C * r * jax.nn.softplus(-lam)
    a = jnp.exp(log_a)
    u = jnp.sqrt(-jnp.expm1(2.0 * log_a)) * (i * x)

    def comb(left, right):
        a1, b1 = left
        a2, b2 = right
        return a1 * a2, a2 * b1 + b2

    _, h = lax.associative_scan(comb, (a, u), axis=1)
    return h


def nsa_compress(k, pe, w1, w2):
    B, T, D = k.shape
    n_cmp = (T - CMP_LEN) // CMP_STRIDE + 1
    idx = jnp.arange(n_cmp)[:, None] * CMP_STRIDE + jnp.arange(CMP_LEN)[None, :]
    blocks = k[:, idx] + pe
    hid = jax.nn.gelu(blocks.reshape(B, n_cmp, CMP_LEN * D) @ w1)
    return hid @ w2


def nsa_attention(q, kc, vc, ks, vs, kw, vw, gates, rel_bias):
    B, T, H, D = q.shape
    nb = T // Q_BLOCK
    n_cmp = kc.shape[1]
    n_slc = T // SLC_LEN
    n_sel = min(N_SELECT, n_slc)
    scale = D ** -0.5
    c_start = jnp.arange(n_cmp) * CMP_STRIDE
    c_end = c_start + CMP_LEN - 1
    s_start = jnp.arange(n_slc) * SLC_LEN
    s_end = s_start + SLC_LEN - 1
    overlap = ((c_start[:, None] <= s_end[None, :]) & (c_end[:, None] >= s_start[None, :])).astype(jnp.float32)
    ks_blk = ks.reshape(B, n_slc, SLC_LEN, D)
    vs_blk = vs.reshape(B, n_slc, SLC_LEN, D)
    kw_pad = jnp.pad(kw, ((0, 0), (WINDOW, 0), (0, 0)))
    vw_pad = jnp.pad(vw, ((0, 0), (WINDOW, 0), (0, 0)))
    bidx = jnp.arange(B)[:, None, None]
    jsl = jnp.arange(n_slc)
    qb = q.reshape(B, nb, Q_BLOCK, H, D).transpose(1, 0, 2, 3, 4)
    gb = gates.reshape(B, nb, Q_BLOCK, H, 3).transpose(1, 0, 2, 3, 4)
    starts = jnp.arange(nb) * Q_BLOCK

    def block(args):
        q_blk, g_blk, s0 = args
        qpos = s0 + jnp.arange(Q_BLOCK)
        dist_c = qpos[:, None] - c_end[None, :]
        s_c = jnp.einsum('bqhd,bnd->bhqn', q_blk, kc, preferred_element_type=jnp.float32) * scale
        s_c = s_c + rel_bias[t5_bucket(dist_c)].transpose(2, 0, 1)
        p_c = masked_softmax(s_c, dist_c >= 0)
        o_c = jnp.einsum('bhqn,bnd->bqhd', p_c.astype(vc.dtype), vc)
        imp = jnp.einsum('bhqn,nj->bqj', p_c, overlap)
        cur = qpos // SLC_LEN
        forced = (jsl[None, :] == 0) | (jsl[None, :] == cur[:, None]) | (jsl[None, :] == cur[:, None] - 1)
        imp = jnp.where(forced[None], FORCED_SCORE, imp)
        imp = jnp.where((jsl[None, :] > cur[:, None])[None], -1.0, imp)
        _, sel = lax.top_k(imp, n_sel)
        k_sel = ks_blk[bidx, sel].reshape(B, Q_BLOCK, n_sel * SLC_LEN, D)
        v_sel = vs_blk[bidx, sel].reshape(B, Q_BLOCK, n_sel * SLC_LEN, D)
        pos_s = (sel[..., None] * SLC_LEN + jnp.arange(SLC_LEN)).reshape(B, Q_BLOCK, n_sel * SLC_LEN)
        dist_s = qpos[None, :, None] - pos_s
        s_s = jnp.einsum('bqhd,bqkd->bhqk', q_blk, k_sel, preferred_element_type=jnp.float32) * scale
        s_s = s_s + rel_bias[t5_bucket(dist_s)].transpose(0, 3, 1, 2)
        p_s = masked_softmax(s_s, (dist_s >= 0)[:, None])
        o_s = jnp.einsum('bhqk,bqkd->bqhd', p_s.astype(v_sel.dtype), v_sel)
        k_win = lax.dynamic_slice_in_dim(kw_pad, s0, WINDOW + Q_BLOCK, axis=1)
        v_win = lax.dynamic_slice_in_dim(vw_pad, s0, WINDOW + Q_BLOCK, axis=1)
        pos_w = s0 - WINDOW + jnp.arange(WINDOW + Q_BLOCK)
        dist_w = qpos[:, None] - pos_w[None, :]
        mask_w = (dist_w >= 0) & (dist_w < WINDOW) & (pos_w[None, :] >= 0)
        s_w = jnp.einsum('bqhd,bkd->bhqk', q_blk, k_win, preferred_element_type=jnp.float32) * scale
        s_w = s_w + rel_bias[t5_bucket(dist_w)].transpose(2, 0, 1)
        p_w = masked_softmax(s_w, mask_w)
        o_w = jnp.einsum('bhqk,bkd->bqhd', p_w.astype(v_win.dtype), v_win)
        out = g_blk[..., 0:1] * o_c + g_blk[..., 1:2] * o_s + g_blk[..., 2:3] * o_w
        return out.astype(q.dtype)

    o = lax.map(block, (qb, gb, starts))
    return o.transpose(1, 0, 2, 3, 4).reshape(B, T, H, D)


def hybrid_mixer(h, w_in, b_in, gdn_conv_w, gdn_a_log, gdn_dt_bias, gdn_norm_g,
                 lru_conv_w, lru_conv_b, lru_w_a, lru_b_a, lru_w_x, lru_b_x, lru_lambda,
                 nsa_pe_k, nsa_w1_k, nsa_w2_k, nsa_pe_v, nsa_w1_v, nsa_w2_v, rel_bias,
                 out_norm_g, w_out):
    B, T, _ = h.shape
    f32 = jnp.float32
    z = h @ w_in + b_in
    (fq, fk, fv, ff, gq, gk, gv, ga, gbeta, gz, lx, lg,
     nq, nkc, nvc, nks, nvs, nkw, nvw, ng) = split_cols(z)
    heads = lambda t, n: t.reshape(B, T, n, HEAD_DIM)
    log_f = jax.nn.log_sigmoid(ff.astype(f32))
    o_a = fox_attention(heads(fq, FOX_HEADS), heads(fk, FOX_HEADS), heads(fv, FOX_HEADS), log_f)
    o_a = o_a.reshape(B, T, GROUP_W)
    qkv = jax.nn.silu(causal_dwconv(jnp.concatenate([gq, gk, gv], axis=-1), gdn_conv_w)).astype(f32)
    dq = l2norm(heads(qkv[..., :GROUP_W], GDN_HEADS))
    dk = l2norm(heads(qkv[..., GROUP_W:2 * GROUP_W], GDN_HEADS))
    dv = heads(qkv[..., 2 * GROUP_W:], GDN_HEADS)
    beta = jax.nn.sigmoid(gbeta.astype(f32))
    g = -jnp.exp(gdn_a_log.astype(f32)) * jax.nn.softplus(ga.astype(f32) + gdn_dt_bias.astype(f32))
    tr = lambda t: t.transpose(0, 2, 1, 3)
    o_b = gated_delta_rule(tr(dq), tr(dk), tr(dv), g.transpose(0, 2, 1), beta.transpose(0, 2, 1))
    o_b = rmsnorm(tr(o_b), gdn_norm_g) * jax.nn.silu(heads(gz, GDN_HEADS).astype(f32))
    o_b = o_b.reshape(B, T, GROUP_W)
    xc = (causal_dwconv(lx, lru_conv_w) + lru_conv_b).astype(f32)
    hc = rg_lru(xc, lru_w_a.astype(f32), lru_b_a.astype(f32), lru_w_x.astype(f32), lru_b_x.astype(f32),
                lru_lambda.astype(f32))
    o_c = hc * jax.nn.gelu(lg.astype(f32))
    kc = nsa_compress(nkc, nsa_pe_k, nsa_w1_k, nsa_w2_k)
    vc = nsa_compress(nvc, nsa_pe_v, nsa_w1_v, nsa_w2_v)
    gates = jax.nn.sigmoid(ng.astype(f32)).reshape(B, T, NSA_HEADS, 3)
    o_d = nsa_attention(heads(nq, NSA_HEADS), kc, vc, nks, nvs, nkw, nvw, gates, rel_bias)
    o_d = o_d.reshape(B, T, GROUP_W)
    y = jnp.concatenate([rmsnorm(o_a, out_norm_g[0]).astype(f32), o_b.astype(f32),
                         rmsnorm(o_c, out_norm_g[1]).astype(f32), rmsnorm(o_d, out_norm_g[2]).astype(f32)], axis=-1)
    return y.astype(h.dtype) @ w_out


def setup_inputs(seed: int = 0) -> dict:
    key = jax.random.key(seed)
    keys = iter(list(jax.random.split(key, 64)))
    nrm = lambda shape, scale: jax.random.normal(next(keys), shape, jnp.float32) * scale
    gain = lambda shape: 1.0 + nrm(shape, 0.02)
    L, D, F = DEPTH, D_MODEL, D_FF
    bw = LRU_W // LRU_BLOCKS
    x = nrm((BATCH, SEQ, D), 1.0)
    ffn1_norm_g = gain((L, D))
    ffn1_w_gate = nrm((L, D, F), D ** -0.5)
    ffn1_w_up = nrm((L, D, F), D ** -0.5)
    ffn1_w_down = nrm((L, F, D), F ** -0.5)
    mix_norm_g = gain((L, D))
    w_in = nrm((L, D, D_IN), D ** -0.5)
    b_in = nrm((L, D_IN), 0.02)
    gdn_conv_w = nrm((L, CONV_W, 3 * GROUP_W), CONV_W ** -0.5)
    gdn_a_log = jnp.log(jax.random.uniform(next(keys), (L, GDN_HEADS), jnp.float32, 1.0, 16.0))
    dt = jnp.exp(jax.random.uniform(next(keys), (L, GDN_HEADS), jnp.float32, math.log(1e-3), math.log(1e-1)))
    gdn_dt_bias = dt + jnp.log(-jnp.expm1(-dt))
    gdn_norm_g = gain((L, HEAD_DIM))
    lru_conv_w = nrm((L, CONV_W, LRU_W), CONV_W ** -0.5)
    lru_conv_b = nrm((L, LRU_W), 0.02)
    lru_w_a = nrm((L, LRU_BLOCKS, bw, bw), bw ** -0.5)
    lru_b_a = nrm((L, LRU_W), 0.02)
    lru_w_x = nrm((L, LRU_BLOCKS, bw, bw), bw ** -0.5)
    lru_b_x = nrm((L, LRU_W), 0.02)
    a_c = jax.random.uniform(next(keys), (L, LRU_W), jnp.float32, 0.9, 0.999)
    s = a_c ** (1.0 / LRU_C)
    lru_lambda = jnp.log(s) - jnp.log1p(-s)
    nsa_pe_k = nrm((L, CMP_LEN, HEAD_DIM), 0.02)
    nsa_w1_k = nrm((L, CMP_LEN * HEAD_DIM, HEAD_DIM), (CMP_LEN * HEAD_DIM) ** -0.5)
    nsa_w2_k = nrm((L, HEAD_DIM, HEAD_DIM), HEAD_DIM ** -0.5)
    nsa_pe_v = nrm((L, CMP_LEN, HEAD_DIM), 0.02)
    nsa_w1_v = nrm((L, CMP_LEN * HEAD_DIM, HEAD_DIM), (CMP_LEN * HEAD_DIM) ** -0.5)
    nsa_w2_v = nrm((L, HEAD_DIM, HEAD_DIM), HEAD_DIM ** -0.5)
    rel_bias = nrm((N_BUCKETS, NSA_HEADS), 0.2)
    out_norm_g = gain((L, 3, GROUP_W))
    w_out = nrm((L, D_MIX, D), D_MIX ** -0.5)
    ffn2_norm_g = gain((L, D))
    ffn2_w_gate = nrm((L, D, F), D ** -0.5)
    ffn2_w_up = nrm((L, D, F), D ** -0.5)
    ffn2_w_down = nrm((L, F, D), F ** -0.5)
    final_norm_g = gain((D,))
    return {"x": x, "ffn1_norm_g": ffn1_norm_g, "ffn1_w_gate": ffn1_w_gate, "ffn1_w_up": ffn1_w_up,
            "ffn1_w_down": ffn1_w_down, "mix_norm_g": mix_norm_g, "w_in": w_in, "b_in": b_in,
            "gdn_conv_w": gdn_conv_w, "gdn_a_log": gdn_a_log, "gdn_dt_bias": gdn_dt_bias, "gdn_norm_g": gdn_norm_g,
            "lru_conv_w": lru_conv_w, "lru_conv_b": lru_conv_b, "lru_w_a": lru_w_a, "lru_b_a": lru_b_a,
            "lru_w_x": lru_w_x, "lru_b_x": lru_b_x, "lru_lambda": lru_lambda,
            "nsa_pe_k": nsa_pe_k, "nsa_w1_k": nsa_w1_k, "nsa_w2_k": nsa_w2_k,
            "nsa_pe_v": nsa_pe_v, "nsa_w1_v": nsa_w1_v, "nsa_w2_v": nsa_w2_v, "rel_bias": rel_bias,
            "out_norm_g": out_norm_g, "w_out": w_out, "ffn2_norm_g": ffn2_norm_g, "ffn2_w_gate": ffn2_w_gate,
            "ffn2_w_up": ffn2_w_up, "ffn2_w_down": ffn2_w_down, "final_norm_g": final_norm_g}


def reference(x, ffn1_norm_g, ffn1_w_gate, ffn1_w_up, ffn1_w_down, mix_norm_g, w_in, b_in,
              gdn_conv_w, gdn_a_log, gdn_dt_bias, gdn_norm_g, lru_conv_w, lru_conv_b, lru_w_a, lru_b_a,
              lru_w_x, lru_b_x, lru_lambda, nsa_pe_k, nsa_w1_k, nsa_w2_k, nsa_pe_v, nsa_w1_v, nsa_w2_v,
              rel_bias, out_norm_g, w_out, ffn2_norm_g, ffn2_w_gate, ffn2_w_up, ffn2_w_down, final_norm_g):
    h = x
    for l in range(DEPTH):
        h = h + 0.5 * swiglu(rmsnorm(h, ffn1_norm_g[l]), ffn1_w_gate[l], ffn1_w_up[l], ffn1_w_down[l])
        h = h + hybrid_mixer(rmsnorm(h, mix_norm_g[l]), w_in[l], b_in[l], gdn_conv_w[l], gdn_a_log[l],
                             gdn_dt_bias[l], gdn_norm_g[l], lru_conv_w[l], lru_conv_b[l], lru_w_a[l], lru_b_a[l],
                             lru_w_x[l], lru_b_x[l], lru_lambda[l], nsa_pe_k[l], nsa_w1_k[l], nsa_w2_k[l],
                             nsa_pe_v[l], nsa_w1_v[l], nsa_w2_v[l], rel_bias, out_norm_g[l], w_out[l])
        h = h + 0.5 * swiglu(rmsnorm(h, ffn2_norm_g[l]), ffn2_w_gate[l], ffn2_w_up[l], ffn2_w_down[l])
    return rmsnorm(h, final_norm_g)
```

```python
import functools
import math

import jax
import jax.numpy as jnp
from jax import lax
from jax.experimental import pallas as pl
from jax.experimental.pallas import tpu as pltpu

F32 = jnp.float32
BF16 = jnp.bfloat16

EPS = 1e-6
LANE = 128
HEAD_DIM = 128
N_HEADS = 4
GROUP_W = N_HEADS * HEAD_DIM
GDN_CHUNK = 64
CONV_W = 4
LRU_BLOCKS = 8
LRU_C = 8.0
CMP_LEN = 32
CMP_STRIDE = 16
SLC_LEN = 64
N_SELECT = 16
WINDOW = 512
FORCED_SCORE = 1.0e6
N_BUCKETS = 32
MAX_EXACT = 16
MAX_DIST = 1024
Q_BLOCK = 128
NEG = -1.0e30
VMEM_LIMIT = 56 * 1024 * 1024

_COL = dict(fox_q=0, fox_k=4, fox_v=8, gdn_q=12, gdn_k=16, gdn_v=20, gdn_z=24, lru_x=28, lru_gate=32,
            nsa_q=36, nsa_kc=40, nsa_vc=41, nsa_ks=42, nsa_vs=43, nsa_kw=44, nsa_vw=45, small=46)
_N_COLBLK = 48
D_IN_PAD = _N_COLBLK * LANE
_S_FOX_F, _S_GDN_A, _S_GDN_B, _S_NSA_G = 0, 4, 8, 12


def _cparams(sem):
    return pltpu.CompilerParams(dimension_semantics=sem, vmem_limit_bytes=VMEM_LIMIT)


def _rms(x, g):
    return x * lax.rsqrt(jnp.mean(x * x, axis=-1, keepdims=True) + EPS) * g


def _softplus(x):
    return jnp.maximum(x, 0.0) + jnp.log1p(jnp.exp(-jnp.abs(x)))


def _gelu_tanh(x):
    return 0.5 * x * (1.0 + jnp.tanh(math.sqrt(2.0 / math.pi) * (x + 0.044715 * (x * x * x))))


def _dot(a, b):
    return jnp.dot(a, b, preferred_element_type=F32)


def _dot_nt(a, b):
    return lax.dot_general(a, b, (((1,), (1,)), ((), ())), preferred_element_type=F32)


def _dot_tn(a, b):
    return lax.dot_general(a, b, (((0,), (0,)), ((), ())), preferred_element_type=F32)


def _dot_hi(a, b):
    return jnp.dot(a, b, preferred_element_type=F32, precision=lax.Precision.HIGHEST)


def _ffn_body(*refs, final):
    if final:
        x_ref, g_ref, wg_ref, wu_ref, wd_ref, fg_ref, o_ref, n_scr, acc_scr = refs
    else:
        x_ref, g_ref, wg_ref, wu_ref, wd_ref, o_ref, n_scr, acc_scr = refs
    j = pl.program_id(1)

    @pl.when(j == 0)
    def _():
        n_scr[...] = _rms(x_ref[...], g_ref[...]).astype(BF16)
        acc_scr[...] = jnp.zeros_like(acc_scr)

    n = n_scr[...]
    gate = _dot(n, wg_ref[...])
    up = _dot(n, wu_ref[...])
    a = (gate * jax.nn.sigmoid(gate) * up).astype(BF16)
    acc_scr[...] += _dot(a, wd_ref[...])

    @pl.when(j == pl.num_programs(1) - 1)
    def _():
        y = x_ref[...] + 0.5 * acc_scr[...]
        if final:
            y = _rms(y, fg_ref[...])
        o_ref[...] = y


def _ffn(x, g, wg, wu, wd, final_g=None, *, tm=512, tf=512):
    nt, d = x.shape
    f = wg.shape[1]
    tm = min(tm, nt)
    final = final_g is not None
    in_specs = [
        pl.BlockSpec((tm, d), lambda i, j: (i, 0)),
        pl.BlockSpec((1, d), lambda i, j: (0, 0)),
        pl.BlockSpec((d, tf), lambda i, j: (0, j)),
        pl.BlockSpec((d, tf), lambda i, j: (0, j)),
        pl.BlockSpec((tf, d), lambda i, j: (j, 0)),
    ]
    args = [x, g.reshape(1, d), wg, wu, wd]
    if final:
        in_specs.append(pl.BlockSpec((1, d), lambda i, j: (0, 0)))
        args.append(final_g.reshape(1, d))
    return pl.pallas_call(
        functools.partial(_ffn_body, final=final),
        grid=(nt // tm, f // tf),
        in_specs=in_specs,
        out_specs=pl.BlockSpec((tm, d), lambda i, j: (i, 0)),
        out_shape=jax.ShapeDtypeStruct((nt, d), F32),
        scratch_shapes=[pltpu.VMEM((tm, d), BF16), pltpu.VMEM((tm, d), F32)],
        compiler_params=_cparams(("parallel", "arbitrary")),
        name="ffn",
    )(*args)


def _inproj_body(x_ref, g_ref, w_ref, b_ref, o32_ref, o16_ref, n_scr):
    @pl.when(pl.program_id(1) == 0)
    def _():
        n_scr[...] = _rms(x_ref[...], g_ref[...]).astype(BF16)

    z = _dot(n_scr[...], w_ref[...]) + b_ref[...]
    o32_ref[...] = z
    o16_ref[...] = z.astype(BF16)


def _inproj(x, g, w, b, *, tm=512, tn=768):
    nt, d = x.shape
    n = w.shape[1]
    tm = min(tm, nt)
    return pl.pallas_call(
        _inproj_body,
        grid=(nt // tm, n // tn),
        in_specs=[
            pl.BlockSpec((tm, d), lambda i, j: (i, 0)),
            pl.BlockSpec((1, d), lambda i, j: (0, 0)),
            pl.BlockSpec((d, tn), lambda i, j: (0, j)),
            pl.BlockSpec((1, tn), lambda i, j: (0, j)),
        ],
        out_specs=[pl.BlockSpec((tm, tn), lambda i, j: (i, j)), pl.BlockSpec((tm, tn), lambda i, j: (i, j))],
        out_shape=[jax.ShapeDtypeStruct((nt, n), F32), jax.ShapeDtypeStruct((nt, n), BF16)],
        scratch_shapes=[pltpu.VMEM((tm, d), BF16)],
        compiler_params=_cparams(("parallel", "arbitrary")),
        name="inproj",
    )(x, g.reshape(1, d), w, b.reshape(1, n))


def _cumsum_rows(y):
    rows = lax.broadcasted_iota(jnp.int32, y.shape, 0)
    s = 1
    while s < y.shape[0]:
        y = y + jnp.where(rows >= s, pltpu.roll(y, s, 0), 0.0)
        s *= 2
    return y


def _fox_gate_body(f_ref, c_ref):
    x = f_ref[...]
    log_f = jnp.minimum(x, 0.0) - jnp.log1p(jnp.exp(-jnp.abs(x)))
    c_ref[...] = _cumsum_rows(log_f)


def _fox_gate(z32, b, t):
    return pl.pallas_call(
        _fox_gate_body,
        grid=(b,),
        in_specs=[pl.BlockSpec((t, LANE), lambda i: (i, _COL["small"]))],
        out_specs=pl.BlockSpec((t, LANE), lambda i: (i, 0)),
        out_shape=jax.ShapeDtypeStruct((b * t, LANE), F32),
        compiler_params=_cparams(("parallel",)),
        name="fox_gate",
    )(z32)


def _fox_body(q_ref, k_ref, v_ref, ccol_ref, crow_ref, o_ref, *, tq, scale):
    qi = pl.program_id(1)
    rows = lax.broadcasted_iota(jnp.int32, (tq, tq), 0)
    cols = lax.broadcasted_iota(jnp.int32, (tq, tq), 1)
    causal = rows >= cols
    for h in range(N_HEADS):
        hs = slice(h * HEAD_DIM, (h + 1) * HEAD_DIM)
        q = q_ref[:, hs]
        cq = ccol_ref[:, h:h + 1]

        def step(kt, carry, masked, q=q, cq=cq, h=h, hs=hs):
            m, l, acc = carry
            k0 = pl.multiple_of(kt * tq, tq)
            k = k_ref[pl.ds(k0, tq), hs]
            v = v_ref[pl.ds(k0, tq), hs]
            s = _dot_nt(q, k) * scale + cq - crow_ref[h:h + 1, pl.ds(k0, tq)]
            if masked:
                s = jnp.where(causal, s, NEG)
            m_new = jnp.maximum(m, jnp.max(s, axis=1, keepdims=True))
            alpha = jnp.exp(m - m_new)
            p = jnp.exp(s - m_new)
            l = alpha * l + jnp.sum(p, axis=1, keepdims=True)
            acc = alpha * acc + _dot(p.astype(BF16), v)
            return m_new, l, acc

        init = (jnp.full((tq, 1), NEG, F32), jnp.zeros((tq, 1), F32), jnp.zeros((tq, HEAD_DIM), F32))
        carry = lax.fori_loop(0, qi, functools.partial(step, masked=False), init)
        m, l, acc = step(qi, carry, True)
        o_ref[:, hs] = acc / l


def _fox(zbf, ccol, crow, b, t, *, tq=256):
    nq = t // tq
    return pl.pallas_call(
        functools.partial(_fox_body, tq=tq, scale=HEAD_DIM ** -0.5),
        grid=(b, nq),
        in_specs=[
            pl.BlockSpec((tq, GROUP_W), lambda i, j: (i * nq + j, _COL["fox_q"] // 4)),
            pl.BlockSpec((t, GROUP_W), lambda i, j: (i, _COL["fox_k"] // 4)),
            pl.BlockSpec((t, GROUP_W), lambda i, j: (i, _COL["fox_v"] // 4)),
            pl.BlockSpec((tq, LANE), lambda i, j: (i * nq + j, 0)),
            pl.BlockSpec((None, 8, t), lambda i, j: (i, 0, 0)),
        ],
        out_specs=pl.BlockSpec((tq, GROUP_W), lambda i, j: (i * nq + j, 0)),
        out_shape=jax.ShapeDtypeStruct((b * t, GROUP_W), F32),
        compiler_params=_cparams(("parallel", "arbitrary")),
        name="fox_attn",
    )(zbf, zbf, zbf, ccol, crow)


def _causal_conv(x, halo, w_ref):
    tt = x.shape[0]
    xx = jnp.concatenate([halo, x], axis=0)
    y = x * w_ref[CONV_W - 1:CONV_W, :]
    for d in range(1, CONV_W):
        y = y + pltpu.roll(xx, d, 0)[8:8 + tt] * w_ref[CONV_W - 1 - d:CONV_W - d, :]
    return y


def _gdn_prep_body(x_ref, halo_ref, s_ref, w_ref, alog_ref, dt_ref, qkv_ref, g_ref, sig_ref):
    first = pl.program_id(1) == 0
    x = x_ref[...]
    halo = jnp.where(first, 0.0, halo_ref[...])
    y = _causal_conv(x, halo, w_ref)
    y = y * jax.nn.sigmoid(y)
    for h in range(N_HEADS):
        for part, post in ((0, HEAD_DIM ** -0.5), (1, 1.0)):
            cs = slice(part * GROUP_W + h * HEAD_DIM, part * GROUP_W + (h + 1) * HEAD_DIM)
            u = y[:, cs]
            un = u * lax.rsqrt(jnp.sum(u * u, axis=-1, keepdims=True) + EPS)
            qkv_ref[:, cs] = un * post if part == 0 else un
    qkv_ref[:, 2 * GROUP_W:] = y[:, 2 * GROUP_W:]
    s = s_ref[...]
    g_ref[...] = -jnp.exp(alog_ref[...]) * _softplus(s + dt_ref[...])
    sig_ref[...] = jax.nn.sigmoid(s)


def _gdn_prep(z32, conv_w, a_log, dt_bias, b, t, *, tt=512):
    tt = min(tt, t)
    nt = t // tt
    w3 = 3 * GROUP_W
    pad = lambda v: jnp.zeros((1, LANE), F32).at[0, _S_GDN_A:_S_GDN_A + N_HEADS].set(v)
    cb = _COL["gdn_q"] // 12
    return pl.pallas_call(
        _gdn_prep_body,
        grid=(b, nt),
        in_specs=[
            pl.BlockSpec((tt, w3), lambda i, j: (i * nt + j, cb)),
            pl.BlockSpec((8, w3), lambda i, j: (jnp.maximum((i * nt + j) * (tt // 8) - 1, 0), cb)),
            pl.BlockSpec((tt, LANE), lambda i, j: (i * nt + j, _COL["small"])),
            pl.BlockSpec((CONV_W, w3), lambda i, j: (0, 0)),
            pl.BlockSpec((1, LANE), lambda i, j: (0, 0)),
            pl.BlockSpec((1, LANE), lambda i, j: (0, 0)),
        ],
        out_specs=[
            pl.BlockSpec((tt, w3), lambda i, j: (i * nt + j, 0)),
            pl.BlockSpec((tt, LANE), lambda i, j: (i * nt + j, 0)),
            pl.BlockSpec((tt, LANE), lambda i, j: (i * nt + j, 0)),
        ],
        out_shape=[jax.ShapeDtypeStruct((b * t, w3), F32), jax.ShapeDtypeStruct((b * t, LANE), F32),
                   jax.ShapeDtypeStruct((b * t, LANE), F32)],
        compiler_params=_cparams(("parallel", "arbitrary")),
        name="gdn_prep",
    )(z32, z32, z32, conv_w, pad(a_log), pad(dt_bias))


def _gdn_chunk_body(qkv_ref, g_ref, grow_ref, sig_ref, z_ref, ng_ref, o_ref, s_scr, *, c):
    @pl.when(pl.program_id(1) == 0)
    def _():
        s_scr[...] = jnp.zeros_like(s_scr)

    ri = lax.broadcasted_iota(jnp.int32, (c, c), 0)
    ci = lax.broadcasted_iota(jnp.int32, (c, c), 1)
    tril = ri >= ci
    strict = ri > ci
    eye = (ri == ci).astype(F32)
    gc_cols = _dot_hi(tril.astype(F32), g_ref[...])
    gc_rows = _dot_hi(grow_ref[...], (ri <= ci).astype(F32))
    sig = sig_ref[...]
    for h in range(N_HEADS):
        hs = slice(h * HEAD_DIM, (h + 1) * HEAD_DIM)
        q = qkv_ref[:, hs]
        k = qkv_ref[:, GROUP_W + h * HEAD_DIM:GROUP_W + (h + 1) * HEAD_DIM]
        v = qkv_ref[:, 2 * GROUP_W + h * HEAD_DIM:2 * GROUP_W + (h + 1) * HEAD_DIM]
        beta = sig[:, _S_GDN_B + h:_S_GDN_B + h + 1]
        gc = gc_cols[:, _S_GDN_A + h:_S_GDN_A + h + 1]
        gr = gc_rows[h:h + 1, :]
        gl = gc[c - 1:c, :]
        decay = jnp.where(tril, jnp.exp(jnp.where(tril, gc - gr, 0.0)), 0.0)
        kb = k * beta
        kb16, k16, q16 = kb.astype(BF16), k.astype(BF16), q.astype(BF16)
        a = jnp.where(strict, _dot_nt(kb16, k16) * decay, 0.0)
        pw = -a
        tm = eye + pw
        n = 2
        while n < c:
            pw = _dot_hi(pw, pw)
            tm = tm + _dot_hi(tm, pw)
            n *= 2
        tm16 = tm.astype(BF16)
        u = _dot(tm16, (v * beta).astype(BF16))
        w = _dot(tm16, (kb * jnp.exp(gc)).astype(BF16))
        qk = _dot_nt(q16, k16) * decay
        s = s_scr[h]
        s16 = s.astype(BF16)
        v_new = u - _dot(w.astype(BF16), s16)
        v_new16 = v_new.astype(BF16)
        o = _dot((q * jnp.exp(gc)).astype(BF16), s16) + _dot(qk.astype(BF16), v_new16)
        k_dec = k * jnp.exp(gl - gc)
        s_scr[h] = s * jnp.exp(gl) + _dot_tn(k_dec.astype(BF16), v_new16)
        gz = z_ref[:, hs]
        o_ref[:, hs] = _rms(o, ng_ref[...]) * (gz * jax.nn.sigmoid(gz))


def _gdn_chunk(qkv, g, grow, sig, z32, norm_g, b, t, *, c=GDN_CHUNK):
    n = t // c
    return pl.pallas_call(
        functools.partial(_gdn_chunk_body, c=c),
        grid=(b, n),
        in_specs=[
            pl.BlockSpec((c, 3 * GROUP_W), lambda i, j: (i * n + j, 0)),
            pl.BlockSpec((c, LANE), lambda i, j: (i * n + j, 0)),
            pl.BlockSpec((None, None, 8, c), lambda i, j: (i, j, 0, 0)),
            pl.BlockSpec((c, LANE), lambda i, j: (i * n + j, 0)),
            pl.BlockSpec((c, GROUP_W), lambda i, j: (i * n + j, _COL["gdn_z"] // 4)),
            pl.BlockSpec((1, HEAD_DIM), lambda i, j: (0, 0)),
        ],
        out_specs=pl.BlockSpec((c, GROUP_W), lambda i, j: (i * n + j, 0)),
        out_shape=jax.ShapeDtypeStruct((b * t, GROUP_W), F32),
        scratch_shapes=[pltpu.VMEM((N_HEADS, HEAD_DIM, HEAD_DIM), F32)],
        compiler_params=_cparams(("parallel", "arbitrary")),
        name="gdn_chunk",
    )(qkv, g, grow, sig, z32, norm_g.reshape(1, HEAD_DIM))


def _lru_body(x_ref, halo_ref, gate_ref, cw_ref, cb_ref, wa_ref, ba_ref, wx_ref, bx_ref, lam_ref, o_ref, h_scr):
    first = pl.program_id(1) == 0

    @pl.when(first)
    def _():
        h_scr[...] = jnp.zeros_like(h_scr)

    x = x_ref[...]
    tt = x.shape[0]
    halo = jnp.where(first, 0.0, halo_ref[...])
    xc = _causal_conv(x, halo, cw_ref) + cb_ref[...]
    xc16 = xc.astype(BF16)
    r = jax.nn.sigmoid(_dot(xc16, wa_ref[...]) + ba_ref[...])
    i = jax.nn.sigmoid(_dot(xc16, wx_ref[...]) + bx_ref[...])
    log_a = -LRU_C * r * _softplus(-lam_ref[...])
    a = jnp.exp(log_a)
    u = jnp.sqrt(-jnp.tanh(log_a) * (a * a + 1.0)) * (i * xc)
    rows = lax.broadcasted_iota(jnp.int32, a.shape, 0)
    s = 1
    while s < tt:
        keep = rows >= s
        u = a * jnp.where(keep, pltpu.roll(u, s, 0), 0.0) + u
        a = a * jnp.where(keep, pltpu.roll(a, s, 0), 1.0)
        s *= 2
    hcur = a * h_scr[0:1, :] + u
    h_scr[0:1, :] = hcur[tt - 1:tt, :]
    o_ref[...] = hcur * _gelu_tanh(gate_ref[...])


def _lru(z32, conv_w, conv_b, wa_bd, b_a, wx_bd, b_x, lam, b, t, *, tt=512):
    tt = min(tt, t)
    nt = t // tt
    w = GROUP_W
    row = lambda v: v.reshape(1, w)
    vec = pl.BlockSpec((1, w), lambda i, j: (0, 0))
    mat = pl.BlockSpec((w, w), lambda i, j: (0, 0))
    cx = _COL["lru_x"] // 4
    return pl.pallas_call(
        _lru_body,
        grid=(b, nt),
        in_specs=[
            pl.BlockSpec((tt, w), lambda i, j: (i * nt + j, cx)),
            pl.BlockSpec((8, w), lambda i, j: (jnp.maximum((i * nt + j) * (tt // 8) - 1, 0), cx)),
            pl.BlockSpec((tt, w), lambda i, j: (i * nt + j, _COL["lru_gate"] // 4)),
            pl.BlockSpec((CONV_W, w), lambda i, j: (0, 0)),
            vec, mat, vec, mat, vec, vec,
        ],
        out_specs=pl.BlockSpec((tt, w), lambda i, j: (i * nt + j, 0)),
        out_shape=jax.ShapeDtypeStruct((b * t, w), F32),
        scratch_shapes=[pltpu.VMEM((8, w), F32)],
        compiler_params=_cparams(("parallel", "arbitrary")),
        name="rg_lru",
    )(z32, z32, z32, conv_w, row(conv_b), wa_bd, row(b_a), wx_bd, row(b_x), row(lam))


def _nsa_cmp_body(rk_ref, rv_ref, pek_ref, pev_ref, w1k_ref, w1v_ref, w2k_ref, w2v_ref, kc_ref, vc_ref):
    for r_ref, pe_ref, w1_ref, w2_ref, o_ref in ((rk_ref, pek_ref, w1k_ref, w2k_ref, kc_ref),
                                                 (rv_ref, pev_ref, w1v_ref, w2v_ref, vc_ref)):
        r = r_ref[...]
        nr = r.shape[0]
        lo = (r + pe_ref[0:1, :]).astype(BF16)
        hi = (pltpu.roll(r, nr - 1, 0) + pe_ref[1:2, :]).astype(BF16)
        hid = _gelu_tanh(_dot(lo, w1_ref[0]) + _dot(hi, w1_ref[1]))
        o_ref[...] = _dot(hid.astype(BF16), w2_ref[...])


def _nsa_compress(rk, rv, pe_k, pe_v, w1_k, w1_v, w2_k, w2_v):
    b, nr, wide = rk.shape
    half = CMP_LEN * HEAD_DIM // 2
    assert wide == half
    rspec = pl.BlockSpec((None, nr, wide), lambda i: (i, 0, 0))
    pspec = pl.BlockSpec((2, half), lambda i: (0, 0))
    w1spec = pl.BlockSpec((2, half, HEAD_DIM), lambda i: (0, 0, 0))
    w2spec = pl.BlockSpec((HEAD_DIM, HEAD_DIM), lambda i: (0, 0))
    ospec = pl.BlockSpec((None, nr, HEAD_DIM), lambda i: (i, 0, 0))
    return pl.pallas_call(
        _nsa_cmp_body,
        grid=(b,),
        in_specs=[rspec, rspec, pspec, pspec, w1spec, w1spec, w2spec, w2spec],
        out_specs=[ospec, ospec],
        out_shape=[jax.ShapeDtypeStruct((b, nr, HEAD_DIM), F32)] * 2,
        compiler_params=_cparams(("parallel",)),
        name="nsa_compress",
    )(rk, rv, pe_k.reshape(2, half), pe_v.reshape(2, half), w1_k.reshape(2, half, HEAD_DIM),
      w1_v.reshape(2, half, HEAD_DIM), w2_k, w2_v)


def _bias_from_dist(dist, th_ref, rb_ref):
    n = jnp.maximum(dist, 0)
    outs = [jnp.full(dist.shape, rb_ref[0, h], F32) for h in range(N_HEADS)]
    for bkt in range(1, N_BUCKETS):
        ge = n >= th_ref[bkt]
        outs = [jnp.where(ge, rb_ref[bkt, h], o) for h, o in enumerate(outs)]
    return outs


def _nsa_body(th_ref, rb_ref, q_ref, sig_ref, kc_ref, vc_ref, ks_ref, vs_ref, kw_ref, vw_ref, ov_ref, ex_ref,
              o_ref, sb_scr, wb_scr, selk_scr, *, tk, n_far, n_slc, scale):
    qi = pl.program_id(1)
    qb = Q_BLOCK
    s0 = qi * qb
    nwin = WINDOW + qb
    r = tk // qb

    @pl.when(qi == 0)
    def _():
        ii = lax.broadcasted_iota(jnp.int32, (qb, tk), 0)
        jj = lax.broadcasted_iota(jnp.int32, (qb, tk), 1)
        for e in range(n_far + 1):
            tiles = _bias_from_dist(e * qb + ii - jj, th_ref, rb_ref)
            for h in range(N_HEADS):
                sb_scr[e, h * qb:(h + 1) * qb, :] = tiles[h]
        iw = lax.broadcasted_iota(jnp.int32, (qb, nwin), 0)
        jw = lax.broadcasted_iota(jnp.int32, (qb, nwin), 1)
        dw = iw - jw + WINDOW
        tiles = _bias_from_dist(dw, th_ref, rb_ref)
        okw = (dw >= 0) & (dw < WINDOW)
        for h in range(N_HEADS):
            wb_scr[h * qb:(h + 1) * qb, :] = jnp.where(okw, tiles[h], NEG)

    q4 = jnp.concatenate([q_ref[:, h * HEAD_DIM:(h + 1) * HEAD_DIM] for h in range(N_HEADS)], axis=0)
    stack = lambda m: jnp.concatenate([m] * N_HEADS, axis=0)

    ncp = kc_ref.shape[0]
    ic = lax.broadcasted_iota(jnp.int32, (qb, ncp), 0)
    nc = lax.broadcasted_iota(jnp.int32, (qb, ncp), 1)
    dist_c = s0 + ic - (nc * CMP_STRIDE + CMP_LEN - 1)
    mask_c = stack(dist_c >= 0)
    bias_c = jnp.concatenate(_bias_from_dist(dist_c, th_ref, rb_ref), axis=0)
    s_c = _dot_nt(q4, kc_ref[...].astype(BF16)) * scale + bias_c
    s_c = jnp.where(mask_c, s_c, NEG)
    m_c = jnp.max(s_c, axis=1, keepdims=True)
    e_c = jnp.where(mask_c, jnp.exp(s_c - m_c), 0.0)
    p_c = e_c / jnp.maximum(jnp.sum(e_c, axis=1, keepdims=True), 1e-30)
    p_c16 = p_c.astype(BF16)
    o_c = _dot(p_c16, vc_ref[...].astype(BF16))
    imp = _dot(p_c16[0:qb], ov_ref[...])
    for h in range(1, N_HEADS):
        imp = imp + _dot(p_c16[h * qb:(h + 1) * qb], ov_ref[...])

    jsl = lax.broadcasted_iota(jnp.int32, (qb, n_slc), 1)
    cur = (s0 + lax.broadcasted_iota(jnp.int32, (qb, n_slc), 0)) // SLC_LEN
    forced = (jsl == 0) | (jsl == cur) | (jsl == cur - 1)
    imp = jnp.where(forced, FORCED_SCORE, imp)
    imp = jnp.where(jsl > cur, -1.0, imp)
    rank = jnp.zeros((qb, n_slc), jnp.int32)
    for j2 in range(n_slc):
        col = imp[:, j2:j2 + 1]
        rank = rank + ((col > imp) | ((col == imp) & (jsl > j2))).astype(jnp.int32)
    sel = (rank < min(N_SELECT, n_slc)).astype(BF16)
    selk_scr[...] = _dot(sel, ex_ref[...])

    it = lax.broadcasted_iota(jnp.int32, (qb, tk), 0)
    jt = lax.broadcasted_iota(jnp.int32, (qb, tk), 1)

    def sel_step(kt, carry, diag):
        m, l, acc = carry
        k0 = pl.multiple_of(kt * tk, tk)
        k = ks_ref[pl.ds(k0, tk), :]
        v = vs_ref[pl.ds(k0, tk), :]
        e = jnp.minimum(qi - kt * r, n_far)
        s = _dot_nt(q4, k) * scale + sb_scr[e]
        ok = selk_scr[:, pl.ds(k0, tk)] > 0.5
        if diag:
            ok = ok & (s0 + it >= k0 + jt)
        ok = stack(ok)
        s = jnp.where(ok, s, NEG)
        m_new = jnp.maximum(m, jnp.max(s, axis=1, keepdims=True))
        alpha = jnp.exp(m - m_new)
        p = jnp.where(ok, jnp.exp(s - m_new), 0.0)
        l = alpha * l + jnp.sum(p, axis=1, keepdims=True)
        acc = alpha * acc + _dot(p.astype(BF16), v)
        return m_new, l, acc

    hq = N_HEADS * qb
    init = (jnp.full((hq, 1), NEG, F32), jnp.zeros((hq, 1), F32), jnp.zeros((hq, HEAD_DIM), F32))
    kd = qi // r
    carry = lax.fori_loop(0, kd, functools.partial(sel_step, diag=False), init)
    _, l_s, acc_s = sel_step(kd, carry, True)
    o_s = acc_s / jnp.maximum(l_s, 1e-30)

    w0 = pl.multiple_of(s0, qb)
    kwin = kw_ref[pl.ds(w0, nwin), :]
    vwin = vw_ref[pl.ds(w0, nwin), :]
    posw = s0 - WINDOW + lax.broadcasted_iota(jnp.int32, (hq, nwin), 1)
    s_w = _dot_nt(q4, kwin) * scale + wb_scr[...]
    s_w = jnp.where(posw >= 0, s_w, NEG)
    m_w = jnp.max(s_w, axis=1, keepdims=True)
    e_w = jnp.exp(s_w - m_w)
    o_w = _dot(e_w.astype(BF16), vwin) / jnp.sum(e_w, axis=1, keepdims=True)

    sig = sig_ref[...]
    for h in range(N_HEADS):
        rs = slice(h * qb, (h + 1) * qb)
        g0 = sig[:, _S_NSA_G + 3 * h:_S_NSA_G + 3 * h + 1]
        g1 = sig[:, _S_NSA_G + 3 * h + 1:_S_NSA_G + 3 * h + 2]
        g2 = sig[:, _S_NSA_G + 3 * h + 2:_S_NSA_G + 3 * h + 3]
        o_ref[:, h * HEAD_DIM:(h + 1) * HEAD_DIM] = g0 * o_c[rs] + g1 * o_s[rs] + g2 * o_w[rs]


def _nsa(zbf, sig, kc, vc, kw_pad, vw_pad, thresholds, rel_bias, overlap, expand, b, t, *, tk=256):
    qb = Q_BLOCK
    nq = t // qb
    tk = min(tk, t)
    n_slc = t // SLC_LEN
    ncp = kc.shape[1]
    far_dist = int(math.ceil(MAX_EXACT * (MAX_DIST / MAX_EXACT) ** ((N_BUCKETS - MAX_EXACT - 1) / (N_BUCKETS - MAX_EXACT)))) + 1
    n_far = -(-(far_dist + tk - 1) // qb)
    nwin = WINDOW + qb
    smem = pl.BlockSpec(memory_space=pltpu.SMEM)
    per_b = lambda rows: pl.BlockSpec((None, rows, HEAD_DIM), lambda i, j: (i, 0, 0))
    return pl.pallas_call(
        functools.partial(_nsa_body, tk=tk, n_far=n_far, n_slc=n_slc, scale=HEAD_DIM ** -0.5),
        grid=(b, nq),
        in_specs=[
            smem, smem,
            pl.BlockSpec((qb, GROUP_W), lambda i, j: (i * nq + j, _COL["nsa_q"] // 4)),
            pl.BlockSpec((qb, LANE), lambda i, j: (i * nq + j, 0)),
            per_b(ncp), per_b(ncp),
            pl.BlockSpec((t, HEAD_DIM), lambda i, j: (i, _COL["nsa_ks"])),
            pl.BlockSpec((t, HEAD_DIM), lambda i, j: (i, _COL["nsa_vs"])),
            per_b(t + WINDOW), per_b(t + WINDOW),
            pl.BlockSpec((ncp, n_slc), lambda i, j: (0, 0)),
            pl.BlockSpec((n_slc, t), lambda i, j: (0, 0)),
        ],
        out_specs=pl.BlockSpec((qb, GROUP_W), lambda i, j: (i * nq + j, 0)),
        out_shape=jax.ShapeDtypeStruct((b * t, GROUP_W), F32),
        scratch_shapes=[
            pltpu.VMEM((n_far + 1, N_HEADS * qb, tk), F32),
            pltpu.VMEM((N_HEADS * qb, nwin), F32),
            pltpu.VMEM((qb, t), F32),
        ],
        compiler_params=_cparams(("parallel", "arbitrary")),
        name="nsa_attn",
    )(thresholds, rel_bias, zbf, sig, kc, vc, zbf, zbf, kw_pad, vw_pad, overlap, expand)


def _outproj_body(h_ref, oa_ref, ob_ref, oc_ref, od_ref, g_ref, w_ref, o_ref):
    y = jnp.concatenate([
        _rms(oa_ref[...], g_ref[0:1, :]), ob_ref[...], _rms(oc_ref[...], g_ref[1:2, :]),
        _rms(od_ref[...], g_ref[2:3, :])], axis=-1)
    o_ref[...] = h_ref[...] + _dot(y.astype(BF16), w_ref[...])


def _outproj(h, oa, ob, oc, od, g, w, *, tm=512):
    nt, d = h.shape
    tm = min(tm, nt)
    grp = pl.BlockSpec((tm, GROUP_W), lambda i: (i, 0))
    return pl.pallas_call(
        _outproj_body,
        grid=(nt // tm,),
        in_specs=[pl.BlockSpec((tm, d), lambda i: (i, 0)), grp, grp, grp, grp,
                  pl.BlockSpec((3, GROUP_W), lambda i: (0, 0)),
                  pl.BlockSpec((4 * GROUP_W, d), lambda i: (0, 0))],
        out_specs=pl.BlockSpec((tm, d), lambda i: (i, 0)),
        out_shape=jax.ShapeDtypeStruct((nt, d), F32),
        compiler_params=_cparams(("parallel",)),
        name="outproj",
    )(h, oa, ob, oc, od, g, w)


def _in_perm():
    widths = (("fox_q", GROUP_W), ("fox_k", GROUP_W), ("fox_v", GROUP_W), ("fox_f", N_HEADS),
              ("gdn_q", GROUP_W), ("gdn_k", GROUP_W), ("gdn_v", GROUP_W), ("gdn_a", N_HEADS), ("gdn_b", N_HEADS),
              ("gdn_z", GROUP_W), ("lru_x", GROUP_W), ("lru_gate", GROUP_W), ("nsa_q", GROUP_W),
              ("nsa_kc", HEAD_DIM), ("nsa_vc", HEAD_DIM), ("nsa_ks", HEAD_DIM), ("nsa_vs", HEAD_DIM),
              ("nsa_kw", HEAD_DIM), ("nsa_vw", HEAD_DIM), ("nsa_g", 3 * N_HEADS))
    src, off = {}, 0
    for name, w in widths:
        src[name] = (off, w)
        off += w
    perm = [-1] * D_IN_PAD
    small = {"fox_f": _S_FOX_F, "gdn_a": _S_GDN_A, "gdn_b": _S_GDN_B, "nsa_g": _S_NSA_G}
    for name, (o, w) in src.items():
        base = _COL["small"] * LANE + small[name] if name in small else _COL[name] * LANE
        for c in range(w):
            perm[base + c] = o + c
    return perm, off


def _pad_cols(w, perm):
    idx = jnp.asarray([max(p, 0) for p in perm], jnp.int32)
    keep = jnp.asarray([p >= 0 for p in perm])
    return jnp.where(keep, jnp.take(w, idx, axis=-1), 0.0)


def _block_diag(w):
    l, nb, bw, _ = w.shape
    eye = jnp.eye(nb, dtype=w.dtype)
    return jnp.einsum("lnde,nm->lndme", w, eye).reshape(l, nb * bw, nb * bw)


def _bucket_thresholds(t):
    n = jnp.arange(max(t, 2 * MAX_DIST), dtype=jnp.int32)
    nf = jnp.maximum(n, 1).astype(F32)
    large = MAX_EXACT + (jnp.log(nf / MAX_EXACT) / math.log(MAX_DIST / MAX_EXACT)
                         * (N_BUCKETS - MAX_EXACT)).astype(jnp.int32)
    large = jnp.minimum(large, N_BUCKETS - 1)
    bucket = jnp.where(n < MAX_EXACT, n, large)
    return jnp.sum(bucket[None, :] < jnp.arange(N_BUCKETS, dtype=jnp.int32)[:, None], axis=1).astype(jnp.int32)


def _mixer(h, l, p, b, t):
    nt = b * t
    z32, zbf = _inproj(h, p["mix_norm_g"][l], p["w_in16"][l], p["b_in_p"][l])

    ccol = _fox_gate(z32, b, t)
    crow = jnp.pad(ccol[:, :N_HEADS].reshape(b, t, N_HEADS).transpose(0, 2, 1), ((0, 0), (0, 8 - N_HEADS), (0, 0)))
    o_a = _fox(zbf, ccol, crow, b, t)

    qkv, gdn_g, sig = _gdn_prep(z32, p["gdn_conv_w"][l], p["gdn_a_log"][l], p["gdn_dt_bias"][l], b, t)
    c = GDN_CHUNK
    grow = gdn_g[:, _S_GDN_A:_S_GDN_A + N_HEADS].reshape(b, t // c, c, N_HEADS).transpose(0, 1, 3, 2)
    grow = jnp.pad(grow, ((0, 0), (0, 0), (0, 8 - N_HEADS), (0, 0)))
    o_b = _gdn_chunk(qkv, gdn_g, grow, sig, z32, p["gdn_norm_g"][l], b, t)

    o_c = _lru(z32, p["lru_conv_w"][l], p["lru_conv_b"][l], p["lru_wa_bd"][l], p["lru_b_a"][l],
               p["lru_wx_bd"][l], p["lru_b_x"][l], p["lru_lambda"][l], b, t)

    col = lambda name: z32[:, _COL[name] * LANE:(_COL[name] + 1) * LANE]
    cmp_rows = lambda a: a.reshape(b, t // CMP_STRIDE, CMP_STRIDE * HEAD_DIM)
    kc, vc = _nsa_compress(cmp_rows(col("nsa_kc")), cmp_rows(col("nsa_vc")), p["nsa_pe_k"][l], p["nsa_pe_v"][l],
                           p["nsa_w1_k16"][l], p["nsa_w1_v16"][l], p["nsa_w2_k16"][l], p["nsa_w2_v16"][l])
    colbf = lambda name: zbf[:, _COL[name] * LANE:(_COL[name] + 1) * LANE].reshape(b, t, HEAD_DIM)
    kw_pad = jnp.pad(colbf("nsa_kw"), ((0, 0), (WINDOW, 0), (0, 0)))
    vw_pad = jnp.pad(colbf("nsa_vw"), ((0, 0), (WINDOW, 0), (0, 0)))
    o_d = _nsa(zbf, sig, kc, vc, kw_pad, vw_pad, p["thresholds"], p["rel_bias"], p["overlap"], p["expand"], b, t)

    return _outproj(h, o_a, o_b, o_c, o_d, p["out_norm_g"][l], p["w_out16"][l])


def kernel(x, ffn1_norm_g, ffn1_w_gate, ffn1_w_up, ffn1_w_down, mix_norm_g, w_in, b_in, gdn_conv_w, gdn_a_log, gdn_dt_bias, gdn_norm_g, lru_conv_w, lru_conv_b, lru_w_a, lru_b_a, lru_w_x, lru_b_x, lru_lambda, nsa_pe_k, nsa_w1_k, nsa_w2_k, nsa_pe_v, nsa_w1_v, nsa_w2_v, rel_bias, out_norm_g, w_out, ffn2_norm_g, ffn2_w_gate, ffn2_w_up, ffn2_w_down, final_norm_g):
    b, t, d = x.shape
    depth = w_in.shape[0]
    perm, d_in = _in_perm()
    assert d_in == w_in.shape[-1]
    n_cmp_rows = t // CMP_STRIDE
    n_slc = t // SLC_LEN
    cn = jnp.arange(n_cmp_rows)[:, None] * CMP_STRIDE
    sj = jnp.arange(n_slc)[None, :] * SLC_LEN
    p = dict(
        mix_norm_g=mix_norm_g, w_in16=_pad_cols(w_in, perm).astype(BF16), b_in_p=_pad_cols(b_in, perm),
        gdn_conv_w=gdn_conv_w, gdn_a_log=gdn_a_log, gdn_dt_bias=gdn_dt_bias, gdn_norm_g=gdn_norm_g,
        lru_conv_w=lru_conv_w, lru_conv_b=lru_conv_b, lru_wa_bd=_block_diag(lru_w_a).astype(BF16), lru_b_a=lru_b_a,
        lru_wx_bd=_block_diag(lru_w_x).astype(BF16), lru_b_x=lru_b_x, lru_lambda=lru_lambda,
        nsa_pe_k=nsa_pe_k, nsa_pe_v=nsa_pe_v, nsa_w1_k16=nsa_w1_k.astype(BF16), nsa_w1_v16=nsa_w1_v.astype(BF16),
        nsa_w2_k16=nsa_w2_k.astype(BF16), nsa_w2_v16=nsa_w2_v.astype(BF16),
        rel_bias=rel_bias, thresholds=_bucket_thresholds(t),
        overlap=((cn <= sj + SLC_LEN - 1) & (cn + CMP_LEN - 1 >= sj)).astype(BF16),
        expand=(jnp.arange(t)[None, :] // SLC_LEN == jnp.arange(n_slc)[:, None]).astype(BF16),
        out_norm_g=out_norm_g, w_out16=w_out.astype(BF16),
    )
    f1 = (ffn1_w_gate.astype(BF16), ffn1_w_up.astype(BF16), ffn1_w_down.astype(BF16))
    f2 = (ffn2_w_gate.astype(BF16), ffn2_w_up.astype(BF16), ffn2_w_down.astype(BF16))
    h = x.reshape(b * t, d)
    for l in range(depth):
        h = _ffn(h, ffn1_norm_g[l], f1[0][l], f1[1][l], f1[2][l])
        h = _mixer(h, l, p, b, t)
        h = _ffn(h, ffn2_norm_g[l], f2[0][l], f2[1][l], f2[2][l],
                 final_g=final_norm_g if l == depth - 1 else None)
    return h.reshape(b, t, d)
```

```python
import functools
import math

import jax
import jax.numpy as jnp
from jax import lax
from jax.experimental import pallas as pl
from jax.experimental.pallas import tpu as pltpu

F32 = jnp.float32
BF16 = jnp.bfloat16

EPS = 1e-6
LANE = 128
HEAD_DIM = 128
N_HEADS = 4
GROUP_W = N_HEADS * HEAD_DIM
GDN_CHUNK = 64
CONV_W = 4
LRU_BLOCKS = 8
LRU_C = 8.0
CMP_LEN = 32
CMP_STRIDE = 16
SLC_LEN = 64
N_SELECT = 16
WINDOW = 512
FORCED_SCORE = 1.0e6
N_BUCKETS = 32
MAX_EXACT = 16
MAX_DIST = 1024
Q_BLOCK = 128
NEG = -1.0e30
LOG2E = math.log2(math.e)
VMEM_LIMIT = 56 * 1024 * 1024

_COL = dict(fox_q=0, fox_k=4, fox_v=8, gdn_q=12, gdn_k=16, gdn_v=20, gdn_z=24, lru_x=28, lru_gate=32,
            nsa_q=36, nsa_kc=40, nsa_vc=41, nsa_ks=42, nsa_vs=43, nsa_kw=44, nsa_vw=45, small=46)
_N_COLBLK = 48
D_IN_PAD = _N_COLBLK * LANE
_S_FOX_F, _S_GDN_A, _S_GDN_B, _S_NSA_G = 0, 4, 8, 12


def _cparams(sem):
    return pltpu.CompilerParams(dimension_semantics=sem, vmem_limit_bytes=VMEM_LIMIT)


def _rms(x, g):
    return x * lax.rsqrt(jnp.mean(x * x, axis=-1, keepdims=True) + EPS) * g


def _softplus(x):
    return jnp.maximum(x, 0.0) + jnp.log1p(jnp.exp(-jnp.abs(x)))


def _gelu_tanh(x):
    return 0.5 * x * (1.0 + jnp.tanh(math.sqrt(2.0 / math.pi) * (x + 0.044715 * (x * x * x))))


def _dot(a, b):
    return jnp.dot(a, b, preferred_element_type=F32)


def _dot_nt(a, b):
    return lax.dot_general(a, b, (((1,), (1,)), ((), ())), preferred_element_type=F32)


def _dot_tn(a, b):
    return lax.dot_general(a, b, (((0,), (0,)), ((), ())), preferred_element_type=F32)


def _dot_hi(a, b):
    return jnp.dot(a, b, preferred_element_type=F32, precision=lax.Precision.HIGHEST)


def _split(x):
    hi = x.astype(BF16)
    return hi, (x - hi.astype(F32)).astype(BF16)


def _dot3(a, b):
    return _dot(a[0], b[0]) + (_dot(a[0], b[1]) + _dot(a[1], b[0]))


def _ffn_body(*refs, final):
    if final:
        x_ref, g_ref, wg_ref, wu_ref, wd_ref, fg_ref, o_ref, n_scr, acc_scr = refs
    else:
        x_ref, g_ref, wg_ref, wu_ref, wd_ref, o_ref, n_scr, acc_scr = refs
    j = pl.program_id(1)

    @pl.when(j == 0)
    def _():
        n_scr[...] = _rms(x_ref[...], g_ref[...]).astype(BF16)
        acc_scr[...] = jnp.zeros_like(acc_scr)

    n = n_scr[...]
    gate = _dot(n, wg_ref[...])
    up = _dot(n, wu_ref[...])
    a = (gate * jax.nn.sigmoid(gate) * up).astype(BF16)
    acc_scr[...] += _dot(a, wd_ref[...])

    @pl.when(j == pl.num_programs(1) - 1)
    def _():
        y = x_ref[...] + 0.5 * acc_scr[...]
        if final:
            y = _rms(y, fg_ref[...])
        o_ref[...] = y


def _ffn(x, g, wg, wu, wd, final_g=None, *, tm=512, tf=512):
    nt, d = x.shape
    f = wg.shape[1]
    tm = min(tm, nt)
    final = final_g is not None
    in_specs = [
        pl.BlockSpec((tm, d), lambda i, j: (i, 0)),
        pl.BlockSpec((1, d), lambda i, j: (0, 0)),
        pl.BlockSpec((d, tf), lambda i, j: (0, j)),
        pl.BlockSpec((d, tf), lambda i, j: (0, j)),
        pl.BlockSpec((tf, d), lambda i, j: (j, 0)),
    ]
    args = [x, g.reshape(1, d), wg, wu, wd]
    if final:
        in_specs.append(pl.BlockSpec((1, d), lambda i, j: (0, 0)))
        args.append(final_g.reshape(1, d))
    return pl.pallas_call(
        functools.partial(_ffn_body, final=final),
        grid=(nt // tm, f // tf),
        in_specs=in_specs,
        out_specs=pl.BlockSpec((tm, d), lambda i, j: (i, 0)),
        out_shape=jax.ShapeDtypeStruct((nt, d), F32),
        scratch_shapes=[pltpu.VMEM((tm, d), BF16), pltpu.VMEM((tm, d), F32)],
        compiler_params=_cparams(("parallel", "arbitrary")),
        name="ffn",
    )(*args)


def _inproj_body(x_ref, g_ref, w_ref, b_ref, o32_ref, o16_ref, n_scr):
    @pl.when(pl.program_id(1) == 0)
    def _():
        n_scr[...] = _rms(x_ref[...], g_ref[...]).astype(BF16)

    z = _dot(n_scr[...], w_ref[...]) + b_ref[...]
    o32_ref[...] = z
    o16_ref[...] = z.astype(BF16)


def _inproj(x, g, w, b, *, tm=512, tn=768):
    nt, d = x.shape
    n = w.shape[1]
    tm = min(tm, nt)
    return pl.pallas_call(
        _inproj_body,
        grid=(nt // tm, n // tn),
        in_specs=[
            pl.BlockSpec((tm, d), lambda i, j: (i, 0)),
            pl.BlockSpec((1, d), lambda i, j: (0, 0)),
            pl.BlockSpec((d, tn), lambda i, j: (0, j)),
            pl.BlockSpec((1, tn), lambda i, j: (0, j)),
        ],
        out_specs=[pl.BlockSpec((tm, tn), lambda i, j: (i, j)), pl.BlockSpec((tm, tn), lambda i, j: (i, j))],
        out_shape=[jax.ShapeDtypeStruct((nt, n), F32), jax.ShapeDtypeStruct((nt, n), BF16)],
        scratch_shapes=[pltpu.VMEM((tm, d), BF16)],
        compiler_params=_cparams(("parallel", "arbitrary")),
        name="inproj",
    )(x, g.reshape(1, d), w, b.reshape(1, n))


def _cumsum_rows(y):
    rows = lax.broadcasted_iota(jnp.int32, y.shape, 0)
    s = 1
    while s < y.shape[0]:
        y = y + jnp.where(rows >= s, pltpu.roll(y, s, 0), 0.0)
        s *= 2
    return y


def _fox_gate_body(f_ref, c_ref):
    x = f_ref[...]
    log_f = jnp.minimum(x, 0.0) - jnp.log1p(jnp.exp(-jnp.abs(x)))
    c_ref[...] = _cumsum_rows(log_f) * LOG2E


def _fox_gate(z32, b, t):
    return pl.pallas_call(
        _fox_gate_body,
        grid=(b,),
        in_specs=[pl.BlockSpec((t, LANE), lambda i: (i, _COL["small"]))],
        out_specs=pl.BlockSpec((t, LANE), lambda i: (i, 0)),
        out_shape=jax.ShapeDtypeStruct((b * t, LANE), F32),
        compiler_params=_cparams(("parallel",)),
        name="fox_gate",
    )(z32)


def _fox_body(q_ref, k_ref, v_ref, ccol_ref, crow_ref, o_ref, m_scr, l_scr, acc_scr, *, tq, tk, sc2):
    qi = pl.program_id(1)
    rows = qi * tq + lax.broadcasted_iota(jnp.int32, (tq, tk), 0)
    cols = lax.broadcasted_iota(jnp.int32, (tq, tk), 1)
    m_scr[...] = jnp.full_like(m_scr, NEG)
    l_scr[...] = jnp.zeros_like(l_scr)
    acc_scr[...] = jnp.zeros_like(acc_scr)

    def step(kt, masked):
        k0 = pl.multiple_of(kt * tk, tk)
        heads = range(N_HEADS)
        hsl = [slice(h * HEAD_DIM, (h + 1) * HEAD_DIM) for h in heads]
        s = [_dot_nt(q_ref[:, hsl[h]], k_ref[pl.ds(k0, tk), hsl[h]]) * sc2 for h in heads]
        s = [s[h] + ccol_ref[:, h:h + 1] - crow_ref[h:h + 1, pl.ds(k0, tk)] for h in heads]
        if masked:
            causal = rows >= k0 + cols
            s = [jnp.where(causal, x, NEG) for x in s]
        m_old = [m_scr[h] for h in heads]
        m_new = [jnp.maximum(m_old[h], jnp.max(s[h], axis=1, keepdims=True)) for h in heads]
        alpha = [jnp.exp2(m_old[h] - m_new[h]) for h in heads]
        p = [jnp.exp2(s[h] - m_new[h]) for h in heads]
        l_new = [alpha[h] * l_scr[h] + jnp.sum(p[h], axis=1, keepdims=True) for h in heads]
        pv = [_dot(p[h].astype(BF16), v_ref[pl.ds(k0, tk), hsl[h]]) for h in heads]
        acc_new = [alpha[h] * acc_scr[h] + pv[h] for h in heads]
        for h in heads:
            m_scr[h] = m_new[h]
            l_scr[h] = l_new[h]
            acc_scr[h] = acc_new[h]

    def body(kt, carry):
        step(kt, False)
        return carry

    kd = (qi * tq) // tk
    lax.fori_loop(0, kd, body, 0)
    step(kd, True)
    for h in range(N_HEADS):
        o_ref[:, h * HEAD_DIM:(h + 1) * HEAD_DIM] = acc_scr[h] / l_scr[h]


def _fox(zbf, ccol, crow, b, t, *, tq=256, tk=512):
    tq, tk = min(tq, t), min(tk, t)
    assert tk % tq == 0
    nq = t // tq
    return pl.pallas_call(
        functools.partial(_fox_body, tq=tq, tk=tk, sc2=HEAD_DIM ** -0.5 * LOG2E),
        grid=(b, nq),
        in_specs=[
            pl.BlockSpec((tq, GROUP_W), lambda i, j: (i * nq + j, _COL["fox_q"] // 4)),
            pl.BlockSpec((t, GROUP_W), lambda i, j: (i, _COL["fox_k"] // 4)),
            pl.BlockSpec((t, GROUP_W), lambda i, j: (i, _COL["fox_v"] // 4)),
            pl.BlockSpec((tq, LANE), lambda i, j: (i * nq + j, 0)),
            pl.BlockSpec((None, 8, t), lambda i, j: (i, 0, 0)),
        ],
        out_specs=pl.BlockSpec((tq, GROUP_W), lambda i, j: (i * nq + j, 0)),
        out_shape=jax.ShapeDtypeStruct((b * t, GROUP_W), F32),
        scratch_shapes=[pltpu.VMEM((N_HEADS, tq, 1), F32), pltpu.VMEM((N_HEADS, tq, 1), F32),
                        pltpu.VMEM((N_HEADS, tq, HEAD_DIM), F32)],
        compiler_params=_cparams(("parallel", "arbitrary")),
        name="fox_attn",
    )(zbf, zbf, zbf, ccol, crow)


def _causal_conv(x, halo, w_ref):
    tt = x.shape[0]
    xx = jnp.concatenate([halo, x], axis=0)
    y = x * w_ref[CONV_W - 1:CONV_W, :]
    for d in range(1, CONV_W):
        y = y + pltpu.roll(xx, d, 0)[8:8 + tt] * w_ref[CONV_W - 1 - d:CONV_W - d, :]
    return y


def _gdn_prep_body(x_ref, halo_ref, s_ref, w_ref, alog_ref, dt_ref, qkv_ref, g_ref, sig_ref):
    first = pl.program_id(1) == 0
    x = x_ref[...]
    halo = jnp.where(first, 0.0, halo_ref[...])
    y = _causal_conv(x, halo, w_ref)
    y = y * jax.nn.sigmoid(y)
    for h in range(N_HEADS):
        for part, post in ((0, HEAD_DIM ** -0.5), (1, 1.0)):
            cs = slice(part * GROUP_W + h * HEAD_DIM, part * GROUP_W + (h + 1) * HEAD_DIM)
            u = y[:, cs]
            un = u * lax.rsqrt(jnp.sum(u * u, axis=-1, keepdims=True) + EPS)
            qkv_ref[:, cs] = un * post if part == 0 else un
    qkv_ref[:, 2 * GROUP_W:] = y[:, 2 * GROUP_W:]
    s = s_ref[...]
    g_ref[...] = -jnp.exp(alog_ref[...]) * _softplus(s + dt_ref[...])
    sig_ref[...] = jax.nn.sigmoid(s)


def _gdn_prep(z32, conv_w, a_log, dt_bias, b, t, *, tt=512):
    tt = min(tt, t)
    nt = t // tt
    w3 = 3 * GROUP_W
    pad = lambda v: jnp.zeros((1, LANE), F32).at[0, _S_GDN_A:_S_GDN_A + N_HEADS].set(v)
    cb = _COL["gdn_q"] // 12
    return pl.pallas_call(
        _gdn_prep_body,
        grid=(b, nt),
        in_specs=[
            pl.BlockSpec((tt, w3), lambda i, j: (i * nt + j, cb)),
            pl.BlockSpec((8, w3), lambda i, j: (jnp.maximum((i * nt + j) * (tt // 8) - 1, 0), cb)),
            pl.BlockSpec((tt, LANE), lambda i, j: (i * nt + j, _COL["small"])),
            pl.BlockSpec((CONV_W, w3), lambda i, j: (0, 0)),
            pl.BlockSpec((1, LANE), lambda i, j: (0, 0)),
            pl.BlockSpec((1, LANE), lambda i, j: (0, 0)),
        ],
        out_specs=[
            pl.BlockSpec((tt, w3), lambda i, j: (i * nt + j, 0)),
            pl.BlockSpec((tt, LANE), lambda i, j: (i * nt + j, 0)),
            pl.BlockSpec((tt, LANE), lambda i, j: (i * nt + j, 0)),
        ],
        out_shape=[jax.ShapeDtypeStruct((b * t, w3), F32), jax.ShapeDtypeStruct((b * t, LANE), F32),
                   jax.ShapeDtypeStruct((b * t, LANE), F32)],
        compiler_params=_cparams(("parallel", "arbitrary")),
        name="gdn_prep",
    )(z32, z32, z32, conv_w, pad(a_log), pad(dt_bias))


def _gdn_intra_body(qkv_ref, g_ref, grow_ref, sig_ref, u_ref, w_ref, qg_ref, kd_ref, qk_ref, gc_ref, *, c, nch):
    ri = lax.broadcasted_iota(jnp.int32, (c, c), 0)
    ci = lax.broadcasted_iota(jnp.int32, (c, c), 1)
    tril = ri >= ci
    strict = ri > ci
    eye = (ri == ci).astype(F32)
    lower = tril.astype(F32)
    upper = (ri <= ci).astype(F32)
    chains = [(ch, h) for ch in range(nch) for h in range(N_HEADS)]
    rsl = lambda ch: slice(ch * c, (ch + 1) * c)
    hsl = lambda h, part=0: slice(part * GROUP_W + h * HEAD_DIM, part * GROUP_W + (h + 1) * HEAD_DIM)
    gc_cols = [_dot_hi(lower, g_ref[rsl(ch), :]) for ch in range(nch)]
    gc_rows = [_dot_hi(grow_ref[ch], upper) for ch in range(nch)]
    for ch in range(nch):
        gc_ref[rsl(ch), :] = gc_cols[ch]
    q = [qkv_ref[rsl(ch), hsl(h, 0)] for ch, h in chains]
    k = [qkv_ref[rsl(ch), hsl(h, 1)] for ch, h in chains]
    v = [qkv_ref[rsl(ch), hsl(h, 2)] for ch, h in chains]
    beta = [sig_ref[rsl(ch), _S_GDN_B + h:_S_GDN_B + h + 1] for ch, h in chains]
    gc = [gc_cols[ch][:, _S_GDN_A + h:_S_GDN_A + h + 1] for ch, h in chains]
    gr = [gc_rows[ch][h:h + 1, :] for ch, h in chains]
    idx = range(len(chains))
    decay = [jnp.where(tril, jnp.exp(jnp.where(tril, gc[i] - gr[i], 0.0)), 0.0) for i in idx]
    kb = [k[i] * beta[i] for i in idx]
    k16 = [x.astype(BF16) for x in k]
    a = [jnp.where(strict, _dot_nt(kb[i].astype(BF16), k16[i]) * decay[i], 0.0) for i in idx]
    pw = [-x for x in a]
    tm = [eye + x for x in pw]
    n = 2
    while n < c:
        pws = [_split(x) for x in pw]
        pw = [_dot3(x, x) for x in pws]
        tm = [tm[i] + _dot3(_split(tm[i]), _split(pw[i])) for i in idx]
        n *= 2
    tm16 = [x.astype(BF16) for x in tm]
    u = [_dot(tm16[i], (v[i] * beta[i]).astype(BF16)) for i in idx]
    w = [_dot(tm16[i], (kb[i] * jnp.exp(gc[i])).astype(BF16)) for i in idx]
    qk = [_dot_nt(q[i].astype(BF16), k16[i]) * decay[i] for i in idx]
    for i, (ch, h) in enumerate(chains):
        rs, hs = rsl(ch), hsl(h)
        gl = gc[i][c - 1:c, :]
        u_ref[rs, hs] = u[i]
        w_ref[rs, hs] = w[i].astype(BF16)
        qk_ref[rs, h * c:(h + 1) * c] = qk[i].astype(BF16)
        qg_ref[rs, hs] = (q[i] * jnp.exp(gc[i])).astype(BF16)
        kd_ref[rs, hs] = (k[i] * jnp.exp(gl - gc[i])).astype(BF16)


def _gdn_intra(qkv, g, grow, sig, b, t, *, c=GDN_CHUNK, nch=2):
    nt = b * t
    rows = nch * c
    tok = lambda w: pl.BlockSpec((rows, w), lambda i: (i, 0))
    return pl.pallas_call(
        functools.partial(_gdn_intra_body, c=c, nch=nch),
        grid=(nt // rows,),
        in_specs=[tok(3 * GROUP_W), tok(LANE), pl.BlockSpec((nch, 8, c), lambda i: (i, 0, 0)), tok(LANE)],
        out_specs=[tok(GROUP_W), tok(GROUP_W), tok(GROUP_W), tok(GROUP_W), tok(N_HEADS * c), tok(LANE)],
        out_shape=[jax.ShapeDtypeStruct((nt, GROUP_W), F32), jax.ShapeDtypeStruct((nt, GROUP_W), BF16),
                   jax.ShapeDtypeStruct((nt, GROUP_W), BF16), jax.ShapeDtypeStruct((nt, GROUP_W), BF16),
                   jax.ShapeDtypeStruct((nt, N_HEADS * c), BF16), jax.ShapeDtypeStruct((nt, LANE), F32)],
        compiler_params=_cparams(("parallel",)),
        name="gdn_intra",
    )(qkv, g, grow, sig)


def _gdn_scan_body(u_ref, w_ref, qg_ref, kd_ref, qk_ref, gc_ref, z_ref, ng_ref, o_ref, s_scr, *, c, nch, nb):
    @pl.when(pl.program_id(0) == 0)
    def _():
        s_scr[...] = jnp.zeros_like(s_scr)

    chains = [(b, h) for b in range(nb) for h in range(N_HEADS)]
    idx = range(len(chains))
    hsl = lambda h: slice(h * HEAD_DIM, (h + 1) * HEAD_DIM)
    s = [s_scr[i] for i in idx]
    for ch in range(nch):
        rs = slice(ch * c, (ch + 1) * c)
        egl = [jnp.exp(gc_ref[b, (ch + 1) * c - 1:(ch + 1) * c, :]) for b in range(nb)]
        s16 = [x.astype(BF16) for x in s]
        ws = [_dot(w_ref[b, rs, hsl(h)], s16[i]) for i, (b, h) in enumerate(chains)]
        v_new = [(u_ref[b, rs, hsl(h)] - ws[i]).astype(BF16) for i, (b, h) in enumerate(chains)]
        s = [s[i] * egl[b][:, _S_GDN_A + h:_S_GDN_A + h + 1] + _dot_tn(kd_ref[b, rs, hsl(h)], v_new[i])
             for i, (b, h) in enumerate(chains)]
        o = [_dot(qg_ref[b, rs, hsl(h)], s16[i]) + _dot(qk_ref[b, rs, h * c:(h + 1) * c], v_new[i])
             for i, (b, h) in enumerate(chains)]
        for i, (b, h) in enumerate(chains):
            gz = z_ref[b, rs, hsl(h)]
            o_ref[b, rs, hsl(h)] = _rms(o[i], ng_ref[...]) * (gz * jax.nn.sigmoid(gz))
    for i in idx:
        s_scr[i] = s[i]


def _gdn_scan(u, w, qg, kd, qk, gc, z32, norm_g, b, t, *, c=GDN_CHUNK, nch=2):
    rows = nch * c
    v3 = lambda a: a.reshape(b, t, a.shape[-1])
    blk = lambda wd, cb=0: pl.BlockSpec((b, rows, wd), lambda i: (0, i, cb))
    out = pl.pallas_call(
        functools.partial(_gdn_scan_body, c=c, nch=nch, nb=b),
        grid=(t // rows,),
        in_specs=[blk(GROUP_W), blk(GROUP_W), blk(GROUP_W), blk(GROUP_W), blk(N_HEADS * c), blk(LANE),
                  blk(GROUP_W, _COL["gdn_z"] // 4), pl.BlockSpec((1, HEAD_DIM), lambda i: (0, 0))],
        out_specs=blk(GROUP_W),
        out_shape=jax.ShapeDtypeStruct((b, t, GROUP_W), F32),
        scratch_shapes=[pltpu.VMEM((b * N_HEADS, HEAD_DIM, HEAD_DIM), F32)],
        compiler_params=_cparams(("arbitrary",)),
        name="gdn_scan",
    )(v3(u), v3(w), v3(qg), v3(kd), v3(qk), v3(gc), v3(z32), norm_g.reshape(1, HEAD_DIM))
    return out.reshape(b * t, GROUP_W)


def _lru_body(x_ref, halo_ref, gate_ref, cw_ref, cb_ref, wa_ref, ba_ref, wx_ref, bx_ref, lam_ref, o_ref, h_scr):
    first = pl.program_id(1) == 0

    @pl.when(first)
    def _():
        h_scr[...] = jnp.zeros_like(h_scr)

    x = x_ref[...]
    tt = x.shape[0]
    halo = jnp.where(first, 0.0, halo_ref[...])
    xc = _causal_conv(x, halo, cw_ref) + cb_ref[...]
    xc16 = xc.astype(BF16)
    r = jax.nn.sigmoid(_dot(xc16, wa_ref[...]) + ba_ref[...])
    i = jax.nn.sigmoid(_dot(xc16, wx_ref[...]) + bx_ref[...])
    log_a = -LRU_C * r * _softplus(-lam_ref[...])
    a = jnp.exp(log_a)
    u = jnp.sqrt(-jnp.tanh(log_a) * (a * a + 1.0)) * (i * xc)
    rows = lax.broadcasted_iota(jnp.int32, a.shape, 0)
    s = 1
    while s < tt:
        keep = rows >= s
        u = a * jnp.where(keep, pltpu.roll(u, s, 0), 0.0) + u
        a = a * jnp.where(keep, pltpu.roll(a, s, 0), 1.0)
        s *= 2
    hcur = a * h_scr[0:1, :] + u
    h_scr[0:1, :] = hcur[tt - 1:tt, :]
    o_ref[...] = hcur * _gelu_tanh(gate_ref[...])


def _lru(z32, conv_w, conv_b, wa_bd, b_a, wx_bd, b_x, lam, b, t, *, tt=512):
    tt = min(tt, t)
    nt = t // tt
    w = GROUP_W
    row = lambda v: v.reshape(1, w)
    vec = pl.BlockSpec((1, w), lambda i, j: (0, 0))
    mat = pl.BlockSpec((w, w), lambda i, j: (0, 0))
    cx = _COL["lru_x"] // 4
    return pl.pallas_call(
        _lru_body,
        grid=(b, nt),
        in_specs=[
            pl.BlockSpec((tt, w), lambda i, j: (i * nt + j, cx)),
            pl.BlockSpec((8, w), lambda i, j: (jnp.maximum((i * nt + j) * (tt // 8) - 1, 0), cx)),
            pl.BlockSpec((tt, w), lambda i, j: (i * nt + j, _COL["lru_gate"] // 4)),
            pl.BlockSpec((CONV_W, w), lambda i, j: (0, 0)),
            vec, mat, vec, mat, vec, vec,
        ],
        out_specs=pl.BlockSpec((tt, w), lambda i, j: (i * nt + j, 0)),
        out_shape=jax.ShapeDtypeStruct((b * t, w), F32),
        scratch_shapes=[pltpu.VMEM((8, w), F32)],
        compiler_params=_cparams(("parallel", "arbitrary")),
        name="rg_lru",
    )(z32, z32, z32, conv_w, row(conv_b), wa_bd, row(b_a), wx_bd, row(b_x), row(lam))


def _nsa_cmp_body(rk_ref, rv_ref, pek_ref, pev_ref, w1k_ref, w1v_ref, w2k_ref, w2v_ref, kc_ref, vc_ref):
    for r_ref, pe_ref, w1_ref, w2_ref, o_ref in ((rk_ref, pek_ref, w1k_ref, w2k_ref, kc_ref),
                                                 (rv_ref, pev_ref, w1v_ref, w2v_ref, vc_ref)):
        r = r_ref[...]
        nr = r.shape[0]
        lo = (r + pe_ref[0:1, :]).astype(BF16)
        hi = (pltpu.roll(r, nr - 1, 0) + pe_ref[1:2, :]).astype(BF16)
        hid = _gelu_tanh(_dot(lo, w1_ref[0]) + _dot(hi, w1_ref[1]))
        o_ref[...] = _dot(hid.astype(BF16), w2_ref[...]).astype(BF16)


def _nsa_compress(rk, rv, pe_k, pe_v, w1_k, w1_v, w2_k, w2_v):
    b, nr, wide = rk.shape
    half = CMP_LEN * HEAD_DIM // 2
    assert wide == half
    rspec = pl.BlockSpec((None, nr, wide), lambda i: (i, 0, 0))
    pspec = pl.BlockSpec((2, half), lambda i: (0, 0))
    w1spec = pl.BlockSpec((2, half, HEAD_DIM), lambda i: (0, 0, 0))
    w2spec = pl.BlockSpec((HEAD_DIM, HEAD_DIM), lambda i: (0, 0))
    ospec = pl.BlockSpec((None, nr, HEAD_DIM), lambda i: (i, 0, 0))
    return pl.pallas_call(
        _nsa_cmp_body,
        grid=(b,),
        in_specs=[rspec, rspec, pspec, pspec, w1spec, w1spec, w2spec, w2spec],
        out_specs=[ospec, ospec],
        out_shape=[jax.ShapeDtypeStruct((b, nr, HEAD_DIM), BF16)] * 2,
        compiler_params=_cparams(("parallel",)),
        name="nsa_compress",
    )(rk, rv, pe_k.reshape(2, half), pe_v.reshape(2, half), w1_k.reshape(2, half, HEAD_DIM),
      w1_v.reshape(2, half, HEAD_DIM), w2_k, w2_v)


def _bias_from_dist(dist, th_ref, rb_ref):
    n = jnp.maximum(dist, 0)
    outs = [jnp.full(dist.shape, rb_ref[0, h] * LOG2E, F32) for h in range(N_HEADS)]
    for bkt in range(1, N_BUCKETS):
        ge = n >= th_ref[bkt]
        outs = [jnp.where(ge, rb_ref[bkt, h] * LOG2E, o) for h, o in enumerate(outs)]
    return outs


def _nsa_bias_body(th_ref, rb_ref, sb_ref, wb_ref, cb_ref, *, tk, n_far, ncp):
    qb = Q_BLOCK
    ii = lax.broadcasted_iota(jnp.int32, (qb, tk), 0)
    jj = lax.broadcasted_iota(jnp.int32, (qb, tk), 1)
    for e in range(n_far + 1):
        tiles = _bias_from_dist(e * qb + ii - jj, th_ref, rb_ref)
        for h in range(N_HEADS):
            sb_ref[e, h * qb:(h + 1) * qb, :] = tiles[h]
    nwin = WINDOW + qb
    dw = lax.broadcasted_iota(jnp.int32, (qb, nwin), 0) - lax.broadcasted_iota(jnp.int32, (qb, nwin), 1) + WINDOW
    tiles = _bias_from_dist(dw, th_ref, rb_ref)
    okw = (dw >= 0) & (dw < WINDOW)
    for h in range(N_HEADS):
        wb_ref[h * qb:(h + 1) * qb, :] = jnp.where(okw, tiles[h], NEG)
    mm = lax.broadcasted_iota(jnp.int32, (2 * ncp, qb), 0) - ncp
    dc = lax.broadcasted_iota(jnp.int32, (2 * ncp, qb), 1) - (mm * CMP_STRIDE + CMP_LEN - 1)
    tiles = _bias_from_dist(dc, th_ref, rb_ref)
    for h in range(N_HEADS):
        cb_ref[:, h * qb:(h + 1) * qb] = jnp.where(dc >= 0, tiles[h], NEG)


def _nsa_far_tiles(tk):
    last = MAX_EXACT * (MAX_DIST / MAX_EXACT) ** ((N_BUCKETS - MAX_EXACT - 1) / (N_BUCKETS - MAX_EXACT))
    return -(-(int(math.ceil(last)) + 1 + tk - 1) // Q_BLOCK)


def _nsa_bias(thresholds, rel_bias, t, *, tk):
    n_far = _nsa_far_tiles(tk)
    ncp = t // CMP_STRIDE
    hq = N_HEADS * Q_BLOCK
    smem = pl.BlockSpec(memory_space=pltpu.SMEM)
    return pl.pallas_call(
        functools.partial(_nsa_bias_body, tk=tk, n_far=n_far, ncp=ncp),
        in_specs=[smem, smem],
        out_shape=[jax.ShapeDtypeStruct((n_far + 1, hq, tk), F32), jax.ShapeDtypeStruct((hq, WINDOW + Q_BLOCK), F32),
                   jax.ShapeDtypeStruct((2 * ncp, hq), F32)],
        compiler_params=pltpu.CompilerParams(vmem_limit_bytes=VMEM_LIMIT),
        name="nsa_bias",
    )(thresholds, rel_bias)


def _nsa_body(q_ref, sig_ref, kc_ref, vc_ref, ks_ref, vs_ref, kw_ref, vw_ref, ovt_ref, ex_ref, sb_ref, wb_ref,
              cb_ref, o_ref, selk_scr, *, tk, n_far, n_slc, sc2):
    qi = pl.program_id(1)
    qb = Q_BLOCK
    hq = N_HEADS * qb
    s0 = qi * qb
    nwin = WINDOW + qb
    r = tk // qb
    q4 = jnp.concatenate([q_ref[:, h * HEAD_DIM:(h + 1) * HEAD_DIM] for h in range(N_HEADS)], axis=0)

    ncp = kc_ref.shape[0]
    c0 = pl.multiple_of(ncp - (qb // CMP_STRIDE) * qi, 8)
    s_c = _dot_nt(kc_ref[...], q4) * sc2 + cb_ref[pl.ds(c0, ncp), :]
    m_c = jnp.maximum(jnp.max(s_c, axis=0, keepdims=True), 0.5 * NEG)
    e_c = jnp.exp2(s_c - m_c)
    p_c = (e_c * (1.0 / jnp.maximum(jnp.sum(e_c, axis=0, keepdims=True), 1e-30))).astype(BF16)
    o_c = _dot_tn(p_c, vc_ref[...])
    imp = _dot(ovt_ref[...], p_c[:, 0:qb])
    for h in range(1, N_HEADS):
        imp = imp + _dot(ovt_ref[...], p_c[:, h * qb:(h + 1) * qb])

    jsl = lax.broadcasted_iota(jnp.int32, (n_slc, qb), 0)
    cur = (s0 + lax.broadcasted_iota(jnp.int32, (n_slc, qb), 1)) // SLC_LEN
    forced = (jsl == 0) | (jsl == cur) | (jsl == cur - 1)
    imp = jnp.where(forced, FORCED_SCORE, imp)
    imp = jnp.where(jsl > cur, -1.0, imp)
    rank = jnp.zeros((n_slc, qb), jnp.int32)
    for j2 in range(n_slc):
        row = imp[j2:j2 + 1, :]
        beats = (row > imp) | ((row >= imp) & (jsl > j2))
        rank = rank + jnp.where(beats, 1, 0)
    sel = jnp.where(rank < min(N_SELECT, n_slc), 1.0, 0.0).astype(BF16)
    selk_scr[...] = (1.0 - _dot_tn(sel, ex_ref[...])) * NEG

    it = lax.broadcasted_iota(jnp.int32, (qb, tk), 0)
    jt = lax.broadcasted_iota(jnp.int32, (qb, tk), 1)

    heads = range(N_HEADS)
    qh = [q_ref[:, h * HEAD_DIM:(h + 1) * HEAD_DIM] for h in heads]

    def sel_step(kt, carry, diag):
        m, l, acc = carry
        k0 = pl.multiple_of(kt * tk, tk)
        k = ks_ref[pl.ds(k0, tk), :]
        v = vs_ref[pl.ds(k0, tk), :]
        add = selk_scr[:, pl.ds(k0, tk)]
        if diag:
            add = jnp.where(s0 + it >= k0 + jt, add, NEG)
        eb = [jnp.clip(qi - (kt * r + j), 0, n_far) for j in range(r)]
        bias = lambda h: jnp.concatenate([sb_ref[eb[j], h * qb:(h + 1) * qb, :] for j in range(r)], axis=1)
        s = [_dot_nt(qh[h], k) * sc2 + bias(h) + add for h in heads]
        m_new = tuple(jnp.maximum(m[h], jnp.max(s[h], axis=1, keepdims=True)) for h in heads)
        alpha = [jnp.exp2(m[h] - m_new[h]) for h in heads]
        p = [jnp.exp2(s[h] - m_new[h]) for h in heads]
        l = tuple(alpha[h] * l[h] + jnp.sum(p[h], axis=1, keepdims=True) for h in heads)
        pv = [_dot(p[h].astype(BF16), v) for h in heads]
        acc = tuple(alpha[h] * acc[h] + pv[h] for h in heads)
        return m_new, l, acc

    init = (tuple(jnp.full((qb, 1), 0.5 * NEG, F32) for _ in heads), tuple(jnp.zeros((qb, 1), F32) for _ in heads),
            tuple(jnp.zeros((qb, HEAD_DIM), F32) for _ in heads))
    kd = qi // r
    carry = lax.fori_loop(0, kd, functools.partial(sel_step, diag=False), init)
    _, l_s, acc_s = sel_step(kd, carry, True)
    o_s = jnp.concatenate([acc_s[h] * (1.0 / jnp.maximum(l_s[h], 1e-30)) for h in heads], axis=0)

    w0 = pl.multiple_of(s0, qb)
    posw = s0 - WINDOW + lax.broadcasted_iota(jnp.int32, (hq, nwin), 1)
    s_w = _dot_nt(q4, kw_ref[pl.ds(w0, nwin), :]) * sc2 + wb_ref[...]
    s_w = jnp.where(posw >= 0, s_w, NEG)
    e_w = jnp.exp2(s_w - jnp.max(s_w, axis=1, keepdims=True))
    o_w = _dot(e_w.astype(BF16), vw_ref[pl.ds(w0, nwin), :]) * (1.0 / jnp.sum(e_w, axis=1, keepdims=True))

    sig = sig_ref[...]
    for h in range(N_HEADS):
        rs = slice(h * qb, (h + 1) * qb)
        g0 = sig[:, _S_NSA_G + 3 * h:_S_NSA_G + 3 * h + 1]
        g1 = sig[:, _S_NSA_G + 3 * h + 1:_S_NSA_G + 3 * h + 2]
        g2 = sig[:, _S_NSA_G + 3 * h + 2:_S_NSA_G + 3 * h + 3]
        o_ref[:, h * HEAD_DIM:(h + 1) * HEAD_DIM] = g0 * o_c[rs] + g1 * o_s[rs] + g2 * o_w[rs]


def _nsa(zbf, sig, kc, vc, kw_pad, vw_pad, overlap_t, expand, sb, wb, cb, b, t, *, tk):
    qb = Q_BLOCK
    nq = t // qb
    n_slc = t // SLC_LEN
    ncp = kc.shape[1]
    hq = N_HEADS * qb
    n_far = sb.shape[0] - 1
    per_b = lambda rows: pl.BlockSpec((None, rows, HEAD_DIM), lambda i, j: (i, 0, 0))
    whole = lambda a: pl.BlockSpec(a.shape, lambda i, j: (0,) * a.ndim)
    return pl.pallas_call(
        functools.partial(_nsa_body, tk=tk, n_far=n_far, n_slc=n_slc, sc2=HEAD_DIM ** -0.5 * LOG2E),
        grid=(b, nq),
        in_specs=[
            pl.BlockSpec((qb, GROUP_W), lambda i, j: (i * nq + j, _COL["nsa_q"] // 4)),
            pl.BlockSpec((qb, LANE), lambda i, j: (i * nq + j, 0)),
            per_b(ncp), per_b(ncp),
            pl.BlockSpec((t, HEAD_DIM), lambda i, j: (i, _COL["nsa_ks"])),
            pl.BlockSpec((t, HEAD_DIM), lambda i, j: (i, _COL["nsa_vs"])),
            per_b(t + WINDOW), per_b(t + WINDOW),
            whole(overlap_t), whole(expand), whole(sb), whole(wb), whole(cb),
        ],
        out_specs=pl.BlockSpec((qb, GROUP_W), lambda i, j: (i * nq + j, 0)),
        out_shape=jax.ShapeDtypeStruct((b * t, GROUP_W), F32),
        scratch_shapes=[pltpu.VMEM((qb, t), F32)],
        compiler_params=_cparams(("parallel", "arbitrary")),
        name="nsa_attn",
    )(zbf, sig, kc, vc, zbf, zbf, kw_pad, vw_pad, overlap_t, expand, sb, wb, cb)


def _outproj_body(h_ref, oa_ref, ob_ref, oc_ref, od_ref, g_ref, w_ref, o_ref):
    y = jnp.concatenate([
        _rms(oa_ref[...], g_ref[0:1, :]), ob_ref[...], _rms(oc_ref[...], g_ref[1:2, :]),
        _rms(od_ref[...], g_ref[2:3, :])], axis=-1)
    o_ref[...] = h_ref[...] + _dot(y.astype(BF16), w_ref[...])


def _outproj(h, oa, ob, oc, od, g, w, *, tm=512):
    nt, d = h.shape
    tm = min(tm, nt)
    grp = pl.BlockSpec((tm, GROUP_W), lambda i: (i, 0))
    return pl.pallas_call(
        _outproj_body,
        grid=(nt // tm,),
        in_specs=[pl.BlockSpec((tm, d), lambda i: (i, 0)), grp, grp, grp, grp,
                  pl.BlockSpec((3, GROUP_W), lambda i: (0, 0)),
                  pl.BlockSpec((4 * GROUP_W, d), lambda i: (0, 0))],
        out_specs=pl.BlockSpec((tm, d), lambda i: (i, 0)),
        out_shape=jax.ShapeDtypeStruct((nt, d), F32),
        compiler_params=_cparams(("parallel",)),
        name="outproj",
    )(h, oa, ob, oc, od, g, w)


_IN_WIDTHS = (("fox_q", GROUP_W), ("fox_k", GROUP_W), ("fox_v", GROUP_W), ("fox_f", N_HEADS),
              ("gdn_q", GROUP_W), ("gdn_k", GROUP_W), ("gdn_v", GROUP_W), ("gdn_a", N_HEADS), ("gdn_b", N_HEADS),
              ("gdn_z", GROUP_W), ("lru_x", GROUP_W), ("lru_gate", GROUP_W), ("nsa_q", GROUP_W),
              ("nsa_kc", HEAD_DIM), ("nsa_vc", HEAD_DIM), ("nsa_ks", HEAD_DIM), ("nsa_vs", HEAD_DIM),
              ("nsa_kw", HEAD_DIM), ("nsa_vw", HEAD_DIM), ("nsa_g", 3 * N_HEADS))
D_IN = sum(w for _, w in _IN_WIDTHS)


def _pad_cols(w):
    src, off = {}, 0
    for name, width in _IN_WIDTHS:
        src[name] = w[..., off:off + width]
        off += width
    zeros = lambda n: jnp.zeros(w.shape[:-1] + (n,), w.dtype)
    order = sorted((blk, name) for name, blk in _COL.items() if name != "small")
    small = [src["fox_f"], src["gdn_a"], src["gdn_b"], src["nsa_g"]]
    small.append(zeros(LANE - sum(s.shape[-1] for s in small)))
    return jnp.concatenate([src[name] for _, name in order] + small + [zeros(D_IN_PAD - (_COL["small"] + 1) * LANE)],
                           axis=-1)


def _block_diag(w):
    l, nb, bw, _ = w.shape
    eye = jnp.eye(nb, dtype=w.dtype)
    return jnp.einsum("lnde,nm->lndme", w, eye).reshape(l, nb * bw, nb * bw)


def _bucket_thresholds(t):
    n = jnp.arange(max(t, 2 * MAX_DIST), dtype=jnp.int32)
    nf = jnp.maximum(n, 1).astype(F32)
    large = MAX_EXACT + (jnp.log(nf / MAX_EXACT) / math.log(MAX_DIST / MAX_EXACT)
                         * (N_BUCKETS - MAX_EXACT)).astype(jnp.int32)
    large = jnp.minimum(large, N_BUCKETS - 1)
    bucket = jnp.where(n < MAX_EXACT, n, large)
    return jnp.sum(bucket[None, :] < jnp.arange(N_BUCKETS, dtype=jnp.int32)[:, None], axis=1).astype(jnp.int32)


def _nsa_tile(t):
    return min(512, t)


def _mixer(h, l, p, b, t):
    z32, zbf = _inproj(h, p["mix_norm_g"][l], p["w_in16"][l], p["b_in_p"][l])

    ccol = _fox_gate(z32, b, t)
    crow = jnp.pad(ccol[:, :N_HEADS].reshape(b, t, N_HEADS).transpose(0, 2, 1), ((0, 0), (0, 8 - N_HEADS), (0, 0)))
    o_a = _fox(zbf, ccol, crow, b, t)

    qkv, gdn_g, sig = _gdn_prep(z32, p["gdn_conv_w"][l], p["gdn_a_log"][l], p["gdn_dt_bias"][l], b, t)
    c = GDN_CHUNK
    grow = gdn_g[:, _S_GDN_A:_S_GDN_A + N_HEADS].reshape(b * t // c, c, N_HEADS).transpose(0, 2, 1)
    grow = jnp.pad(grow, ((0, 0), (0, 8 - N_HEADS), (0, 0)))
    u, w, qg, kd, qk, gc = _gdn_intra(qkv, gdn_g, grow, sig, b, t)
    o_b = _gdn_scan(u, w, qg, kd, qk, gc, z32, p["gdn_norm_g"][l], b, t)

    o_c = _lru(z32, p["lru_conv_w"][l], p["lru_conv_b"][l], p["lru_wa_bd"][l], p["lru_b_a"][l],
               p["lru_wx_bd"][l], p["lru_b_x"][l], p["lru_lambda"][l], b, t)

    col = lambda name: z32[:, _COL[name] * LANE:(_COL[name] + 1) * LANE]
    cmp_rows = lambda a: a.reshape(b, t // CMP_STRIDE, CMP_STRIDE * HEAD_DIM)
    kc, vc = _nsa_compress(cmp_rows(col("nsa_kc")), cmp_rows(col("nsa_vc")), p["nsa_pe_k"][l], p["nsa_pe_v"][l],
                           p["nsa_w1_k16"][l], p["nsa_w1_v16"][l], p["nsa_w2_k16"][l], p["nsa_w2_v16"][l])
    colbf = lambda name: zbf[:, _COL[name] * LANE:(_COL[name] + 1) * LANE].reshape(b, t, HEAD_DIM)
    kw_pad = jnp.pad(colbf("nsa_kw"), ((0, 0), (WINDOW, 0), (0, 0)))
    vw_pad = jnp.pad(colbf("nsa_vw"), ((0, 0), (WINDOW, 0), (0, 0)))
    o_d = _nsa(zbf, sig, kc, vc, kw_pad, vw_pad, p["overlap_t"], p["expand"], p["sb"], p["wb"], p["cb"], b, t,
               tk=_nsa_tile(t))

    return _outproj(h, o_a, o_b, o_c, o_d, p["out_norm_g"][l], p["w_out16"][l])


def _mixer_consts(rel_bias, t):
    n_cmp_rows = t // CMP_STRIDE
    n_slc = t // SLC_LEN
    cn = jnp.arange(n_cmp_rows)[None, :] * CMP_STRIDE
    sj = jnp.arange(n_slc)[:, None] * SLC_LEN
    sb, wb, cb = _nsa_bias(_bucket_thresholds(t), rel_bias, t, tk=Q_BLOCK)
    return dict(
        overlap_t=((cn <= sj + SLC_LEN - 1) & (cn + CMP_LEN - 1 >= sj)).astype(BF16),
        expand=(jnp.arange(t)[None, :] // SLC_LEN == jnp.arange(n_slc)[:, None]).astype(BF16),
        sb=sb, wb=wb, cb=cb)


def kernel(x, ffn1_norm_g, ffn1_w_gate, ffn1_w_up, ffn1_w_down, mix_norm_g, w_in, b_in, gdn_conv_w, gdn_a_log, gdn_dt_bias, gdn_norm_g, lru_conv_w, lru_conv_b, lru_w_a, lru_b_a, lru_w_x, lru_b_x, lru_lambda, nsa_pe_k, nsa_w1_k, nsa_w2_k, nsa_pe_v, nsa_w1_v, nsa_w2_v, rel_bias, out_norm_g, w_out, ffn2_norm_g, ffn2_w_gate, ffn2_w_up, ffn2_w_down, final_norm_g):
    b, t, d = x.shape
    depth = w_in.shape[0]
    assert w_in.shape[-1] == D_IN
    p = dict(
        mix_norm_g=mix_norm_g, w_in16=_pad_cols(w_in.astype(BF16)), b_in_p=_pad_cols(b_in),
        gdn_conv_w=gdn_conv_w, gdn_a_log=gdn_a_log, gdn_dt_bias=gdn_dt_bias, gdn_norm_g=gdn_norm_g,
        lru_conv_w=lru_conv_w, lru_conv_b=lru_conv_b, lru_wa_bd=_block_diag(lru_w_a).astype(BF16), lru_b_a=lru_b_a,
        lru_wx_bd=_block_diag(lru_w_x).astype(BF16), lru_b_x=lru_b_x, lru_lambda=lru_lambda,
        nsa_pe_k=nsa_pe_k, nsa_pe_v=nsa_pe_v, nsa_w1_k16=nsa_w1_k.astype(BF16), nsa_w1_v16=nsa_w1_v.astype(BF16),
        nsa_w2_k16=nsa_w2_k.astype(BF16), nsa_w2_v16=nsa_w2_v.astype(BF16),
        out_norm_g=out_norm_g, w_out16=w_out.astype(BF16),
        **_mixer_consts(rel_bias, t),
    )
    f1 = (ffn1_w_gate.astype(BF16), ffn1_w_up.astype(BF16), ffn1_w_down.astype(BF16))
    f2 = (ffn2_w_gate.astype(BF16), ffn2_w_up.astype(BF16), ffn2_w_down.astype(BF16))
    h = x.reshape(b * t, d)
    for l in range(depth):
        h = _ffn(h, ffn1_norm_g[l], f1[0][l], f1[1][l], f1[2][l])
        h = _mixer(h, l, p, b, t)
        h = _ffn(h, ffn2_norm_g[l], f2[0][l], f2[1][l], f2[2][l],
                 final_g=final_norm_g if l == depth - 1 else None)
    return h.reshape(b, t, d)
```

```python
import functools
import math

import jax
import jax.numpy as jnp
from jax import lax
from jax.experimental import pallas as pl
from jax.experimental.pallas import tpu as pltpu

F32 = jnp.float32
BF16 = jnp.bfloat16

EPS = 1e-6
LANE = 128
HEAD_DIM = 128
N_HEADS = 4
GROUP_W = N_HEADS * HEAD_DIM
GDN_CHUNK = 64
CONV_W = 4
LRU_BLOCKS = 8
LRU_C = 8.0
CMP_LEN = 32
CMP_STRIDE = 16
SLC_LEN = 64
N_SELECT = 16
WINDOW = 512
FORCED_SCORE = 1.0e6
N_BUCKETS = 32
MAX_EXACT = 16
MAX_DIST = 1024
Q_BLOCK = 128
NEG = -1.0e30
LOG2E = math.log2(math.e)
VMEM_LIMIT = 56 * 1024 * 1024

_COLS16 = ("fox_q", "fox_k", "fox_v", "nsa_q", "nsa_ks", "nsa_vs", "nsa_kw", "nsa_vw")
_COLS32 = ("gdn_q", "gdn_k", "gdn_v", "gdn_z", "lru_x", "lru_gate", "nsa_kc", "nsa_vc", "small")
_COL = dict(fox_q=0, fox_k=4, fox_v=8, nsa_q=12, nsa_ks=16, nsa_vs=17, nsa_kw=18, nsa_vw=19,
            gdn_q=0, gdn_k=4, gdn_v=8, gdn_z=12, lru_x=16, lru_gate=20, nsa_kc=24, nsa_vc=25, small=26)
W16 = 20 * LANE
W32 = 28 * LANE
_S_FOX_F, _S_GDN_A, _S_GDN_B, _S_NSA_G = 0, 4, 8, 12


def _cparams(sem):
    return pltpu.CompilerParams(dimension_semantics=sem, vmem_limit_bytes=VMEM_LIMIT)


def _rms(x, g):
    return x * lax.rsqrt(jnp.mean(x * x, axis=-1, keepdims=True) + EPS) * g


def _softplus(x):
    return jnp.maximum(x, 0.0) + jnp.log1p(jnp.exp(-jnp.abs(x)))


def _gelu_tanh(x):
    return 0.5 * x * (1.0 + jnp.tanh(math.sqrt(2.0 / math.pi) * (x + 0.044715 * (x * x * x))))


def _dot(a, b):
    return jnp.dot(a, b, preferred_element_type=F32)


def _dot_nt(a, b):
    return lax.dot_general(a, b, (((1,), (1,)), ((), ())), preferred_element_type=F32)


def _dot_tn(a, b):
    return lax.dot_general(a, b, (((0,), (0,)), ((), ())), preferred_element_type=F32)


def _dot_hi(a, b):
    return jnp.dot(a, b, preferred_element_type=F32, precision=lax.Precision.HIGHEST)


def _split(x):
    hi = x.astype(BF16)
    return hi, (x - hi.astype(F32)).astype(BF16)


def _dot3(a, b):
    return _dot(a[0], b[0]) + (_dot(a[0], b[1]) + _dot(a[1], b[0]))


def _ffn_body(*refs, final):
    if final:
        x_ref, g_ref, wg_ref, wu_ref, wd_ref, fg_ref, o_ref, n_scr, acc_scr = refs
    else:
        x_ref, g_ref, wg_ref, wu_ref, wd_ref, o_ref, n_scr, acc_scr = refs
    j = pl.program_id(1)

    @pl.when(j == 0)
    def _():
        n_scr[...] = _rms(x_ref[...], g_ref[...]).astype(BF16)
        acc_scr[...] = jnp.zeros_like(acc_scr)

    n = n_scr[...]
    gate = _dot(n, wg_ref[...])
    up = _dot(n, wu_ref[...])
    a = (gate * jax.nn.sigmoid(gate) * up).astype(BF16)
    acc_scr[...] += _dot(a, wd_ref[...])

    @pl.when(j == pl.num_programs(1) - 1)
    def _():
        y = x_ref[...] + 0.5 * acc_scr[...]
        if final:
            y = _rms(y, fg_ref[...])
        o_ref[...] = y


def _ffn(x, g, wg, wu, wd, l, final_g=None, *, tm=512, tf=512):
    nt, d = x.shape
    f = wg.shape[-1]
    tm = min(tm, nt)
    final = final_g is not None
    in_specs = [
        pl.BlockSpec((tm, d), lambda i, j: (i, 0)),
        pl.BlockSpec((1, d), lambda i, j: (0, 0)),
        pl.BlockSpec((None, d, tf), lambda i, j: (l, 0, j)),
        pl.BlockSpec((None, d, tf), lambda i, j: (l, 0, j)),
        pl.BlockSpec((None, tf, d), lambda i, j: (l, j, 0)),
    ]
    args = [x, g.reshape(1, d), wg, wu, wd]
    if final:
        in_specs.append(pl.BlockSpec((1, d), lambda i, j: (0, 0)))
        args.append(final_g.reshape(1, d))
    return pl.pallas_call(
        functools.partial(_ffn_body, final=final),
        grid=(nt // tm, f // tf),
        in_specs=in_specs,
        out_specs=pl.BlockSpec((tm, d), lambda i, j: (i, 0)),
        out_shape=jax.ShapeDtypeStruct((nt, d), F32),
        scratch_shapes=[pltpu.VMEM((tm, d), BF16), pltpu.VMEM((tm, d), F32)],
        compiler_params=_cparams(("parallel", "arbitrary")),
        name="ffn",
    )(*args)


def _inproj_body(x_ref, g_ref, w_ref, b_ref, o_ref, n_scr):
    @pl.when(pl.program_id(1) == 0)
    def _():
        n_scr[...] = _rms(x_ref[...], g_ref[...]).astype(BF16)

    o_ref[...] = (_dot(n_scr[...], w_ref[...]) + b_ref[...]).astype(o_ref.dtype)


def _inproj(x, g, w, b, l, out_dtype, *, tm=1024, tn=512):
    nt, d = x.shape
    n = w.shape[-1]
    tm = min(tm, nt)
    return pl.pallas_call(
        _inproj_body,
        grid=(nt // tm, n // tn),
        in_specs=[
            pl.BlockSpec((tm, d), lambda i, j: (i, 0)),
            pl.BlockSpec((1, d), lambda i, j: (0, 0)),
            pl.BlockSpec((None, d, tn), lambda i, j: (l, 0, j)),
            pl.BlockSpec((None, 1, tn), lambda i, j: (l, 0, j)),
        ],
        out_specs=pl.BlockSpec((tm, tn), lambda i, j: (i, j)),
        out_shape=jax.ShapeDtypeStruct((nt, n), out_dtype),
        scratch_shapes=[pltpu.VMEM((tm, d), BF16)],
        compiler_params=_cparams(("parallel", "arbitrary")),
        name="inproj",
    )(x, g.reshape(1, d), w, b)


def _cumsum_rows(y):
    rows = lax.broadcasted_iota(jnp.int32, y.shape, 0)
    s = 1
    while s < y.shape[0]:
        y = y + jnp.where(rows >= s, pltpu.roll(y, s, 0), 0.0)
        s *= 2
    return y


def _fox_gate_body(f_ref, c_ref):
    x = f_ref[...]
    log_f = jnp.minimum(x, 0.0) - jnp.log1p(jnp.exp(-jnp.abs(x)))
    c_ref[...] = _cumsum_rows(log_f) * LOG2E


def _fox_gate(z32, b, t):
    return pl.pallas_call(
        _fox_gate_body,
        grid=(b,),
        in_specs=[pl.BlockSpec((t, LANE), lambda i: (i, _COL["small"]))],
        out_specs=pl.BlockSpec((t, LANE), lambda i: (i, 0)),
        out_shape=jax.ShapeDtypeStruct((b * t, LANE), F32),
        compiler_params=_cparams(("parallel",)),
        name="fox_gate",
    )(z32)


def _fox_body(q_ref, k_ref, v_ref, ccol_ref, crow_ref, o_ref, m_scr, l_scr, acc_scr, *, tq, tk, sc2):
    qi = pl.program_id(1)
    rows = qi * tq + lax.broadcasted_iota(jnp.int32, (tq, tk), 0)
    cols = lax.broadcasted_iota(jnp.int32, (tq, tk), 1)
    m_scr[...] = jnp.full_like(m_scr, NEG)
    l_scr[...] = jnp.zeros_like(l_scr)
    acc_scr[...] = jnp.zeros_like(acc_scr)

    def step(kt, masked):
        k0 = pl.multiple_of(kt * tk, tk)
        heads = range(N_HEADS)
        hsl = [slice(h * HEAD_DIM, (h + 1) * HEAD_DIM) for h in heads]
        s = [_dot_nt(q_ref[:, hsl[h]], k_ref[pl.ds(k0, tk), hsl[h]]) * sc2 for h in heads]
        s = [s[h] + ccol_ref[:, h:h + 1] - crow_ref[h:h + 1, pl.ds(k0, tk)] for h in heads]
        if masked:
            causal = rows >= k0 + cols
            s = [jnp.where(causal, x, NEG) for x in s]
        m_old = [m_scr[h] for h in heads]
        m_new = [jnp.maximum(m_old[h], jnp.max(s[h], axis=1, keepdims=True)) for h in heads]
        alpha = [jnp.exp2(m_old[h] - m_new[h]) for h in heads]
        p = [jnp.exp2(s[h] - m_new[h]) for h in heads]
        l_new = [alpha[h] * l_scr[h] + jnp.sum(p[h], axis=1, keepdims=True) for h in heads]
        pv = [_dot(p[h].astype(BF16), v_ref[pl.ds(k0, tk), hsl[h]]) for h in heads]
        acc_new = [alpha[h] * acc_scr[h] + pv[h] for h in heads]
        for h in heads:
            m_scr[h] = m_new[h]
            l_scr[h] = l_new[h]
            acc_scr[h] = acc_new[h]

    def body(kt, carry):
        step(kt, False)
        return carry

    kd = (qi * tq) // tk
    lax.fori_loop(0, kd, body, 0)
    step(kd, True)
    for h in range(N_HEADS):
        o_ref[:, h * HEAD_DIM:(h + 1) * HEAD_DIM] = acc_scr[h] / l_scr[h]


def _fox(zbf, ccol, crow, b, t, *, tq=256, tk=512):
    tq, tk = min(tq, t), min(tk, t)
    assert tk % tq == 0
    nq = t // tq
    return pl.pallas_call(
        functools.partial(_fox_body, tq=tq, tk=tk, sc2=HEAD_DIM ** -0.5 * LOG2E),
        grid=(b, nq),
        in_specs=[
            pl.BlockSpec((tq, GROUP_W), lambda i, j: (i * nq + j, _COL["fox_q"] // 4)),
            pl.BlockSpec((t, GROUP_W), lambda i, j: (i, _COL["fox_k"] // 4)),
            pl.BlockSpec((t, GROUP_W), lambda i, j: (i, _COL["fox_v"] // 4)),
            pl.BlockSpec((tq, LANE), lambda i, j: (i * nq + j, 0)),
            pl.BlockSpec((None, 8, t), lambda i, j: (i, 0, 0)),
        ],
        out_specs=pl.BlockSpec((tq, GROUP_W), lambda i, j: (i * nq + j, 0)),
        out_shape=jax.ShapeDtypeStruct((b * t, GROUP_W), F32),
        scratch_shapes=[pltpu.VMEM((N_HEADS, tq, 1), F32), pltpu.VMEM((N_HEADS, tq, 1), F32),
                        pltpu.VMEM((N_HEADS, tq, HEAD_DIM), F32)],
        compiler_params=_cparams(("parallel", "arbitrary")),
        name="fox_attn",
    )(zbf, zbf, zbf, ccol, crow)


def _causal_conv(x, halo, w_ref):
    tt = x.shape[0]
    xx = jnp.concatenate([halo, x], axis=0)
    y = x * w_ref[CONV_W - 1:CONV_W, :]
    for d in range(1, CONV_W):
        y = y + pltpu.roll(xx, d, 0)[8:8 + tt] * w_ref[CONV_W - 1 - d:CONV_W - d, :]
    return y


def _gdn_prep_body(x_ref, halo_ref, s_ref, w_ref, alog_ref, dt_ref, qkv_ref, g_ref, sig_ref):
    first = pl.program_id(1) == 0
    x = x_ref[...]
    halo = jnp.where(first, 0.0, halo_ref[...])
    y = _causal_conv(x, halo, w_ref)
    y = y * jax.nn.sigmoid(y)
    for h in range(N_HEADS):
        for part, post in ((0, HEAD_DIM ** -0.5), (1, 1.0)):
            cs = slice(part * GROUP_W + h * HEAD_DIM, part * GROUP_W + (h + 1) * HEAD_DIM)
            u = y[:, cs]
            un = u * lax.rsqrt(jnp.sum(u * u, axis=-1, keepdims=True) + EPS)
            qkv_ref[:, cs] = un * post if part == 0 else un
    qkv_ref[:, 2 * GROUP_W:] = y[:, 2 * GROUP_W:]
    s = s_ref[...]
    g_ref[...] = -jnp.exp(alog_ref[...]) * _softplus(s + dt_ref[...])
    sig_ref[...] = jax.nn.sigmoid(s)


def _gdn_prep(z32, conv_w, a_log, dt_bias, b, t, *, tt=512):
    tt = min(tt, t)
    nt = t // tt
    w3 = 3 * GROUP_W
    pad = lambda v: jnp.zeros((1, LANE), F32).at[0, _S_GDN_A:_S_GDN_A + N_HEADS].set(v)
    cb = _COL["gdn_q"] // 12
    return pl.pallas_call(
        _gdn_prep_body,
        grid=(b, nt),
        in_specs=[
            pl.BlockSpec((tt, w3), lambda i, j: (i * nt + j, cb)),
            pl.BlockSpec((8, w3), lambda i, j: (jnp.maximum((i * nt + j) * (tt // 8) - 1, 0), cb)),
            pl.BlockSpec((tt, LANE), lambda i, j: (i * nt + j, _COL["small"])),
            pl.BlockSpec((CONV_W, w3), lambda i, j: (0, 0)),
            pl.BlockSpec((1, LANE), lambda i, j: (0, 0)),
            pl.BlockSpec((1, LANE), lambda i, j: (0, 0)),
        ],
        out_specs=[
            pl.BlockSpec((tt, w3), lambda i, j: (i * nt + j, 0)),
            pl.BlockSpec((tt, LANE), lambda i, j: (i * nt + j, 0)),
            pl.BlockSpec((tt, LANE), lambda i, j: (i * nt + j, 0)),
        ],
        out_shape=[jax.ShapeDtypeStruct((b * t, w3), F32), jax.ShapeDtypeStruct((b * t, LANE), F32),
                   jax.ShapeDtypeStruct((b * t, LANE), F32)],
        compiler_params=_cparams(("parallel", "arbitrary")),
        name="gdn_prep",
    )(z32, z32, z32, conv_w, pad(a_log), pad(dt_bias))


def _gdn_intra_body(qkv_ref, g_ref, grow_ref, sig_ref, u_ref, w_ref, qg_ref, kd_ref, qk_ref, gc_ref, *, c, nch):
    ri = lax.broadcasted_iota(jnp.int32, (c, c), 0)
    ci = lax.broadcasted_iota(jnp.int32, (c, c), 1)
    tril = ri >= ci
    strict = ri > ci
    eye = (ri == ci).astype(F32)
    lower = tril.astype(F32)
    upper = (ri <= ci).astype(F32)
    chains = [(ch, h) for ch in range(nch) for h in range(N_HEADS)]
    rsl = lambda ch: slice(ch * c, (ch + 1) * c)
    hsl = lambda h, part=0: slice(part * GROUP_W + h * HEAD_DIM, part * GROUP_W + (h + 1) * HEAD_DIM)
    gc_cols = [_dot_hi(lower, g_ref[rsl(ch), :]) for ch in range(nch)]
    gc_rows = [_dot_hi(grow_ref[ch], upper) for ch in range(nch)]
    for ch in range(nch):
        gc_ref[rsl(ch), :] = gc_cols[ch]
    q = [qkv_ref[rsl(ch), hsl(h, 0)] for ch, h in chains]
    k = [qkv_ref[rsl(ch), hsl(h, 1)] for ch, h in chains]
    v = [qkv_ref[rsl(ch), hsl(h, 2)] for ch, h in chains]
    beta = [sig_ref[rsl(ch), _S_GDN_B + h:_S_GDN_B + h + 1] for ch, h in chains]
    gc = [gc_cols[ch][:, _S_GDN_A + h:_S_GDN_A + h + 1] for ch, h in chains]
    gr = [gc_rows[ch][h:h + 1, :] for ch, h in chains]
    idx = range(len(chains))
    decay = [jnp.where(tril, jnp.exp(jnp.where(tril, gc[i] - gr[i], 0.0)), 0.0) for i in idx]
    kb = [k[i] * beta[i] for i in idx]
    k16 = [x.astype(BF16) for x in k]
    a = [jnp.where(strict, _dot_nt(kb[i].astype(BF16), k16[i]) * decay[i], 0.0) for i in idx]
    pw = [-x for x in a]
    tm = [eye + x for x in pw]
    n = 2
    while n < c:
        pws = [_split(x) for x in pw]
        pw = [_dot3(x, x) for x in pws]
        tm = [tm[i] + _dot3(_split(tm[i]), _split(pw[i])) for i in idx]
        n *= 2
    tm16 = [x.astype(BF16) for x in tm]
    u = [_dot(tm16[i], (v[i] * beta[i]).astype(BF16)) for i in idx]
    w = [_dot(tm16[i], (kb[i] * jnp.exp(gc[i])).astype(BF16)) for i in idx]
    qk = [_dot_nt(q[i].astype(BF16), k16[i]) * decay[i] for i in idx]
    for i, (ch, h) in enumerate(chains):
        rs, hs = rsl(ch), hsl(h)
        gl = gc[i][c - 1:c, :]
        u_ref[rs, hs] = u[i]
        w_ref[rs, hs] = w[i].astype(BF16)
        qk_ref[rs, h * c:(h + 1) * c] = qk[i].astype(BF16)
        qg_ref[rs, hs] = (q[i] * jnp.exp(gc[i])).astype(BF16)
        kd_ref[rs, hs] = (k[i] * jnp.exp(gl - gc[i])).astype(BF16)


def _gdn_intra(qkv, g, grow, sig, b, t, *, c=GDN_CHUNK, nch=2):
    nt = b * t
    rows = nch * c
    tok = lambda w: pl.BlockSpec((rows, w), lambda i: (i, 0))
    return pl.pallas_call(
        functools.partial(_gdn_intra_body, c=c, nch=nch),
        grid=(nt // rows,),
        in_specs=[tok(3 * GROUP_W), tok(LANE), pl.BlockSpec((nch, 8, c), lambda i: (i, 0, 0)), tok(LANE)],
        out_specs=[tok(GROUP_W), tok(GROUP_W), tok(GROUP_W), tok(GROUP_W), tok(N_HEADS * c), tok(LANE)],
        out_shape=[jax.ShapeDtypeStruct((nt, GROUP_W), F32), jax.ShapeDtypeStruct((nt, GROUP_W), BF16),
                   jax.ShapeDtypeStruct((nt, GROUP_W), BF16), jax.ShapeDtypeStruct((nt, GROUP_W), BF16),
                   jax.ShapeDtypeStruct((nt, N_HEADS * c), BF16), jax.ShapeDtypeStruct((nt, LANE), F32)],
        compiler_params=_cparams(("parallel",)),
        name="gdn_intra",
    )(qkv, g, grow, sig)


def _gdn_scan_body(u_ref, w_ref, qg_ref, kd_ref, qk_ref, gc_ref, z_ref, ng_ref, o_ref, s_scr, *, c, nch, nb):
    @pl.when(pl.program_id(0) == 0)
    def _():
        s_scr[...] = jnp.zeros_like(s_scr)

    chains = [(b, h) for b in range(nb) for h in range(N_HEADS)]
    idx = range(len(chains))
    hsl = lambda h: slice(h * HEAD_DIM, (h + 1) * HEAD_DIM)
    s = [s_scr[i] for i in idx]
    for ch in range(nch):
        rs = slice(ch * c, (ch + 1) * c)
        egl = [jnp.exp(gc_ref[b, (ch + 1) * c - 1:(ch + 1) * c, :]) for b in range(nb)]
        s16 = [x.astype(BF16) for x in s]
        ws = [_dot(w_ref[b, rs, hsl(h)], s16[i]) for i, (b, h) in enumerate(chains)]
        v_new = [(u_ref[b, rs, hsl(h)] - ws[i]).astype(BF16) for i, (b, h) in enumerate(chains)]
        s = [s[i] * egl[b][:, _S_GDN_A + h:_S_GDN_A + h + 1] + _dot_tn(kd_ref[b, rs, hsl(h)], v_new[i])
             for i, (b, h) in enumerate(chains)]
        o = [_dot(qg_ref[b, rs, hsl(h)], s16[i]) + _dot(qk_ref[b, rs, h * c:(h + 1) * c], v_new[i])
             for i, (b, h) in enumerate(chains)]
        for i, (b, h) in enumerate(chains):
            gz = z_ref[b, rs, hsl(h)]
            o_ref[b, rs, hsl(h)] = _rms(o[i], ng_ref[...]) * (gz * jax.nn.sigmoid(gz))
    for i in idx:
        s_scr[i] = s[i]


def _gdn_scan(u, w, qg, kd, qk, gc, z32, norm_g, b, t, *, c=GDN_CHUNK, nch=2):
    rows = nch * c
    v3 = lambda a: a.reshape(b, t, a.shape[-1])
    blk = lambda wd, cb=0: pl.BlockSpec((b, rows, wd), lambda i: (0, i, cb))
    out = pl.pallas_call(
        functools.partial(_gdn_scan_body, c=c, nch=nch, nb=b),
        grid=(t // rows,),
        in_specs=[blk(GROUP_W), blk(GROUP_W), blk(GROUP_W), blk(GROUP_W), blk(N_HEADS * c), blk(LANE),
                  blk(GROUP_W, _COL["gdn_z"] // 4), pl.BlockSpec((1, HEAD_DIM), lambda i: (0, 0))],
        out_specs=blk(GROUP_W),
        out_shape=jax.ShapeDtypeStruct((b, t, GROUP_W), F32),
        scratch_shapes=[pltpu.VMEM((b * N_HEADS, HEAD_DIM, HEAD_DIM), F32)],
        compiler_params=_cparams(("arbitrary",)),
        name="gdn_scan",
    )(v3(u), v3(w), v3(qg), v3(kd), v3(qk), v3(gc), v3(z32), norm_g.reshape(1, HEAD_DIM))
    return out.reshape(b * t, GROUP_W)


def _lru_body(x_ref, halo_ref, gate_ref, cw_ref, cb_ref, wa_ref, ba_ref, wx_ref, bx_ref, lam_ref, o_ref, h_scr):
    first = pl.program_id(1) == 0

    @pl.when(first)
    def _():
        h_scr[...] = jnp.zeros_like(h_scr)

    x = x_ref[...]
    tt = x.shape[0]
    halo = jnp.where(first, 0.0, halo_ref[...])
    xc = _causal_conv(x, halo, cw_ref) + cb_ref[...]
    xc16 = xc.astype(BF16)
    r = jax.nn.sigmoid(_dot(xc16, wa_ref[...]) + ba_ref[...])
    i = jax.nn.sigmoid(_dot(xc16, wx_ref[...]) + bx_ref[...])
    log_a = -LRU_C * r * _softplus(-lam_ref[...])
    a = jnp.exp(log_a)
    u = jnp.sqrt(-jnp.tanh(log_a) * (a * a + 1.0)) * (i * xc)
    rows = lax.broadcasted_iota(jnp.int32, a.shape, 0)
    s = 1
    while s < tt:
        keep = rows >= s
        u = a * jnp.where(keep, pltpu.roll(u, s, 0), 0.0) + u
        a = a * jnp.where(keep, pltpu.roll(a, s, 0), 1.0)
        s *= 2
    hcur = a * h_scr[0:1, :] + u
    h_scr[0:1, :] = hcur[tt - 1:tt, :]
    o_ref[...] = hcur * _gelu_tanh(gate_ref[...])


def _lru(z32, conv_w, conv_b, wa_bd, b_a, wx_bd, b_x, lam, b, t, *, tt=512):
    tt = min(tt, t)
    nt = t // tt
    w = GROUP_W
    row = lambda v: v.reshape(1, w)
    vec = pl.BlockSpec((1, w), lambda i, j: (0, 0))
    mat = pl.BlockSpec((w, w), lambda i, j: (0, 0))
    cx = _COL["lru_x"] // 4
    return pl.pallas_call(
        _lru_body,
        grid=(b, nt),
        in_specs=[
            pl.BlockSpec((tt, w), lambda i, j: (i * nt + j, cx)),
            pl.BlockSpec((8, w), lambda i, j: (jnp.maximum((i * nt + j) * (tt // 8) - 1, 0), cx)),
            pl.BlockSpec((tt, w), lambda i, j: (i * nt + j, _COL["lru_gate"] // 4)),
            pl.BlockSpec((CONV_W, w), lambda i, j: (0, 0)),
            vec, mat, vec, mat, vec, vec,
        ],
        out_specs=pl.BlockSpec((tt, w), lambda i, j: (i * nt + j, 0)),
        out_shape=jax.ShapeDtypeStruct((b * t, w), F32),
        scratch_shapes=[pltpu.VMEM((8, w), F32)],
        compiler_params=_cparams(("parallel", "arbitrary")),
        name="rg_lru",
    )(z32, z32, z32, conv_w, row(conv_b), wa_bd, row(b_a), wx_bd, row(b_x), row(lam))


def _nsa_cmp_body(rk_ref, rv_ref, pek_ref, pev_ref, w1k_ref, w1v_ref, w2k_ref, w2v_ref, kc_ref, vc_ref):
    for r_ref, pe_ref, w1_ref, w2_ref, o_ref in ((rk_ref, pek_ref, w1k_ref, w2k_ref, kc_ref),
                                                 (rv_ref, pev_ref, w1v_ref, w2v_ref, vc_ref)):
        r = r_ref[...]
        nr = r.shape[0]
        lo = (r + pe_ref[0:1, :]).astype(BF16)
        hi = (pltpu.roll(r, nr - 1, 0) + pe_ref[1:2, :]).astype(BF16)
        hid = _gelu_tanh(_dot(lo, w1_ref[0]) + _dot(hi, w1_ref[1]))
        o_ref[...] = _dot(hid.astype(BF16), w2_ref[...]).astype(BF16)


def _nsa_compress(rk, rv, pe_k, pe_v, w1_k, w1_v, w2_k, w2_v):
    b, nr, wide = rk.shape
    half = CMP_LEN * HEAD_DIM // 2
    assert wide == half
    rspec = pl.BlockSpec((None, nr, wide), lambda i: (i, 0, 0))
    pspec = pl.BlockSpec((2, half), lambda i: (0, 0))
    w1spec = pl.BlockSpec((2, half, HEAD_DIM), lambda i: (0, 0, 0))
    w2spec = pl.BlockSpec((HEAD_DIM, HEAD_DIM), lambda i: (0, 0))
    ospec = pl.BlockSpec((None, nr, HEAD_DIM), lambda i: (i, 0, 0))
    return pl.pallas_call(
        _nsa_cmp_body,
        grid=(b,),
        in_specs=[rspec, rspec, pspec, pspec, w1spec, w1spec, w2spec, w2spec],
        out_specs=[ospec, ospec],
        out_shape=[jax.ShapeDtypeStruct((b, nr, HEAD_DIM), BF16)] * 2,
        compiler_params=_cparams(("parallel",)),
        name="nsa_compress",
    )(rk, rv, pe_k.reshape(2, half), pe_v.reshape(2, half), w1_k.reshape(2, half, HEAD_DIM),
      w1_v.reshape(2, half, HEAD_DIM), w2_k, w2_v)


def _bias_from_dist(dist, th_ref, rb_ref):
    n = jnp.maximum(dist, 0)
    outs = [jnp.full(dist.shape, rb_ref[0, h] * LOG2E, F32) for h in range(N_HEADS)]
    for bkt in range(1, N_BUCKETS):
        ge = n >= th_ref[bkt]
        outs = [jnp.where(ge, rb_ref[bkt, h] * LOG2E, o) for h, o in enumerate(outs)]
    return outs


def _nsa_bias_body(th_ref, rb_ref, sb_ref, wb_ref, cb_ref, *, tk, n_far, ncp):
    qb = Q_BLOCK
    ii = lax.broadcasted_iota(jnp.int32, (qb, tk), 0)
    jj = lax.broadcasted_iota(jnp.int32, (qb, tk), 1)
    for e in range(n_far + 1):
        tiles = _bias_from_dist(e * qb + ii - jj, th_ref, rb_ref)
        for h in range(N_HEADS):
            sb_ref[e, h * qb:(h + 1) * qb, :] = tiles[h]
    nwin = WINDOW + qb
    dw = lax.broadcasted_iota(jnp.int32, (qb, nwin), 0) - lax.broadcasted_iota(jnp.int32, (qb, nwin), 1) + WINDOW
    tiles = _bias_from_dist(dw, th_ref, rb_ref)
    okw = (dw >= 0) & (dw < WINDOW)
    for h in range(N_HEADS):
        wb_ref[h * qb:(h + 1) * qb, :] = jnp.where(okw, tiles[h], NEG)
    mm = lax.broadcasted_iota(jnp.int32, (2 * ncp, qb), 0) - ncp
    dc = lax.broadcasted_iota(jnp.int32, (2 * ncp, qb), 1) - (mm * CMP_STRIDE + CMP_LEN - 1)
    tiles = _bias_from_dist(dc, th_ref, rb_ref)
    for h in range(N_HEADS):
        cb_ref[:, h * qb:(h + 1) * qb] = jnp.where(dc >= 0, tiles[h], NEG)


def _nsa_far_tiles(tk):
    last = MAX_EXACT * (MAX_DIST / MAX_EXACT) ** ((N_BUCKETS - MAX_EXACT - 1) / (N_BUCKETS - MAX_EXACT))
    return -(-(int(math.ceil(last)) + 1 + tk - 1) // Q_BLOCK)


def _nsa_bias(thresholds, rel_bias, t, *, tk):
    n_far = _nsa_far_tiles(tk)
    ncp = t // CMP_STRIDE
    hq = N_HEADS * Q_BLOCK
    smem = pl.BlockSpec(memory_space=pltpu.SMEM)
    return pl.pallas_call(
        functools.partial(_nsa_bias_body, tk=tk, n_far=n_far, ncp=ncp),
        in_specs=[smem, smem],
        out_shape=[jax.ShapeDtypeStruct((n_far + 1, hq, tk), F32), jax.ShapeDtypeStruct((hq, WINDOW + Q_BLOCK), F32),
                   jax.ShapeDtypeStruct((2 * ncp, hq), F32)],
        compiler_params=pltpu.CompilerParams(vmem_limit_bytes=VMEM_LIMIT),
        name="nsa_bias",
    )(thresholds, rel_bias)


def _nsa_body(q_ref, sig_ref, kc_ref, vc_ref, ks_ref, vs_ref, kw_ref, vw_ref, ovt_ref, ex_ref, sb_ref, wb_ref,
              cb_ref, o_ref, *, tk, n_far, n_slc, sc2):
    qi = pl.program_id(1)
    qb = Q_BLOCK
    hq = N_HEADS * qb
    s0 = qi * qb
    nwin = WINDOW + qb
    r = tk // qb
    q4 = jnp.concatenate([q_ref[:, h * HEAD_DIM:(h + 1) * HEAD_DIM] for h in range(N_HEADS)], axis=0)

    heads = range(N_HEADS)
    qh = [q_ref[:, h * HEAD_DIM:(h + 1) * HEAD_DIM] for h in heads]

    ncp = kc_ref.shape[0]
    c0 = pl.multiple_of(ncp - (qb // CMP_STRIDE) * qi, 8)
    w0 = pl.multiple_of(s0, qb)
    kwin = kw_ref[pl.ds(w0, nwin), :]
    vwin = vw_ref[pl.ds(w0, nwin), :]
    s_c = _dot_nt(kc_ref[...], q4) * sc2 + cb_ref[pl.ds(c0, ncp), :]
    before_start = s0 - WINDOW + lax.broadcasted_iota(jnp.int32, (qb, nwin), 1) < 0
    s_w = [jnp.where(before_start, NEG, _dot_nt(qh[h], kwin) * sc2 + wb_ref[h * qb:(h + 1) * qb, :]) for h in heads]
    m_c = jnp.maximum(jnp.max(s_c, axis=0, keepdims=True), 0.5 * NEG)
    e_c = jnp.exp2(s_c - m_c)
    e_w = [jnp.exp2(s_w[h] - jnp.max(s_w[h], axis=1, keepdims=True)) for h in heads]
    p_c = (e_c * (1.0 / jnp.maximum(jnp.sum(e_c, axis=0, keepdims=True), 1e-30))).astype(BF16)
    r_w = [1.0 / jnp.sum(e_w[h], axis=1, keepdims=True) for h in heads]
    o_c = _dot_tn(p_c, vc_ref[...])
    imp = _dot(ovt_ref[...], p_c[:, 0:qb])
    for h in range(1, N_HEADS):
        imp = imp + _dot(ovt_ref[...], p_c[:, h * qb:(h + 1) * qb])
    o_w = [_dot(e_w[h].astype(BF16), vwin) * r_w[h] for h in heads]

    jsl = lax.broadcasted_iota(jnp.int32, (n_slc, qb), 0)
    cur = (s0 + lax.broadcasted_iota(jnp.int32, (n_slc, qb), 1)) // SLC_LEN
    forced = (jsl == 0) | (jsl == cur) | (jsl == cur - 1)
    imp = jnp.where(forced, FORCED_SCORE, imp)
    imp = jnp.where(jsl > cur, -1.0, imp)
    groups = range(n_slc // 8)
    impg = [imp[8 * v:8 * v + 8, :] for v in groups]
    jloc = lax.broadcasted_iota(jnp.int32, (8, qb), 0)
    rank = [jnp.zeros((8, qb), jnp.int32) for _ in groups]
    for j2 in range(n_slc):
        row = impg[j2 // 8][j2 % 8:j2 % 8 + 1, :]
        for v in groups:
            if 8 * v > j2:
                beats = row >= impg[v]
            elif 8 * v + 7 < j2:
                beats = row > impg[v]
            else:
                beats = (row > impg[v]) | ((row >= impg[v]) & (jloc > j2 % 8))
            rank[v] = rank[v] + jnp.where(beats, 1, 0)
    selneg = jnp.where(jnp.concatenate(rank, axis=0) < min(N_SELECT, n_slc), 0.0, NEG).astype(BF16)

    it = lax.broadcasted_iota(jnp.int32, (qb, tk), 0)
    jt = lax.broadcasted_iota(jnp.int32, (qb, tk), 1)

    def sel_step(kt, carry, diag):
        m, l, acc = carry
        k0 = pl.multiple_of(kt * tk, tk)
        k = ks_ref[pl.ds(k0, tk), :]
        v = vs_ref[pl.ds(k0, tk), :]
        add = _dot_tn(selneg, ex_ref[:, pl.ds(k0, tk)])
        if diag:
            add = jnp.where(s0 + it >= k0 + jt, add, NEG)
        eb = [jnp.clip(qi - (kt * r + j), 0, n_far) for j in range(r)]
        bias = lambda h: jnp.concatenate([sb_ref[eb[j], h * qb:(h + 1) * qb, :] for j in range(r)], axis=1)
        s = [_dot_nt(qh[h], k) * sc2 + bias(h) + add for h in heads]
        m_new = tuple(jnp.maximum(m[h], jnp.max(s[h], axis=1, keepdims=True)) for h in heads)
        alpha = [jnp.exp2(m[h] - m_new[h]) for h in heads]
        p = [jnp.exp2(s[h] - m_new[h]) for h in heads]
        l = tuple(alpha[h] * l[h] + jnp.sum(p[h], axis=1, keepdims=True) for h in heads)
        pv = [_dot(p[h].astype(BF16), v) for h in heads]
        acc = tuple(alpha[h] * acc[h] + pv[h] for h in heads)
        return m_new, l, acc

    init = (tuple(jnp.full((qb, 1), 0.5 * NEG, F32) for _ in heads), tuple(jnp.zeros((qb, 1), F32) for _ in heads),
            tuple(jnp.zeros((qb, HEAD_DIM), F32) for _ in heads))
    kd = qi // r
    carry = lax.fori_loop(0, kd, functools.partial(sel_step, diag=False), init)
    _, l_s, acc_s = sel_step(kd, carry, True)
    o_s = [acc_s[h] * (1.0 / jnp.maximum(l_s[h], 1e-30)) for h in heads]

    sig = sig_ref[...]
    for h in heads:
        g0 = sig[:, _S_NSA_G + 3 * h:_S_NSA_G + 3 * h + 1]
        g1 = sig[:, _S_NSA_G + 3 * h + 1:_S_NSA_G + 3 * h + 2]
        g2 = sig[:, _S_NSA_G + 3 * h + 2:_S_NSA_G + 3 * h + 3]
        o_ref[:, h * HEAD_DIM:(h + 1) * HEAD_DIM] = g0 * o_c[h * qb:(h + 1) * qb] + g1 * o_s[h] + g2 * o_w[h]


def _nsa(zbf, sig, kc, vc, kw_pad, vw_pad, overlap_t, expand, sb, wb, cb, b, t, *, tk):
    qb = Q_BLOCK
    nq = t // qb
    n_slc = t // SLC_LEN
    ncp = kc.shape[1]
    hq = N_HEADS * qb
    n_far = sb.shape[0] - 1
    per_b = lambda rows: pl.BlockSpec((None, rows, HEAD_DIM), lambda i, j: (i, 0, 0))
    whole = lambda a: pl.BlockSpec(a.shape, lambda i, j: (0,) * a.ndim)
    return pl.pallas_call(
        functools.partial(_nsa_body, tk=tk, n_far=n_far, n_slc=n_slc, sc2=HEAD_DIM ** -0.5 * LOG2E),
        grid=(b, nq),
        in_specs=[
            pl.BlockSpec((qb, GROUP_W), lambda i, j: (i * nq + j, _COL["nsa_q"] // 4)),
            pl.BlockSpec((qb, LANE), lambda i, j: (i * nq + j, 0)),
            per_b(ncp), per_b(ncp),
            pl.BlockSpec((t, HEAD_DIM), lambda i, j: (i, _COL["nsa_ks"])),
            pl.BlockSpec((t, HEAD_DIM), lambda i, j: (i, _COL["nsa_vs"])),
            per_b(t + WINDOW), per_b(t + WINDOW),
            whole(overlap_t), whole(expand), whole(sb), whole(wb), whole(cb),
        ],
        out_specs=pl.BlockSpec((qb, GROUP_W), lambda i, j: (i * nq + j, 0)),
        out_shape=jax.ShapeDtypeStruct((b * t, GROUP_W), F32),
        compiler_params=_cparams(("parallel", "arbitrary")),
        name="nsa_attn",
    )(zbf, sig, kc, vc, zbf, zbf, kw_pad, vw_pad, overlap_t, expand, sb, wb, cb)


def _outproj_body(h_ref, oa_ref, ob_ref, oc_ref, od_ref, g_ref, w_ref, o_ref):
    y = jnp.concatenate([
        _rms(oa_ref[...], g_ref[0:1, :]), ob_ref[...], _rms(oc_ref[...], g_ref[1:2, :]),
        _rms(od_ref[...], g_ref[2:3, :])], axis=-1)
    o_ref[...] = h_ref[...] + _dot(y.astype(BF16), w_ref[...])


def _outproj(h, oa, ob, oc, od, g, w, l, *, tm=512):
    nt, d = h.shape
    tm = min(tm, nt)
    grp = pl.BlockSpec((tm, GROUP_W), lambda i: (i, 0))
    return pl.pallas_call(
        _outproj_body,
        grid=(nt // tm,),
        in_specs=[pl.BlockSpec((tm, d), lambda i: (i, 0)), grp, grp, grp, grp,
                  pl.BlockSpec((3, GROUP_W), lambda i: (0, 0)),
                  pl.BlockSpec((None, 4 * GROUP_W, d), lambda i: (l, 0, 0))],
        out_specs=pl.BlockSpec((tm, d), lambda i: (i, 0)),
        out_shape=jax.ShapeDtypeStruct((nt, d), F32),
        compiler_params=_cparams(("parallel",)),
        name="outproj",
    )(h, oa, ob, oc, od, g, w)


_IN_WIDTHS = (("fox_q", GROUP_W), ("fox_k", GROUP_W), ("fox_v", GROUP_W), ("fox_f", N_HEADS),
              ("gdn_q", GROUP_W), ("gdn_k", GROUP_W), ("gdn_v", GROUP_W), ("gdn_a", N_HEADS), ("gdn_b", N_HEADS),
              ("gdn_z", GROUP_W), ("lru_x", GROUP_W), ("lru_gate", GROUP_W), ("nsa_q", GROUP_W),
              ("nsa_kc", HEAD_DIM), ("nsa_vc", HEAD_DIM), ("nsa_ks", HEAD_DIM), ("nsa_vs", HEAD_DIM),
              ("nsa_kw", HEAD_DIM), ("nsa_vw", HEAD_DIM), ("nsa_g", 3 * N_HEADS))
D_IN = sum(w for _, w in _IN_WIDTHS)


def _pad_cols(w):
    src, off = {}, 0
    for name, width in _IN_WIDTHS:
        src[name] = w[..., off:off + width]
        off += width
    zeros = lambda n: jnp.zeros(w.shape[:-1] + (n,), w.dtype)
    small = [src["fox_f"], src["gdn_a"], src["gdn_b"], src["nsa_g"]]
    src["small"] = jnp.concatenate(small + [zeros(LANE - sum(s.shape[-1] for s in small))], axis=-1)
    w16 = jnp.concatenate([src[name] for name in _COLS16], axis=-1)
    w32 = jnp.concatenate([src[name] for name in _COLS32] + [zeros(W32 - (_COL["small"] + 1) * LANE)], axis=-1)
    assert w16.shape[-1] == W16 and w32.shape[-1] == W32
    return w16, w32


def _block_diag(w):
    l, nb, bw, _ = w.shape
    eye = jnp.eye(nb, dtype=w.dtype)
    return jnp.einsum("lnde,nm->lndme", w, eye).reshape(l, nb * bw, nb * bw)


def _bucket_thresholds(t):
    n = jnp.arange(max(t, 2 * MAX_DIST), dtype=jnp.int32)
    nf = jnp.maximum(n, 1).astype(F32)
    large = MAX_EXACT + (jnp.log(nf / MAX_EXACT) / math.log(MAX_DIST / MAX_EXACT)
                         * (N_BUCKETS - MAX_EXACT)).astype(jnp.int32)
    large = jnp.minimum(large, N_BUCKETS - 1)
    bucket = jnp.where(n < MAX_EXACT, n, large)
    return jnp.sum(bucket[None, :] < jnp.arange(N_BUCKETS, dtype=jnp.int32)[:, None], axis=1).astype(jnp.int32)


def _nsa_tile(t):
    return min(512, t)


def _mixer(h, l, p, b, t):
    zbf = _inproj(h, p["mix_norm_g"][l], p["w16"], p["b16"], l, BF16)
    z32 = _inproj(h, p["mix_norm_g"][l], p["w32"], p["b32"], l, F32)

    ccol = _fox_gate(z32, b, t)
    crow = jnp.pad(ccol[:, :N_HEADS].reshape(b, t, N_HEADS).transpose(0, 2, 1), ((0, 0), (0, 8 - N_HEADS), (0, 0)))
    o_a = _fox(zbf, ccol, crow, b, t)

    qkv, gdn_g, sig = _gdn_prep(z32, p["gdn_conv_w"][l], p["gdn_a_log"][l], p["gdn_dt_bias"][l], b, t)
    c = GDN_CHUNK
    grow = gdn_g[:, _S_GDN_A:_S_GDN_A + N_HEADS].reshape(b * t // c, c, N_HEADS).transpose(0, 2, 1)
    grow = jnp.pad(grow, ((0, 0), (0, 8 - N_HEADS), (0, 0)))
    u, w, qg, kd, qk, gc = _gdn_intra(qkv, gdn_g, grow, sig, b, t)
    o_b = _gdn_scan(u, w, qg, kd, qk, gc, z32, p["gdn_norm_g"][l], b, t)

    o_c = _lru(z32, p["lru_conv_w"][l], p["lru_conv_b"][l], p["lru_wa_bd"][l], p["lru_b_a"][l],
               p["lru_wx_bd"][l], p["lru_b_x"][l], p["lru_lambda"][l], b, t)

    col = lambda name: z32[:, _COL[name] * LANE:(_COL[name] + 1) * LANE]
    cmp_rows = lambda a: a.reshape(b, t // CMP_STRIDE, CMP_STRIDE * HEAD_DIM)
    kc, vc = _nsa_compress(cmp_rows(col("nsa_kc")), cmp_rows(col("nsa_vc")), p["nsa_pe_k"][l], p["nsa_pe_v"][l],
                           p["nsa_w1_k16"][l], p["nsa_w1_v16"][l], p["nsa_w2_k16"][l], p["nsa_w2_v16"][l])
    colbf = lambda name: zbf[:, _COL[name] * LANE:(_COL[name] + 1) * LANE].reshape(b, t, HEAD_DIM)
    kw_pad = jnp.pad(colbf("nsa_kw"), ((0, 0), (WINDOW, 0), (0, 0)))
    vw_pad = jnp.pad(colbf("nsa_vw"), ((0, 0), (WINDOW, 0), (0, 0)))
    o_d = _nsa(zbf, sig, kc, vc, kw_pad, vw_pad, p["overlap_t"], p["expand"], p["sb"], p["wb"], p["cb"], b, t,
               tk=_nsa_tile(t))

    return _outproj(h, o_a, o_b, o_c, o_d, p["out_norm_g"][l], p["w_out16"], l)


def _mixer_consts(rel_bias, t):
    n_cmp_rows = t // CMP_STRIDE
    n_slc = t // SLC_LEN
    cn = jnp.arange(n_cmp_rows)[None, :] * CMP_STRIDE
    sj = jnp.arange(n_slc)[:, None] * SLC_LEN
    sb, wb, cb = _nsa_bias(_bucket_thresholds(t), rel_bias, t, tk=Q_BLOCK)
    return dict(
        overlap_t=((cn <= sj + SLC_LEN - 1) & (cn + CMP_LEN - 1 >= sj)).astype(BF16),
        expand=(jnp.arange(t)[None, :] // SLC_LEN == jnp.arange(n_slc)[:, None]).astype(BF16),
        sb=sb, wb=wb, cb=cb)


def kernel(x, ffn1_norm_g, ffn1_w_gate, ffn1_w_up, ffn1_w_down, mix_norm_g, w_in, b_in, gdn_conv_w, gdn_a_log, gdn_dt_bias, gdn_norm_g, lru_conv_w, lru_conv_b, lru_w_a, lru_b_a, lru_w_x, lru_b_x, lru_lambda, nsa_pe_k, nsa_w1_k, nsa_w2_k, nsa_pe_v, nsa_w1_v, nsa_w2_v, rel_bias, out_norm_g, w_out, ffn2_norm_g, ffn2_w_gate, ffn2_w_up, ffn2_w_down, final_norm_g):
    b, t, d = x.shape
    depth = w_in.shape[0]
    assert w_in.shape[-1] == D_IN
    w16, w32 = _pad_cols(w_in.astype(BF16))
    b16, b32 = _pad_cols(b_in[:, None, :])
    p = dict(
        mix_norm_g=mix_norm_g, w16=w16, w32=w32, b16=b16, b32=b32,
        gdn_conv_w=gdn_conv_w, gdn_a_log=gdn_a_log, gdn_dt_bias=gdn_dt_bias, gdn_norm_g=gdn_norm_g,
        lru_conv_w=lru_conv_w, lru_conv_b=lru_conv_b, lru_wa_bd=_block_diag(lru_w_a).astype(BF16), lru_b_a=lru_b_a,
        lru_wx_bd=_block_diag(lru_w_x).astype(BF16), lru_b_x=lru_b_x, lru_lambda=lru_lambda,
        nsa_pe_k=nsa_pe_k, nsa_pe_v=nsa_pe_v, nsa_w1_k16=nsa_w1_k.astype(BF16), nsa_w1_v16=nsa_w1_v.astype(BF16),
        nsa_w2_k16=nsa_w2_k.astype(BF16), nsa_w2_v16=nsa_w2_v.astype(BF16),
        out_norm_g=out_norm_g, w_out16=w_out.astype(BF16),
        **_mixer_consts(rel_bias, t),
    )
    f1 = (ffn1_w_gate.astype(BF16), ffn1_w_up.astype(BF16), ffn1_w_down.astype(BF16))
    f2 = (ffn2_w_gate.astype(BF16), ffn2_w_up.astype(BF16), ffn2_w_down.astype(BF16))
    h = x.reshape(b * t, d)
    for l in range(depth):
        h = _ffn(h, ffn1_norm_g[l], *f1, l)
        h = _mixer(h, l, p, b, t)
        h = _ffn(h, ffn2_norm_g[l], *f2, l, final_g=final_norm_g if l == depth - 1 else None)
    return h.reshape(b, t, d)
```

```python
import functools
import math

import jax
import jax.numpy as jnp
from jax import lax
from jax.experimental import pallas as pl
from jax.experimental.pallas import tpu as pltpu

F32 = jnp.float32
BF16 = jnp.bfloat16

EPS = 1e-6
LANE = 128
HEAD_DIM = 128
N_HEADS = 4
GROUP_W = N_HEADS * HEAD_DIM
GDN_CHUNK = 64
CONV_W = 4
LRU_BLOCKS = 8
LRU_C = 8.0
CMP_LEN = 32
CMP_STRIDE = 16
SLC_LEN = 64
N_SELECT = 16
WINDOW = 512
FORCED_SCORE = 1.0e6
N_BUCKETS = 32
MAX_EXACT = 16
MAX_DIST = 1024
Q_BLOCK = 128
NEG = -1.0e30
LOG2E = math.log2(math.e)
VMEM_LIMIT = 56 * 1024 * 1024

_COLS16 = ("fox_q", "fox_k", "fox_v", "nsa_q", "nsa_ks", "nsa_vs", "nsa_kw", "nsa_vw")
_COLS32 = ("gdn_q", "gdn_k", "gdn_v", "gdn_z", "lru_x", "lru_gate", "nsa_kc", "nsa_vc", "small")
_COL = dict(fox_q=0, fox_k=4, fox_v=8, nsa_q=12, nsa_ks=16, nsa_vs=17, nsa_kw=18, nsa_vw=19,
            gdn_q=0, gdn_k=4, gdn_v=8, gdn_z=12, lru_x=16, lru_gate=20, nsa_kc=24, nsa_vc=25, small=26)
W16 = 20 * LANE
W32 = 28 * LANE
_S_FOX_F, _S_GDN_A, _S_GDN_B, _S_NSA_G = 0, 4, 8, 12


def _cparams(sem):
    return pltpu.CompilerParams(dimension_semantics=sem, vmem_limit_bytes=VMEM_LIMIT)


def _rms(x, g):
    return x * lax.rsqrt(jnp.mean(x * x, axis=-1, keepdims=True) + EPS) * g


def _softplus(x):
    return jnp.maximum(x, 0.0) + jnp.log1p(jnp.exp(-jnp.abs(x)))


def _gelu_tanh(x):
    return 0.5 * x * (1.0 + jnp.tanh(math.sqrt(2.0 / math.pi) * (x + 0.044715 * (x * x * x))))


def _dot(a, b):
    return jnp.dot(a, b, preferred_element_type=F32)


def _dot_nt(a, b):
    return lax.dot_general(a, b, (((1,), (1,)), ((), ())), preferred_element_type=F32)


def _dot_tn(a, b):
    return lax.dot_general(a, b, (((0,), (0,)), ((), ())), preferred_element_type=F32)


def _dot_hi(a, b):
    return jnp.dot(a, b, preferred_element_type=F32, precision=lax.Precision.HIGHEST)


def _split(x):
    hi = x.astype(BF16)
    return hi, (x - hi.astype(F32)).astype(BF16)


def _dot3(a, b):
    return _dot(a[0], b[0]) + (_dot(a[0], b[1]) + _dot(a[1], b[0]))


def _ffn_body(*refs, final):
    if final:
        x_ref, g_ref, wg_ref, wu_ref, wd_ref, fg_ref, o_ref, n_scr = refs
    else:
        x_ref, g_ref, wg_ref, wu_ref, wd_ref, o_ref, n_scr = refs
    j = pl.program_id(1)

    @pl.when(j == 0)
    def _():
        n_scr[...] = _rms(x_ref[...], g_ref[...]).astype(BF16)
        o_ref[...] = jnp.zeros_like(o_ref)

    n = n_scr[...]
    gate = _dot(n, wg_ref[...])
    up = _dot(n, wu_ref[...])
    a = (gate * jax.nn.sigmoid(gate) * up).astype(BF16)
    o_ref[...] += _dot(a, wd_ref[...])

    @pl.when(j == pl.num_programs(1) - 1)
    def _():
        y = x_ref[...] + 0.5 * o_ref[...]
        if final:
            y = _rms(y, fg_ref[...])
        o_ref[...] = y


def _ffn(x, g, wg, wu, wd, l, final_g=None, *, tm=512, tf=512):
    nt, d = x.shape
    f = wg.shape[-1]
    tm = min(tm, nt)
    final = final_g is not None
    in_specs = [
        pl.BlockSpec((tm, d), lambda i, j: (i, 0)),
        pl.BlockSpec((1, d), lambda i, j: (0, 0)),
        pl.BlockSpec((None, d, tf), lambda i, j: (l, 0, j)),
        pl.BlockSpec((None, d, tf), lambda i, j: (l, 0, j)),
        pl.BlockSpec((None, tf, d), lambda i, j: (l, j, 0)),
    ]
    args = [x, g.reshape(1, d), wg, wu, wd]
    if final:
        in_specs.append(pl.BlockSpec((1, d), lambda i, j: (0, 0)))
        args.append(final_g.reshape(1, d))
    return pl.pallas_call(
        functools.partial(_ffn_body, final=final),
        grid=(nt // tm, f // tf),
        in_specs=in_specs,
        out_specs=pl.BlockSpec((tm, d), lambda i, j: (i, 0)),
        out_shape=jax.ShapeDtypeStruct((nt, d), F32),
        scratch_shapes=[pltpu.VMEM((tm, d), BF16)],
        compiler_params=_cparams(("parallel", "arbitrary")),
        name="ffn",
    )(*args)


def _inproj_body(x_ref, g_ref, w_ref, b_ref, o_ref, n_scr):
    @pl.when(pl.program_id(1) == 0)
    def _():
        n_scr[...] = _rms(x_ref[...], g_ref[...]).astype(BF16)

    o_ref[...] = (_dot(n_scr[...], w_ref[...]) + b_ref[...]).astype(o_ref.dtype)


def _inproj(x, g, w, b, l, out_dtype, *, tm=1024, tn=512):
    nt, d = x.shape
    n = w.shape[-1]
    tm = min(tm, nt)
    return pl.pallas_call(
        _inproj_body,
        grid=(nt // tm, n // tn),
        in_specs=[
            pl.BlockSpec((tm, d), lambda i, j: (i, 0)),
            pl.BlockSpec((1, d), lambda i, j: (0, 0)),
            pl.BlockSpec((None, d, tn), lambda i, j: (l, 0, j)),
            pl.BlockSpec((None, 1, tn), lambda i, j: (l, 0, j)),
        ],
        out_specs=pl.BlockSpec((tm, tn), lambda i, j: (i, j)),
        out_shape=jax.ShapeDtypeStruct((nt, n), out_dtype),
        scratch_shapes=[pltpu.VMEM((tm, d), BF16)],
        compiler_params=_cparams(("parallel", "arbitrary")),
        name="inproj",
    )(x, g.reshape(1, d), w, b)


def _cumsum_rows(y):
    rows = lax.broadcasted_iota(jnp.int32, y.shape, 0)
    s = 1
    while s < y.shape[0]:
        y = y + jnp.where(rows >= s, pltpu.roll(y, s, 0), 0.0)
        s *= 2
    return y


def _fox_gate_body(f_ref, c_ref):
    x = f_ref[...]
    log_f = jnp.minimum(x, 0.0) - jnp.log1p(jnp.exp(-jnp.abs(x)))
    c = _cumsum_rows(log_f) * LOG2E
    for h in range(N_HEADS):
        c_ref[:, h * LANE:(h + 1) * LANE] = jnp.broadcast_to(c[:, _S_FOX_F + h:_S_FOX_F + h + 1], (c.shape[0], LANE))


def _fox_gate(z32, b, t):
    return pl.pallas_call(
        _fox_gate_body,
        grid=(b,),
        in_specs=[pl.BlockSpec((t, LANE), lambda i: (i, _COL["small"]))],
        out_specs=pl.BlockSpec((t, N_HEADS * LANE), lambda i: (i, 0)),
        out_shape=jax.ShapeDtypeStruct((b * t, N_HEADS * LANE), F32),
        compiler_params=_cparams(("parallel",)),
        name="fox_gate",
    )(z32)


def _fox_body(q_ref, k_ref, v_ref, ccol_ref, crow_ref, o_ref, m_scr, l_scr, acc_scr, *, tq, tk, sc2):
    qi = pl.program_id(1)
    rows = qi * tq + lax.broadcasted_iota(jnp.int32, (tq, tk), 0)
    cols = lax.broadcasted_iota(jnp.int32, (tq, tk), 1)
    m_scr[...] = jnp.full_like(m_scr, NEG)
    l_scr[...] = jnp.zeros_like(l_scr)
    acc_scr[...] = jnp.zeros_like(acc_scr)

    def step(kt, masked):
        k0 = pl.multiple_of(kt * tk, tk)
        heads = range(N_HEADS)
        hsl = [slice(h * HEAD_DIM, (h + 1) * HEAD_DIM) for h in heads]
        wide = lambda x: jnp.concatenate([x] * (tk // LANE), axis=1)
        s = [_dot_nt(q_ref[:, hsl[h]], k_ref[pl.ds(k0, tk), hsl[h]]) * sc2 for h in heads]
        s = [s[h] + wide(ccol_ref[:, hsl[h]]) - crow_ref[h:h + 1, pl.ds(k0, tk)] for h in heads]
        if masked:
            causal = rows >= k0 + cols
            s = [jnp.where(causal, x, NEG) for x in s]
        m_old = [m_scr[h] for h in heads]
        m_new = [jnp.maximum(m_old[h], jnp.max(s[h], axis=1, keepdims=True)) for h in heads]
        alpha = [jnp.exp2(m_old[h] - m_new[h]) for h in heads]
        p = [jnp.exp2(s[h] - wide(m_new[h])) for h in heads]
        l_new = [alpha[h] * l_scr[h] + jnp.sum(p[h], axis=1, keepdims=True) for h in heads]
        pv = [_dot(p[h].astype(BF16), v_ref[pl.ds(k0, tk), hsl[h]]) for h in heads]
        acc_new = [alpha[h] * acc_scr[h] + pv[h] for h in heads]
        for h in heads:
            m_scr[h] = m_new[h]
            l_scr[h] = l_new[h]
            acc_scr[h] = acc_new[h]

    def body(kt, carry):
        step(kt, False)
        return carry

    kd = (qi * tq) // tk
    lax.fori_loop(0, kd, body, 0)
    step(kd, True)
    for h in range(N_HEADS):
        o_ref[:, h * HEAD_DIM:(h + 1) * HEAD_DIM] = acc_scr[h] / l_scr[h]


def _fox(zbf, ccol, crow, b, t, *, tq=256, tk=512):
    tq, tk = min(tq, t), min(tk, t)
    assert tk % tq == 0
    nq = t // tq
    return pl.pallas_call(
        functools.partial(_fox_body, tq=tq, tk=tk, sc2=HEAD_DIM ** -0.5 * LOG2E),
        grid=(b, nq),
        in_specs=[
            pl.BlockSpec((tq, GROUP_W), lambda i, j: (i * nq + j, _COL["fox_q"] // 4)),
            pl.BlockSpec((t, GROUP_W), lambda i, j: (i, _COL["fox_k"] // 4)),
            pl.BlockSpec((t, GROUP_W), lambda i, j: (i, _COL["fox_v"] // 4)),
            pl.BlockSpec((tq, N_HEADS * LANE), lambda i, j: (i * nq + j, 0)),
            pl.BlockSpec((None, 8, t), lambda i, j: (i, 0, 0)),
        ],
        out_specs=pl.BlockSpec((tq, GROUP_W), lambda i, j: (i * nq + j, 0)),
        out_shape=jax.ShapeDtypeStruct((b * t, GROUP_W), F32),
        scratch_shapes=[pltpu.VMEM((N_HEADS, tq, LANE), F32), pltpu.VMEM((N_HEADS, tq, LANE), F32),
                        pltpu.VMEM((N_HEADS, tq, HEAD_DIM), F32)],
        compiler_params=_cparams(("parallel", "arbitrary")),
        name="fox_attn",
    )(zbf, zbf, zbf, ccol, crow)


def _causal_conv(x, halo, w_ref):
    tt = x.shape[0]
    xx = jnp.concatenate([halo, x], axis=0)
    y = x * w_ref[CONV_W - 1:CONV_W, :]
    for d in range(1, CONV_W):
        y = y + pltpu.roll(xx, d, 0)[8:8 + tt] * w_ref[CONV_W - 1 - d:CONV_W - d, :]
    return y


def _gdn_prep_body(x_ref, halo_ref, s_ref, w_ref, alog_ref, dt_ref, qkv_ref, g_ref, sig_ref):
    first = pl.program_id(1) == 0
    x = x_ref[...]
    halo = jnp.where(first, 0.0, halo_ref[...])
    y = _causal_conv(x, halo, w_ref)
    y = y * jax.nn.sigmoid(y)
    for h in range(N_HEADS):
        for part, post in ((0, HEAD_DIM ** -0.5), (1, 1.0)):
            cs = slice(part * GROUP_W + h * HEAD_DIM, part * GROUP_W + (h + 1) * HEAD_DIM)
            u = y[:, cs]
            un = u * lax.rsqrt(jnp.sum(u * u, axis=-1, keepdims=True) + EPS)
            qkv_ref[:, cs] = un * post if part == 0 else un
    qkv_ref[:, 2 * GROUP_W:] = y[:, 2 * GROUP_W:]
    s = s_ref[...]
    g_ref[...] = -jnp.exp(alog_ref[...]) * _softplus(s + dt_ref[...])
    sig_ref[...] = jax.nn.sigmoid(s)


def _gdn_prep(z32, conv_w, a_log, dt_bias, b, t, *, tt=512):
    tt = min(tt, t)
    nt = t // tt
    w3 = 3 * GROUP_W
    pad = lambda v: jnp.zeros((1, LANE), F32).at[0, _S_GDN_A:_S_GDN_A + N_HEADS].set(v)
    cb = _COL["gdn_q"] // 12
    return pl.pallas_call(
        _gdn_prep_body,
        grid=(b, nt),
        in_specs=[
            pl.BlockSpec((tt, w3), lambda i, j: (i * nt + j, cb)),
            pl.BlockSpec((8, w3), lambda i, j: (jnp.maximum((i * nt + j) * (tt // 8) - 1, 0), cb)),
            pl.BlockSpec((tt, LANE), lambda i, j: (i * nt + j, _COL["small"])),
            pl.BlockSpec((CONV_W, w3), lambda i, j: (0, 0)),
            pl.BlockSpec((1, LANE), lambda i, j: (0, 0)),
            pl.BlockSpec((1, LANE), lambda i, j: (0, 0)),
        ],
        out_specs=[
            pl.BlockSpec((tt, w3), lambda i, j: (i * nt + j, 0)),
            pl.BlockSpec((tt, LANE), lambda i, j: (i * nt + j, 0)),
            pl.BlockSpec((tt, LANE), lambda i, j: (i * nt + j, 0)),
        ],
        out_shape=[jax.ShapeDtypeStruct((b * t, w3), F32), jax.ShapeDtypeStruct((b * t, LANE), F32),
                   jax.ShapeDtypeStruct((b * t, LANE), F32)],
        compiler_params=_cparams(("parallel", "arbitrary")),
        name="gdn_prep",
    )(z32, z32, z32, conv_w, pad(a_log), pad(dt_bias))


def _gdn_intra_body(qkv_ref, g_ref, grow_ref, sig_ref, u_ref, w_ref, qg_ref, kd_ref, qk_ref, gc_ref, *, c, nch):
    ri = lax.broadcasted_iota(jnp.int32, (c, c), 0)
    ci = lax.broadcasted_iota(jnp.int32, (c, c), 1)
    tril = ri >= ci
    strict = ri > ci
    eye = (ri == ci).astype(F32)
    lower = tril.astype(F32)
    upper = (ri <= ci).astype(F32)
    chains = [(ch, h) for ch in range(nch) for h in range(N_HEADS)]
    rsl = lambda ch: slice(ch * c, (ch + 1) * c)
    hsl = lambda h, part=0: slice(part * GROUP_W + h * HEAD_DIM, part * GROUP_W + (h + 1) * HEAD_DIM)
    gc_cols = [_dot_hi(lower, g_ref[rsl(ch), :]) for ch in range(nch)]
    gc_rows = [_dot_hi(grow_ref[ch], upper) for ch in range(nch)]
    for ch in range(nch):
        gc_ref[rsl(ch), :] = gc_cols[ch]
    q = [qkv_ref[rsl(ch), hsl(h, 0)] for ch, h in chains]
    k = [qkv_ref[rsl(ch), hsl(h, 1)] for ch, h in chains]
    v = [qkv_ref[rsl(ch), hsl(h, 2)] for ch, h in chains]
    beta = [sig_ref[rsl(ch), _S_GDN_B + h:_S_GDN_B + h + 1] for ch, h in chains]
    gc = [gc_cols[ch][:, _S_GDN_A + h:_S_GDN_A + h + 1] for ch, h in chains]
    gr = [gc_rows[ch][h:h + 1, :] for ch, h in chains]
    idx = range(len(chains))
    decay = [jnp.where(tril, jnp.exp(jnp.where(tril, gc[i] - gr[i], 0.0)), 0.0) for i in idx]
    kb = [k[i] * beta[i] for i in idx]
    k16 = [x.astype(BF16) for x in k]
    a = [jnp.where(strict, _dot_nt(kb[i].astype(BF16), k16[i]) * decay[i], 0.0) for i in idx]
    pw = [-x for x in a]
    tm = [eye + x for x in pw]
    n = 2
    while n < c:
        pws = [_split(x) for x in pw]
        pw = [_dot3(x, x) for x in pws]
        tm = [tm[i] + _dot3(_split(tm[i]), _split(pw[i])) for i in idx]
        n *= 2
    tm16 = [x.astype(BF16) for x in tm]
    u = [_dot(tm16[i], (v[i] * beta[i]).astype(BF16)) for i in idx]
    w = [_dot(tm16[i], (kb[i] * jnp.exp(gc[i])).astype(BF16)) for i in idx]
    qk = [_dot_nt(q[i].astype(BF16), k16[i]) * decay[i] for i in idx]
    for i, (ch, h) in enumerate(chains):
        rs, hs = rsl(ch), hsl(h)
        gl = gc[i][c - 1:c, :]
        u_ref[rs, hs] = u[i]
        w_ref[rs, hs] = w[i].astype(BF16)
        qk_ref[rs, h * c:(h + 1) * c] = qk[i].astype(BF16)
        qg_ref[rs, hs] = (q[i] * jnp.exp(gc[i])).astype(BF16)
        kd_ref[rs, hs] = (k[i] * jnp.exp(gl - gc[i])).astype(BF16)


def _gdn_intra(qkv, g, grow, sig, b, t, *, c=GDN_CHUNK, nch=2):
    nt = b * t
    rows = nch * c
    tok = lambda w: pl.BlockSpec((rows, w), lambda i: (i, 0))
    return pl.pallas_call(
        functools.partial(_gdn_intra_body, c=c, nch=nch),
        grid=(nt // rows,),
        in_specs=[tok(3 * GROUP_W), tok(LANE), pl.BlockSpec((nch, 8, c), lambda i: (i, 0, 0)), tok(LANE)],
        out_specs=[tok(GROUP_W), tok(GROUP_W), tok(GROUP_W), tok(GROUP_W), tok(N_HEADS * c), tok(LANE)],
        out_shape=[jax.ShapeDtypeStruct((nt, GROUP_W), F32), jax.ShapeDtypeStruct((nt, GROUP_W), BF16),
                   jax.ShapeDtypeStruct((nt, GROUP_W), BF16), jax.ShapeDtypeStruct((nt, GROUP_W), BF16),
                   jax.ShapeDtypeStruct((nt, N_HEADS * c), BF16), jax.ShapeDtypeStruct((nt, LANE), F32)],
        compiler_params=_cparams(("parallel",)),
        name="gdn_intra",
    )(qkv, g, grow, sig)


def _gdn_scan_body(u_ref, w_ref, qg_ref, kd_ref, qk_ref, gc_ref, z_ref, ng_ref, o_ref, s_scr, *, c, nch, nb):
    @pl.when(pl.program_id(0) == 0)
    def _():
        s_scr[...] = jnp.zeros_like(s_scr)

    chains = [(b, h) for b in range(nb) for h in range(N_HEADS)]
    idx = range(len(chains))
    hsl = lambda h: slice(h * HEAD_DIM, (h + 1) * HEAD_DIM)
    s = [s_scr[i] for i in idx]
    for ch in range(nch):
        rs = slice(ch * c, (ch + 1) * c)
        egl = [jnp.exp(gc_ref[b, (ch + 1) * c - 1:(ch + 1) * c, :]) for b in range(nb)]
        s16 = [x.astype(BF16) for x in s]
        ws = [_dot(w_ref[b, rs, hsl(h)], s16[i]) for i, (b, h) in enumerate(chains)]
        v_new = [(u_ref[b, rs, hsl(h)] - ws[i]).astype(BF16) for i, (b, h) in enumerate(chains)]
        s = [s[i] * egl[b][:, _S_GDN_A + h:_S_GDN_A + h + 1] + _dot_tn(kd_ref[b, rs, hsl(h)], v_new[i])
             for i, (b, h) in enumerate(chains)]
        o = [_dot(qg_ref[b, rs, hsl(h)], s16[i]) + _dot(qk_ref[b, rs, h * c:(h + 1) * c], v_new[i])
             for i, (b, h) in enumerate(chains)]
        for i, (b, h) in enumerate(chains):
            gz = z_ref[b, rs, hsl(h)]
            o_ref[b, rs, hsl(h)] = _rms(o[i], ng_ref[...]) * (gz * jax.nn.sigmoid(gz))
    for i in idx:
        s_scr[i] = s[i]


def _gdn_scan(u, w, qg, kd, qk, gc, z32, norm_g, b, t, *, c=GDN_CHUNK, nch=2):
    rows = nch * c
    v3 = lambda a: a.reshape(b, t, a.shape[-1])
    blk = lambda wd, cb=0: pl.BlockSpec((b, rows, wd), lambda i: (0, i, cb))
    out = pl.pallas_call(
        functools.partial(_gdn_scan_body, c=c, nch=nch, nb=b),
        grid=(t // rows,),
        in_specs=[blk(GROUP_W), blk(GROUP_W), blk(GROUP_W), blk(GROUP_W), blk(N_HEADS * c), blk(LANE),
                  blk(GROUP_W, _COL["gdn_z"] // 4), pl.BlockSpec((1, HEAD_DIM), lambda i: (0, 0))],
        out_specs=blk(GROUP_W),
        out_shape=jax.ShapeDtypeStruct((b, t, GROUP_W), F32),
        scratch_shapes=[pltpu.VMEM((b * N_HEADS, HEAD_DIM, HEAD_DIM), F32)],
        compiler_params=_cparams(("arbitrary",)),
        name="gdn_scan",
    )(v3(u), v3(w), v3(qg), v3(kd), v3(qk), v3(gc), v3(z32), norm_g.reshape(1, HEAD_DIM))
    return out.reshape(b * t, GROUP_W)


def _lru_body(x_ref, halo_ref, gate_ref, cw_ref, cb_ref, wa_ref, ba_ref, wx_ref, bx_ref, lam_ref, o_ref, h_scr):
    first = pl.program_id(1) == 0

    @pl.when(first)
    def _():
        h_scr[...] = jnp.zeros_like(h_scr)

    x = x_ref[...]
    tt = x.shape[0]
    halo = jnp.where(first, 0.0, halo_ref[...])
    xc = _causal_conv(x, halo, cw_ref) + cb_ref[...]
    xc16 = xc.astype(BF16)
    r = jax.nn.sigmoid(_dot(xc16, wa_ref[...]) + ba_ref[...])
    i = jax.nn.sigmoid(_dot(xc16, wx_ref[...]) + bx_ref[...])
    log_a = -LRU_C * r * _softplus(-lam_ref[...])
    a = jnp.exp(log_a)
    u = jnp.sqrt(-jnp.tanh(log_a) * (a * a + 1.0)) * (i * xc)
    rows = lax.broadcasted_iota(jnp.int32, a.shape, 0)
    s = 1
    while s < tt:
        keep = rows >= s
        u = a * jnp.where(keep, pltpu.roll(u, s, 0), 0.0) + u
        a = a * jnp.where(keep, pltpu.roll(a, s, 0), 1.0)
        s *= 2
    hcur = a * h_scr[0:1, :] + u
    h_scr[0:1, :] = hcur[tt - 1:tt, :]
    o_ref[...] = hcur * _gelu_tanh(gate_ref[...])


def _lru(z32, conv_w, conv_b, wa_bd, b_a, wx_bd, b_x, lam, b, t, *, tt=512):
    tt = min(tt, t)
    nt = t // tt
    w = GROUP_W
    row = lambda v: v.reshape(1, w)
    vec = pl.BlockSpec((1, w), lambda i, j: (0, 0))
    mat = pl.BlockSpec((w, w), lambda i, j: (0, 0))
    cx = _COL["lru_x"] // 4
    return pl.pallas_call(
        _lru_body,
        grid=(b, nt),
        in_specs=[
            pl.BlockSpec((tt, w), lambda i, j: (i * nt + j, cx)),
            pl.BlockSpec((8, w), lambda i, j: (jnp.maximum((i * nt + j) * (tt // 8) - 1, 0), cx)),
            pl.BlockSpec((tt, w), lambda i, j: (i * nt + j, _COL["lru_gate"] // 4)),
            pl.BlockSpec((CONV_W, w), lambda i, j: (0, 0)),
            vec, mat, vec, mat, vec, vec,
        ],
        out_specs=pl.BlockSpec((tt, w), lambda i, j: (i * nt + j, 0)),
        out_shape=jax.ShapeDtypeStruct((b * t, w), F32),
        scratch_shapes=[pltpu.VMEM((8, w), F32)],
        compiler_params=_cparams(("parallel", "arbitrary")),
        name="rg_lru",
    )(z32, z32, z32, conv_w, row(conv_b), wa_bd, row(b_a), wx_bd, row(b_x), row(lam))


def _nsa_cmp_body(rk_ref, rv_ref, pek_ref, pev_ref, w1k_ref, w1v_ref, w2k_ref, w2v_ref, kc_ref, vc_ref):
    for r_ref, pe_ref, w1_ref, w2_ref, o_ref in ((rk_ref, pek_ref, w1k_ref, w2k_ref, kc_ref),
                                                 (rv_ref, pev_ref, w1v_ref, w2v_ref, vc_ref)):
        r = r_ref[...]
        nr = r.shape[0]
        lo = (r + pe_ref[0:1, :]).astype(BF16)
        hi = (pltpu.roll(r, nr - 1, 0) + pe_ref[1:2, :]).astype(BF16)
        hid = _gelu_tanh(_dot(lo, w1_ref[0]) + _dot(hi, w1_ref[1]))
        o_ref[...] = _dot(hid.astype(BF16), w2_ref[...]).astype(BF16)


def _nsa_compress(rk, rv, pe_k, pe_v, w1_k, w1_v, w2_k, w2_v):
    b, nr, wide = rk.shape
    half = CMP_LEN * HEAD_DIM // 2
    assert wide == half
    rspec = pl.BlockSpec((None, nr, wide), lambda i: (i, 0, 0))
    pspec = pl.BlockSpec((2, half), lambda i: (0, 0))
    w1spec = pl.BlockSpec((2, half, HEAD_DIM), lambda i: (0, 0, 0))
    w2spec = pl.BlockSpec((HEAD_DIM, HEAD_DIM), lambda i: (0, 0))
    ospec = pl.BlockSpec((None, nr, HEAD_DIM), lambda i: (i, 0, 0))
    return pl.pallas_call(
        _nsa_cmp_body,
        grid=(b,),
        in_specs=[rspec, rspec, pspec, pspec, w1spec, w1spec, w2spec, w2spec],
        out_specs=[ospec, ospec],
        out_shape=[jax.ShapeDtypeStruct((b, nr, HEAD_DIM), BF16)] * 2,
        compiler_params=_cparams(("parallel",)),
        name="nsa_compress",
    )(rk, rv, pe_k.reshape(2, half), pe_v.reshape(2, half), w1_k.reshape(2, half, HEAD_DIM),
      w1_v.reshape(2, half, HEAD_DIM), w2_k, w2_v)


def _bias_from_dist(dist, th_ref, rb_ref):
    n = jnp.maximum(dist, 0)
    outs = [jnp.full(dist.shape, rb_ref[0, h] * LOG2E, F32) for h in range(N_HEADS)]
    for bkt in range(1, N_BUCKETS):
        ge = n >= th_ref[bkt]
        outs = [jnp.where(ge, rb_ref[bkt, h] * LOG2E, o) for h, o in enumerate(outs)]
    return outs


def _nsa_bias_body(th_ref, rb_ref, sb_ref, wb_ref, cb_ref, *, tk, n_far, ncp):
    qb = Q_BLOCK
    ii = lax.broadcasted_iota(jnp.int32, (qb, tk), 0)
    jj = lax.broadcasted_iota(jnp.int32, (qb, tk), 1)
    for e in range(n_far + 1):
        tiles = _bias_from_dist(e * qb + ii - jj, th_ref, rb_ref)
        for h in range(N_HEADS):
            sb_ref[e, h * qb:(h + 1) * qb, :] = tiles[h]
    nwin = WINDOW + qb
    dw = lax.broadcasted_iota(jnp.int32, (qb, nwin), 0) - lax.broadcasted_iota(jnp.int32, (qb, nwin), 1) + WINDOW
    tiles = _bias_from_dist(dw, th_ref, rb_ref)
    okw = (dw >= 0) & (dw < WINDOW)
    for h in range(N_HEADS):
        wb_ref[h * qb:(h + 1) * qb, :] = jnp.where(okw, tiles[h], NEG)
    mm = lax.broadcasted_iota(jnp.int32, (2 * ncp, qb), 0) - ncp
    dc = lax.broadcasted_iota(jnp.int32, (2 * ncp, qb), 1) - (mm * CMP_STRIDE + CMP_LEN - 1)
    tiles = _bias_from_dist(dc, th_ref, rb_ref)
    for h in range(N_HEADS):
        cb_ref[:, h * qb:(h + 1) * qb] = jnp.where(dc >= 0, tiles[h], NEG)


def _nsa_far_tiles(tk):
    last = MAX_EXACT * (MAX_DIST / MAX_EXACT) ** ((N_BUCKETS - MAX_EXACT - 1) / (N_BUCKETS - MAX_EXACT))
    return -(-(int(math.ceil(last)) + 1 + tk - 1) // Q_BLOCK)


def _nsa_bias(thresholds, rel_bias, t, *, tk):
    n_far = _nsa_far_tiles(tk)
    ncp = t // CMP_STRIDE
    hq = N_HEADS * Q_BLOCK
    smem = pl.BlockSpec(memory_space=pltpu.SMEM)
    return pl.pallas_call(
        functools.partial(_nsa_bias_body, tk=tk, n_far=n_far, ncp=ncp),
        in_specs=[smem, smem],
        out_shape=[jax.ShapeDtypeStruct((n_far + 1, hq, tk), F32), jax.ShapeDtypeStruct((hq, WINDOW + Q_BLOCK), F32),
                   jax.ShapeDtypeStruct((2 * ncp, hq), F32)],
        compiler_params=pltpu.CompilerParams(vmem_limit_bytes=VMEM_LIMIT),
        name="nsa_bias",
    )(thresholds, rel_bias)


def _nsa_body(q_ref, sig_ref, kc_ref, vc_ref, ks_ref, vs_ref, kw_ref, vw_ref, ovt_ref, ex_ref, sb_ref, wb_ref,
              cb_ref, o_ref, *, tk, n_far, n_slc, sc2):
    qi = pl.program_id(1)
    qb = Q_BLOCK
    hq = N_HEADS * qb
    s0 = qi * qb
    nwin = WINDOW + qb
    r = tk // qb
    q4 = jnp.concatenate([q_ref[:, h * HEAD_DIM:(h + 1) * HEAD_DIM] for h in range(N_HEADS)], axis=0)

    heads = range(N_HEADS)
    qh = [q_ref[:, h * HEAD_DIM:(h + 1) * HEAD_DIM] for h in heads]

    ncp = kc_ref.shape[0]
    c0 = pl.multiple_of(ncp - (qb // CMP_STRIDE) * qi, 8)
    w0 = pl.multiple_of(s0, qb)
    kwin = kw_ref[pl.ds(w0, nwin), :]
    vwin = vw_ref[pl.ds(w0, nwin), :]
    s_c = _dot_nt(kc_ref[...], q4) * sc2 + cb_ref[pl.ds(c0, ncp), :]
    before_start = s0 - WINDOW + lax.broadcasted_iota(jnp.int32, (qb, nwin), 1) < 0
    s_w = [jnp.where(before_start, NEG, _dot_nt(qh[h], kwin) * sc2 + wb_ref[h * qb:(h + 1) * qb, :]) for h in heads]
    m_c = jnp.maximum(jnp.max(s_c, axis=0, keepdims=True), 0.5 * NEG)
    e_c = jnp.exp2(s_c - m_c)
    e_w = [jnp.exp2(s_w[h] - jnp.max(s_w[h], axis=1, keepdims=True)) for h in heads]
    p_c = (e_c * (1.0 / jnp.maximum(jnp.sum(e_c, axis=0, keepdims=True), 1e-30))).astype(BF16)
    r_w = [1.0 / jnp.sum(e_w[h], axis=1, keepdims=True) for h in heads]
    o_c = _dot_tn(p_c, vc_ref[...])
    imp = _dot(ovt_ref[...], p_c[:, 0:qb])
    for h in range(1, N_HEADS):
        imp = imp + _dot(ovt_ref[...], p_c[:, h * qb:(h + 1) * qb])
    o_w = [_dot(e_w[h].astype(BF16), vwin) * r_w[h] for h in heads]

    jsl = lax.broadcasted_iota(jnp.int32, (n_slc, qb), 0)
    cur = (s0 + lax.broadcasted_iota(jnp.int32, (n_slc, qb), 1)) // SLC_LEN
    forced = (jsl == 0) | (jsl == cur) | (jsl == cur - 1)
    imp = jnp.where(forced, FORCED_SCORE, imp)
    imp = jnp.where(jsl > cur, -1.0, imp)
    groups = range(n_slc // 8)
    impg = [imp[8 * v:8 * v + 8, :] for v in groups]
    jloc = lax.broadcasted_iota(jnp.int32, (8, qb), 0)
    rank = [jnp.zeros((8, qb), jnp.int32) for _ in groups]
    for j2 in range(n_slc):
        row = impg[j2 // 8][j2 % 8:j2 % 8 + 1, :]
        for v in groups:
            if 8 * v > j2:
                beats = row >= impg[v]
            elif 8 * v + 7 < j2:
                beats = row > impg[v]
            else:
                beats = (row > impg[v]) | ((row >= impg[v]) & (jloc > j2 % 8))
            rank[v] = rank[v] + jnp.where(beats, 1, 0)
    selneg = jnp.where(jnp.concatenate(rank, axis=0) < min(N_SELECT, n_slc), 0.0, NEG).astype(BF16)

    it = lax.broadcasted_iota(jnp.int32, (qb, tk), 0)
    jt = lax.broadcasted_iota(jnp.int32, (qb, tk), 1)

    def sel_step(kt, carry, diag):
        m, l, acc = carry
        k0 = pl.multiple_of(kt * tk, tk)
        k = ks_ref[pl.ds(k0, tk), :]
        v = vs_ref[pl.ds(k0, tk), :]
        add = _dot_tn(selneg, ex_ref[:, pl.ds(k0, tk)])
        if diag:
            add = jnp.where(s0 + it >= k0 + jt, add, NEG)
        eb = [jnp.clip(qi - (kt * r + j), 0, n_far) for j in range(r)]
        bias = lambda h: jnp.concatenate([sb_ref[eb[j], h * qb:(h + 1) * qb, :] for j in range(r)], axis=1)
        s = [_dot_nt(qh[h], k) * sc2 + bias(h) + add for h in heads]
        wide = lambda x: jnp.concatenate([x] * r, axis=1)
        m_new = tuple(jnp.maximum(m[h], jnp.max(s[h], axis=1, keepdims=True)) for h in heads)
        alpha = [jnp.exp2(m[h] - m_new[h]) for h in heads]
        p = [jnp.exp2(s[h] - wide(m_new[h])) for h in heads]
        l = tuple(alpha[h] * l[h] + jnp.sum(p[h], axis=1, keepdims=True) for h in heads)
        pv = [_dot(p[h].astype(BF16), v) for h in heads]
        acc = tuple(alpha[h] * acc[h] + pv[h] for h in heads)
        return m_new, l, acc

    init = (tuple(jnp.full((qb, LANE), 0.5 * NEG, F32) for _ in heads),
            tuple(jnp.zeros((qb, LANE), F32) for _ in heads), tuple(jnp.zeros((qb, HEAD_DIM), F32) for _ in heads))
    kd = qi // r
    carry = lax.fori_loop(0, kd, functools.partial(sel_step, diag=False), init)
    _, l_s, acc_s = sel_step(kd, carry, True)
    o_s = [acc_s[h] * (1.0 / jnp.maximum(l_s[h], 1e-30)) for h in heads]

    sig = sig_ref[...]
    for h in heads:
        g0 = sig[:, _S_NSA_G + 3 * h:_S_NSA_G + 3 * h + 1]
        g1 = sig[:, _S_NSA_G + 3 * h + 1:_S_NSA_G + 3 * h + 2]
        g2 = sig[:, _S_NSA_G + 3 * h + 2:_S_NSA_G + 3 * h + 3]
        o_ref[:, h * HEAD_DIM:(h + 1) * HEAD_DIM] = g0 * o_c[h * qb:(h + 1) * qb] + g1 * o_s[h] + g2 * o_w[h]


def _nsa(zbf, sig, kc, vc, kw_pad, vw_pad, overlap_t, expand, sb, wb, cb, b, t, *, tk):
    qb = Q_BLOCK
    nq = t // qb
    n_slc = t // SLC_LEN
    ncp = kc.shape[1]
    hq = N_HEADS * qb
    n_far = sb.shape[0] - 1
    per_b = lambda rows: pl.BlockSpec((None, rows, HEAD_DIM), lambda i, j: (i, 0, 0))
    whole = lambda a: pl.BlockSpec(a.shape, lambda i, j: (0,) * a.ndim)
    return pl.pallas_call(
        functools.partial(_nsa_body, tk=tk, n_far=n_far, n_slc=n_slc, sc2=HEAD_DIM ** -0.5 * LOG2E),
        grid=(b, nq),
        in_specs=[
            pl.BlockSpec((qb, GROUP_W), lambda i, j: (i * nq + j, _COL["nsa_q"] // 4)),
            pl.BlockSpec((qb, LANE), lambda i, j: (i * nq + j, 0)),
            per_b(ncp), per_b(ncp),
            pl.BlockSpec((t, HEAD_DIM), lambda i, j: (i, _COL["nsa_ks"])),
            pl.BlockSpec((t, HEAD_DIM), lambda i, j: (i, _COL["nsa_vs"])),
            per_b(t + WINDOW), per_b(t + WINDOW),
            whole(overlap_t), whole(expand), whole(sb), whole(wb), whole(cb),
        ],
        out_specs=pl.BlockSpec((qb, GROUP_W), lambda i, j: (i * nq + j, 0)),
        out_shape=jax.ShapeDtypeStruct((b * t, GROUP_W), F32),
        compiler_params=_cparams(("parallel", "arbitrary")),
        name="nsa_attn",
    )(zbf, sig, kc, vc, zbf, zbf, kw_pad, vw_pad, overlap_t, expand, sb, wb, cb)


def _outproj_body(h_ref, oa_ref, ob_ref, oc_ref, od_ref, g_ref, w_ref, o_ref):
    y = jnp.concatenate([
        _rms(oa_ref[...], g_ref[0:1, :]), ob_ref[...], _rms(oc_ref[...], g_ref[1:2, :]),
        _rms(od_ref[...], g_ref[2:3, :])], axis=-1)
    o_ref[...] = h_ref[...] + _dot(y.astype(BF16), w_ref[...])


def _outproj(h, oa, ob, oc, od, g, w, l, *, tm=512):
    nt, d = h.shape
    tm = min(tm, nt)
    grp = pl.BlockSpec((tm, GROUP_W), lambda i: (i, 0))
    return pl.pallas_call(
        _outproj_body,
        grid=(nt // tm,),
        in_specs=[pl.BlockSpec((tm, d), lambda i: (i, 0)), grp, grp, grp, grp,
                  pl.BlockSpec((3, GROUP_W), lambda i: (0, 0)),
                  pl.BlockSpec((None, 4 * GROUP_W, d), lambda i: (l, 0, 0))],
        out_specs=pl.BlockSpec((tm, d), lambda i: (i, 0)),
        out_shape=jax.ShapeDtypeStruct((nt, d), F32),
        compiler_params=_cparams(("parallel",)),
        name="outproj",
    )(h, oa, ob, oc, od, g, w)


_IN_WIDTHS = (("fox_q", GROUP_W), ("fox_k", GROUP_W), ("fox_v", GROUP_W), ("fox_f", N_HEADS),
              ("gdn_q", GROUP_W), ("gdn_k", GROUP_W), ("gdn_v", GROUP_W), ("gdn_a", N_HEADS), ("gdn_b", N_HEADS),
              ("gdn_z", GROUP_W), ("lru_x", GROUP_W), ("lru_gate", GROUP_W), ("nsa_q", GROUP_W),
              ("nsa_kc", HEAD_DIM), ("nsa_vc", HEAD_DIM), ("nsa_ks", HEAD_DIM), ("nsa_vs", HEAD_DIM),
              ("nsa_kw", HEAD_DIM), ("nsa_vw", HEAD_DIM), ("nsa_g", 3 * N_HEADS))
D_IN = sum(w for _, w in _IN_WIDTHS)


def _pad_cols(w):
    src, off = {}, 0
    for name, width in _IN_WIDTHS:
        src[name] = w[..., off:off + width]
        off += width
    zeros = lambda n: jnp.zeros(w.shape[:-1] + (n,), w.dtype)
    small = [src["fox_f"], src["gdn_a"], src["gdn_b"], src["nsa_g"]]
    src["small"] = jnp.concatenate(small + [zeros(LANE - sum(s.shape[-1] for s in small))], axis=-1)
    w16 = jnp.concatenate([src[name] for name in _COLS16], axis=-1)
    w32 = jnp.concatenate([src[name] for name in _COLS32] + [zeros(W32 - (_COL["small"] + 1) * LANE)], axis=-1)
    assert w16.shape[-1] == W16 and w32.shape[-1] == W32
    return w16, w32


def _block_diag(w):
    l, nb, bw, _ = w.shape
    eye = jnp.eye(nb, dtype=w.dtype)
    return jnp.einsum("lnde,nm->lndme", w, eye).reshape(l, nb * bw, nb * bw)


def _bucket_thresholds(t):
    n = jnp.arange(max(t, 2 * MAX_DIST), dtype=jnp.int32)
    nf = jnp.maximum(n, 1).astype(F32)
    large = MAX_EXACT + (jnp.log(nf / MAX_EXACT) / math.log(MAX_DIST / MAX_EXACT)
                         * (N_BUCKETS - MAX_EXACT)).astype(jnp.int32)
    large = jnp.minimum(large, N_BUCKETS - 1)
    bucket = jnp.where(n < MAX_EXACT, n, large)
    return jnp.sum(bucket[None, :] < jnp.arange(N_BUCKETS, dtype=jnp.int32)[:, None], axis=1).astype(jnp.int32)


def _nsa_tile(t):
    return min(512, t)


def _mixer(h, l, p, b, t):
    zbf = _inproj(h, p["mix_norm_g"][l], p["w16"], p["b16"], l, BF16)
    z32 = _inproj(h, p["mix_norm_g"][l], p["w32"], p["b32"], l, F32)

    ccol = _fox_gate(z32, b, t)
    crow = jnp.pad(ccol[:, ::LANE].reshape(b, t, N_HEADS).transpose(0, 2, 1), ((0, 0), (0, 8 - N_HEADS), (0, 0)))
    o_a = _fox(zbf, ccol, crow, b, t)

    qkv, gdn_g, sig = _gdn_prep(z32, p["gdn_conv_w"][l], p["gdn_a_log"][l], p["gdn_dt_bias"][l], b, t)
    c = GDN_CHUNK
    grow = gdn_g[:, _S_GDN_A:_S_GDN_A + N_HEADS].reshape(b * t // c, c, N_HEADS).transpose(0, 2, 1)
    grow = jnp.pad(grow, ((0, 0), (0, 8 - N_HEADS), (0, 0)))
    u, w, qg, kd, qk, gc = _gdn_intra(qkv, gdn_g, grow, sig, b, t)
    o_b = _gdn_scan(u, w, qg, kd, qk, gc, z32, p["gdn_norm_g"][l], b, t)

    o_c = _lru(z32, p["lru_conv_w"][l], p["lru_conv_b"][l], p["lru_wa_bd"][l], p["lru_b_a"][l],
               p["lru_wx_bd"][l], p["lru_b_x"][l], p["lru_lambda"][l], b, t)

    col = lambda name: z32[:, _COL[name] * LANE:(_COL[name] + 1) * LANE]
    cmp_rows = lambda a: a.reshape(b, t // CMP_STRIDE, CMP_STRIDE * HEAD_DIM)
    kc, vc = _nsa_compress(cmp_rows(col("nsa_kc")), cmp_rows(col("nsa_vc")), p["nsa_pe_k"][l], p["nsa_pe_v"][l],
                           p["nsa_w1_k16"][l], p["nsa_w1_v16"][l], p["nsa_w2_k16"][l], p["nsa_w2_v16"][l])
    colbf = lambda name: zbf[:, _COL[name] * LANE:(_COL[name] + 1) * LANE].reshape(b, t, HEAD_DIM)
    kw_pad = jnp.pad(colbf("nsa_kw"), ((0, 0), (WINDOW, 0), (0, 0)))
    vw_pad = jnp.pad(colbf("nsa_vw"), ((0, 0), (WINDOW, 0), (0, 0)))
    o_d = _nsa(zbf, sig, kc, vc, kw_pad, vw_pad, p["overlap_t"], p["expand"], p["sb"], p["wb"], p["cb"], b, t,
               tk=_nsa_tile(t))

    return _outproj(h, o_a, o_b, o_c, o_d, p["out_norm_g"][l], p["w_out16"], l)


def _mixer_consts(rel_bias, t):
    n_cmp_rows = t // CMP_STRIDE
    n_slc = t // SLC_LEN
    cn = jnp.arange(n_cmp_rows)[None, :] * CMP_STRIDE
    sj = jnp.arange(n_slc)[:, None] * SLC_LEN
    sb, wb, cb = _nsa_bias(_bucket_thresholds(t), rel_bias, t, tk=Q_BLOCK)
    return dict(
        overlap_t=((cn <= sj + SLC_LEN - 1) & (cn + CMP_LEN - 1 >= sj)).astype(BF16),
        expand=(jnp.arange(t)[None, :] // SLC_LEN == jnp.arange(n_slc)[:, None]).astype(BF16),
        sb=sb, wb=wb, cb=cb)


def kernel(x, ffn1_norm_g, ffn1_w_gate, ffn1_w_up, ffn1_w_down, mix_norm_g, w_in, b_in, gdn_conv_w, gdn_a_log, gdn_dt_bias, gdn_norm_g, lru_conv_w, lru_conv_b, lru_w_a, lru_b_a, lru_w_x, lru_b_x, lru_lambda, nsa_pe_k, nsa_w1_k, nsa_w2_k, nsa_pe_v, nsa_w1_v, nsa_w2_v, rel_bias, out_norm_g, w_out, ffn2_norm_g, ffn2_w_gate, ffn2_w_up, ffn2_w_down, final_norm_g):
    b, t, d = x.shape
    depth = w_in.shape[0]
    assert w_in.shape[-1] == D_IN
    w16, w32 = _pad_cols(w_in.astype(BF16))
    b16, b32 = _pad_cols(b_in[:, None, :])
    p = dict(
        mix_norm_g=mix_norm_g, w16=w16, w32=w32, b16=b16, b32=b32,
        gdn_conv_w=gdn_conv_w, gdn_a_log=gdn_a_log, gdn_dt_bias=gdn_dt_bias, gdn_norm_g=gdn_norm_g,
        lru_conv_w=lru_conv_w, lru_conv_b=lru_conv_b, lru_wa_bd=_block_diag(lru_w_a).astype(BF16), lru_b_a=lru_b_a,
        lru_wx_bd=_block_diag(lru_w_x).astype(BF16), lru_b_x=lru_b_x, lru_lambda=lru_lambda,
        nsa_pe_k=nsa_pe_k, nsa_pe_v=nsa_pe_v, nsa_w1_k16=nsa_w1_k.astype(BF16), nsa_w1_v16=nsa_w1_v.astype(BF16),
        nsa_w2_k16=nsa_w2_k.astype(BF16), nsa_w2_v16=nsa_w2_v.astype(BF16),
        out_norm_g=out_norm_g, w_out16=w_out.astype(BF16),
        **_mixer_consts(rel_bias, t),
    )
    f1 = (ffn1_w_gate.astype(BF16), ffn1_w_up.astype(BF16), ffn1_w_down.astype(BF16))
    f2 = (ffn2_w_gate.astype(BF16), ffn2_w_up.astype(BF16), ffn2_w_down.astype(BF16))
    h = x.reshape(b * t, d)
    for l in range(depth):
        h = _ffn(h, ffn1_norm_g[l], *f1, l)
        h = _mixer(h, l, p, b, t)
        h = _ffn(h, ffn2_norm_g[l], *f2, l, final_g=final_norm_g if l == depth - 1 else None)
    return h.reshape(b, t, d)
```

```python
import functools
import math

import jax
import jax.numpy as jnp
from jax import lax
from jax.experimental import pallas as pl
from jax.experimental.pallas import tpu as pltpu

F32 = jnp.float32
BF16 = jnp.bfloat16

EPS = 1e-6
LANE = 128
HEAD_DIM = 128
N_HEADS = 4
GROUP_W = N_HEADS * HEAD_DIM
GDN_CHUNK = 64
CONV_W = 4
LRU_BLOCKS = 8
LRU_C = 8.0
CMP_LEN = 32
CMP_STRIDE = 16
SLC_LEN = 64
N_SELECT = 16
WINDOW = 512
FORCED_SCORE = 1.0e6
N_BUCKETS = 32
MAX_EXACT = 16
MAX_DIST = 1024
Q_BLOCK = 128
NEG = -1.0e30
LOG2E = math.log2(math.e)
VMEM_LIMIT = 56 * 1024 * 1024

_COLS16 = ("fox_q", "fox_k", "fox_v", "nsa_q", "nsa_ks", "nsa_vs", "nsa_kw", "nsa_vw")
_COLS32 = ("gdn_q", "gdn_k", "gdn_v", "gdn_z", "lru_x", "lru_gate", "nsa_kc", "nsa_vc", "small")
_COL = dict(fox_q=0, fox_k=4, fox_v=8, nsa_q=12, nsa_ks=16, nsa_vs=17, nsa_kw=18, nsa_vw=19,
            gdn_q=0, gdn_k=4, gdn_v=8, gdn_z=12, lru_x=16, lru_gate=20, nsa_kc=24, nsa_vc=25, small=26)
W16 = 20 * LANE
W32 = 28 * LANE
_S_FOX_F, _S_GDN_A, _S_GDN_B, _S_NSA_G = 0, 4, 8, 12


def _cparams(sem):
    return pltpu.CompilerParams(dimension_semantics=sem, vmem_limit_bytes=VMEM_LIMIT)


def _rms(x, g):
    return x * lax.rsqrt(jnp.mean(x * x, axis=-1, keepdims=True) + EPS) * g


def _softplus(x):
    return jnp.maximum(x, 0.0) + jnp.log1p(jnp.exp(-jnp.abs(x)))


def _gelu_tanh(x):
    return 0.5 * x * (1.0 + jnp.tanh(math.sqrt(2.0 / math.pi) * (x + 0.044715 * (x * x * x))))


def _dot(a, b):
    return jnp.dot(a, b, preferred_element_type=F32)


def _dot_nt(a, b):
    return lax.dot_general(a, b, (((1,), (1,)), ((), ())), preferred_element_type=F32)


def _dot_tn(a, b):
    return lax.dot_general(a, b, (((0,), (0,)), ((), ())), preferred_element_type=F32)


def _dot_hi(a, b):
    return jnp.dot(a, b, preferred_element_type=F32, precision=lax.Precision.HIGHEST)


def _split(x):
    hi = x.astype(BF16)
    return hi, (x - hi.astype(F32)).astype(BF16)


def _dot3(a, b):
    return _dot(a[0], b[0]) + (_dot(a[0], b[1]) + _dot(a[1], b[0]))


def _ffn_body(*refs, final):
    if final:
        x_ref, g_ref, wg_ref, wu_ref, wd_ref, fg_ref, o_ref, n_scr = refs
    else:
        x_ref, g_ref, wg_ref, wu_ref, wd_ref, o_ref, n_scr = refs
    j = pl.program_id(1)

    @pl.when(j == 0)
    def _():
        n_scr[...] = _rms(x_ref[...], g_ref[...]).astype(BF16)
        o_ref[...] = jnp.zeros_like(o_ref)

    n = n_scr[...]
    gate = _dot(n, wg_ref[...])
    up = _dot(n, wu_ref[...])
    a = (gate * jax.nn.sigmoid(gate) * up).astype(BF16)
    o_ref[...] += _dot(a, wd_ref[...])

    @pl.when(j == pl.num_programs(1) - 1)
    def _():
        y = x_ref[...] + 0.5 * o_ref[...]
        if final:
            y = _rms(y, fg_ref[...])
        o_ref[...] = y


def _ffn(x, g, wg, wu, wd, l, final_g=None, *, tm=512, tf=512):
    nt, d = x.shape
    f = wg.shape[-1]
    tm = min(tm, nt)
    final = final_g is not None
    in_specs = [
        pl.BlockSpec((tm, d), lambda i, j: (i, 0)),
        pl.BlockSpec((1, d), lambda i, j: (0, 0)),
        pl.BlockSpec((None, d, tf), lambda i, j: (l, 0, j)),
        pl.BlockSpec((None, d, tf), lambda i, j: (l, 0, j)),
        pl.BlockSpec((None, tf, d), lambda i, j: (l, j, 0)),
    ]
    args = [x, g.reshape(1, d), wg, wu, wd]
    if final:
        in_specs.append(pl.BlockSpec((1, d), lambda i, j: (0, 0)))
        args.append(final_g.reshape(1, d))
    return pl.pallas_call(
        functools.partial(_ffn_body, final=final),
        grid=(nt // tm, f // tf),
        in_specs=in_specs,
        out_specs=pl.BlockSpec((tm, d), lambda i, j: (i, 0)),
        out_shape=jax.ShapeDtypeStruct((nt, d), F32),
        scratch_shapes=[pltpu.VMEM((tm, d), BF16)],
        compiler_params=_cparams(("parallel", "arbitrary")),
        name="ffn",
    )(*args)


def _inproj_body(x_ref, g_ref, w_ref, b_ref, o_ref, n_scr):
    @pl.when(pl.program_id(1) == 0)
    def _():
        n_scr[...] = _rms(x_ref[...], g_ref[...]).astype(BF16)

    o_ref[...] = (_dot(n_scr[...], w_ref[...]) + b_ref[...]).astype(o_ref.dtype)


def _inproj(x, g, w, b, l, out_dtype, *, tm=1024, tn=512):
    nt, d = x.shape
    n = w.shape[-1]
    tm = min(tm, nt)
    return pl.pallas_call(
        _inproj_body,
        grid=(nt // tm, n // tn),
        in_specs=[
            pl.BlockSpec((tm, d), lambda i, j: (i, 0)),
            pl.BlockSpec((1, d), lambda i, j: (0, 0)),
            pl.BlockSpec((None, d, tn), lambda i, j: (l, 0, j)),
            pl.BlockSpec((None, 1, tn), lambda i, j: (l, 0, j)),
        ],
        out_specs=pl.BlockSpec((tm, tn), lambda i, j: (i, j)),
        out_shape=jax.ShapeDtypeStruct((nt, n), out_dtype),
        scratch_shapes=[pltpu.VMEM((tm, d), BF16)],
        compiler_params=_cparams(("parallel", "arbitrary")),
        name="inproj",
    )(x, g.reshape(1, d), w, b)


def _cumsum_rows(y):
    rows = lax.broadcasted_iota(jnp.int32, y.shape, 0)
    s = 1
    while s < y.shape[0]:
        y = y + jnp.where(rows >= s, pltpu.roll(y, s, 0), 0.0)
        s *= 2
    return y


def _fox_gate_body(f_ref, c_ref):
    x = f_ref[...]
    log_f = jnp.minimum(x, 0.0) - jnp.log1p(jnp.exp(-jnp.abs(x)))
    c = _cumsum_rows(log_f) * LOG2E
    for h in range(N_HEADS):
        c_ref[:, h * LANE:(h + 1) * LANE] = jnp.broadcast_to(c[:, _S_FOX_F + h:_S_FOX_F + h + 1], (c.shape[0], LANE))


def _fox_gate(z32, b, t):
    return pl.pallas_call(
        _fox_gate_body,
        grid=(b,),
        in_specs=[pl.BlockSpec((t, LANE), lambda i: (i, _COL["small"]))],
        out_specs=pl.BlockSpec((t, N_HEADS * LANE), lambda i: (i, 0)),
        out_shape=jax.ShapeDtypeStruct((b * t, N_HEADS * LANE), F32),
        compiler_params=_cparams(("parallel",)),
        name="fox_gate",
    )(z32)


def _fox_body(q_ref, k_ref, v_ref, ccol_ref, crow_ref, o_ref, m_scr, l_scr, acc_scr, *, tq, tk, sc2):
    qi = pl.program_id(1)
    rows = qi * tq + lax.broadcasted_iota(jnp.int32, (tq, tk), 0)
    cols = lax.broadcasted_iota(jnp.int32, (tq, tk), 1)
    m_scr[...] = jnp.full_like(m_scr, NEG)
    l_scr[...] = jnp.zeros_like(l_scr)
    acc_scr[...] = jnp.zeros_like(acc_scr)

    def step(kt, masked):
        k0 = pl.multiple_of(kt * tk, tk)
        heads = range(N_HEADS)
        hsl = [slice(h * HEAD_DIM, (h + 1) * HEAD_DIM) for h in heads]
        wide = lambda x: jnp.concatenate([x] * (tk // LANE), axis=1)
        s = [_dot_nt(q_ref[:, hsl[h]], k_ref[pl.ds(k0, tk), hsl[h]]) * sc2 for h in heads]
        s = [s[h] + wide(ccol_ref[:, hsl[h]]) - crow_ref[h:h + 1, pl.ds(k0, tk)] for h in heads]
        if masked:
            causal = rows >= k0 + cols
            s = [jnp.where(causal, x, NEG) for x in s]
        m_old = [m_scr[h] for h in heads]
        m_new = [jnp.maximum(m_old[h], jnp.max(s[h], axis=1, keepdims=True)) for h in heads]
        alpha = [jnp.exp2(m_old[h] - m_new[h]) for h in heads]
        p = [jnp.exp2(s[h] - wide(m_new[h])) for h in heads]
        l_new = [alpha[h] * l_scr[h] + jnp.sum(p[h], axis=1, keepdims=True) for h in heads]
        pv = [_dot(p[h].astype(BF16), v_ref[pl.ds(k0, tk), hsl[h]]) for h in heads]
        acc_new = [alpha[h] * acc_scr[h] + pv[h] for h in heads]
        for h in heads:
            m_scr[h] = m_new[h]
            l_scr[h] = l_new[h]
            acc_scr[h] = acc_new[h]

    def body(kt, carry):
        step(kt, False)
        return carry

    kd = (qi * tq) // tk
    lax.fori_loop(0, kd, body, 0)
    step(kd, True)
    for h in range(N_HEADS):
        o_ref[:, h * HEAD_DIM:(h + 1) * HEAD_DIM] = acc_scr[h] / l_scr[h]


def _fox(zbf, ccol, crow, b, t, *, tq=256, tk=512):
    tq, tk = min(tq, t), min(tk, t)
    assert tk % tq == 0
    nq = t // tq
    return pl.pallas_call(
        functools.partial(_fox_body, tq=tq, tk=tk, sc2=HEAD_DIM ** -0.5 * LOG2E),
        grid=(b, nq),
        in_specs=[
            pl.BlockSpec((tq, GROUP_W), lambda i, j: (i * nq + j, _COL["fox_q"] // 4)),
            pl.BlockSpec((t, GROUP_W), lambda i, j: (i, _COL["fox_k"] // 4)),
            pl.BlockSpec((t, GROUP_W), lambda i, j: (i, _COL["fox_v"] // 4)),
            pl.BlockSpec((tq, N_HEADS * LANE), lambda i, j: (i * nq + j, 0)),
            pl.BlockSpec((None, 8, t), lambda i, j: (i, 0, 0)),
        ],
        out_specs=pl.BlockSpec((tq, GROUP_W), lambda i, j: (i * nq + j, 0)),
        out_shape=jax.ShapeDtypeStruct((b * t, GROUP_W), F32),
        scratch_shapes=[pltpu.VMEM((N_HEADS, tq, LANE), F32), pltpu.VMEM((N_HEADS, tq, LANE), F32),
                        pltpu.VMEM((N_HEADS, tq, HEAD_DIM), F32)],
        compiler_params=_cparams(("parallel", "arbitrary")),
        name="fox_attn",
    )(zbf, zbf, zbf, ccol, crow)


def _causal_conv(x, halo, w_ref):
    tt = x.shape[0]
    xx = jnp.concatenate([halo, x], axis=0)
    y = x * w_ref[CONV_W - 1:CONV_W, :]
    for d in range(1, CONV_W):
        y = y + pltpu.roll(xx, d, 0)[8:8 + tt] * w_ref[CONV_W - 1 - d:CONV_W - d, :]
    return y


def _gdn_prep_body(x_ref, halo_ref, s_ref, w_ref, alog_ref, dt_ref, qkv_ref, g_ref, sig_ref):
    first = pl.program_id(1) == 0
    x = x_ref[...]
    halo = jnp.where(first, 0.0, halo_ref[...])
    y = _causal_conv(x, halo, w_ref)
    y = y * jax.nn.sigmoid(y)
    for h in range(N_HEADS):
        for part, post in ((0, HEAD_DIM ** -0.5), (1, 1.0)):
            cs = slice(part * GROUP_W + h * HEAD_DIM, part * GROUP_W + (h + 1) * HEAD_DIM)
            u = y[:, cs]
            un = u * lax.rsqrt(jnp.sum(u * u, axis=-1, keepdims=True) + EPS)
            qkv_ref[:, cs] = un * post if part == 0 else un
    qkv_ref[:, 2 * GROUP_W:] = y[:, 2 * GROUP_W:]
    s = s_ref[...]
    g_ref[...] = -jnp.exp(alog_ref[...]) * _softplus(s + dt_ref[...])
    sig_ref[...] = jax.nn.sigmoid(s)


def _gdn_prep(z32, conv_w, a_log, dt_bias, b, t, *, tt=512):
    tt = min(tt, t)
    nt = t // tt
    w3 = 3 * GROUP_W
    pad = lambda v: jnp.zeros((1, LANE), F32).at[0, _S_GDN_A:_S_GDN_A + N_HEADS].set(v)
    cb = _COL["gdn_q"] // 12
    return pl.pallas_call(
        _gdn_prep_body,
        grid=(b, nt),
        in_specs=[
            pl.BlockSpec((tt, w3), lambda i, j: (i * nt + j, cb)),
            pl.BlockSpec((8, w3), lambda i, j: (jnp.maximum((i * nt + j) * (tt // 8) - 1, 0), cb)),
            pl.BlockSpec((tt, LANE), lambda i, j: (i * nt + j, _COL["small"])),
            pl.BlockSpec((CONV_W, w3), lambda i, j: (0, 0)),
            pl.BlockSpec((1, LANE), lambda i, j: (0, 0)),
            pl.BlockSpec((1, LANE), lambda i, j: (0, 0)),
        ],
        out_specs=[
            pl.BlockSpec((tt, w3), lambda i, j: (i * nt + j, 0)),
            pl.BlockSpec((tt, LANE), lambda i, j: (i * nt + j, 0)),
            pl.BlockSpec((tt, LANE), lambda i, j: (i * nt + j, 0)),
        ],
        out_shape=[jax.ShapeDtypeStruct((b * t, w3), F32), jax.ShapeDtypeStruct((b * t, LANE), F32),
                   jax.ShapeDtypeStruct((b * t, LANE), F32)],
        compiler_params=_cparams(("parallel", "arbitrary")),
        name="gdn_prep",
    )(z32, z32, z32, conv_w, pad(a_log), pad(dt_bias))


def _gdn_intra_body(qkv_ref, g_ref, grow_ref, sig_ref, u_ref, w_ref, qg_ref, kd_ref, qk_ref, gc_ref, *, c, nch):
    ri = lax.broadcasted_iota(jnp.int32, (c, c), 0)
    ci = lax.broadcasted_iota(jnp.int32, (c, c), 1)
    tril = ri >= ci
    strict = ri > ci
    eye = (ri == ci).astype(F32)
    lower = tril.astype(F32)
    upper = (ri <= ci).astype(F32)
    chains = [(ch, h) for ch in range(nch) for h in range(N_HEADS)]
    rsl = lambda ch: slice(ch * c, (ch + 1) * c)
    hsl = lambda h, part=0: slice(part * GROUP_W + h * HEAD_DIM, part * GROUP_W + (h + 1) * HEAD_DIM)
    gc_cols = [_dot_hi(lower, g_ref[rsl(ch), :]) for ch in range(nch)]
    gc_rows = [_dot_hi(grow_ref[ch], upper) for ch in range(nch)]
    for ch in range(nch):
        gc_ref[rsl(ch), :] = gc_cols[ch]
    q = [qkv_ref[rsl(ch), hsl(h, 0)] for ch, h in chains]
    k = [qkv_ref[rsl(ch), hsl(h, 1)] for ch, h in chains]
    v = [qkv_ref[rsl(ch), hsl(h, 2)] for ch, h in chains]
    beta = [sig_ref[rsl(ch), _S_GDN_B + h:_S_GDN_B + h + 1] for ch, h in chains]
    gc = [gc_cols[ch][:, _S_GDN_A + h:_S_GDN_A + h + 1] for ch, h in chains]
    gr = [gc_rows[ch][h:h + 1, :] for ch, h in chains]
    idx = range(len(chains))
    decay = [jnp.where(tril, jnp.exp(jnp.where(tril, gc[i] - gr[i], 0.0)), 0.0) for i in idx]
    kb = [k[i] * beta[i] for i in idx]
    k16 = [x.astype(BF16) for x in k]
    a = [jnp.where(strict, _dot_nt(kb[i].astype(BF16), k16[i]) * decay[i], 0.0) for i in idx]
    same = lambda s: (ri // s) == (ci // s)
    base = 8
    pw = [-jnp.where(same(base), x, 0.0) for x in a]
    tm = [eye + x for x in pw]
    n = 2
    while n < base:
        pws = [_split(x) for x in pw]
        pw = [_dot3(x, x) for x in pws]
        tm = [tm[i] + _dot3(_split(tm[i]), _split(pw[i])) for i in idx]
        n *= 2
    s = base
    while s < c:
        join = same(2 * s) & jnp.logical_not(same(s))
        tms = [_split(x) for x in tm]
        et = [_dot3(_split(jnp.where(join, a[i], 0.0)), tms[i]) for i in idx]
        tm = [tm[i] - _dot3(tms[i], _split(et[i])) for i in idx]
        s *= 2
    tm16 = [x.astype(BF16) for x in tm]
    u = [_dot(tm16[i], (v[i] * beta[i]).astype(BF16)) for i in idx]
    w = [_dot(tm16[i], (kb[i] * jnp.exp(gc[i])).astype(BF16)) for i in idx]
    qk = [_dot_nt(q[i].astype(BF16), k16[i]) * decay[i] for i in idx]
    for i, (ch, h) in enumerate(chains):
        rs, hs = rsl(ch), hsl(h)
        gl = gc[i][c - 1:c, :]
        u_ref[rs, hs] = u[i]
        w_ref[rs, hs] = w[i].astype(BF16)
        qk_ref[rs, h * c:(h + 1) * c] = qk[i].astype(BF16)
        qg_ref[rs, hs] = (q[i] * jnp.exp(gc[i])).astype(BF16)
        kd_ref[rs, hs] = (k[i] * jnp.exp(gl - gc[i])).astype(BF16)


def _gdn_intra(qkv, g, grow, sig, b, t, *, c=GDN_CHUNK, nch=2):
    nt = b * t
    rows = nch * c
    tok = lambda w: pl.BlockSpec((rows, w), lambda i: (i, 0))
    return pl.pallas_call(
        functools.partial(_gdn_intra_body, c=c, nch=nch),
        grid=(nt // rows,),
        in_specs=[tok(3 * GROUP_W), tok(LANE), pl.BlockSpec((nch, 8, c), lambda i: (i, 0, 0)), tok(LANE)],
        out_specs=[tok(GROUP_W), tok(GROUP_W), tok(GROUP_W), tok(GROUP_W), tok(N_HEADS * c), tok(LANE)],
        out_shape=[jax.ShapeDtypeStruct((nt, GROUP_W), F32), jax.ShapeDtypeStruct((nt, GROUP_W), BF16),
                   jax.ShapeDtypeStruct((nt, GROUP_W), BF16), jax.ShapeDtypeStruct((nt, GROUP_W), BF16),
                   jax.ShapeDtypeStruct((nt, N_HEADS * c), BF16), jax.ShapeDtypeStruct((nt, LANE), F32)],
        compiler_params=_cparams(("parallel",)),
        name="gdn_intra",
    )(qkv, g, grow, sig)


def _gdn_scan_body(u_ref, w_ref, qg_ref, kd_ref, qk_ref, gc_ref, z_ref, ng_ref, o_ref, s_scr, *, c, nch, nb):
    @pl.when(pl.program_id(0) == 0)
    def _():
        s_scr[...] = jnp.zeros_like(s_scr)

    chains = [(b, h) for b in range(nb) for h in range(N_HEADS)]
    idx = range(len(chains))
    hsl = lambda h: slice(h * HEAD_DIM, (h + 1) * HEAD_DIM)
    s = [s_scr[i] for i in idx]
    for ch in range(nch):
        rs = slice(ch * c, (ch + 1) * c)
        egl = [jnp.exp(gc_ref[b, (ch + 1) * c - 1:(ch + 1) * c, :]) for b in range(nb)]
        s16 = [x.astype(BF16) for x in s]
        ws = [_dot(w_ref[b, rs, hsl(h)], s16[i]) for i, (b, h) in enumerate(chains)]
        v_new = [(u_ref[b, rs, hsl(h)] - ws[i]).astype(BF16) for i, (b, h) in enumerate(chains)]
        s = [s[i] * egl[b][:, _S_GDN_A + h:_S_GDN_A + h + 1] + _dot_tn(kd_ref[b, rs, hsl(h)], v_new[i])
             for i, (b, h) in enumerate(chains)]
        o = [_dot(qg_ref[b, rs, hsl(h)], s16[i]) + _dot(qk_ref[b, rs, h * c:(h + 1) * c], v_new[i])
             for i, (b, h) in enumerate(chains)]
        for i, (b, h) in enumerate(chains):
            gz = z_ref[b, rs, hsl(h)]
            o_ref[b, rs, hsl(h)] = _rms(o[i], ng_ref[...]) * (gz * jax.nn.sigmoid(gz))
    for i in idx:
        s_scr[i] = s[i]


def _gdn_scan(u, w, qg, kd, qk, gc, z32, norm_g, b, t, *, c=GDN_CHUNK, nch=2):
    rows = nch * c
    v3 = lambda a: a.reshape(b, t, a.shape[-1])
    blk = lambda wd, cb=0: pl.BlockSpec((b, rows, wd), lambda i: (0, i, cb))
    out = pl.pallas_call(
        functools.partial(_gdn_scan_body, c=c, nch=nch, nb=b),
        grid=(t // rows,),
        in_specs=[blk(GROUP_W), blk(GROUP_W), blk(GROUP_W), blk(GROUP_W), blk(N_HEADS * c), blk(LANE),
                  blk(GROUP_W, _COL["gdn_z"] // 4), pl.BlockSpec((1, HEAD_DIM), lambda i: (0, 0))],
        out_specs=blk(GROUP_W),
        out_shape=jax.ShapeDtypeStruct((b, t, GROUP_W), F32),
        scratch_shapes=[pltpu.VMEM((b * N_HEADS, HEAD_DIM, HEAD_DIM), F32)],
        compiler_params=_cparams(("arbitrary",)),
        name="gdn_scan",
    )(v3(u), v3(w), v3(qg), v3(kd), v3(qk), v3(gc), v3(z32), norm_g.reshape(1, HEAD_DIM))
    return out.reshape(b * t, GROUP_W)


def _lru_body(x_ref, halo_ref, gate_ref, cw_ref, cb_ref, wa_ref, ba_ref, wx_ref, bx_ref, lam_ref, o_ref, h_scr):
    first = pl.program_id(1) == 0

    @pl.when(first)
    def _():
        h_scr[...] = jnp.zeros_like(h_scr)

    x = x_ref[...]
    tt = x.shape[0]
    halo = jnp.where(first, 0.0, halo_ref[...])
    xc = _causal_conv(x, halo, cw_ref) + cb_ref[...]
    xc16 = xc.astype(BF16)
    r = jax.nn.sigmoid(_dot(xc16, wa_ref[...]) + ba_ref[...])
    i = jax.nn.sigmoid(_dot(xc16, wx_ref[...]) + bx_ref[...])
    log_a = -LRU_C * r * _softplus(-lam_ref[...])
    a = jnp.exp(log_a)
    u = jnp.sqrt(-jnp.tanh(log_a) * (a * a + 1.0)) * (i * xc)
    rows = lax.broadcasted_iota(jnp.int32, a.shape, 0)
    s = 1
    while s < tt:
        keep = rows >= s
        u = a * jnp.where(keep, pltpu.roll(u, s, 0), 0.0) + u
        a = a * jnp.where(keep, pltpu.roll(a, s, 0), 1.0)
        s *= 2
    hcur = a * h_scr[0:1, :] + u
    h_scr[0:1, :] = hcur[tt - 1:tt, :]
    o_ref[...] = hcur * _gelu_tanh(gate_ref[...])


def _lru(z32, conv_w, conv_b, wa_bd, b_a, wx_bd, b_x, lam, b, t, *, tt=512):
    tt = min(tt, t)
    nt = t // tt
    w = GROUP_W
    row = lambda v: v.reshape(1, w)
    vec = pl.BlockSpec((1, w), lambda i, j: (0, 0))
    mat = pl.BlockSpec((w, w), lambda i, j: (0, 0))
    cx = _COL["lru_x"] // 4
    return pl.pallas_call(
        _lru_body,
        grid=(b, nt),
        in_specs=[
            pl.BlockSpec((tt, w), lambda i, j: (i * nt + j, cx)),
            pl.BlockSpec((8, w), lambda i, j: (jnp.maximum((i * nt + j) * (tt // 8) - 1, 0), cx)),
            pl.BlockSpec((tt, w), lambda i, j: (i * nt + j, _COL["lru_gate"] // 4)),
            pl.BlockSpec((CONV_W, w), lambda i, j: (0, 0)),
            vec, mat, vec, mat, vec, vec,
        ],
        out_specs=pl.BlockSpec((tt, w), lambda i, j: (i * nt + j, 0)),
        out_shape=jax.ShapeDtypeStruct((b * t, w), F32),
        scratch_shapes=[pltpu.VMEM((8, w), F32)],
        compiler_params=_cparams(("parallel", "arbitrary")),
        name="rg_lru",
    )(z32, z32, z32, conv_w, row(conv_b), wa_bd, row(b_a), wx_bd, row(b_x), row(lam))


def _nsa_cmp_body(rk_ref, rv_ref, pek_ref, pev_ref, w1k_ref, w1v_ref, w2k_ref, w2v_ref, kc_ref, vc_ref):
    for r_ref, pe_ref, w1_ref, w2_ref, o_ref in ((rk_ref, pek_ref, w1k_ref, w2k_ref, kc_ref),
                                                 (rv_ref, pev_ref, w1v_ref, w2v_ref, vc_ref)):
        r = r_ref[...]
        nr = r.shape[0]
        lo = (r + pe_ref[0:1, :]).astype(BF16)
        hi = (pltpu.roll(r, nr - 1, 0) + pe_ref[1:2, :]).astype(BF16)
        hid = _gelu_tanh(_dot(lo, w1_ref[0]) + _dot(hi, w1_ref[1]))
        o_ref[...] = _dot(hid.astype(BF16), w2_ref[...]).astype(BF16)


def _nsa_compress(rk, rv, pe_k, pe_v, w1_k, w1_v, w2_k, w2_v):
    b, nr, wide = rk.shape
    half = CMP_LEN * HEAD_DIM // 2
    assert wide == half
    rspec = pl.BlockSpec((None, nr, wide), lambda i: (i, 0, 0))
    pspec = pl.BlockSpec((2, half), lambda i: (0, 0))
    w1spec = pl.BlockSpec((2, half, HEAD_DIM), lambda i: (0, 0, 0))
    w2spec = pl.BlockSpec((HEAD_DIM, HEAD_DIM), lambda i: (0, 0))
    ospec = pl.BlockSpec((None, nr, HEAD_DIM), lambda i: (i, 0, 0))
    return pl.pallas_call(
        _nsa_cmp_body,
        grid=(b,),
        in_specs=[rspec, rspec, pspec, pspec, w1spec, w1spec, w2spec, w2spec],
        out_specs=[ospec, ospec],
        out_shape=[jax.ShapeDtypeStruct((b, nr, HEAD_DIM), BF16)] * 2,
        compiler_params=_cparams(("parallel",)),
        name="nsa_compress",
    )(rk, rv, pe_k.reshape(2, half), pe_v.reshape(2, half), w1_k.reshape(2, half, HEAD_DIM),
      w1_v.reshape(2, half, HEAD_DIM), w2_k, w2_v)


def _bias_from_dist(dist, th_ref, rb_ref):
    n = jnp.maximum(dist, 0)
    outs = [jnp.full(dist.shape, rb_ref[0, h] * LOG2E, F32) for h in range(N_HEADS)]
    for bkt in range(1, N_BUCKETS):
        ge = n >= th_ref[bkt]
        outs = [jnp.where(ge, rb_ref[bkt, h] * LOG2E, o) for h, o in enumerate(outs)]
    return outs


def _nsa_bias_body(th_ref, rb_ref, sb_ref, wb_ref, cb_ref, *, tk, n_far, ncp):
    qb = Q_BLOCK
    ii = lax.broadcasted_iota(jnp.int32, (qb, tk), 0)
    jj = lax.broadcasted_iota(jnp.int32, (qb, tk), 1)
    for e in range(n_far + 1):
        tiles = _bias_from_dist(e * qb + ii - jj, th_ref, rb_ref)
        for h in range(N_HEADS):
            sb_ref[e, h * qb:(h + 1) * qb, :] = tiles[h]
    nwin = WINDOW + qb
    dw = lax.broadcasted_iota(jnp.int32, (qb, nwin), 0) - lax.broadcasted_iota(jnp.int32, (qb, nwin), 1) + WINDOW
    tiles = _bias_from_dist(dw, th_ref, rb_ref)
    okw = (dw >= 0) & (dw < WINDOW)
    for h in range(N_HEADS):
        wb_ref[h * qb:(h + 1) * qb, :] = jnp.where(okw, tiles[h], NEG)
    mm = lax.broadcasted_iota(jnp.int32, (2 * ncp, qb), 0) - ncp
    dc = lax.broadcasted_iota(jnp.int32, (2 * ncp, qb), 1) - (mm * CMP_STRIDE + CMP_LEN - 1)
    tiles = _bias_from_dist(dc, th_ref, rb_ref)
    for h in range(N_HEADS):
        cb_ref[:, h * qb:(h + 1) * qb] = jnp.where(dc >= 0, tiles[h], NEG)


def _nsa_far_tiles(tk):
    last = MAX_EXACT * (MAX_DIST / MAX_EXACT) ** ((N_BUCKETS - MAX_EXACT - 1) / (N_BUCKETS - MAX_EXACT))
    return -(-(int(math.ceil(last)) + 1 + tk - 1) // Q_BLOCK)


def _nsa_bias(thresholds, rel_bias, t, *, tk):
    n_far = _nsa_far_tiles(tk)
    ncp = t // CMP_STRIDE
    hq = N_HEADS * Q_BLOCK
    smem = pl.BlockSpec(memory_space=pltpu.SMEM)
    return pl.pallas_call(
        functools.partial(_nsa_bias_body, tk=tk, n_far=n_far, ncp=ncp),
        in_specs=[smem, smem],
        out_shape=[jax.ShapeDtypeStruct((n_far + 1, hq, tk), F32), jax.ShapeDtypeStruct((hq, WINDOW + Q_BLOCK), F32),
                   jax.ShapeDtypeStruct((2 * ncp, hq), F32)],
        compiler_params=pltpu.CompilerParams(vmem_limit_bytes=VMEM_LIMIT),
        name="nsa_bias",
    )(thresholds, rel_bias)


def _nsa_body(q_ref, sig_ref, kc_ref, vc_ref, ks_ref, vs_ref, kw_ref, vw_ref, ovt_ref, ex_ref, sb_ref, wb_ref,
              cb_ref, o_ref, *, tk, n_far, n_slc, sc2):
    qi = pl.program_id(1)
    qb = Q_BLOCK
    hq = N_HEADS * qb
    s0 = qi * qb
    nwin = WINDOW + qb
    r = tk // qb
    q4 = jnp.concatenate([q_ref[:, h * HEAD_DIM:(h + 1) * HEAD_DIM] for h in range(N_HEADS)], axis=0)

    heads = range(N_HEADS)
    qh = [q_ref[:, h * HEAD_DIM:(h + 1) * HEAD_DIM] for h in heads]

    ncp = kc_ref.shape[0]
    c0 = pl.multiple_of(ncp - (qb // CMP_STRIDE) * qi, 8)
    w0 = pl.multiple_of(s0, qb)
    kwin = kw_ref[pl.ds(w0, nwin), :]
    vwin = vw_ref[pl.ds(w0, nwin), :]
    s_c = _dot_nt(kc_ref[...], q4) * sc2 + cb_ref[pl.ds(c0, ncp), :]
    before_start = s0 - WINDOW + lax.broadcasted_iota(jnp.int32, (qb, nwin), 1) < 0
    s_w = [jnp.where(before_start, NEG, _dot_nt(qh[h], kwin) * sc2 + wb_ref[h * qb:(h + 1) * qb, :]) for h in heads]
    m_c = jnp.maximum(jnp.max(s_c, axis=0, keepdims=True), 0.5 * NEG)
    e_c = jnp.exp2(s_c - m_c)
    e_w = [jnp.exp2(s_w[h] - jnp.max(s_w[h], axis=1, keepdims=True)) for h in heads]
    p_c = (e_c * (1.0 / jnp.maximum(jnp.sum(e_c, axis=0, keepdims=True), 1e-30))).astype(BF16)
    r_w = [1.0 / jnp.sum(e_w[h], axis=1, keepdims=True) for h in heads]
    o_c = _dot_tn(p_c, vc_ref[...])
    imp = _dot(ovt_ref[...], p_c[:, 0:qb])
    for h in range(1, N_HEADS):
        imp = imp + _dot(ovt_ref[...], p_c[:, h * qb:(h + 1) * qb])
    o_w = [_dot(e_w[h].astype(BF16), vwin) * r_w[h] for h in heads]

    jsl = lax.broadcasted_iota(jnp.int32, (n_slc, qb), 0)
    cur = (s0 + lax.broadcasted_iota(jnp.int32, (n_slc, qb), 1)) // SLC_LEN
    forced = (jsl == 0) | (jsl == cur) | (jsl == cur - 1)
    imp = jnp.where(forced, FORCED_SCORE, imp)
    imp = jnp.where(jsl > cur, -1.0, imp)
    groups = range(n_slc // 8)
    impg = [imp[8 * v:8 * v + 8, :] for v in groups]
    jloc = lax.broadcasted_iota(jnp.int32, (8, qb), 0)
    rank = [jnp.zeros((8, qb), jnp.int32) for _ in groups]
    for j2 in range(n_slc):
        row = impg[j2 // 8][j2 % 8:j2 % 8 + 1, :]
        for v in groups:
            if 8 * v > j2:
                beats = row >= impg[v]
            elif 8 * v + 7 < j2:
                beats = row > impg[v]
            else:
                beats = (row > impg[v]) | ((row >= impg[v]) & (jloc > j2 % 8))
            rank[v] = rank[v] + jnp.where(beats, 1, 0)
    selneg = jnp.where(jnp.concatenate(rank, axis=0) < min(N_SELECT, n_slc), 0.0, NEG).astype(BF16)

    it = lax.broadcasted_iota(jnp.int32, (qb, tk), 0)
    jt = lax.broadcasted_iota(jnp.int32, (qb, tk), 1)

    def sel_step(kt, carry, diag):
        m, l, acc = carry
        k0 = pl.multiple_of(kt * tk, tk)
        k = ks_ref[pl.ds(k0, tk), :]
        v = vs_ref[pl.ds(k0, tk), :]
        add = _dot_tn(selneg, ex_ref[:, pl.ds(k0, tk)])
        if diag:
            add = jnp.where(s0 + it >= k0 + jt, add, NEG)
        eb = [jnp.clip(qi - (kt * r + j), 0, n_far) for j in range(r)]
        bias = lambda h: jnp.concatenate([sb_ref[eb[j], h * qb:(h + 1) * qb, :] for j in range(r)], axis=1)
        s = [_dot_nt(qh[h], k) * sc2 + bias(h) + add for h in heads]
        wide = lambda x: jnp.concatenate([x] * r, axis=1)
        m_new = tuple(jnp.maximum(m[h], jnp.max(s[h], axis=1, keepdims=True)) for h in heads)
        alpha = [jnp.exp2(m[h] - m_new[h]) for h in heads]
        p = [jnp.exp2(s[h] - wide(m_new[h])) for h in heads]
        l = tuple(alpha[h] * l[h] + jnp.sum(p[h], axis=1, keepdims=True) for h in heads)
        pv = [_dot(p[h].astype(BF16), v) for h in heads]
        acc = tuple(alpha[h] * acc[h] + pv[h] for h in heads)
        return m_new, l, acc

    init = (tuple(jnp.full((qb, LANE), 0.5 * NEG, F32) for _ in heads),
            tuple(jnp.zeros((qb, LANE), F32) for _ in heads), tuple(jnp.zeros((qb, HEAD_DIM), F32) for _ in heads))
    kd = qi // r
    carry = lax.fori_loop(0, kd, functools.partial(sel_step, diag=False), init)
    _, l_s, acc_s = sel_step(kd, carry, True)
    o_s = [acc_s[h] * (1.0 / jnp.maximum(l_s[h], 1e-30)) for h in heads]

    sig = sig_ref[...]
    for h in heads:
        g0 = sig[:, _S_NSA_G + 3 * h:_S_NSA_G + 3 * h + 1]
        g1 = sig[:, _S_NSA_G + 3 * h + 1:_S_NSA_G + 3 * h + 2]
        g2 = sig[:, _S_NSA_G + 3 * h + 2:_S_NSA_G + 3 * h + 3]
        o_ref[:, h * HEAD_DIM:(h + 1) * HEAD_DIM] = g0 * o_c[h * qb:(h + 1) * qb] + g1 * o_s[h] + g2 * o_w[h]


def _nsa(zbf, sig, kc, vc, kw_pad, vw_pad, overlap_t, expand, sb, wb, cb, b, t, *, tk):
    qb = Q_BLOCK
    nq = t // qb
    n_slc = t // SLC_LEN
    ncp = kc.shape[1]
    hq = N_HEADS * qb
    n_far = sb.shape[0] - 1
    per_b = lambda rows: pl.BlockSpec((None, rows, HEAD_DIM), lambda i, j: (i, 0, 0))
    whole = lambda a: pl.BlockSpec(a.shape, lambda i, j: (0,) * a.ndim)
    return pl.pallas_call(
        functools.partial(_nsa_body, tk=tk, n_far=n_far, n_slc=n_slc, sc2=HEAD_DIM ** -0.5 * LOG2E),
        grid=(b, nq),
        in_specs=[
            pl.BlockSpec((qb, GROUP_W), lambda i, j: (i * nq + j, _COL["nsa_q"] // 4)),
            pl.BlockSpec((qb, LANE), lambda i, j: (i * nq + j, 0)),
            per_b(ncp), per_b(ncp),
            pl.BlockSpec((t, HEAD_DIM), lambda i, j: (i, _COL["nsa_ks"])),
            pl.BlockSpec((t, HEAD_DIM), lambda i, j: (i, _COL["nsa_vs"])),
            per_b(t + WINDOW), per_b(t + WINDOW),
            whole(overlap_t), whole(expand), whole(sb), whole(wb), whole(cb),
        ],
        out_specs=pl.BlockSpec((qb, GROUP_W), lambda i, j: (i * nq + j, 0)),
        out_shape=jax.ShapeDtypeStruct((b * t, GROUP_W), F32),
        compiler_params=_cparams(("parallel", "arbitrary")),
        name="nsa_attn",
    )(zbf, sig, kc, vc, zbf, zbf, kw_pad, vw_pad, overlap_t, expand, sb, wb, cb)


def _outproj_body(h_ref, oa_ref, ob_ref, oc_ref, od_ref, g_ref, w_ref, o_ref):
    y = jnp.concatenate([
        _rms(oa_ref[...], g_ref[0:1, :]), ob_ref[...], _rms(oc_ref[...], g_ref[1:2, :]),
        _rms(od_ref[...], g_ref[2:3, :])], axis=-1)
    o_ref[...] = h_ref[...] + _dot(y.astype(BF16), w_ref[...])


def _outproj(h, oa, ob, oc, od, g, w, l, *, tm=512):
    nt, d = h.shape
    tm = min(tm, nt)
    grp = pl.BlockSpec((tm, GROUP_W), lambda i: (i, 0))
    return pl.pallas_call(
        _outproj_body,
        grid=(nt // tm,),
        in_specs=[pl.BlockSpec((tm, d), lambda i: (i, 0)), grp, grp, grp, grp,
                  pl.BlockSpec((3, GROUP_W), lambda i: (0, 0)),
                  pl.BlockSpec((None, 4 * GROUP_W, d), lambda i: (l, 0, 0))],
        out_specs=pl.BlockSpec((tm, d), lambda i: (i, 0)),
        out_shape=jax.ShapeDtypeStruct((nt, d), F32),
        compiler_params=_cparams(("parallel",)),
        name="outproj",
    )(h, oa, ob, oc, od, g, w)


_IN_WIDTHS = (("fox_q", GROUP_W), ("fox_k", GROUP_W), ("fox_v", GROUP_W), ("fox_f", N_HEADS),
              ("gdn_q", GROUP_W), ("gdn_k", GROUP_W), ("gdn_v", GROUP_W), ("gdn_a", N_HEADS), ("gdn_b", N_HEADS),
              ("gdn_z", GROUP_W), ("lru_x", GROUP_W), ("lru_gate", GROUP_W), ("nsa_q", GROUP_W),
              ("nsa_kc", HEAD_DIM), ("nsa_vc", HEAD_DIM), ("nsa_ks", HEAD_DIM), ("nsa_vs", HEAD_DIM),
              ("nsa_kw", HEAD_DIM), ("nsa_vw", HEAD_DIM), ("nsa_g", 3 * N_HEADS))
D_IN = sum(w for _, w in _IN_WIDTHS)


def _in_pieces():
    small_lane = {"fox_f": _S_FOX_F, "gdn_a": _S_GDN_A, "gdn_b": _S_GDN_B, "nsa_g": _S_NSA_G}
    out, off = [], 0
    for name, width in _IN_WIDTHS:
        if name in small_lane:
            out.append((off, width, 1, _COL["small"] * LANE + small_lane[name]))
        else:
            out.append((off, width, 0 if name in _COLS16 else 1, _COL[name] * LANE))
        off += width
    return out


def _relayout_body(w_ref, o16_ref, o32_ref):
    outs = (o16_ref, o32_ref)
    tail = _COL["small"] * LANE
    o32_ref[:, tail:] = jnp.zeros((o32_ref.shape[0], W32 - tail), o32_ref.dtype)
    for src, width, which, dst in _in_pieces():
        outs[which][:, dst:dst + width] = w_ref[:, src:src + width].astype(outs[which].dtype)


def _relayout_w_in(w_in, *, tr=256):
    nl, d, d_in = w_in.shape
    return pl.pallas_call(
        _relayout_body,
        grid=(nl, d // tr),
        in_specs=[pl.BlockSpec((None, tr, d_in), lambda l, i: (l, i, 0))],
        out_specs=[pl.BlockSpec((None, tr, W16), lambda l, i: (l, i, 0)),
                   pl.BlockSpec((None, tr, W32), lambda l, i: (l, i, 0))],
        out_shape=[jax.ShapeDtypeStruct((nl, d, W16), BF16), jax.ShapeDtypeStruct((nl, d, W32), BF16)],
        compiler_params=_cparams(("parallel", "parallel")),
        name="w_in_relayout",
    )(w_in)


def _pad_cols(w):
    src, off = {}, 0
    for name, width in _IN_WIDTHS:
        src[name] = w[..., off:off + width]
        off += width
    zeros = lambda n: jnp.zeros(w.shape[:-1] + (n,), w.dtype)
    small = [src["fox_f"], src["gdn_a"], src["gdn_b"], src["nsa_g"]]
    src["small"] = jnp.concatenate(small + [zeros(LANE - sum(s.shape[-1] for s in small))], axis=-1)
    w16 = jnp.concatenate([src[name] for name in _COLS16], axis=-1)
    w32 = jnp.concatenate([src[name] for name in _COLS32] + [zeros(W32 - (_COL["small"] + 1) * LANE)], axis=-1)
    assert w16.shape[-1] == W16 and w32.shape[-1] == W32
    return w16, w32


def _block_diag(w):
    l, nb, bw, _ = w.shape
    eye = jnp.eye(nb, dtype=w.dtype)
    return jnp.einsum("lnde,nm->lndme", w, eye).reshape(l, nb * bw, nb * bw)


def _bucket_thresholds(t):
    n = jnp.arange(max(t, 2 * MAX_DIST), dtype=jnp.int32)
    nf = jnp.maximum(n, 1).astype(F32)
    large = MAX_EXACT + (jnp.log(nf / MAX_EXACT) / math.log(MAX_DIST / MAX_EXACT)
                         * (N_BUCKETS - MAX_EXACT)).astype(jnp.int32)
    large = jnp.minimum(large, N_BUCKETS - 1)
    bucket = jnp.where(n < MAX_EXACT, n, large)
    return jnp.sum(bucket[None, :] < jnp.arange(N_BUCKETS, dtype=jnp.int32)[:, None], axis=1).astype(jnp.int32)


def _nsa_tile(t):
    return min(512, t)


def _mixer(h, l, p, b, t):
    zbf = _inproj(h, p["mix_norm_g"][l], p["w16"], p["b16"], l, BF16, tn=W16 // 2)
    z32 = _inproj(h, p["mix_norm_g"][l], p["w32"], p["b32"], l, F32, tn=W32 // 4)

    ccol = _fox_gate(z32, b, t)
    crow = jnp.pad(ccol[:, ::LANE].reshape(b, t, N_HEADS).transpose(0, 2, 1), ((0, 0), (0, 8 - N_HEADS), (0, 0)))
    o_a = _fox(zbf, ccol, crow, b, t)

    qkv, gdn_g, sig = _gdn_prep(z32, p["gdn_conv_w"][l], p["gdn_a_log"][l], p["gdn_dt_bias"][l], b, t)
    c = GDN_CHUNK
    grow = gdn_g[:, _S_GDN_A:_S_GDN_A + N_HEADS].reshape(b * t // c, c, N_HEADS).transpose(0, 2, 1)
    grow = jnp.pad(grow, ((0, 0), (0, 8 - N_HEADS), (0, 0)))
    u, w, qg, kd, qk, gc = _gdn_intra(qkv, gdn_g, grow, sig, b, t)
    o_b = _gdn_scan(u, w, qg, kd, qk, gc, z32, p["gdn_norm_g"][l], b, t)

    o_c = _lru(z32, p["lru_conv_w"][l], p["lru_conv_b"][l], p["lru_wa_bd"][l], p["lru_b_a"][l],
               p["lru_wx_bd"][l], p["lru_b_x"][l], p["lru_lambda"][l], b, t)

    col = lambda name: z32[:, _COL[name] * LANE:(_COL[name] + 1) * LANE]
    cmp_rows = lambda a: a.reshape(b, t // CMP_STRIDE, CMP_STRIDE * HEAD_DIM)
    kc, vc = _nsa_compress(cmp_rows(col("nsa_kc")), cmp_rows(col("nsa_vc")), p["nsa_pe_k"][l], p["nsa_pe_v"][l],
                           p["nsa_w1_k16"][l], p["nsa_w1_v16"][l], p["nsa_w2_k16"][l], p["nsa_w2_v16"][l])
    colbf = lambda name: zbf[:, _COL[name] * LANE:(_COL[name] + 1) * LANE].reshape(b, t, HEAD_DIM)
    kw_pad = jnp.pad(colbf("nsa_kw"), ((0, 0), (WINDOW, 0), (0, 0)))
    vw_pad = jnp.pad(colbf("nsa_vw"), ((0, 0), (WINDOW, 0), (0, 0)))
    o_d = _nsa(zbf, sig, kc, vc, kw_pad, vw_pad, p["overlap_t"], p["expand"], p["sb"], p["wb"], p["cb"], b, t,
               tk=_nsa_tile(t))

    return _outproj(h, o_a, o_b, o_c, o_d, p["out_norm_g"][l], p["w_out16"], l)


def _mixer_consts(rel_bias, t):
    n_cmp_rows = t // CMP_STRIDE
    n_slc = t // SLC_LEN
    cn = jnp.arange(n_cmp_rows)[None, :] * CMP_STRIDE
    sj = jnp.arange(n_slc)[:, None] * SLC_LEN
    sb, wb, cb = _nsa_bias(_bucket_thresholds(t), rel_bias, t, tk=Q_BLOCK)
    return dict(
        overlap_t=((cn <= sj + SLC_LEN - 1) & (cn + CMP_LEN - 1 >= sj)).astype(BF16),
        expand=(jnp.arange(t)[None, :] // SLC_LEN == jnp.arange(n_slc)[:, None]).astype(BF16),
        sb=sb, wb=wb, cb=cb)


def kernel(x, ffn1_norm_g, ffn1_w_gate, ffn1_w_up, ffn1_w_down, mix_norm_g, w_in, b_in, gdn_conv_w, gdn_a_log, gdn_dt_bias, gdn_norm_g, lru_conv_w, lru_conv_b, lru_w_a, lru_b_a, lru_w_x, lru_b_x, lru_lambda, nsa_pe_k, nsa_w1_k, nsa_w2_k, nsa_pe_v, nsa_w1_v, nsa_w2_v, rel_bias, out_norm_g, w_out, ffn2_norm_g, ffn2_w_gate, ffn2_w_up, ffn2_w_down, final_norm_g):
    b, t, d = x.shape
    depth = w_in.shape[0]
    assert w_in.shape[-1] == D_IN
    w16, w32 = _relayout_w_in(w_in)
    b16, b32 = _pad_cols(b_in[:, None, :])
    p = dict(
        mix_norm_g=mix_norm_g, w16=w16, w32=w32, b16=b16, b32=b32,
        gdn_conv_w=gdn_conv_w, gdn_a_log=gdn_a_log, gdn_dt_bias=gdn_dt_bias, gdn_norm_g=gdn_norm_g,
        lru_conv_w=lru_conv_w, lru_conv_b=lru_conv_b, lru_wa_bd=_block_diag(lru_w_a).astype(BF16), lru_b_a=lru_b_a,
        lru_wx_bd=_block_diag(lru_w_x).astype(BF16), lru_b_x=lru_b_x, lru_lambda=lru_lambda,
        nsa_pe_k=nsa_pe_k, nsa_pe_v=nsa_pe_v, nsa_w1_k16=nsa_w1_k.astype(BF16), nsa_w1_v16=nsa_w1_v.astype(BF16),
        nsa_w2_k16=nsa_w2_k.astype(BF16), nsa_w2_v16=nsa_w2_v.astype(BF16),
        out_norm_g=out_norm_g, w_out16=w_out.astype(BF16),
        **_mixer_consts(rel_bias, t),
    )
    f1 = (ffn1_w_gate.astype(BF16), ffn1_w_up.astype(BF16), ffn1_w_down.astype(BF16))
    f2 = (ffn2_w_gate.astype(BF16), ffn2_w_up.astype(BF16), ffn2_w_down.astype(BF16))
    h = x.reshape(b * t, d)
    for l in range(depth):
        h = _ffn(h, ffn1_norm_g[l], *f1, l)
        h = _mixer(h, l, p, b, t)
        h = _ffn(h, ffn2_norm_g[l], *f2, l, final_g=final_norm_g if l == depth - 1 else None)
    return h.reshape(b, t, d)
```

```python
import functools
import math

import jax
import jax.numpy as jnp
from jax import lax
from jax.experimental import pallas as pl
from jax.experimental.pallas import tpu as pltpu

F32 = jnp.float32
BF16 = jnp.bfloat16

EPS = 1e-6
LANE = 128
HEAD_DIM = 128
N_HEADS = 4
GROUP_W = N_HEADS * HEAD_DIM
GDN_CHUNK = 64
CONV_W = 4
LRU_BLOCKS = 8
LRU_C = 8.0
CMP_LEN = 32
CMP_STRIDE = 16
SLC_LEN = 64
N_SELECT = 16
WINDOW = 512
FORCED_SCORE = 1.0e6
N_BUCKETS = 32
MAX_EXACT = 16
MAX_DIST = 1024
Q_BLOCK = 128
NEG = -1.0e30
LOG2E = math.log2(math.e)
VMEM_LIMIT = 56 * 1024 * 1024

_COLS16 = ("fox_q", "fox_k", "fox_v", "nsa_q", "nsa_ks", "nsa_vs", "nsa_kw", "nsa_vw")
_COLS32 = ("gdn_q", "gdn_k", "gdn_v", "gdn_z", "lru_x", "lru_gate", "nsa_kc", "nsa_vc", "small")
_COL = dict(fox_q=0, fox_k=4, fox_v=8, nsa_q=12, nsa_ks=16, nsa_vs=17, nsa_kw=18, nsa_vw=19,
            gdn_q=0, gdn_k=4, gdn_v=8, gdn_z=12, lru_x=16, lru_gate=20, nsa_kc=24, nsa_vc=25, small=26)
W16 = 20 * LANE
W32 = 28 * LANE
_S_FOX_F, _S_GDN_A, _S_GDN_B, _S_NSA_G = 0, 4, 8, 12


def _cparams(sem):
    return pltpu.CompilerParams(dimension_semantics=sem, vmem_limit_bytes=VMEM_LIMIT)


def _rms(x, g):
    return x * lax.rsqrt(jnp.mean(x * x, axis=-1, keepdims=True) + EPS) * g


def _softplus(x):
    return jnp.maximum(x, 0.0) + jnp.log1p(jnp.exp(-jnp.abs(x)))


def _gelu_tanh(x):
    return 0.5 * x * (1.0 + jnp.tanh(math.sqrt(2.0 / math.pi) * (x + 0.044715 * (x * x * x))))


def _dot(a, b):
    return jnp.dot(a, b, preferred_element_type=F32)


def _dot_nt(a, b):
    return lax.dot_general(a, b, (((1,), (1,)), ((), ())), preferred_element_type=F32)


def _dot_tn(a, b):
    return lax.dot_general(a, b, (((0,), (0,)), ((), ())), preferred_element_type=F32)


def _dot_hi(a, b):
    return jnp.dot(a, b, preferred_element_type=F32, precision=lax.Precision.HIGHEST)


def _split(x):
    hi = x.astype(BF16)
    return hi, (x - hi.astype(F32)).astype(BF16)


def _dot3(a, b):
    return _dot(a[0], b[0]) + (_dot(a[0], b[1]) + _dot(a[1], b[0]))


def _ffn_body(*refs, final):
    if final:
        x_ref, g_ref, wg_ref, wu_ref, wd_ref, fg_ref, o_ref, n_scr = refs
    else:
        x_ref, g_ref, wg_ref, wu_ref, wd_ref, o_ref, n_scr = refs
    j = pl.program_id(1)

    @pl.when(j == 0)
    def _():
        n_scr[...] = _rms(x_ref[...], g_ref[...]).astype(BF16)
        o_ref[...] = jnp.zeros_like(o_ref)

    n = n_scr[...]
    gate = _dot(n, wg_ref[...])
    up = _dot(n, wu_ref[...])
    a = (gate * jax.nn.sigmoid(gate) * up).astype(BF16)
    o_ref[...] += _dot(a, wd_ref[...])

    @pl.when(j == pl.num_programs(1) - 1)
    def _():
        y = x_ref[...] + 0.5 * o_ref[...]
        if final:
            y = _rms(y, fg_ref[...])
        o_ref[...] = y


def _ffn(x, g, wg, wu, wd, l, final_g=None, *, tm=512, tf=512):
    nt, d = x.shape
    f = wg.shape[-1]
    tm = min(tm, nt)
    final = final_g is not None
    in_specs = [
        pl.BlockSpec((tm, d), lambda i, j: (i, 0)),
        pl.BlockSpec((1, d), lambda i, j: (0, 0)),
        pl.BlockSpec((None, d, tf), lambda i, j: (l, 0, j)),
        pl.BlockSpec((None, d, tf), lambda i, j: (l, 0, j)),
        pl.BlockSpec((None, tf, d), lambda i, j: (l, j, 0)),
    ]
    args = [x, g.reshape(1, d), wg, wu, wd]
    if final:
        in_specs.append(pl.BlockSpec((1, d), lambda i, j: (0, 0)))
        args.append(final_g.reshape(1, d))
    return pl.pallas_call(
        functools.partial(_ffn_body, final=final),
        grid=(nt // tm, f // tf),
        in_specs=in_specs,
        out_specs=pl.BlockSpec((tm, d), lambda i, j: (i, 0)),
        out_shape=jax.ShapeDtypeStruct((nt, d), F32),
        scratch_shapes=[pltpu.VMEM((tm, d), BF16)],
        compiler_params=_cparams(("parallel", "arbitrary")),
        name="ffn",
    )(*args)


def _inproj_body(x_ref, g_ref, w_ref, b_ref, o_ref, n_scr):
    @pl.when(pl.program_id(1) == 0)
    def _():
        n_scr[...] = _rms(x_ref[...], g_ref[...]).astype(BF16)

    o_ref[...] = (_dot_nt(n_scr[...], w_ref[...]) + b_ref[...]).astype(o_ref.dtype)


def _inproj(x, g, w, b, l, out_dtype, *, tm=1024, tn=512):
    nt, d = x.shape
    n = w.shape[1]
    tm = min(tm, nt)
    return pl.pallas_call(
        _inproj_body,
        grid=(nt // tm, n // tn),
        in_specs=[
            pl.BlockSpec((tm, d), lambda i, j: (i, 0)),
            pl.BlockSpec((1, d), lambda i, j: (0, 0)),
            pl.BlockSpec((None, tn, d), lambda i, j: (l, j, 0)),
            pl.BlockSpec((None, 1, tn), lambda i, j: (l, 0, j)),
        ],
        out_specs=pl.BlockSpec((tm, tn), lambda i, j: (i, j)),
        out_shape=jax.ShapeDtypeStruct((nt, n), out_dtype),
        scratch_shapes=[pltpu.VMEM((tm, d), BF16)],
        compiler_params=_cparams(("parallel", "arbitrary")),
        name="inproj",
    )(x, g.reshape(1, d), w, b)


def _cumsum_rows(y):
    rows = lax.broadcasted_iota(jnp.int32, y.shape, 0)
    s = 1
    while s < y.shape[0]:
        y = y + jnp.where(rows >= s, pltpu.roll(y, s, 0), 0.0)
        s *= 2
    return y


def _fox_gate_body(f_ref, c_ref):
    x = f_ref[...]
    log_f = jnp.minimum(x, 0.0) - jnp.log1p(jnp.exp(-jnp.abs(x)))
    c = _cumsum_rows(log_f) * LOG2E
    for h in range(N_HEADS):
        c_ref[:, h * LANE:(h + 1) * LANE] = jnp.broadcast_to(c[:, _S_FOX_F + h:_S_FOX_F + h + 1], (c.shape[0], LANE))


def _fox_gate(z32, b, t):
    return pl.pallas_call(
        _fox_gate_body,
        grid=(b,),
        in_specs=[pl.BlockSpec((t, LANE), lambda i: (i, _COL["small"]))],
        out_specs=pl.BlockSpec((t, N_HEADS * LANE), lambda i: (i, 0)),
        out_shape=jax.ShapeDtypeStruct((b * t, N_HEADS * LANE), F32),
        compiler_params=_cparams(("parallel",)),
        name="fox_gate",
    )(z32)


def _fox_body(q_ref, k_ref, v_ref, ccol_ref, crow_ref, o_ref, m_scr, l_scr, acc_scr, *, tq, tk, sc2):
    qi = pl.program_id(1)
    rows = qi * tq + lax.broadcasted_iota(jnp.int32, (tq, tk), 0)
    cols = lax.broadcasted_iota(jnp.int32, (tq, tk), 1)
    m_scr[...] = jnp.full_like(m_scr, NEG)
    l_scr[...] = jnp.zeros_like(l_scr)
    acc_scr[...] = jnp.zeros_like(acc_scr)

    def step(kt, masked):
        k0 = pl.multiple_of(kt * tk, tk)
        heads = range(N_HEADS)
        hsl = [slice(h * HEAD_DIM, (h + 1) * HEAD_DIM) for h in heads]
        wide = lambda x: jnp.concatenate([x] * (tk // LANE), axis=1)
        s = [_dot_nt(q_ref[:, hsl[h]], k_ref[pl.ds(k0, tk), hsl[h]]) * sc2 for h in heads]
        s = [s[h] + wide(ccol_ref[:, hsl[h]]) - crow_ref[h:h + 1, pl.ds(k0, tk)] for h in heads]
        if masked:
            causal = rows >= k0 + cols
            s = [jnp.where(causal, x, NEG) for x in s]
        m_old = [m_scr[h] for h in heads]
        m_new = [jnp.maximum(m_old[h], jnp.max(s[h], axis=1, keepdims=True)) for h in heads]
        alpha = [jnp.exp2(m_old[h] - m_new[h]) for h in heads]
        p = [jnp.exp2(s[h] - wide(m_new[h])) for h in heads]
        l_new = [alpha[h] * l_scr[h] + jnp.sum(p[h], axis=1, keepdims=True) for h in heads]
        pv = [_dot(p[h].astype(BF16), v_ref[pl.ds(k0, tk), hsl[h]]) for h in heads]
        acc_new = [alpha[h] * acc_scr[h] + pv[h] for h in heads]
        for h in heads:
            m_scr[h] = m_new[h]
            l_scr[h] = l_new[h]
            acc_scr[h] = acc_new[h]

    def body(kt, carry):
        step(kt, False)
        return carry

    kd = (qi * tq) // tk
    lax.fori_loop(0, kd, body, 0)
    step(kd, True)
    for h in range(N_HEADS):
        o_ref[:, h * HEAD_DIM:(h + 1) * HEAD_DIM] = acc_scr[h] / l_scr[h]


def _fox(zbf, ccol, crow, b, t, *, tq=256, tk=1024):
    tq, tk = min(tq, t), min(tk, t)
    assert tk % tq == 0
    nq = t // tq
    return pl.pallas_call(
        functools.partial(_fox_body, tq=tq, tk=tk, sc2=HEAD_DIM ** -0.5 * LOG2E),
        grid=(b, nq),
        in_specs=[
            pl.BlockSpec((tq, GROUP_W), lambda i, j: (i * nq + j, _COL["fox_q"] // 4)),
            pl.BlockSpec((t, GROUP_W), lambda i, j: (i, _COL["fox_k"] // 4)),
            pl.BlockSpec((t, GROUP_W), lambda i, j: (i, _COL["fox_v"] // 4)),
            pl.BlockSpec((tq, N_HEADS * LANE), lambda i, j: (i * nq + j, 0)),
            pl.BlockSpec((None, 8, t), lambda i, j: (i, 0, 0)),
        ],
        out_specs=pl.BlockSpec((tq, GROUP_W), lambda i, j: (i * nq + j, 0)),
        out_shape=jax.ShapeDtypeStruct((b * t, GROUP_W), F32),
        scratch_shapes=[pltpu.VMEM((N_HEADS, tq, LANE), F32), pltpu.VMEM((N_HEADS, tq, LANE), F32),
                        pltpu.VMEM((N_HEADS, tq, HEAD_DIM), F32)],
        compiler_params=_cparams(("parallel", "arbitrary")),
        name="fox_attn",
    )(zbf, zbf, zbf, ccol, crow)


def _causal_conv(x, halo, w_ref):
    tt = x.shape[0]
    xx = jnp.concatenate([halo, x], axis=0)
    y = x * w_ref[CONV_W - 1:CONV_W, :]
    for d in range(1, CONV_W):
        y = y + pltpu.roll(xx, d, 0)[8:8 + tt] * w_ref[CONV_W - 1 - d:CONV_W - d, :]
    return y


def _gdn_prep_body(x_ref, halo_ref, s_ref, w_ref, alog_ref, dt_ref, qkv_ref, g_ref, sig_ref):
    first = pl.program_id(1) == 0
    x = x_ref[...]
    halo = jnp.where(first, 0.0, halo_ref[...])
    y = _causal_conv(x, halo, w_ref)
    y = y * jax.nn.sigmoid(y)
    for h in range(N_HEADS):
        for part, post in ((0, HEAD_DIM ** -0.5), (1, 1.0)):
            cs = slice(part * GROUP_W + h * HEAD_DIM, part * GROUP_W + (h + 1) * HEAD_DIM)
            u = y[:, cs]
            un = u * lax.rsqrt(jnp.sum(u * u, axis=-1, keepdims=True) + EPS)
            qkv_ref[:, cs] = un * post if part == 0 else un
    qkv_ref[:, 2 * GROUP_W:] = y[:, 2 * GROUP_W:]
    s = s_ref[...]
    g_ref[...] = -jnp.exp(alog_ref[...]) * _softplus(s + dt_ref[...])
    sig_ref[...] = jax.nn.sigmoid(s)


def _gdn_prep(z32, conv_w, a_log, dt_bias, b, t, *, tt=512):
    tt = min(tt, t)
    nt = t // tt
    w3 = 3 * GROUP_W
    pad = lambda v: jnp.zeros((1, LANE), F32).at[0, _S_GDN_A:_S_GDN_A + N_HEADS].set(v)
    cb = _COL["gdn_q"] // 12
    return pl.pallas_call(
        _gdn_prep_body,
        grid=(b, nt),
        in_specs=[
            pl.BlockSpec((tt, w3), lambda i, j: (i * nt + j, cb)),
            pl.BlockSpec((8, w3), lambda i, j: (jnp.maximum((i * nt + j) * (tt // 8) - 1, 0), cb)),
            pl.BlockSpec((tt, LANE), lambda i, j: (i * nt + j, _COL["small"])),
            pl.BlockSpec((CONV_W, w3), lambda i, j: (0, 0)),
            pl.BlockSpec((1, LANE), lambda i, j: (0, 0)),
            pl.BlockSpec((1, LANE), lambda i, j: (0, 0)),
        ],
        out_specs=[
            pl.BlockSpec((tt, w3), lambda i, j: (i * nt + j, 0)),
            pl.BlockSpec((tt, LANE), lambda i, j: (i * nt + j, 0)),
            pl.BlockSpec((tt, LANE), lambda i, j: (i * nt + j, 0)),
        ],
        out_shape=[jax.ShapeDtypeStruct((b * t, w3), F32), jax.ShapeDtypeStruct((b * t, LANE), F32),
                   jax.ShapeDtypeStruct((b * t, LANE), F32)],
        compiler_params=_cparams(("parallel", "arbitrary")),
        name="gdn_prep",
    )(z32, z32, z32, conv_w, pad(a_log), pad(dt_bias))


def _gdn_intra_body(qkv_ref, g_ref, grow_ref, sig_ref, u_ref, w_ref, qg_ref, kd_ref, qk_ref, gc_ref, *, c, nch):
    ri = lax.broadcasted_iota(jnp.int32, (c, c), 0)
    ci = lax.broadcasted_iota(jnp.int32, (c, c), 1)
    tril = ri >= ci
    strict = ri > ci
    eye = (ri == ci).astype(F32)
    lower = tril.astype(F32)
    upper = (ri <= ci).astype(F32)
    chains = [(ch, h) for ch in range(nch) for h in range(N_HEADS)]
    rsl = lambda ch: slice(ch * c, (ch + 1) * c)
    hsl = lambda h, part=0: slice(part * GROUP_W + h * HEAD_DIM, part * GROUP_W + (h + 1) * HEAD_DIM)
    gc_cols = [_dot_hi(lower, g_ref[rsl(ch), :]) for ch in range(nch)]
    gc_rows = [_dot_hi(grow_ref[ch], upper) for ch in range(nch)]
    for ch in range(nch):
        gc_ref[rsl(ch), :] = gc_cols[ch]
    q = [qkv_ref[rsl(ch), hsl(h, 0)] for ch, h in chains]
    k = [qkv_ref[rsl(ch), hsl(h, 1)] for ch, h in chains]
    v = [qkv_ref[rsl(ch), hsl(h, 2)] for ch, h in chains]
    beta = [sig_ref[rsl(ch), _S_GDN_B + h:_S_GDN_B + h + 1] for ch, h in chains]
    gc = [gc_cols[ch][:, _S_GDN_A + h:_S_GDN_A + h + 1] for ch, h in chains]
    gr = [gc_rows[ch][h:h + 1, :] for ch, h in chains]
    idx = range(len(chains))
    decay = [jnp.where(tril, jnp.exp(jnp.where(tril, gc[i] - gr[i], 0.0)), 0.0) for i in idx]
    kb = [k[i] * beta[i] for i in idx]
    k16 = [x.astype(BF16) for x in k]
    a = [jnp.where(strict, _dot_nt(kb[i].astype(BF16), k16[i]) * decay[i], 0.0) for i in idx]
    same = lambda s: (ri // s) == (ci // s)
    base = 8
    pw = [-jnp.where(same(base), x, 0.0) for x in a]
    tm = [eye + x for x in pw]
    n = 2
    while n < base:
        pws = [_split(x) for x in pw]
        pw = [_dot3(x, x) for x in pws]
        tm = [tm[i] + _dot3(_split(tm[i]), _split(pw[i])) for i in idx]
        n *= 2
    s = base
    while s < c:
        join = same(2 * s) & jnp.logical_not(same(s))
        tms = [_split(x) for x in tm]
        et = [_dot3(_split(jnp.where(join, a[i], 0.0)), tms[i]) for i in idx]
        tm = [tm[i] - _dot3(tms[i], _split(et[i])) for i in idx]
        s *= 2
    tm16 = [x.astype(BF16) for x in tm]
    u = [_dot(tm16[i], (v[i] * beta[i]).astype(BF16)) for i in idx]
    w = [_dot(tm16[i], (kb[i] * jnp.exp(gc[i])).astype(BF16)) for i in idx]
    qk = [_dot_nt(q[i].astype(BF16), k16[i]) * decay[i] for i in idx]
    for i, (ch, h) in enumerate(chains):
        rs, hs = rsl(ch), hsl(h)
        gl = gc[i][c - 1:c, :]
        u_ref[rs, hs] = u[i]
        w_ref[rs, hs] = w[i].astype(BF16)
        qk_ref[rs, h * c:(h + 1) * c] = qk[i].astype(BF16)
        qg_ref[rs, hs] = (q[i] * jnp.exp(gc[i])).astype(BF16)
        kd_ref[rs, hs] = (k[i] * jnp.exp(gl - gc[i])).astype(BF16)


def _gdn_intra(qkv, g, grow, sig, b, t, *, c=GDN_CHUNK, nch=2):
    nt = b * t
    rows = nch * c
    tok = lambda w: pl.BlockSpec((rows, w), lambda i: (i, 0))
    return pl.pallas_call(
        functools.partial(_gdn_intra_body, c=c, nch=nch),
        grid=(nt // rows,),
        in_specs=[tok(3 * GROUP_W), tok(LANE), pl.BlockSpec((nch, 8, c), lambda i: (i, 0, 0)), tok(LANE)],
        out_specs=[tok(GROUP_W), tok(GROUP_W), tok(GROUP_W), tok(GROUP_W), tok(N_HEADS * c), tok(LANE)],
        out_shape=[jax.ShapeDtypeStruct((nt, GROUP_W), F32), jax.ShapeDtypeStruct((nt, GROUP_W), BF16),
                   jax.ShapeDtypeStruct((nt, GROUP_W), BF16), jax.ShapeDtypeStruct((nt, GROUP_W), BF16),
                   jax.ShapeDtypeStruct((nt, N_HEADS * c), BF16), jax.ShapeDtypeStruct((nt, LANE), F32)],
        compiler_params=_cparams(("parallel",)),
        name="gdn_intra",
    )(qkv, g, grow, sig)


def _gdn_scan_body(u_ref, w_ref, qg_ref, kd_ref, qk_ref, gc_ref, z_ref, ng_ref, o_ref, s_scr, *, c, nch, nb):
    @pl.when(pl.program_id(0) == 0)
    def _():
        s_scr[...] = jnp.zeros_like(s_scr)

    chains = [(b, h) for b in range(nb) for h in range(N_HEADS)]
    idx = range(len(chains))
    hsl = lambda h: slice(h * HEAD_DIM, (h + 1) * HEAD_DIM)
    s = [s_scr[i] for i in idx]
    for ch in range(nch):
        rs = slice(ch * c, (ch + 1) * c)
        egl = [jnp.exp(gc_ref[b, (ch + 1) * c - 1:(ch + 1) * c, :]) for b in range(nb)]
        s16 = [x.astype(BF16) for x in s]
        ws = [_dot(w_ref[b, rs, hsl(h)], s16[i]) for i, (b, h) in enumerate(chains)]
        v_new = [(u_ref[b, rs, hsl(h)] - ws[i]).astype(BF16) for i, (b, h) in enumerate(chains)]
        s = [s[i] * egl[b][:, _S_GDN_A + h:_S_GDN_A + h + 1] + _dot_tn(kd_ref[b, rs, hsl(h)], v_new[i])
             for i, (b, h) in enumerate(chains)]
        o = [_dot(qg_ref[b, rs, hsl(h)], s16[i]) + _dot(qk_ref[b, rs, h * c:(h + 1) * c], v_new[i])
             for i, (b, h) in enumerate(chains)]
        for i, (b, h) in enumerate(chains):
            gz = z_ref[b, rs, hsl(h)]
            o_ref[b, rs, hsl(h)] = _rms(o[i], ng_ref[...]) * (gz * jax.nn.sigmoid(gz))
    for i in idx:
        s_scr[i] = s[i]


def _gdn_scan(u, w, qg, kd, qk, gc, z32, norm_g, b, t, *, c=GDN_CHUNK, nch=2):
    rows = nch * c
    v3 = lambda a: a.reshape(b, t, a.shape[-1])
    blk = lambda wd, cb=0: pl.BlockSpec((b, rows, wd), lambda i: (0, i, cb))
    out = pl.pallas_call(
        functools.partial(_gdn_scan_body, c=c, nch=nch, nb=b),
        grid=(t // rows,),
        in_specs=[blk(GROUP_W), blk(GROUP_W), blk(GROUP_W), blk(GROUP_W), blk(N_HEADS * c), blk(LANE),
                  blk(GROUP_W, _COL["gdn_z"] // 4), pl.BlockSpec((1, HEAD_DIM), lambda i: (0, 0))],
        out_specs=blk(GROUP_W),
        out_shape=jax.ShapeDtypeStruct((b, t, GROUP_W), F32),
        scratch_shapes=[pltpu.VMEM((b * N_HEADS, HEAD_DIM, HEAD_DIM), F32)],
        compiler_params=_cparams(("arbitrary",)),
        name="gdn_scan",
    )(v3(u), v3(w), v3(qg), v3(kd), v3(qk), v3(gc), v3(z32), norm_g.reshape(1, HEAD_DIM))
    return out.reshape(b * t, GROUP_W)


def _lru_body(x_ref, halo_ref, gate_ref, cw_ref, cb_ref, wa_ref, ba_ref, wx_ref, bx_ref, lam_ref, o_ref, h_scr):
    first = pl.program_id(1) == 0

    @pl.when(first)
    def _():
        h_scr[...] = jnp.zeros_like(h_scr)

    x = x_ref[...]
    tt = x.shape[0]
    halo = jnp.where(first, 0.0, halo_ref[...])
    xc = _causal_conv(x, halo, cw_ref) + cb_ref[...]
    xc16 = xc.astype(BF16)
    r = jax.nn.sigmoid(_dot(xc16, wa_ref[...]) + ba_ref[...])
    i = jax.nn.sigmoid(_dot(xc16, wx_ref[...]) + bx_ref[...])
    log_a = -LRU_C * r * _softplus(-lam_ref[...])
    a = jnp.exp(log_a)
    u = jnp.sqrt(-jnp.tanh(log_a) * (a * a + 1.0)) * (i * xc)
    rows = lax.broadcasted_iota(jnp.int32, a.shape, 0) % 8
    s = 1
    while s < 8:
        keep = rows >= s
        u = a * jnp.where(keep, pltpu.roll(u, s, 0), 0.0) + u
        a = a * jnp.where(keep, pltpu.roll(a, s, 0), 1.0)
        s *= 2
    gate = _gelu_tanh(gate_ref[...])
    h = h_scr[0:1, :]
    for g in range(tt // 8):
        rs = slice(8 * g, 8 * g + 8)
        hg = a[rs, :] * h + u[rs, :]
        o_ref[rs, :] = hg * gate[rs, :]
        h = hg[7:8, :]
    h_scr[0:1, :] = h


def _lru(z32, conv_w, conv_b, wa_bd, b_a, wx_bd, b_x, lam, b, t, *, tt=512):
    tt = min(tt, t)
    nt = t // tt
    w = GROUP_W
    row = lambda v: v.reshape(1, w)
    vec = pl.BlockSpec((1, w), lambda i, j: (0, 0))
    mat = pl.BlockSpec((w, w), lambda i, j: (0, 0))
    cx = _COL["lru_x"] // 4
    return pl.pallas_call(
        _lru_body,
        grid=(b, nt),
        in_specs=[
            pl.BlockSpec((tt, w), lambda i, j: (i * nt + j, cx)),
            pl.BlockSpec((8, w), lambda i, j: (jnp.maximum((i * nt + j) * (tt // 8) - 1, 0), cx)),
            pl.BlockSpec((tt, w), lambda i, j: (i * nt + j, _COL["lru_gate"] // 4)),
            pl.BlockSpec((CONV_W, w), lambda i, j: (0, 0)),
            vec, mat, vec, mat, vec, vec,
        ],
        out_specs=pl.BlockSpec((tt, w), lambda i, j: (i * nt + j, 0)),
        out_shape=jax.ShapeDtypeStruct((b * t, w), F32),
        scratch_shapes=[pltpu.VMEM((8, w), F32)],
        compiler_params=_cparams(("parallel", "arbitrary")),
        name="rg_lru",
    )(z32, z32, z32, conv_w, row(conv_b), wa_bd, row(b_a), wx_bd, row(b_x), row(lam))


def _nsa_cmp_body(rk_ref, rv_ref, pek_ref, pev_ref, w1k_ref, w1v_ref, w2k_ref, w2v_ref, kc_ref, vc_ref):
    for r_ref, pe_ref, w1_ref, w2_ref, o_ref in ((rk_ref, pek_ref, w1k_ref, w2k_ref, kc_ref),
                                                 (rv_ref, pev_ref, w1v_ref, w2v_ref, vc_ref)):
        r = r_ref[...]
        nr = r.shape[0]
        lo = (r + pe_ref[0:1, :]).astype(BF16)
        hi = (pltpu.roll(r, nr - 1, 0) + pe_ref[1:2, :]).astype(BF16)
        hid = _gelu_tanh(_dot(lo, w1_ref[0]) + _dot(hi, w1_ref[1]))
        o_ref[...] = _dot(hid.astype(BF16), w2_ref[...]).astype(BF16)


def _nsa_compress(rk, rv, pe_k, pe_v, w1_k, w1_v, w2_k, w2_v):
    b, nr, wide = rk.shape
    half = CMP_LEN * HEAD_DIM // 2
    assert wide == half
    rspec = pl.BlockSpec((None, nr, wide), lambda i: (i, 0, 0))
    pspec = pl.BlockSpec((2, half), lambda i: (0, 0))
    w1spec = pl.BlockSpec((2, half, HEAD_DIM), lambda i: (0, 0, 0))
    w2spec = pl.BlockSpec((HEAD_DIM, HEAD_DIM), lambda i: (0, 0))
    ospec = pl.BlockSpec((None, nr, HEAD_DIM), lambda i: (i, 0, 0))
    return pl.pallas_call(
        _nsa_cmp_body,
        grid=(b,),
        in_specs=[rspec, rspec, pspec, pspec, w1spec, w1spec, w2spec, w2spec],
        out_specs=[ospec, ospec],
        out_shape=[jax.ShapeDtypeStruct((b, nr, HEAD_DIM), BF16)] * 2,
        compiler_params=_cparams(("parallel",)),
        name="nsa_compress",
    )(rk, rv, pe_k.reshape(2, half), pe_v.reshape(2, half), w1_k.reshape(2, half, HEAD_DIM),
      w1_v.reshape(2, half, HEAD_DIM), w2_k, w2_v)


def _bias_from_dist(dist, th_ref, rb_ref):
    n = jnp.maximum(dist, 0)
    outs = [jnp.full(dist.shape, rb_ref[0, h] * LOG2E, F32) for h in range(N_HEADS)]
    for bkt in range(1, N_BUCKETS):
        ge = n >= th_ref[bkt]
        outs = [jnp.where(ge, rb_ref[bkt, h] * LOG2E, o) for h, o in enumerate(outs)]
    return outs


def _nsa_bias_body(th_ref, rb_ref, sb_ref, wb_ref, cb_ref, *, tk, n_far, ncp):
    qb = Q_BLOCK
    ii = lax.broadcasted_iota(jnp.int32, (qb, tk), 0)
    jj = lax.broadcasted_iota(jnp.int32, (qb, tk), 1)
    for e in range(n_far + 1):
        tiles = _bias_from_dist(e * qb + ii - jj, th_ref, rb_ref)
        for h in range(N_HEADS):
            sb_ref[e, h * qb:(h + 1) * qb, :] = tiles[h]
    nwin = WINDOW + qb
    dw = lax.broadcasted_iota(jnp.int32, (qb, nwin), 0) - lax.broadcasted_iota(jnp.int32, (qb, nwin), 1) + WINDOW
    tiles = _bias_from_dist(dw, th_ref, rb_ref)
    okw = (dw >= 0) & (dw < WINDOW)
    for h in range(N_HEADS):
        wb_ref[h * qb:(h + 1) * qb, :] = jnp.where(okw, tiles[h], NEG)
    mm = lax.broadcasted_iota(jnp.int32, (2 * ncp, qb), 0) - ncp
    dc = lax.broadcasted_iota(jnp.int32, (2 * ncp, qb), 1) - (mm * CMP_STRIDE + CMP_LEN - 1)
    tiles = _bias_from_dist(dc, th_ref, rb_ref)
    for h in range(N_HEADS):
        cb_ref[:, h * qb:(h + 1) * qb] = jnp.where(dc >= 0, tiles[h], NEG)


def _nsa_far_tiles(tk):
    last = MAX_EXACT * (MAX_DIST / MAX_EXACT) ** ((N_BUCKETS - MAX_EXACT - 1) / (N_BUCKETS - MAX_EXACT))
    return -(-(int(math.ceil(last)) + 1 + tk - 1) // Q_BLOCK)


def _nsa_bias(thresholds, rel_bias, t, *, tk):
    n_far = _nsa_far_tiles(tk)
    ncp = t // CMP_STRIDE
    hq = N_HEADS * Q_BLOCK
    smem = pl.BlockSpec(memory_space=pltpu.SMEM)
    return pl.pallas_call(
        functools.partial(_nsa_bias_body, tk=tk, n_far=n_far, ncp=ncp),
        in_specs=[smem, smem],
        out_shape=[jax.ShapeDtypeStruct((n_far + 1, hq, tk), F32), jax.ShapeDtypeStruct((hq, WINDOW + Q_BLOCK), F32),
                   jax.ShapeDtypeStruct((2 * ncp, hq), F32)],
        compiler_params=pltpu.CompilerParams(vmem_limit_bytes=VMEM_LIMIT),
        name="nsa_bias",
    )(thresholds, rel_bias)


def _nsa_body(q_ref, sig_ref, kc_ref, vc_ref, ks_ref, vs_ref, kw_ref, vw_ref, ovt_ref, ex_ref, sb_ref, wb_ref,
              cb_ref, o_ref, *, tk, n_far, n_slc, sc2):
    qi = pl.program_id(1)
    qb = Q_BLOCK
    hq = N_HEADS * qb
    s0 = qi * qb
    nwin = WINDOW + qb
    r = tk // qb
    q4 = jnp.concatenate([q_ref[:, h * HEAD_DIM:(h + 1) * HEAD_DIM] for h in range(N_HEADS)], axis=0)

    heads = range(N_HEADS)
    qh = [q_ref[:, h * HEAD_DIM:(h + 1) * HEAD_DIM] for h in heads]

    ncp = kc_ref.shape[0]
    c0 = pl.multiple_of(ncp - (qb // CMP_STRIDE) * qi, 8)
    w0 = pl.multiple_of(s0, qb)
    kwin = kw_ref[pl.ds(w0, nwin), :]
    vwin = vw_ref[pl.ds(w0, nwin), :]
    s_c = _dot_nt(kc_ref[...], q4) * sc2 + cb_ref[pl.ds(c0, ncp), :]
    before_start = s0 - WINDOW + lax.broadcasted_iota(jnp.int32, (qb, nwin), 1) < 0
    s_w = [jnp.where(before_start, NEG, _dot_nt(qh[h], kwin) * sc2 + wb_ref[h * qb:(h + 1) * qb, :]) for h in heads]
    m_c = jnp.maximum(jnp.max(s_c, axis=0, keepdims=True), 0.5 * NEG)
    e_c = jnp.exp2(s_c - m_c)
    e_w = [jnp.exp2(s_w[h] - jnp.max(s_w[h], axis=1, keepdims=True)) for h in heads]
    p_c = (e_c * (1.0 / jnp.maximum(jnp.sum(e_c, axis=0, keepdims=True), 1e-30))).astype(BF16)
    r_w = [1.0 / jnp.sum(e_w[h], axis=1, keepdims=True) for h in heads]
    o_c = _dot_tn(p_c, vc_ref[...])
    imp = _dot(ovt_ref[...], p_c[:, 0:qb])
    for h in range(1, N_HEADS):
        imp = imp + _dot(ovt_ref[...], p_c[:, h * qb:(h + 1) * qb])
    o_w = [_dot(e_w[h].astype(BF16), vwin) * r_w[h] for h in heads]

    jsl = lax.broadcasted_iota(jnp.int32, (n_slc, qb), 0)
    cur = (s0 + lax.broadcasted_iota(jnp.int32, (n_slc, qb), 1)) // SLC_LEN
    forced = (jsl == 0) | (jsl == cur) | (jsl == cur - 1)
    imp = jnp.where(forced, FORCED_SCORE, imp)
    imp = jnp.where(jsl > cur, -1.0, imp)
    groups = range(n_slc // 8)
    impg = [imp[8 * v:8 * v + 8, :] for v in groups]
    jloc = lax.broadcasted_iota(jnp.int32, (8, qb), 0)
    rank = [jnp.zeros((8, qb), jnp.int32) for _ in groups]
    for j2 in range(n_slc):
        row = impg[j2 // 8][j2 % 8:j2 % 8 + 1, :]
        for v in groups:
            if 8 * v > j2:
                beats = row >= impg[v]
            elif 8 * v + 7 < j2:
                beats = row > impg[v]
            else:
                beats = (row > impg[v]) | ((row >= impg[v]) & (jloc > j2 % 8))
            rank[v] = rank[v] + jnp.where(beats, 1, 0)
    selneg = jnp.where(jnp.concatenate(rank, axis=0) < min(N_SELECT, n_slc), 0.0, NEG).astype(BF16)

    it = lax.broadcasted_iota(jnp.int32, (qb, tk), 0)
    jt = lax.broadcasted_iota(jnp.int32, (qb, tk), 1)

    def sel_step(kt, carry, diag):
        m, l, acc = carry
        k0 = pl.multiple_of(kt * tk, tk)
        k = ks_ref[pl.ds(k0, tk), :]
        v = vs_ref[pl.ds(k0, tk), :]
        add = _dot_tn(selneg, ex_ref[:, pl.ds(k0, tk)])
        if diag:
            add = jnp.where(s0 + it >= k0 + jt, add, NEG)
        eb = [jnp.clip(qi - (kt * r + j), 0, n_far) for j in range(r)]
        bias = lambda h: jnp.concatenate([sb_ref[eb[j], h * qb:(h + 1) * qb, :] for j in range(r)], axis=1)
        s = [_dot_nt(qh[h], k) * sc2 + bias(h) + add for h in heads]
        wide = lambda x: jnp.concatenate([x] * r, axis=1)
        m_new = tuple(jnp.maximum(m[h], jnp.max(s[h], axis=1, keepdims=True)) for h in heads)
        alpha = [jnp.exp2(m[h] - m_new[h]) for h in heads]
        p = [jnp.exp2(s[h] - wide(m_new[h])) for h in heads]
        l = tuple(alpha[h] * l[h] + jnp.sum(p[h], axis=1, keepdims=True) for h in heads)
        pv = [_dot(p[h].astype(BF16), v) for h in heads]
        acc = tuple(alpha[h] * acc[h] + pv[h] for h in heads)
        return m_new, l, acc

    init = (tuple(jnp.full((qb, LANE), 0.5 * NEG, F32) for _ in heads),
            tuple(jnp.zeros((qb, LANE), F32) for _ in heads), tuple(jnp.zeros((qb, HEAD_DIM), F32) for _ in heads))
    kd = qi // r
    carry = lax.fori_loop(0, kd, functools.partial(sel_step, diag=False), init)
    _, l_s, acc_s = sel_step(kd, carry, True)
    o_s = [acc_s[h] * (1.0 / jnp.maximum(l_s[h], 1e-30)) for h in heads]

    sig = sig_ref[...]
    for h in heads:
        g0 = sig[:, _S_NSA_G + 3 * h:_S_NSA_G + 3 * h + 1]
        g1 = sig[:, _S_NSA_G + 3 * h + 1:_S_NSA_G + 3 * h + 2]
        g2 = sig[:, _S_NSA_G + 3 * h + 2:_S_NSA_G + 3 * h + 3]
        o_ref[:, h * HEAD_DIM:(h + 1) * HEAD_DIM] = g0 * o_c[h * qb:(h + 1) * qb] + g1 * o_s[h] + g2 * o_w[h]


def _nsa(zbf, sig, kc, vc, kw_pad, vw_pad, overlap_t, expand, sb, wb, cb, b, t, *, tk):
    qb = Q_BLOCK
    nq = t // qb
    n_slc = t // SLC_LEN
    ncp = kc.shape[1]
    hq = N_HEADS * qb
    n_far = sb.shape[0] - 1
    per_b = lambda rows: pl.BlockSpec((None, rows, HEAD_DIM), lambda i, j: (i, 0, 0))
    whole = lambda a: pl.BlockSpec(a.shape, lambda i, j: (0,) * a.ndim)
    return pl.pallas_call(
        functools.partial(_nsa_body, tk=tk, n_far=n_far, n_slc=n_slc, sc2=HEAD_DIM ** -0.5 * LOG2E),
        grid=(b, nq),
        in_specs=[
            pl.BlockSpec((qb, GROUP_W), lambda i, j: (i * nq + j, _COL["nsa_q"] // 4)),
            pl.BlockSpec((qb, LANE), lambda i, j: (i * nq + j, 0)),
            per_b(ncp), per_b(ncp),
            pl.BlockSpec((t, HEAD_DIM), lambda i, j: (i, _COL["nsa_ks"])),
            pl.BlockSpec((t, HEAD_DIM), lambda i, j: (i, _COL["nsa_vs"])),
            per_b(t + WINDOW), per_b(t + WINDOW),
            whole(overlap_t), whole(expand), whole(sb), whole(wb), whole(cb),
        ],
        out_specs=pl.BlockSpec((qb, GROUP_W), lambda i, j: (i * nq + j, 0)),
        out_shape=jax.ShapeDtypeStruct((b * t, GROUP_W), F32),
        compiler_params=_cparams(("parallel", "arbitrary")),
        name="nsa_attn",
    )(zbf, sig, kc, vc, zbf, zbf, kw_pad, vw_pad, overlap_t, expand, sb, wb, cb)


def _outproj_body(h_ref, oa_ref, ob_ref, oc_ref, od_ref, g_ref, w_ref, o_ref):
    y = jnp.concatenate([
        _rms(oa_ref[...], g_ref[0:1, :]), ob_ref[...], _rms(oc_ref[...], g_ref[1:2, :]),
        _rms(od_ref[...], g_ref[2:3, :])], axis=-1)
    o_ref[...] = h_ref[...] + _dot(y.astype(BF16), w_ref[...])


def _outproj(h, oa, ob, oc, od, g, w, l, *, tm=512):
    nt, d = h.shape
    tm = min(tm, nt)
    grp = pl.BlockSpec((tm, GROUP_W), lambda i: (i, 0))
    return pl.pallas_call(
        _outproj_body,
        grid=(nt // tm,),
        in_specs=[pl.BlockSpec((tm, d), lambda i: (i, 0)), grp, grp, grp, grp,
                  pl.BlockSpec((3, GROUP_W), lambda i: (0, 0)),
                  pl.BlockSpec((None, 4 * GROUP_W, d), lambda i: (l, 0, 0))],
        out_specs=pl.BlockSpec((tm, d), lambda i: (i, 0)),
        out_shape=jax.ShapeDtypeStruct((nt, d), F32),
        compiler_params=_cparams(("parallel",)),
        name="outproj",
    )(h, oa, ob, oc, od, g, w)


_IN_WIDTHS = (("fox_q", GROUP_W), ("fox_k", GROUP_W), ("fox_v", GROUP_W), ("fox_f", N_HEADS),
              ("gdn_q", GROUP_W), ("gdn_k", GROUP_W), ("gdn_v", GROUP_W), ("gdn_a", N_HEADS), ("gdn_b", N_HEADS),
              ("gdn_z", GROUP_W), ("lru_x", GROUP_W), ("lru_gate", GROUP_W), ("nsa_q", GROUP_W),
              ("nsa_kc", HEAD_DIM), ("nsa_vc", HEAD_DIM), ("nsa_ks", HEAD_DIM), ("nsa_vs", HEAD_DIM),
              ("nsa_kw", HEAD_DIM), ("nsa_vw", HEAD_DIM), ("nsa_g", 3 * N_HEADS))
D_IN = sum(w for _, w in _IN_WIDTHS)


def _in_pieces():
    small_lane = {"fox_f": _S_FOX_F, "gdn_a": _S_GDN_A, "gdn_b": _S_GDN_B, "nsa_g": _S_NSA_G}
    out, off = [], 0
    for name, width in _IN_WIDTHS:
        if name in small_lane:
            out.append((off, width, 1, _COL["small"] * LANE + small_lane[name]))
        else:
            out.append((off, width, 0 if name in _COLS16 else 1, _COL[name] * LANE))
        off += width
    return out


def _relayout_body(w_ref, o16_ref, o32_ref):
    outs = (o16_ref, o32_ref)
    tc = w_ref.shape[1]
    tail = _COL["small"] * LANE
    small = []
    for src, width, which, dst in _in_pieces():
        if dst >= tail and which == 1:
            small.append(w_ref[src:src + width, :])
        else:
            outs[which][dst:dst + width, :] = w_ref[src:src + width, :].astype(outs[which].dtype)
    used = sum(s.shape[0] for s in small)
    small.append(jnp.zeros((W32 - tail - used, tc), F32))
    o32_ref[tail:, :] = jnp.concatenate(small, axis=0).astype(o32_ref.dtype)


def _relayout_w_in(w_t, *, tc=256):
    nl, d_in, d = w_t.shape
    return pl.pallas_call(
        _relayout_body,
        grid=(nl, d // tc),
        in_specs=[pl.BlockSpec((None, d_in, tc), lambda l, i: (l, 0, i))],
        out_specs=[pl.BlockSpec((None, W16, tc), lambda l, i: (l, 0, i)),
                   pl.BlockSpec((None, W32, tc), lambda l, i: (l, 0, i))],
        out_shape=[jax.ShapeDtypeStruct((nl, W16, d), BF16), jax.ShapeDtypeStruct((nl, W32, d), BF16)],
        compiler_params=_cparams(("parallel", "parallel")),
        name="w_in_relayout",
    )(w_t)


def _pad_cols(w):
    src, off = {}, 0
    for name, width in _IN_WIDTHS:
        src[name] = w[..., off:off + width]
        off += width
    zeros = lambda n: jnp.zeros(w.shape[:-1] + (n,), w.dtype)
    small = [src["fox_f"], src["gdn_a"], src["gdn_b"], src["nsa_g"]]
    src["small"] = jnp.concatenate(small + [zeros(LANE - sum(s.shape[-1] for s in small))], axis=-1)
    w16 = jnp.concatenate([src[name] for name in _COLS16], axis=-1)
    w32 = jnp.concatenate([src[name] for name in _COLS32] + [zeros(W32 - (_COL["small"] + 1) * LANE)], axis=-1)
    assert w16.shape[-1] == W16 and w32.shape[-1] == W32
    return w16, w32


def _block_diag(w):
    l, nb, bw, _ = w.shape
    eye = jnp.eye(nb, dtype=w.dtype)
    return jnp.einsum("lnde,nm->lndme", w, eye).reshape(l, nb * bw, nb * bw)


def _bucket_thresholds(t):
    n = jnp.arange(max(t, 2 * MAX_DIST), dtype=jnp.int32)
    nf = jnp.maximum(n, 1).astype(F32)
    large = MAX_EXACT + (jnp.log(nf / MAX_EXACT) / math.log(MAX_DIST / MAX_EXACT)
                         * (N_BUCKETS - MAX_EXACT)).astype(jnp.int32)
    large = jnp.minimum(large, N_BUCKETS - 1)
    bucket = jnp.where(n < MAX_EXACT, n, large)
    return jnp.sum(bucket[None, :] < jnp.arange(N_BUCKETS, dtype=jnp.int32)[:, None], axis=1).astype(jnp.int32)


def _nsa_tile(t):
    return min(1024, t)


def _mixer(h, l, p, b, t):
    zbf = _inproj(h, p["mix_norm_g"][l], p["w16"], p["b16"], l, BF16, tn=W16 // 2)
    z32 = _inproj(h, p["mix_norm_g"][l], p["w32"], p["b32"], l, F32, tn=W32 // 4)

    ccol = _fox_gate(z32, b, t)
    crow = jnp.pad(ccol[:, ::LANE].reshape(b, t, N_HEADS).transpose(0, 2, 1), ((0, 0), (0, 8 - N_HEADS), (0, 0)))
    o_a = _fox(zbf, ccol, crow, b, t)

    qkv, gdn_g, sig = _gdn_prep(z32, p["gdn_conv_w"][l], p["gdn_a_log"][l], p["gdn_dt_bias"][l], b, t)
    c = GDN_CHUNK
    grow = gdn_g[:, _S_GDN_A:_S_GDN_A + N_HEADS].reshape(b * t // c, c, N_HEADS).transpose(0, 2, 1)
    grow = jnp.pad(grow, ((0, 0), (0, 8 - N_HEADS), (0, 0)))
    u, w, qg, kd, qk, gc = _gdn_intra(qkv, gdn_g, grow, sig, b, t)
    o_b = _gdn_scan(u, w, qg, kd, qk, gc, z32, p["gdn_norm_g"][l], b, t)

    o_c = _lru(z32, p["lru_conv_w"][l], p["lru_conv_b"][l], p["lru_wa_bd"][l], p["lru_b_a"][l],
               p["lru_wx_bd"][l], p["lru_b_x"][l], p["lru_lambda"][l], b, t)

    col = lambda name: z32[:, _COL[name] * LANE:(_COL[name] + 1) * LANE]
    cmp_rows = lambda a: a.reshape(b, t // CMP_STRIDE, CMP_STRIDE * HEAD_DIM)
    kc, vc = _nsa_compress(cmp_rows(col("nsa_kc")), cmp_rows(col("nsa_vc")), p["nsa_pe_k"][l], p["nsa_pe_v"][l],
                           p["nsa_w1_k16"][l], p["nsa_w1_v16"][l], p["nsa_w2_k16"][l], p["nsa_w2_v16"][l])
    colbf = lambda name: zbf[:, _COL[name] * LANE:(_COL[name] + 1) * LANE].reshape(b, t, HEAD_DIM)
    kw_pad = jnp.pad(colbf("nsa_kw"), ((0, 0), (WINDOW, 0), (0, 0)))
    vw_pad = jnp.pad(colbf("nsa_vw"), ((0, 0), (WINDOW, 0), (0, 0)))
    o_d = _nsa(zbf, sig, kc, vc, kw_pad, vw_pad, p["overlap_t"], p["expand"], p["sb"], p["wb"], p["cb"], b, t,
               tk=_nsa_tile(t))

    return _outproj(h, o_a, o_b, o_c, o_d, p["out_norm_g"][l], p["w_out16"], l)


def _mixer_consts(rel_bias, t):
    n_cmp_rows = t // CMP_STRIDE
    n_slc = t // SLC_LEN
    cn = jnp.arange(n_cmp_rows)[None, :] * CMP_STRIDE
    sj = jnp.arange(n_slc)[:, None] * SLC_LEN
    sb, wb, cb = _nsa_bias(_bucket_thresholds(t), rel_bias, t, tk=Q_BLOCK)
    return dict(
        overlap_t=((cn <= sj + SLC_LEN - 1) & (cn + CMP_LEN - 1 >= sj)).astype(BF16),
        expand=(jnp.arange(t)[None, :] // SLC_LEN == jnp.arange(n_slc)[:, None]).astype(BF16),
        sb=sb, wb=wb, cb=cb)


def kernel(x, ffn1_norm_g, ffn1_w_gate, ffn1_w_up, ffn1_w_down, mix_norm_g, w_in, b_in, gdn_conv_w, gdn_a_log, gdn_dt_bias, gdn_norm_g, lru_conv_w, lru_conv_b, lru_w_a, lru_b_a, lru_w_x, lru_b_x, lru_lambda, nsa_pe_k, nsa_w1_k, nsa_w2_k, nsa_pe_v, nsa_w1_v, nsa_w2_v, rel_bias, out_norm_g, w_out, ffn2_norm_g, ffn2_w_gate, ffn2_w_up, ffn2_w_down, final_norm_g):
    b, t, d = x.shape
    depth = w_in.shape[0]
    assert w_in.shape[-1] == D_IN
    w16, w32 = _relayout_w_in(jnp.swapaxes(w_in, 1, 2))
    b16, b32 = _pad_cols(b_in[:, None, :])
    p = dict(
        mix_norm_g=mix_norm_g, w16=w16, w32=w32, b16=b16, b32=b32,
        gdn_conv_w=gdn_conv_w, gdn_a_log=gdn_a_log, gdn_dt_bias=gdn_dt_bias, gdn_norm_g=gdn_norm_g,
        lru_conv_w=lru_conv_w, lru_conv_b=lru_conv_b, lru_wa_bd=_block_diag(lru_w_a).astype(BF16), lru_b_a=lru_b_a,
        lru_wx_bd=_block_diag(lru_w_x).astype(BF16), lru_b_x=lru_b_x, lru_lambda=lru_lambda,
        nsa_pe_k=nsa_pe_k, nsa_pe_v=nsa_pe_v, nsa_w1_k16=nsa_w1_k.astype(BF16), nsa_w1_v16=nsa_w1_v.astype(BF16),
        nsa_w2_k16=nsa_w2_k.astype(BF16), nsa_w2_v16=nsa_w2_v.astype(BF16),
        out_norm_g=out_norm_g, w_out16=w_out.astype(BF16),
        **_mixer_consts(rel_bias, t),
    )
    f1 = (ffn1_w_gate.astype(BF16), ffn1_w_up.astype(BF16), ffn1_w_down.astype(BF16))
    f2 = (ffn2_w_gate.astype(BF16), ffn2_w_up.astype(BF16), ffn2_w_down.astype(BF16))
    h = x.reshape(b * t, d)
    for l in range(depth):
        h = _ffn(h, ffn1_norm_g[l], *f1, l)
        h = _mixer(h, l, p, b, t)
        h = _ffn(h, ffn2_norm_g[l], *f2, l, final_g=final_norm_g if l == depth - 1 else None)
    return h.reshape(b, t, d)
```

```python
import functools
import math

import jax
import jax.numpy as jnp
from jax import lax
from jax.experimental import pallas as pl
from jax.experimental.pallas import tpu as pltpu

F32 = jnp.float32
BF16 = jnp.bfloat16

EPS = 1e-6
LANE = 128
HEAD_DIM = 128
N_HEADS = 4
GROUP_W = N_HEADS * HEAD_DIM
GDN_CHUNK = 64
CONV_W = 4
LRU_BLOCKS = 8
LRU_C = 8.0
CMP_LEN = 32
CMP_STRIDE = 16
SLC_LEN = 64
N_SELECT = 16
WINDOW = 512
FORCED_SCORE = 1.0e6
N_BUCKETS = 32
MAX_EXACT = 16
MAX_DIST = 1024
Q_BLOCK = 128
NEG = -1.0e30
LOG2E = math.log2(math.e)
VMEM_LIMIT = 56 * 1024 * 1024

_COLS16 = ("fox_q", "fox_k", "fox_v", "nsa_q", "nsa_ks", "nsa_vs", "nsa_kw", "nsa_vw")
_COLS32 = ("gdn_q", "gdn_k", "gdn_v", "gdn_z", "lru_x", "lru_gate", "nsa_kc", "nsa_vc", "small")
_COL = dict(fox_q=0, fox_k=4, fox_v=8, nsa_q=12, nsa_ks=16, nsa_vs=17, nsa_kw=18, nsa_vw=19,
            gdn_q=0, gdn_k=4, gdn_v=8, gdn_z=12, lru_x=16, lru_gate=20, nsa_kc=24, nsa_vc=25, small=26)
W16 = 20 * LANE
W32 = 28 * LANE
_S_FOX_F, _S_GDN_A, _S_GDN_B, _S_NSA_G = 0, 4, 8, 12


def _cparams(sem):
    return pltpu.CompilerParams(dimension_semantics=sem, vmem_limit_bytes=VMEM_LIMIT)


def _rms(x, g):
    return x * lax.rsqrt(jnp.mean(x * x, axis=-1, keepdims=True) + EPS) * g


def _softplus(x):
    return jnp.maximum(x, 0.0) + jnp.log1p(jnp.exp(-jnp.abs(x)))


def _gelu_tanh(x):
    return 0.5 * x * (1.0 + jnp.tanh(math.sqrt(2.0 / math.pi) * (x + 0.044715 * (x * x * x))))


def _dot(a, b):
    return jnp.dot(a, b, preferred_element_type=F32)


def _dot_nt(a, b):
    return lax.dot_general(a, b, (((1,), (1,)), ((), ())), preferred_element_type=F32)


def _dot_tn(a, b):
    return lax.dot_general(a, b, (((0,), (0,)), ((), ())), preferred_element_type=F32)


def _dot_hi(a, b):
    return jnp.dot(a, b, preferred_element_type=F32, precision=lax.Precision.HIGHEST)


def _split(x):
    hi = x.astype(BF16)
    return hi, (x - hi.astype(F32)).astype(BF16)


def _dot3(a, b):
    return _dot(a[0], b[0]) + (_dot(a[0], b[1]) + _dot(a[1], b[0]))


def _ffn_body(*refs, final):
    if final:
        x_ref, g_ref, wg_ref, wu_ref, wd_ref, fg_ref, o_ref, n_scr = refs
    else:
        x_ref, g_ref, wg_ref, wu_ref, wd_ref, o_ref, n_scr = refs
    j = pl.program_id(1)

    @pl.when(j == 0)
    def _():
        n_scr[...] = _rms(x_ref[...], g_ref[...]).astype(BF16)
        o_ref[...] = jnp.zeros_like(o_ref)

    n = n_scr[...]
    gate = _dot(n, wg_ref[...])
    up = _dot(n, wu_ref[...])
    a = (gate * jax.nn.sigmoid(gate) * up).astype(BF16)
    o_ref[...] += _dot(a, wd_ref[...])

    @pl.when(j == pl.num_programs(1) - 1)
    def _():
        y = x_ref[...] + 0.5 * o_ref[...]
        if final:
            y = _rms(y, fg_ref[...])
        o_ref[...] = y


def _ffn(x, g, wg, wu, wd, l, final_g=None, *, tm=512, tf=512):
    nt, d = x.shape
    f = wg.shape[-1]
    tm = min(tm, nt)
    final = final_g is not None
    in_specs = [
        pl.BlockSpec((tm, d), lambda i, j: (i, 0)),
        pl.BlockSpec((1, d), lambda i, j: (0, 0)),
        pl.BlockSpec((None, d, tf), lambda i, j: (l, 0, j)),
        pl.BlockSpec((None, d, tf), lambda i, j: (l, 0, j)),
        pl.BlockSpec((None, tf, d), lambda i, j: (l, j, 0)),
    ]
    args = [x, g.reshape(1, d), wg, wu, wd]
    if final:
        in_specs.append(pl.BlockSpec((1, d), lambda i, j: (0, 0)))
        args.append(final_g.reshape(1, d))
    return pl.pallas_call(
        functools.partial(_ffn_body, final=final),
        grid=(nt // tm, f // tf),
        in_specs=in_specs,
        out_specs=pl.BlockSpec((tm, d), lambda i, j: (i, 0)),
        out_shape=jax.ShapeDtypeStruct((nt, d), F32),
        scratch_shapes=[pltpu.VMEM((tm, d), BF16)],
        compiler_params=_cparams(("parallel", "arbitrary")),
        name="ffn",
    )(*args)


def _inproj_body(x_ref, g_ref, w_ref, b_ref, o_ref, n_scr):
    @pl.when(pl.program_id(1) == 0)
    def _():
        n_scr[...] = _rms(x_ref[...], g_ref[...]).astype(BF16)

    o_ref[...] = (_dot_nt(n_scr[...], w_ref[...]) + b_ref[...]).astype(o_ref.dtype)


def _inproj(x, g, w, b, l, out_dtype, *, tm=1024, tn=512):
    nt, d = x.shape
    n = w.shape[1]
    tm = min(tm, nt)
    return pl.pallas_call(
        _inproj_body,
        grid=(nt // tm, n // tn),
        in_specs=[
            pl.BlockSpec((tm, d), lambda i, j: (i, 0)),
            pl.BlockSpec((1, d), lambda i, j: (0, 0)),
            pl.BlockSpec((None, tn, d), lambda i, j: (l, j, 0)),
            pl.BlockSpec((None, 1, tn), lambda i, j: (l, 0, j)),
        ],
        out_specs=pl.BlockSpec((tm, tn), lambda i, j: (i, j)),
        out_shape=jax.ShapeDtypeStruct((nt, n), out_dtype),
        scratch_shapes=[pltpu.VMEM((tm, d), BF16)],
        compiler_params=_cparams(("parallel", "arbitrary")),
        name="inproj",
    )(x, g.reshape(1, d), w, b)


def _cumsum_rows(y):
    rows = lax.broadcasted_iota(jnp.int32, y.shape, 0)
    s = 1
    while s < y.shape[0]:
        y = y + jnp.where(rows >= s, pltpu.roll(y, s, 0), 0.0)
        s *= 2
    return y


def _fox_gate_body(f_ref, c_ref):
    x = f_ref[...]
    log_f = jnp.minimum(x, 0.0) - jnp.log1p(jnp.exp(-jnp.abs(x)))
    c = _cumsum_rows(log_f) * LOG2E
    for h in range(N_HEADS):
        c_ref[:, h * LANE:(h + 1) * LANE] = jnp.broadcast_to(c[:, _S_FOX_F + h:_S_FOX_F + h + 1], (c.shape[0], LANE))


def _fox_gate(z32, b, t):
    return pl.pallas_call(
        _fox_gate_body,
        grid=(b,),
        in_specs=[pl.BlockSpec((t, LANE), lambda i: (i, _COL["small"]))],
        out_specs=pl.BlockSpec((t, N_HEADS * LANE), lambda i: (i, 0)),
        out_shape=jax.ShapeDtypeStruct((b * t, N_HEADS * LANE), F32),
        compiler_params=_cparams(("parallel",)),
        name="fox_gate",
    )(z32)


def _fox_body(q_ref, k_ref, v_ref, ccol_ref, crow_ref, o_ref, m_scr, l_scr, acc_scr, *, tq, tk, sc2):
    qi = pl.program_id(1)
    rows = qi * tq + lax.broadcasted_iota(jnp.int32, (tq, tk), 0)
    cols = lax.broadcasted_iota(jnp.int32, (tq, tk), 1)
    m_scr[...] = jnp.full_like(m_scr, NEG)
    l_scr[...] = jnp.zeros_like(l_scr)
    acc_scr[...] = jnp.zeros_like(acc_scr)

    def step(kt, masked):
        k0 = pl.multiple_of(kt * tk, tk)
        heads = range(N_HEADS)
        hsl = [slice(h * HEAD_DIM, (h + 1) * HEAD_DIM) for h in heads]
        wide = lambda x: jnp.concatenate([x] * (tk // LANE), axis=1)
        s = [_dot_nt(q_ref[:, hsl[h]], k_ref[pl.ds(k0, tk), hsl[h]]) * sc2 for h in heads]
        s = [s[h] + wide(ccol_ref[:, hsl[h]]) - crow_ref[h:h + 1, pl.ds(k0, tk)] for h in heads]
        if masked:
            causal = rows >= k0 + cols
            s = [jnp.where(causal, x, NEG) for x in s]
        m_old = [m_scr[h] for h in heads]
        m_new = [jnp.maximum(m_old[h], jnp.max(s[h], axis=1, keepdims=True)) for h in heads]
        alpha = [jnp.exp2(m_old[h] - m_new[h]) for h in heads]
        p = [jnp.exp2(s[h] - wide(m_new[h])) for h in heads]
        l_new = [alpha[h] * l_scr[h] + jnp.sum(p[h], axis=1, keepdims=True) for h in heads]
        pv = [_dot(p[h].astype(BF16), v_ref[pl.ds(k0, tk), hsl[h]]) for h in heads]
        acc_new = [alpha[h] * acc_scr[h] + pv[h] for h in heads]
        for h in heads:
            m_scr[h] = m_new[h]
            l_scr[h] = l_new[h]
            acc_scr[h] = acc_new[h]

    def body(kt, carry):
        step(kt, False)
        return carry

    kd = (qi * tq) // tk
    lax.fori_loop(0, kd, body, 0)
    step(kd, True)
    for h in range(N_HEADS):
        o_ref[:, h * HEAD_DIM:(h + 1) * HEAD_DIM] = acc_scr[h] / l_scr[h]


def _fox(zbf, ccol, crow, b, t, *, tq=256, tk=1024):
    tq, tk = min(tq, t), min(tk, t)
    assert tk % tq == 0
    nq = t // tq
    return pl.pallas_call(
        functools.partial(_fox_body, tq=tq, tk=tk, sc2=HEAD_DIM ** -0.5 * LOG2E),
        grid=(b, nq),
        in_specs=[
            pl.BlockSpec((tq, GROUP_W), lambda i, j: (i * nq + j, _COL["fox_q"] // 4)),
            pl.BlockSpec((t, GROUP_W), lambda i, j: (i, _COL["fox_k"] // 4)),
            pl.BlockSpec((t, GROUP_W), lambda i, j: (i, _COL["fox_v"] // 4)),
            pl.BlockSpec((tq, N_HEADS * LANE), lambda i, j: (i * nq + j, 0)),
            pl.BlockSpec((None, 8, t), lambda i, j: (i, 0, 0)),
        ],
        out_specs=pl.BlockSpec((tq, GROUP_W), lambda i, j: (i * nq + j, 0)),
        out_shape=jax.ShapeDtypeStruct((b * t, GROUP_W), F32),
        scratch_shapes=[pltpu.VMEM((N_HEADS, tq, LANE), F32), pltpu.VMEM((N_HEADS, tq, LANE), F32),
                        pltpu.VMEM((N_HEADS, tq, HEAD_DIM), F32)],
        compiler_params=_cparams(("parallel", "arbitrary")),
        name="fox_attn",
    )(zbf, zbf, zbf, ccol, crow)


def _causal_conv(x, halo, w_ref):
    tt = x.shape[0]
    xx = jnp.concatenate([halo, x], axis=0)
    y = x * w_ref[CONV_W - 1:CONV_W, :]
    for d in range(1, CONV_W):
        y = y + pltpu.roll(xx, d, 0)[8:8 + tt] * w_ref[CONV_W - 1 - d:CONV_W - d, :]
    return y


def _gdn_prep_body(x_ref, halo_ref, s_ref, w_ref, alog_ref, dt_ref, qkv_ref, g_ref, sig_ref):
    first = pl.program_id(1) == 0
    x = x_ref[...]
    halo = jnp.where(first, 0.0, halo_ref[...])
    y = _causal_conv(x, halo, w_ref)
    y = y * jax.nn.sigmoid(y)
    for h in range(N_HEADS):
        for part, post in ((0, HEAD_DIM ** -0.5), (1, 1.0)):
            cs = slice(part * GROUP_W + h * HEAD_DIM, part * GROUP_W + (h + 1) * HEAD_DIM)
            u = y[:, cs]
            un = u * lax.rsqrt(jnp.sum(u * u, axis=-1, keepdims=True) + EPS)
            qkv_ref[:, cs] = un * post if part == 0 else un
    qkv_ref[:, 2 * GROUP_W:] = y[:, 2 * GROUP_W:]
    s = s_ref[...]
    g_ref[...] = -jnp.exp(alog_ref[...]) * _softplus(s + dt_ref[...])
    sig_ref[...] = jax.nn.sigmoid(s)


def _gdn_prep(z32, conv_w, a_log, dt_bias, b, t, *, tt=512):
    tt = min(tt, t)
    nt = t // tt
    w3 = 3 * GROUP_W
    pad = lambda v: jnp.zeros((1, LANE), F32).at[0, _S_GDN_A:_S_GDN_A + N_HEADS].set(v)
    cb = _COL["gdn_q"] // 12
    return pl.pallas_call(
        _gdn_prep_body,
        grid=(b, nt),
        in_specs=[
            pl.BlockSpec((tt, w3), lambda i, j: (i * nt + j, cb)),
            pl.BlockSpec((8, w3), lambda i, j: (jnp.maximum((i * nt + j) * (tt // 8) - 1, 0), cb)),
            pl.BlockSpec((tt, LANE), lambda i, j: (i * nt + j, _COL["small"])),
            pl.BlockSpec((CONV_W, w3), lambda i, j: (0, 0)),
            pl.BlockSpec((1, LANE), lambda i, j: (0, 0)),
            pl.BlockSpec((1, LANE), lambda i, j: (0, 0)),
        ],
        out_specs=[
            pl.BlockSpec((tt, w3), lambda i, j: (i * nt + j, 0)),
            pl.BlockSpec((tt, LANE), lambda i, j: (i * nt + j, 0)),
            pl.BlockSpec((tt, LANE), lambda i, j: (i * nt + j, 0)),
        ],
        out_shape=[jax.ShapeDtypeStruct((b * t, w3), F32), jax.ShapeDtypeStruct((b * t, LANE), F32),
                   jax.ShapeDtypeStruct((b * t, LANE), F32)],
        compiler_params=_cparams(("parallel", "arbitrary")),
        name="gdn_prep",
    )(z32, z32, z32, conv_w, pad(a_log), pad(dt_bias))


def _gdn_intra_body(qkv_ref, g_ref, grow_ref, sig_ref, u_ref, w_ref, qg_ref, kd_ref, qk_ref, gc_ref, *, c, nch):
    ri = lax.broadcasted_iota(jnp.int32, (c, c), 0)
    ci = lax.broadcasted_iota(jnp.int32, (c, c), 1)
    tril = ri >= ci
    strict = ri > ci
    eye = (ri == ci).astype(F32)
    lower = tril.astype(F32)
    upper = (ri <= ci).astype(F32)
    chains = [(ch, h) for ch in range(nch) for h in range(N_HEADS)]
    rsl = lambda ch: slice(ch * c, (ch + 1) * c)
    hsl = lambda h, part=0: slice(part * GROUP_W + h * HEAD_DIM, part * GROUP_W + (h + 1) * HEAD_DIM)
    gc_cols = [_dot_hi(lower, g_ref[rsl(ch), :]) for ch in range(nch)]
    gc_rows = [_dot_hi(grow_ref[ch], upper) for ch in range(nch)]
    for ch in range(nch):
        gc_ref[rsl(ch), :] = gc_cols[ch]
    q = [qkv_ref[rsl(ch), hsl(h, 0)] for ch, h in chains]
    k = [qkv_ref[rsl(ch), hsl(h, 1)] for ch, h in chains]
    v = [qkv_ref[rsl(ch), hsl(h, 2)] for ch, h in chains]
    beta = [sig_ref[rsl(ch), _S_GDN_B + h:_S_GDN_B + h + 1] for ch, h in chains]
    gc = [gc_cols[ch][:, _S_GDN_A + h:_S_GDN_A + h + 1] for ch, h in chains]
    gr = [gc_rows[ch][h:h + 1, :] for ch, h in chains]
    idx = range(len(chains))
    decay = [jnp.where(tril, jnp.exp(jnp.where(tril, gc[i] - gr[i], 0.0)), 0.0) for i in idx]
    kb = [k[i] * beta[i] for i in idx]
    k16 = [x.astype(BF16) for x in k]
    a = [jnp.where(strict, _dot_nt(kb[i].astype(BF16), k16[i]) * decay[i], 0.0) for i in idx]
    same = lambda s: (ri // s) == (ci // s)
    base = 8
    pw = [-jnp.where(same(base), x, 0.0) for x in a]
    tm = [eye + x for x in pw]
    n = 2
    while n < base:
        pws = [_split(x) for x in pw]
        pw = [_dot3(x, x) for x in pws]
        tm = [tm[i] + _dot3(_split(tm[i]), _split(pw[i])) for i in idx]
        n *= 2
    s = base
    while s < c:
        join = same(2 * s) & jnp.logical_not(same(s))
        tms = [_split(x) for x in tm]
        et = [_dot3(_split(jnp.where(join, a[i], 0.0)), tms[i]) for i in idx]
        tm = [tm[i] - _dot3(tms[i], _split(et[i])) for i in idx]
        s *= 2
    tm16 = [x.astype(BF16) for x in tm]
    u = [_dot(tm16[i], (v[i] * beta[i]).astype(BF16)) for i in idx]
    w = [_dot(tm16[i], (kb[i] * jnp.exp(gc[i])).astype(BF16)) for i in idx]
    qk = [_dot_nt(q[i].astype(BF16), k16[i]) * decay[i] for i in idx]
    for i, (ch, h) in enumerate(chains):
        rs, hs = rsl(ch), hsl(h)
        gl = gc[i][c - 1:c, :]
        u_ref[rs, hs] = u[i]
        w_ref[rs, hs] = w[i].astype(BF16)
        qk_ref[rs, h * c:(h + 1) * c] = qk[i].astype(BF16)
        qg_ref[rs, hs] = (q[i] * jnp.exp(gc[i])).astype(BF16)
        kd_ref[rs, hs] = (k[i] * jnp.exp(gl - gc[i])).astype(BF16)


def _gdn_intra(qkv, g, grow, sig, b, t, *, c=GDN_CHUNK, nch=8):
    nt = b * t
    rows = nch * c
    tok = lambda w: pl.BlockSpec((rows, w), lambda i: (i, 0))
    return pl.pallas_call(
        functools.partial(_gdn_intra_body, c=c, nch=nch),
        grid=(nt // rows,),
        in_specs=[tok(3 * GROUP_W), tok(LANE), pl.BlockSpec((nch, 8, c), lambda i: (i, 0, 0)), tok(LANE)],
        out_specs=[tok(GROUP_W), tok(GROUP_W), tok(GROUP_W), tok(GROUP_W), tok(N_HEADS * c), tok(LANE)],
        out_shape=[jax.ShapeDtypeStruct((nt, GROUP_W), F32), jax.ShapeDtypeStruct((nt, GROUP_W), BF16),
                   jax.ShapeDtypeStruct((nt, GROUP_W), BF16), jax.ShapeDtypeStruct((nt, GROUP_W), BF16),
                   jax.ShapeDtypeStruct((nt, N_HEADS * c), BF16), jax.ShapeDtypeStruct((nt, LANE), F32)],
        compiler_params=_cparams(("parallel",)),
        name="gdn_intra",
    )(qkv, g, grow, sig)


def _gdn_scan_body(u_ref, w_ref, qg_ref, kd_ref, qk_ref, gc_ref, z_ref, ng_ref, o_ref, s_scr, *, c, nch, nb):
    @pl.when(pl.program_id(0) == 0)
    def _():
        s_scr[...] = jnp.zeros_like(s_scr)

    chains = [(b, h) for b in range(nb) for h in range(N_HEADS)]
    idx = range(len(chains))
    hsl = lambda h: slice(h * HEAD_DIM, (h + 1) * HEAD_DIM)
    s = [s_scr[i] for i in idx]
    for ch in range(nch):
        rs = slice(ch * c, (ch + 1) * c)
        egl = [jnp.exp(gc_ref[b, (ch + 1) * c - 1:(ch + 1) * c, :]) for b in range(nb)]
        s16 = [x.astype(BF16) for x in s]
        ws = [_dot(w_ref[b, rs, hsl(h)], s16[i]) for i, (b, h) in enumerate(chains)]
        v_new = [(u_ref[b, rs, hsl(h)] - ws[i]).astype(BF16) for i, (b, h) in enumerate(chains)]
        s = [s[i] * egl[b][:, _S_GDN_A + h:_S_GDN_A + h + 1] + _dot_tn(kd_ref[b, rs, hsl(h)], v_new[i])
             for i, (b, h) in enumerate(chains)]
        o = [_dot(qg_ref[b, rs, hsl(h)], s16[i]) + _dot(qk_ref[b, rs, h * c:(h + 1) * c], v_new[i])
             for i, (b, h) in enumerate(chains)]
        for i, (b, h) in enumerate(chains):
            gz = z_ref[b, rs, hsl(h)]
            o_ref[b, rs, hsl(h)] = _rms(o[i], ng_ref[...]) * (gz * jax.nn.sigmoid(gz))
    for i in idx:
        s_scr[i] = s[i]


def _gdn_scan(u, w, qg, kd, qk, gc, z32, norm_g, b, t, *, c=GDN_CHUNK, nch=4):
    rows = nch * c
    v3 = lambda a: a.reshape(b, t, a.shape[-1])
    blk = lambda wd, cb=0: pl.BlockSpec((b, rows, wd), lambda i: (0, i, cb))
    out = pl.pallas_call(
        functools.partial(_gdn_scan_body, c=c, nch=nch, nb=b),
        grid=(t // rows,),
        in_specs=[blk(GROUP_W), blk(GROUP_W), blk(GROUP_W), blk(GROUP_W), blk(N_HEADS * c), blk(LANE),
                  blk(GROUP_W, _COL["gdn_z"] // 4), pl.BlockSpec((1, HEAD_DIM), lambda i: (0, 0))],
        out_specs=blk(GROUP_W),
        out_shape=jax.ShapeDtypeStruct((b, t, GROUP_W), F32),
        scratch_shapes=[pltpu.VMEM((b * N_HEADS, HEAD_DIM, HEAD_DIM), F32)],
        compiler_params=_cparams(("arbitrary",)),
        name="gdn_scan",
    )(v3(u), v3(w), v3(qg), v3(kd), v3(qk), v3(gc), v3(z32), norm_g.reshape(1, HEAD_DIM))
    return out.reshape(b * t, GROUP_W)


def _lru_body(x_ref, halo_ref, gate_ref, cw_ref, cb_ref, wa_ref, ba_ref, wx_ref, bx_ref, lam_ref, o_ref, h_scr):
    first = pl.program_id(1) == 0

    @pl.when(first)
    def _():
        h_scr[...] = jnp.zeros_like(h_scr)

    x = x_ref[...]
    tt = x.shape[0]
    halo = jnp.where(first, 0.0, halo_ref[...])
    xc = _causal_conv(x, halo, cw_ref) + cb_ref[...]
    xc16 = xc.astype(BF16)
    r = jax.nn.sigmoid(_dot(xc16, wa_ref[...]) + ba_ref[...])
    i = jax.nn.sigmoid(_dot(xc16, wx_ref[...]) + bx_ref[...])
    log_a = -LRU_C * r * _softplus(-lam_ref[...])
    a = jnp.exp(log_a)
    u = jnp.sqrt(-jnp.tanh(log_a) * (a * a + 1.0)) * (i * xc)
    rows = lax.broadcasted_iota(jnp.int32, a.shape, 0) % 8
    s = 1
    while s < 8:
        keep = rows >= s
        u = a * jnp.where(keep, pltpu.roll(u, s, 0), 0.0) + u
        a = a * jnp.where(keep, pltpu.roll(a, s, 0), 1.0)
        s *= 2
    gate = _gelu_tanh(gate_ref[...])
    h = h_scr[0:1, :]
    for g in range(tt // 8):
        rs = slice(8 * g, 8 * g + 8)
        hg = a[rs, :] * h + u[rs, :]
        o_ref[rs, :] = hg * gate[rs, :]
        h = hg[7:8, :]
    h_scr[0:1, :] = h


def _lru(z32, conv_w, conv_b, wa_bd, b_a, wx_bd, b_x, lam, b, t, *, tt=512):
    tt = min(tt, t)
    nt = t // tt
    w = GROUP_W
    row = lambda v: v.reshape(1, w)
    vec = pl.BlockSpec((1, w), lambda i, j: (0, 0))
    mat = pl.BlockSpec((w, w), lambda i, j: (0, 0))
    cx = _COL["lru_x"] // 4
    return pl.pallas_call(
        _lru_body,
        grid=(b, nt),
        in_specs=[
            pl.BlockSpec((tt, w), lambda i, j: (i * nt + j, cx)),
            pl.BlockSpec((8, w), lambda i, j: (jnp.maximum((i * nt + j) * (tt // 8) - 1, 0), cx)),
            pl.BlockSpec((tt, w), lambda i, j: (i * nt + j, _COL["lru_gate"] // 4)),
            pl.BlockSpec((CONV_W, w), lambda i, j: (0, 0)),
            vec, mat, vec, mat, vec, vec,
        ],
        out_specs=pl.BlockSpec((tt, w), lambda i, j: (i * nt + j, 0)),
        out_shape=jax.ShapeDtypeStruct((b * t, w), F32),
        scratch_shapes=[pltpu.VMEM((8, w), F32)],
        compiler_params=_cparams(("parallel", "arbitrary")),
        name="rg_lru",
    )(z32, z32, z32, conv_w, row(conv_b), wa_bd, row(b_a), wx_bd, row(b_x), row(lam))


def _nsa_cmp_body(rk_ref, rv_ref, pek_ref, pev_ref, w1k_ref, w1v_ref, w2k_ref, w2v_ref, kc_ref, vc_ref):
    for r_ref, pe_ref, w1_ref, w2_ref, o_ref in ((rk_ref, pek_ref, w1k_ref, w2k_ref, kc_ref),
                                                 (rv_ref, pev_ref, w1v_ref, w2v_ref, vc_ref)):
        r = r_ref[...]
        nr = r.shape[0]
        lo = (r + pe_ref[0:1, :]).astype(BF16)
        hi = (pltpu.roll(r, nr - 1, 0) + pe_ref[1:2, :]).astype(BF16)
        hid = _gelu_tanh(_dot(lo, w1_ref[0]) + _dot(hi, w1_ref[1]))
        o_ref[...] = _dot(hid.astype(BF16), w2_ref[...]).astype(BF16)


def _nsa_compress(rk, rv, pe_k, pe_v, w1_k, w1_v, w2_k, w2_v):
    b, nr, wide = rk.shape
    half = CMP_LEN * HEAD_DIM // 2
    assert wide == half
    rspec = pl.BlockSpec((None, nr, wide), lambda i: (i, 0, 0))
    pspec = pl.BlockSpec((2, half), lambda i: (0, 0))
    w1spec = pl.BlockSpec((2, half, HEAD_DIM), lambda i: (0, 0, 0))
    w2spec = pl.BlockSpec((HEAD_DIM, HEAD_DIM), lambda i: (0, 0))
    ospec = pl.BlockSpec((None, nr, HEAD_DIM), lambda i: (i, 0, 0))
    return pl.pallas_call(
        _nsa_cmp_body,
        grid=(b,),
        in_specs=[rspec, rspec, pspec, pspec, w1spec, w1spec, w2spec, w2spec],
        out_specs=[ospec, ospec],
        out_shape=[jax.ShapeDtypeStruct((b, nr, HEAD_DIM), BF16)] * 2,
        compiler_params=_cparams(("parallel",)),
        name="nsa_compress",
    )(rk, rv, pe_k.reshape(2, half), pe_v.reshape(2, half), w1_k.reshape(2, half, HEAD_DIM),
      w1_v.reshape(2, half, HEAD_DIM), w2_k, w2_v)


def _bias_from_dist(dist, th_ref, rb_ref):
    n = jnp.maximum(dist, 0)
    outs = [jnp.full(dist.shape, rb_ref[0, h] * LOG2E, F32) for h in range(N_HEADS)]
    for bkt in range(1, N_BUCKETS):
        ge = n >= th_ref[bkt]
        outs = [jnp.where(ge, rb_ref[bkt, h] * LOG2E, o) for h, o in enumerate(outs)]
    return outs


def _nsa_bias_body(th_ref, rb_ref, sb_ref, wb_ref, cb_ref, *, tk, n_far, ncp):
    qb = Q_BLOCK
    ii = lax.broadcasted_iota(jnp.int32, (qb, tk), 0)
    jj = lax.broadcasted_iota(jnp.int32, (qb, tk), 1)
    for e in range(n_far + 1):
        tiles = _bias_from_dist(e * qb + ii - jj, th_ref, rb_ref)
        for h in range(N_HEADS):
            sb_ref[e, h * qb:(h + 1) * qb, :] = tiles[h]
    nwin = WINDOW + qb
    dw = lax.broadcasted_iota(jnp.int32, (qb, nwin), 0) - lax.broadcasted_iota(jnp.int32, (qb, nwin), 1) + WINDOW
    tiles = _bias_from_dist(dw, th_ref, rb_ref)
    okw = (dw >= 0) & (dw < WINDOW)
    for h in range(N_HEADS):
        wb_ref[h * qb:(h + 1) * qb, :] = jnp.where(okw, tiles[h], NEG)
    mm = lax.broadcasted_iota(jnp.int32, (2 * ncp, qb), 0) - ncp
    dc = lax.broadcasted_iota(jnp.int32, (2 * ncp, qb), 1) - (mm * CMP_STRIDE + CMP_LEN - 1)
    tiles = _bias_from_dist(dc, th_ref, rb_ref)
    for h in range(N_HEADS):
        cb_ref[:, h * qb:(h + 1) * qb] = jnp.where(dc >= 0, tiles[h], NEG)


def _nsa_far_tiles(tk):
    last = MAX_EXACT * (MAX_DIST / MAX_EXACT) ** ((N_BUCKETS - MAX_EXACT - 1) / (N_BUCKETS - MAX_EXACT))
    return -(-(int(math.ceil(last)) + 1 + tk - 1) // Q_BLOCK)


def _nsa_bias(thresholds, rel_bias, t, *, tk):
    n_far = _nsa_far_tiles(tk)
    ncp = t // CMP_STRIDE
    hq = N_HEADS * Q_BLOCK
    smem = pl.BlockSpec(memory_space=pltpu.SMEM)
    return pl.pallas_call(
        functools.partial(_nsa_bias_body, tk=tk, n_far=n_far, ncp=ncp),
        in_specs=[smem, smem],
        out_shape=[jax.ShapeDtypeStruct((n_far + 1, hq, tk), F32), jax.ShapeDtypeStruct((hq, WINDOW + Q_BLOCK), F32),
                   jax.ShapeDtypeStruct((2 * ncp, hq), F32)],
        compiler_params=pltpu.CompilerParams(vmem_limit_bytes=VMEM_LIMIT),
        name="nsa_bias",
    )(thresholds, rel_bias)


def _nsa_body(q_ref, sig_ref, kc_ref, vc_ref, ks_ref, vs_ref, kw_ref, vw_ref, ovt_ref, ex_ref, sb_ref, wb_ref,
              cb_ref, o_ref, *, tk, n_far, n_slc, sc2):
    qi = pl.program_id(1)
    qb = Q_BLOCK
    hq = N_HEADS * qb
    s0 = qi * qb
    nwin = WINDOW + qb
    r = tk // qb
    q4 = jnp.concatenate([q_ref[:, h * HEAD_DIM:(h + 1) * HEAD_DIM] for h in range(N_HEADS)], axis=0)

    heads = range(N_HEADS)
    qh = [q_ref[:, h * HEAD_DIM:(h + 1) * HEAD_DIM] for h in heads]

    ncp = kc_ref.shape[0]
    c0 = pl.multiple_of(ncp - (qb // CMP_STRIDE) * qi, 8)
    w0 = pl.multiple_of(s0, qb)
    kwin = kw_ref[pl.ds(w0, nwin), :]
    vwin = vw_ref[pl.ds(w0, nwin), :]
    s_c = _dot_nt(kc_ref[...], q4) * sc2 + cb_ref[pl.ds(c0, ncp), :]
    before_start = s0 - WINDOW + lax.broadcasted_iota(jnp.int32, (qb, nwin), 1) < 0
    s_w = [jnp.where(before_start, NEG, _dot_nt(qh[h], kwin) * sc2 + wb_ref[h * qb:(h + 1) * qb, :]) for h in heads]
    m_c = jnp.maximum(jnp.max(s_c, axis=0, keepdims=True), 0.5 * NEG)
    e_c = jnp.exp2(s_c - m_c)
    e_w = [jnp.exp2(s_w[h] - jnp.max(s_w[h], axis=1, keepdims=True)) for h in heads]
    p_c = (e_c * (1.0 / jnp.maximum(jnp.sum(e_c, axis=0, keepdims=True), 1e-30))).astype(BF16)
    r_w = [1.0 / jnp.sum(e_w[h], axis=1, keepdims=True) for h in heads]
    o_c = _dot_tn(p_c, vc_ref[...])
    imp = _dot(ovt_ref[...], p_c[:, 0:qb])
    for h in range(1, N_HEADS):
        imp = imp + _dot(ovt_ref[...], p_c[:, h * qb:(h + 1) * qb])
    o_w = [_dot(e_w[h].astype(BF16), vwin) * r_w[h] for h in heads]

    jsl = lax.broadcasted_iota(jnp.int32, (n_slc, qb), 0)
    cur = (s0 + lax.broadcasted_iota(jnp.int32, (n_slc, qb), 1)) // SLC_LEN
    forced = (jsl == 0) | (jsl == cur) | (jsl == cur - 1)
    imp = jnp.where(forced, FORCED_SCORE, imp)
    imp = jnp.where(jsl > cur, -1.0, imp)
    groups = range(n_slc // 8)
    impg = [imp[8 * v:8 * v + 8, :] for v in groups]
    jloc = lax.broadcasted_iota(jnp.int32, (8, qb), 0)
    rank = [jnp.zeros((8, qb), jnp.int32) for _ in groups]
    for j2 in range(n_slc):
        row = impg[j2 // 8][j2 % 8:j2 % 8 + 1, :]
        for v in groups:
            if 8 * v > j2:
                beats = row >= impg[v]
            elif 8 * v + 7 < j2:
                beats = row > impg[v]
            else:
                beats = (row > impg[v]) | ((row >= impg[v]) & (jloc > j2 % 8))
            rank[v] = rank[v] + jnp.where(beats, 1, 0)
    selneg = jnp.where(jnp.concatenate(rank, axis=0) < min(N_SELECT, n_slc), 0.0, NEG).astype(BF16)

    it = lax.broadcasted_iota(jnp.int32, (qb, tk), 0)
    jt = lax.broadcasted_iota(jnp.int32, (qb, tk), 1)

    def sel_step(kt, carry, diag):
        m, l, acc = carry
        k0 = pl.multiple_of(kt * tk, tk)
        k = ks_ref[pl.ds(k0, tk), :]
        v = vs_ref[pl.ds(k0, tk), :]
        add = _dot_tn(selneg, ex_ref[:, pl.ds(k0, tk)])
        if diag:
            add = jnp.where(s0 + it >= k0 + jt, add, NEG)
        eb = [jnp.clip(qi - (kt * r + j), 0, n_far) for j in range(r)]
        bias = lambda h: jnp.concatenate([sb_ref[eb[j], h * qb:(h + 1) * qb, :] for j in range(r)], axis=1)
        s = [_dot_nt(qh[h], k) * sc2 + bias(h) + add for h in heads]
        wide = lambda x: jnp.concatenate([x] * r, axis=1)
        m_new = tuple(jnp.maximum(m[h], jnp.max(s[h], axis=1, keepdims=True)) for h in heads)
        alpha = [jnp.exp2(m[h] - m_new[h]) for h in heads]
        p = [jnp.exp2(s[h] - wide(m_new[h])) for h in heads]
        l = tuple(alpha[h] * l[h] + jnp.sum(p[h], axis=1, keepdims=True) for h in heads)
        pv = [_dot(p[h].astype(BF16), v) for h in heads]
        acc = tuple(alpha[h] * acc[h] + pv[h] for h in heads)
        return m_new, l, acc

    init = (tuple(jnp.full((qb, LANE), 0.5 * NEG, F32) for _ in heads),
            tuple(jnp.zeros((qb, LANE), F32) for _ in heads), tuple(jnp.zeros((qb, HEAD_DIM), F32) for _ in heads))
    kd = qi // r
    carry = lax.fori_loop(0, kd, functools.partial(sel_step, diag=False), init)
    _, l_s, acc_s = sel_step(kd, carry, True)
    o_s = [acc_s[h] * (1.0 / jnp.maximum(l_s[h], 1e-30)) for h in heads]

    sig = sig_ref[...]
    for h in heads:
        g0 = sig[:, _S_NSA_G + 3 * h:_S_NSA_G + 3 * h + 1]
        g1 = sig[:, _S_NSA_G + 3 * h + 1:_S_NSA_G + 3 * h + 2]
        g2 = sig[:, _S_NSA_G + 3 * h + 2:_S_NSA_G + 3 * h + 3]
        o_ref[:, h * HEAD_DIM:(h + 1) * HEAD_DIM] = g0 * o_c[h * qb:(h + 1) * qb] + g1 * o_s[h] + g2 * o_w[h]


def _nsa(zbf, sig, kc, vc, kw_pad, vw_pad, overlap_t, expand, sb, wb, cb, b, t, *, tk):
    qb = Q_BLOCK
    nq = t // qb
    n_slc = t // SLC_LEN
    ncp = kc.shape[1]
    hq = N_HEADS * qb
    n_far = sb.shape[0] - 1
    per_b = lambda rows: pl.BlockSpec((None, rows, HEAD_DIM), lambda i, j: (i, 0, 0))
    whole = lambda a: pl.BlockSpec(a.shape, lambda i, j: (0,) * a.ndim)
    return pl.pallas_call(
        functools.partial(_nsa_body, tk=tk, n_far=n_far, n_slc=n_slc, sc2=HEAD_DIM ** -0.5 * LOG2E),
        grid=(b, nq),
        in_specs=[
            pl.BlockSpec((qb, GROUP_W), lambda i, j: (i * nq + j, _COL["nsa_q"] // 4)),
            pl.BlockSpec((qb, LANE), lambda i, j: (i * nq + j, 0)),
            per_b(ncp), per_b(ncp),
            pl.BlockSpec((t, HEAD_DIM), lambda i, j: (i, _COL["nsa_ks"])),
            pl.BlockSpec((t, HEAD_DIM), lambda i, j: (i, _COL["nsa_vs"])),
            per_b(t + WINDOW), per_b(t + WINDOW),
            whole(overlap_t), whole(expand), whole(sb), whole(wb), whole(cb),
        ],
        out_specs=pl.BlockSpec((qb, GROUP_W), lambda i, j: (i * nq + j, 0)),
        out_shape=jax.ShapeDtypeStruct((b * t, GROUP_W), F32),
        compiler_params=_cparams(("parallel", "arbitrary")),
        name="nsa_attn",
    )(zbf, sig, kc, vc, zbf, zbf, kw_pad, vw_pad, overlap_t, expand, sb, wb, cb)


def _outproj_body(h_ref, oa_ref, ob_ref, oc_ref, od_ref, g_ref, w_ref, o_ref):
    y = jnp.concatenate([
        _rms(oa_ref[...], g_ref[0:1, :]), ob_ref[...], _rms(oc_ref[...], g_ref[1:2, :]),
        _rms(od_ref[...], g_ref[2:3, :])], axis=-1)
    o_ref[...] = h_ref[...] + _dot(y.astype(BF16), w_ref[...])


def _outproj(h, oa, ob, oc, od, g, w, l, *, tm=512):
    nt, d = h.shape
    tm = min(tm, nt)
    grp = pl.BlockSpec((tm, GROUP_W), lambda i: (i, 0))
    return pl.pallas_call(
        _outproj_body,
        grid=(nt // tm,),
        in_specs=[pl.BlockSpec((tm, d), lambda i: (i, 0)), grp, grp, grp, grp,
                  pl.BlockSpec((3, GROUP_W), lambda i: (0, 0)),
                  pl.BlockSpec((None, 4 * GROUP_W, d), lambda i: (l, 0, 0))],
        out_specs=pl.BlockSpec((tm, d), lambda i: (i, 0)),
        out_shape=jax.ShapeDtypeStruct((nt, d), F32),
        compiler_params=_cparams(("parallel",)),
        name="outproj",
    )(h, oa, ob, oc, od, g, w)


_IN_WIDTHS = (("fox_q", GROUP_W), ("fox_k", GROUP_W), ("fox_v", GROUP_W), ("fox_f", N_HEADS),
              ("gdn_q", GROUP_W), ("gdn_k", GROUP_W), ("gdn_v", GROUP_W), ("gdn_a", N_HEADS), ("gdn_b", N_HEADS),
              ("gdn_z", GROUP_W), ("lru_x", GROUP_W), ("lru_gate", GROUP_W), ("nsa_q", GROUP_W),
              ("nsa_kc", HEAD_DIM), ("nsa_vc", HEAD_DIM), ("nsa_ks", HEAD_DIM), ("nsa_vs", HEAD_DIM),
              ("nsa_kw", HEAD_DIM), ("nsa_vw", HEAD_DIM), ("nsa_g", 3 * N_HEADS))
D_IN = sum(w for _, w in _IN_WIDTHS)


def _in_pieces():
    small_lane = {"fox_f": _S_FOX_F, "gdn_a": _S_GDN_A, "gdn_b": _S_GDN_B, "nsa_g": _S_NSA_G}
    out, off = [], 0
    for name, width in _IN_WIDTHS:
        if name in small_lane:
            out.append((off, width, 1, _COL["small"] * LANE + small_lane[name]))
        else:
            out.append((off, width, 0 if name in _COLS16 else 1, _COL[name] * LANE))
        off += width
    return out


def _relayout_body(w_ref, o16_ref, o32_ref):
    outs = (o16_ref, o32_ref)
    tc = w_ref.shape[1]
    tail = _COL["small"] * LANE
    small = []
    for src, width, which, dst in _in_pieces():
        if dst >= tail and which == 1:
            small.append(w_ref[src:src + width, :])
        else:
            outs[which][dst:dst + width, :] = w_ref[src:src + width, :].astype(outs[which].dtype)
    used = sum(s.shape[0] for s in small)
    small.append(jnp.zeros((W32 - tail - used, tc), F32))
    o32_ref[tail:, :] = jnp.concatenate(small, axis=0).astype(o32_ref.dtype)


def _relayout_w_in(w_t, *, tc=256):
    nl, d_in, d = w_t.shape
    return pl.pallas_call(
        _relayout_body,
        grid=(nl, d // tc),
        in_specs=[pl.BlockSpec((None, d_in, tc), lambda l, i: (l, 0, i))],
        out_specs=[pl.BlockSpec((None, W16, tc), lambda l, i: (l, 0, i)),
                   pl.BlockSpec((None, W32, tc), lambda l, i: (l, 0, i))],
        out_shape=[jax.ShapeDtypeStruct((nl, W16, d), BF16), jax.ShapeDtypeStruct((nl, W32, d), BF16)],
        compiler_params=_cparams(("parallel", "parallel")),
        name="w_in_relayout",
    )(w_t)


def _pad_cols(w):
    src, off = {}, 0
    for name, width in _IN_WIDTHS:
        src[name] = w[..., off:off + width]
        off += width
    zeros = lambda n: jnp.zeros(w.shape[:-1] + (n,), w.dtype)
    small = [src["fox_f"], src["gdn_a"], src["gdn_b"], src["nsa_g"]]
    src["small"] = jnp.concatenate(small + [zeros(LANE - sum(s.shape[-1] for s in small))], axis=-1)
    w16 = jnp.concatenate([src[name] for name in _COLS16], axis=-1)
    w32 = jnp.concatenate([src[name] for name in _COLS32] + [zeros(W32 - (_COL["small"] + 1) * LANE)], axis=-1)
    assert w16.shape[-1] == W16 and w32.shape[-1] == W32
    return w16, w32


def _block_diag(w):
    l, nb, bw, _ = w.shape
    eye = jnp.eye(nb, dtype=w.dtype)
    return jnp.einsum("lnde,nm->lndme", w, eye).reshape(l, nb * bw, nb * bw)


def _bucket_thresholds(t):
    n = jnp.arange(max(t, 2 * MAX_DIST), dtype=jnp.int32)
    nf = jnp.maximum(n, 1).astype(F32)
    large = MAX_EXACT + (jnp.log(nf / MAX_EXACT) / math.log(MAX_DIST / MAX_EXACT)
                         * (N_BUCKETS - MAX_EXACT)).astype(jnp.int32)
    large = jnp.minimum(large, N_BUCKETS - 1)
    bucket = jnp.where(n < MAX_EXACT, n, large)
    return jnp.sum(bucket[None, :] < jnp.arange(N_BUCKETS, dtype=jnp.int32)[:, None], axis=1).astype(jnp.int32)


def _nsa_tile(t):
    return min(1024, t)


def _mixer(h, l, p, b, t):
    zbf = _inproj(h, p["mix_norm_g"][l], p["w16"], p["b16"], l, BF16, tn=W16 // 2)
    z32 = _inproj(h, p["mix_norm_g"][l], p["w32"], p["b32"], l, F32, tn=W32 // 7)

    ccol = _fox_gate(z32, b, t)
    crow = jnp.pad(ccol[:, ::LANE].reshape(b, t, N_HEADS).transpose(0, 2, 1), ((0, 0), (0, 8 - N_HEADS), (0, 0)))
    o_a = _fox(zbf, ccol, crow, b, t)

    qkv, gdn_g, sig = _gdn_prep(z32, p["gdn_conv_w"][l], p["gdn_a_log"][l], p["gdn_dt_bias"][l], b, t)
    c = GDN_CHUNK
    grow = gdn_g[:, _S_GDN_A:_S_GDN_A + N_HEADS].reshape(b * t // c, c, N_HEADS).transpose(0, 2, 1)
    grow = jnp.pad(grow, ((0, 0), (0, 8 - N_HEADS), (0, 0)))
    u, w, qg, kd, qk, gc = _gdn_intra(qkv, gdn_g, grow, sig, b, t)
    o_b = _gdn_scan(u, w, qg, kd, qk, gc, z32, p["gdn_norm_g"][l], b, t)

    o_c = _lru(z32, p["lru_conv_w"][l], p["lru_conv_b"][l], p["lru_wa_bd"][l], p["lru_b_a"][l],
               p["lru_wx_bd"][l], p["lru_b_x"][l], p["lru_lambda"][l], b, t)

    col = lambda name: z32[:, _COL[name] * LANE:(_COL[name] + 1) * LANE]
    cmp_rows = lambda a: a.reshape(b, t // CMP_STRIDE, CMP_STRIDE * HEAD_DIM)
    kc, vc = _nsa_compress(cmp_rows(col("nsa_kc")), cmp_rows(col("nsa_vc")), p["nsa_pe_k"][l], p["nsa_pe_v"][l],
                           p["nsa_w1_k16"][l], p["nsa_w1_v16"][l], p["nsa_w2_k16"][l], p["nsa_w2_v16"][l])
    colbf = lambda name: zbf[:, _COL[name] * LANE:(_COL[name] + 1) * LANE].reshape(b, t, HEAD_DIM)
    kw_pad = jnp.pad(colbf("nsa_kw"), ((0, 0), (WINDOW, 0), (0, 0)))
    vw_pad = jnp.pad(colbf("nsa_vw"), ((0, 0), (WINDOW, 0), (0, 0)))
    o_d = _nsa(zbf, sig, kc, vc, kw_pad, vw_pad, p["overlap_t"], p["expand"], p["sb"], p["wb"], p["cb"], b, t,
               tk=_nsa_tile(t))

    return _outproj(h, o_a, o_b, o_c, o_d, p["out_norm_g"][l], p["w_out16"], l)


def _mixer_consts(rel_bias, t):
    n_cmp_rows = t // CMP_STRIDE
    n_slc = t // SLC_LEN
    cn = jnp.arange(n_cmp_rows)[None, :] * CMP_STRIDE
    sj = jnp.arange(n_slc)[:, None] * SLC_LEN
    sb, wb, cb = _nsa_bias(_bucket_thresholds(t), rel_bias, t, tk=Q_BLOCK)
    return dict(
        overlap_t=((cn <= sj + SLC_LEN - 1) & (cn + CMP_LEN - 1 >= sj)).astype(BF16),
        expand=(jnp.arange(t)[None, :] // SLC_LEN == jnp.arange(n_slc)[:, None]).astype(BF16),
        sb=sb, wb=wb, cb=cb)


def kernel(x, ffn1_norm_g, ffn1_w_gate, ffn1_w_up, ffn1_w_down, mix_norm_g, w_in, b_in, gdn_conv_w, gdn_a_log, gdn_dt_bias, gdn_norm_g, lru_conv_w, lru_conv_b, lru_w_a, lru_b_a, lru_w_x, lru_b_x, lru_lambda, nsa_pe_k, nsa_w1_k, nsa_w2_k, nsa_pe_v, nsa_w1_v, nsa_w2_v, rel_bias, out_norm_g, w_out, ffn2_norm_g, ffn2_w_gate, ffn2_w_up, ffn2_w_down, final_norm_g):
    b, t, d = x.shape
    depth = w_in.shape[0]
    assert w_in.shape[-1] == D_IN
    w16, w32 = _relayout_w_in(jnp.swapaxes(w_in, 1, 2))
    b16, b32 = _pad_cols(b_in[:, None, :])
    p = dict(
        mix_norm_g=mix_norm_g, w16=w16, w32=w32, b16=b16, b32=b32,
        gdn_conv_w=gdn_conv_w, gdn_a_log=gdn_a_log, gdn_dt_bias=gdn_dt_bias, gdn_norm_g=gdn_norm_g,
        lru_conv_w=lru_conv_w, lru_conv_b=lru_conv_b, lru_wa_bd=_block_diag(lru_w_a).astype(BF16), lru_b_a=lru_b_a,
        lru_wx_bd=_block_diag(lru_w_x).astype(BF16), lru_b_x=lru_b_x, lru_lambda=lru_lambda,
        nsa_pe_k=nsa_pe_k, nsa_pe_v=nsa_pe_v, nsa_w1_k16=nsa_w1_k.astype(BF16), nsa_w1_v16=nsa_w1_v.astype(BF16),
        nsa_w2_k16=nsa_w2_k.astype(BF16), nsa_w2_v16=nsa_w2_v.astype(BF16),
        out_norm_g=out_norm_g, w_out16=w_out.astype(BF16),
        **_mixer_consts(rel_bias, t),
    )
    f1 = (ffn1_w_gate.astype(BF16), ffn1_w_up.astype(BF16), ffn1_w_down.astype(BF16))
    f2 = (ffn2_w_gate.astype(BF16), ffn2_w_up.astype(BF16), ffn2_w_down.astype(BF16))
    h = x.reshape(b * t, d)
    for l in range(depth):
        h = _ffn(h, ffn1_norm_g[l], *f1, l)
        h = _mixer(h, l, p, b, t)
        h = _ffn(h, ffn2_norm_g[l], *f2, l, final_g=final_norm_g if l == depth - 1 else None)
    return h.reshape(b, t, d)
```

```python
import functools
import math

import jax
import jax.numpy as jnp
from jax import lax
from jax.experimental import pallas as pl
from jax.experimental.pallas import tpu as pltpu

F32 = jnp.float32
BF16 = jnp.bfloat16

EPS = 1e-6
LANE = 128
HEAD_DIM = 128
N_HEADS = 4
GROUP_W = N_HEADS * HEAD_DIM
GDN_CHUNK = 64
CONV_W = 4
LRU_BLOCKS = 8
LRU_C = 8.0
CMP_LEN = 32
CMP_STRIDE = 16
SLC_LEN = 64
N_SELECT = 16
WINDOW = 512
FORCED_SCORE = 1.0e6
N_BUCKETS = 32
MAX_EXACT = 16
MAX_DIST = 1024
Q_BLOCK = 128
NEG = -1.0e30
LOG2E = math.log2(math.e)
VMEM_LIMIT = 56 * 1024 * 1024

_COLS16 = ("fox_q", "fox_k", "fox_v", "nsa_q", "nsa_ks", "nsa_vs", "nsa_kw", "nsa_vw")
_COLS32 = ("gdn_q", "gdn_k", "gdn_v", "gdn_z", "lru_x", "lru_gate", "nsa_kc", "nsa_vc", "small")
_COL = dict(fox_q=0, fox_k=4, fox_v=8, nsa_q=12, nsa_ks=16, nsa_vs=17, nsa_kw=18, nsa_vw=19,
            gdn_q=0, gdn_k=4, gdn_v=8, gdn_z=12, lru_x=16, lru_gate=20, nsa_kc=24, nsa_vc=25, small=26)
W16 = 20 * LANE
W32 = 28 * LANE
_S_FOX_F, _S_GDN_A, _S_GDN_B, _S_NSA_G = 0, 4, 8, 12


def _cparams(sem):
    return pltpu.CompilerParams(dimension_semantics=sem, vmem_limit_bytes=VMEM_LIMIT)


def _rms(x, g):
    return x * lax.rsqrt(jnp.mean(x * x, axis=-1, keepdims=True) + EPS) * g


def _softplus(x):
    return jnp.maximum(x, 0.0) + jnp.log1p(jnp.exp(-jnp.abs(x)))


def _gelu_tanh(x):
    return 0.5 * x * (1.0 + jnp.tanh(math.sqrt(2.0 / math.pi) * (x + 0.044715 * (x * x * x))))


def _dot(a, b):
    return jnp.dot(a, b, preferred_element_type=F32)


def _dot_nt(a, b):
    return lax.dot_general(a, b, (((1,), (1,)), ((), ())), preferred_element_type=F32)


def _dot_tn(a, b):
    return lax.dot_general(a, b, (((0,), (0,)), ((), ())), preferred_element_type=F32)


def _dot_hi(a, b):
    return jnp.dot(a, b, preferred_element_type=F32, precision=lax.Precision.HIGHEST)


def _split(x):
    hi = x.astype(BF16)
    return hi, (x - hi.astype(F32)).astype(BF16)


def _dot3(a, b):
    return _dot(a[0], b[0]) + (_dot(a[0], b[1]) + _dot(a[1], b[0]))


def _ffn_body(*refs, final):
    if final:
        x_ref, g_ref, wg_ref, wu_ref, wd_ref, fg_ref, o_ref, n_scr = refs
    else:
        x_ref, g_ref, wg_ref, wu_ref, wd_ref, o_ref, n_scr = refs
    j = pl.program_id(1)

    @pl.when(j == 0)
    def _():
        n_scr[...] = _rms(x_ref[...], g_ref[...]).astype(BF16)
        o_ref[...] = jnp.zeros_like(o_ref)

    n = n_scr[...]
    gate = _dot(n, wg_ref[...])
    up = _dot(n, wu_ref[...])
    a = (gate * jax.nn.sigmoid(gate) * up).astype(BF16)
    o_ref[...] += _dot(a, wd_ref[...])

    @pl.when(j == pl.num_programs(1) - 1)
    def _():
        y = x_ref[...] + 0.5 * o_ref[...]
        if final:
            y = _rms(y, fg_ref[...])
        o_ref[...] = y


def _ffn(x, g, wg, wu, wd, l, final_g=None, *, tm=512, tf=512):
    nt, d = x.shape
    f = wg.shape[-1]
    tm = min(tm, nt)
    final = final_g is not None
    in_specs = [
        pl.BlockSpec((tm, d), lambda i, j: (i, 0)),
        pl.BlockSpec((1, d), lambda i, j: (0, 0)),
        pl.BlockSpec((None, d, tf), lambda i, j: (l, 0, j)),
        pl.BlockSpec((None, d, tf), lambda i, j: (l, 0, j)),
        pl.BlockSpec((None, tf, d), lambda i, j: (l, j, 0)),
    ]
    args = [x, g.reshape(1, d), wg, wu, wd]
    if final:
        in_specs.append(pl.BlockSpec((1, d), lambda i, j: (0, 0)))
        args.append(final_g.reshape(1, d))
    return pl.pallas_call(
        functools.partial(_ffn_body, final=final),
        grid=(nt // tm, f // tf),
        in_specs=in_specs,
        out_specs=pl.BlockSpec((tm, d), lambda i, j: (i, 0)),
        out_shape=jax.ShapeDtypeStruct((nt, d), F32),
        scratch_shapes=[pltpu.VMEM((tm, d), BF16)],
        compiler_params=_cparams(("parallel", "arbitrary")),
        name="ffn",
    )(*args)


def _inproj_body(x_ref, g_ref, w16_ref, b16_ref, w32_ref, b32_ref, o16_ref, o32_ref, n_scr, *, n16):
    j = pl.program_id(1)

    @pl.when(j == 0)
    def _():
        n_scr[...] = _rms(x_ref[...], g_ref[...]).astype(BF16)

    @pl.when(j < n16)
    def _():
        o16_ref[...] = (_dot_nt(n_scr[...], w16_ref[...]) + b16_ref[...]).astype(o16_ref.dtype)

    @pl.when(j >= n16)
    def _():
        o32_ref[...] = _dot_nt(n_scr[...], w32_ref[...]) + b32_ref[...]


def _inproj(x, g, w16, b16, w32, b32, l, *, tm=1024, n16=2, n32=4):
    nt, d = x.shape
    tm = min(tm, nt)
    t16, t32 = w16.shape[1] // n16, w32.shape[1] // n32
    j16 = lambda j: jnp.minimum(j, n16 - 1)
    j32 = lambda j: jnp.maximum(j - n16, 0)
    return pl.pallas_call(
        functools.partial(_inproj_body, n16=n16),
        grid=(nt // tm, n16 + n32),
        in_specs=[
            pl.BlockSpec((tm, d), lambda i, j: (i, 0)),
            pl.BlockSpec((1, d), lambda i, j: (0, 0)),
            pl.BlockSpec((None, t16, d), lambda i, j: (l, j16(j), 0)),
            pl.BlockSpec((None, 1, t16), lambda i, j: (l, 0, j16(j))),
            pl.BlockSpec((None, t32, d), lambda i, j: (l, j32(j), 0)),
            pl.BlockSpec((None, 1, t32), lambda i, j: (l, 0, j32(j))),
        ],
        out_specs=[pl.BlockSpec((tm, t16), lambda i, j: (i, j16(j))),
                   pl.BlockSpec((tm, t32), lambda i, j: (i, j32(j)))],
        out_shape=[jax.ShapeDtypeStruct((nt, w16.shape[1]), BF16), jax.ShapeDtypeStruct((nt, w32.shape[1]), F32)],
        scratch_shapes=[pltpu.VMEM((tm, d), BF16)],
        compiler_params=_cparams(("parallel", "arbitrary")),
        name="inproj",
    )(x, g.reshape(1, d), w16, b16, w32, b32)


def _cumsum_rows(y):
    rows = lax.broadcasted_iota(jnp.int32, y.shape, 0)
    s = 1
    while s < y.shape[0]:
        y = y + jnp.where(rows >= s, pltpu.roll(y, s, 0), 0.0)
        s *= 2
    return y


def _fox_gate_body(f_ref, c_ref):
    x = f_ref[...]
    log_f = jnp.minimum(x, 0.0) - jnp.log1p(jnp.exp(-jnp.abs(x)))
    c = _cumsum_rows(log_f) * LOG2E
    for h in range(N_HEADS):
        c_ref[:, h * LANE:(h + 1) * LANE] = jnp.broadcast_to(c[:, _S_FOX_F + h:_S_FOX_F + h + 1], (c.shape[0], LANE))


def _fox_gate(z32, b, t):
    return pl.pallas_call(
        _fox_gate_body,
        grid=(b,),
        in_specs=[pl.BlockSpec((t, LANE), lambda i: (i, _COL["small"]))],
        out_specs=pl.BlockSpec((t, N_HEADS * LANE), lambda i: (i, 0)),
        out_shape=jax.ShapeDtypeStruct((b * t, N_HEADS * LANE), F32),
        compiler_params=_cparams(("parallel",)),
        name="fox_gate",
    )(z32)


def _fox_body(q_ref, k_ref, v_ref, ccol_ref, crow_ref, o_ref, m_scr, l_scr, acc_scr, *, tq, tk, sc2):
    qi = pl.program_id(1)
    rows = qi * tq + lax.broadcasted_iota(jnp.int32, (tq, tk), 0)
    cols = lax.broadcasted_iota(jnp.int32, (tq, tk), 1)
    m_scr[...] = jnp.full_like(m_scr, NEG)
    l_scr[...] = jnp.zeros_like(l_scr)
    acc_scr[...] = jnp.zeros_like(acc_scr)

    def step(kt, masked):
        k0 = pl.multiple_of(kt * tk, tk)
        heads = range(N_HEADS)
        hsl = [slice(h * HEAD_DIM, (h + 1) * HEAD_DIM) for h in heads]
        wide = lambda x: jnp.concatenate([x] * (tk // LANE), axis=1)
        s = [_dot_nt(q_ref[:, hsl[h]], k_ref[pl.ds(k0, tk), hsl[h]]) * sc2 - crow_ref[h:h + 1, pl.ds(k0, tk)]
             for h in heads]
        if masked:
            causal = rows >= k0 + cols
            s = [jnp.where(causal, x, NEG) for x in s]
        cq = [ccol_ref[:, hsl[h]] for h in heads]
        m_old = [m_scr[h] for h in heads]
        m_new = [jnp.maximum(m_old[h], jnp.max(s[h], axis=1, keepdims=True) + cq[h]) for h in heads]
        alpha = [jnp.exp2(m_old[h] - m_new[h]) for h in heads]
        p = [jnp.exp2(s[h] + wide(cq[h] - m_new[h])) for h in heads]
        l_new = [alpha[h] * l_scr[h] + jnp.sum(p[h], axis=1, keepdims=True) for h in heads]
        pv = [_dot(p[h].astype(BF16), v_ref[pl.ds(k0, tk), hsl[h]]) for h in heads]
        acc_new = [alpha[h] * acc_scr[h] + pv[h] for h in heads]
        for h in heads:
            m_scr[h] = m_new[h]
            l_scr[h] = l_new[h]
            acc_scr[h] = acc_new[h]

    def body(kt, carry):
        step(kt, False)
        return carry

    kd = (qi * tq) // tk
    lax.fori_loop(0, kd, body, 0)
    step(kd, True)
    for h in range(N_HEADS):
        o_ref[:, h * HEAD_DIM:(h + 1) * HEAD_DIM] = acc_scr[h] / l_scr[h]


def _fox(zbf, ccol, crow, b, t, *, tq=256, tk=1024):
    tq, tk = min(tq, t), min(tk, t)
    assert tk % tq == 0
    nq = t // tq
    return pl.pallas_call(
        functools.partial(_fox_body, tq=tq, tk=tk, sc2=HEAD_DIM ** -0.5 * LOG2E),
        grid=(b, nq),
        in_specs=[
            pl.BlockSpec((tq, GROUP_W), lambda i, j: (i * nq + j, _COL["fox_q"] // 4)),
            pl.BlockSpec((t, GROUP_W), lambda i, j: (i, _COL["fox_k"] // 4)),
            pl.BlockSpec((t, GROUP_W), lambda i, j: (i, _COL["fox_v"] // 4)),
            pl.BlockSpec((tq, N_HEADS * LANE), lambda i, j: (i * nq + j, 0)),
            pl.BlockSpec((None, 8, t), lambda i, j: (i, 0, 0)),
        ],
        out_specs=pl.BlockSpec((tq, GROUP_W), lambda i, j: (i * nq + j, 0)),
        out_shape=jax.ShapeDtypeStruct((b * t, GROUP_W), F32),
        scratch_shapes=[pltpu.VMEM((N_HEADS, tq, LANE), F32), pltpu.VMEM((N_HEADS, tq, LANE), F32),
                        pltpu.VMEM((N_HEADS, tq, HEAD_DIM), F32)],
        compiler_params=_cparams(("parallel", "arbitrary")),
        name="fox_attn",
    )(zbf, zbf, zbf, ccol, crow)


def _causal_conv(x, halo, w_ref):
    tt = x.shape[0]
    xx = jnp.concatenate([halo, x], axis=0)
    y = x * w_ref[CONV_W - 1:CONV_W, :]
    for d in range(1, CONV_W):
        y = y + pltpu.roll(xx, d, 0)[8:8 + tt] * w_ref[CONV_W - 1 - d:CONV_W - d, :]
    return y


def _gdn_prep_body(x_ref, halo_ref, s_ref, w_ref, alog_ref, dt_ref, qkv_ref, g_ref, sig_ref):
    first = pl.program_id(1) == 0
    x = x_ref[...]
    halo = jnp.where(first, 0.0, halo_ref[...])
    y = _causal_conv(x, halo, w_ref)
    y = y * jax.nn.sigmoid(y)
    for h in range(N_HEADS):
        for part, post in ((0, HEAD_DIM ** -0.5), (1, 1.0)):
            cs = slice(part * GROUP_W + h * HEAD_DIM, part * GROUP_W + (h + 1) * HEAD_DIM)
            u = y[:, cs]
            un = u * lax.rsqrt(jnp.sum(u * u, axis=-1, keepdims=True) + EPS)
            qkv_ref[:, cs] = un * post if part == 0 else un
    qkv_ref[:, 2 * GROUP_W:] = y[:, 2 * GROUP_W:]
    s = s_ref[...]
    g_ref[...] = -jnp.exp(alog_ref[...]) * _softplus(s + dt_ref[...])
    sig_ref[...] = jax.nn.sigmoid(s)


def _gdn_prep(z32, conv_w, a_log, dt_bias, b, t, *, tt=512):
    tt = min(tt, t)
    nt = t // tt
    w3 = 3 * GROUP_W
    pad = lambda v: jnp.zeros((1, LANE), F32).at[0, _S_GDN_A:_S_GDN_A + N_HEADS].set(v)
    cb = _COL["gdn_q"] // 12
    return pl.pallas_call(
        _gdn_prep_body,
        grid=(b, nt),
        in_specs=[
            pl.BlockSpec((tt, w3), lambda i, j: (i * nt + j, cb)),
            pl.BlockSpec((8, w3), lambda i, j: (jnp.maximum((i * nt + j) * (tt // 8) - 1, 0), cb)),
            pl.BlockSpec((tt, LANE), lambda i, j: (i * nt + j, _COL["small"])),
            pl.BlockSpec((CONV_W, w3), lambda i, j: (0, 0)),
            pl.BlockSpec((1, LANE), lambda i, j: (0, 0)),
            pl.BlockSpec((1, LANE), lambda i, j: (0, 0)),
        ],
        out_specs=[
            pl.BlockSpec((tt, w3), lambda i, j: (i * nt + j, 0)),
            pl.BlockSpec((tt, LANE), lambda i, j: (i * nt + j, 0)),
            pl.BlockSpec((tt, LANE), lambda i, j: (i * nt + j, 0)),
        ],
        out_shape=[jax.ShapeDtypeStruct((b * t, w3), F32), jax.ShapeDtypeStruct((b * t, LANE), F32),
                   jax.ShapeDtypeStruct((b * t, LANE), F32)],
        compiler_params=_cparams(("parallel", "arbitrary")),
        name="gdn_prep",
    )(z32, z32, z32, conv_w, pad(a_log), pad(dt_bias))


def _gdn_intra_body(qkv_ref, g_ref, grow_ref, sig_ref, u_ref, w_ref, qg_ref, kd_ref, qk_ref, gc_ref, *, c, nch):
    ri = lax.broadcasted_iota(jnp.int32, (c, c), 0)
    ci = lax.broadcasted_iota(jnp.int32, (c, c), 1)
    tril = ri >= ci
    strict = ri > ci
    eye = (ri == ci).astype(F32)
    lower = tril.astype(F32)
    upper = (ri <= ci).astype(F32)
    chains = [(ch, h) for ch in range(nch) for h in range(N_HEADS)]
    rsl = lambda ch: slice(ch * c, (ch + 1) * c)
    hsl = lambda h, part=0: slice(part * GROUP_W + h * HEAD_DIM, part * GROUP_W + (h + 1) * HEAD_DIM)
    gc_cols = [_dot_hi(lower, g_ref[rsl(ch), :]) for ch in range(nch)]
    gc_rows = [_dot_hi(grow_ref[ch], upper) for ch in range(nch)]
    for ch in range(nch):
        gc_ref[rsl(ch), :] = gc_cols[ch]
    q = [qkv_ref[rsl(ch), hsl(h, 0)] for ch, h in chains]
    k = [qkv_ref[rsl(ch), hsl(h, 1)] for ch, h in chains]
    v = [qkv_ref[rsl(ch), hsl(h, 2)] for ch, h in chains]
    beta = [sig_ref[rsl(ch), _S_GDN_B + h:_S_GDN_B + h + 1] for ch, h in chains]
    gc = [gc_cols[ch][:, _S_GDN_A + h:_S_GDN_A + h + 1] for ch, h in chains]
    gr = [gc_rows[ch][h:h + 1, :] for ch, h in chains]
    idx = range(len(chains))
    decay = [jnp.where(tril, jnp.exp(jnp.where(tril, gc[i] - gr[i], 0.0)), 0.0) for i in idx]
    kb = [k[i] * beta[i] for i in idx]
    k16 = [x.astype(BF16) for x in k]
    a = [jnp.where(strict, _dot_nt(kb[i].astype(BF16), k16[i]) * decay[i], 0.0) for i in idx]
    same = lambda s: (ri // s) == (ci // s)
    base = 8
    pw = [-jnp.where(same(base), x, 0.0) for x in a]
    tm = [eye + x for x in pw]
    n = 2
    while n < base:
        pws = [_split(x) for x in pw]
        pw = [_dot3(x, x) for x in pws]
        tm = [tm[i] + _dot3(_split(tm[i]), _split(pw[i])) for i in idx]
        n *= 2
    s = base
    while s < c:
        join = same(2 * s) & jnp.logical_not(same(s))
        tms = [_split(x) for x in tm]
        et = [_dot3(_split(jnp.where(join, a[i], 0.0)), tms[i]) for i in idx]
        tm = [tm[i] - _dot3(tms[i], _split(et[i])) for i in idx]
        s *= 2
    tm16 = [x.astype(BF16) for x in tm]
    u = [_dot(tm16[i], (v[i] * beta[i]).astype(BF16)) for i in idx]
    w = [_dot(tm16[i], (kb[i] * jnp.exp(gc[i])).astype(BF16)) for i in idx]
    qk = [_dot_nt(q[i].astype(BF16), k16[i]) * decay[i] for i in idx]
    for i, (ch, h) in enumerate(chains):
        rs, hs = rsl(ch), hsl(h)
        gl = gc[i][c - 1:c, :]
        u_ref[rs, hs] = u[i]
        w_ref[rs, hs] = w[i].astype(BF16)
        qk_ref[rs, h * c:(h + 1) * c] = qk[i].astype(BF16)
        qg_ref[rs, hs] = (q[i] * jnp.exp(gc[i])).astype(BF16)
        kd_ref[rs, hs] = (k[i] * jnp.exp(gl - gc[i])).astype(BF16)


def _gdn_intra(qkv, g, grow, sig, b, t, *, c=GDN_CHUNK, nch=8):
    nt = b * t
    rows = nch * c
    tok = lambda w: pl.BlockSpec((rows, w), lambda i: (i, 0))
    return pl.pallas_call(
        functools.partial(_gdn_intra_body, c=c, nch=nch),
        grid=(nt // rows,),
        in_specs=[tok(3 * GROUP_W), tok(LANE), pl.BlockSpec((nch, 8, c), lambda i: (i, 0, 0)), tok(LANE)],
        out_specs=[tok(GROUP_W), tok(GROUP_W), tok(GROUP_W), tok(GROUP_W), tok(N_HEADS * c), tok(LANE)],
        out_shape=[jax.ShapeDtypeStruct((nt, GROUP_W), F32), jax.ShapeDtypeStruct((nt, GROUP_W), BF16),
                   jax.ShapeDtypeStruct((nt, GROUP_W), BF16), jax.ShapeDtypeStruct((nt, GROUP_W), BF16),
                   jax.ShapeDtypeStruct((nt, N_HEADS * c), BF16), jax.ShapeDtypeStruct((nt, LANE), F32)],
        compiler_params=_cparams(("parallel",)),
        name="gdn_intra",
    )(qkv, g, grow, sig)


def _gdn_scan_body(u_ref, w_ref, qg_ref, kd_ref, qk_ref, gc_ref, z_ref, ng_ref, o_ref, s_scr, *, c, nch, nb):
    @pl.when(pl.program_id(0) == 0)
    def _():
        s_scr[...] = jnp.zeros_like(s_scr)

    chains = [(b, h) for b in range(nb) for h in range(N_HEADS)]
    idx = range(len(chains))
    hsl = lambda h: slice(h * HEAD_DIM, (h + 1) * HEAD_DIM)
    s = [s_scr[i] for i in idx]
    for ch in range(nch):
        rs = slice(ch * c, (ch + 1) * c)
        egl = [jnp.exp(gc_ref[b, (ch + 1) * c - 1:(ch + 1) * c, :]) for b in range(nb)]
        s16 = [x.astype(BF16) for x in s]
        ws = [_dot(w_ref[b, rs, hsl(h)], s16[i]) for i, (b, h) in enumerate(chains)]
        v_new = [(u_ref[b, rs, hsl(h)] - ws[i]).astype(BF16) for i, (b, h) in enumerate(chains)]
        s = [s[i] * egl[b][:, _S_GDN_A + h:_S_GDN_A + h + 1] + _dot_tn(kd_ref[b, rs, hsl(h)], v_new[i])
             for i, (b, h) in enumerate(chains)]
        o = [_dot(qg_ref[b, rs, hsl(h)], s16[i]) + _dot(qk_ref[b, rs, h * c:(h + 1) * c], v_new[i])
             for i, (b, h) in enumerate(chains)]
        for i, (b, h) in enumerate(chains):
            gz = z_ref[b, rs, hsl(h)]
            o_ref[b, rs, hsl(h)] = _rms(o[i], ng_ref[...]) * (gz * jax.nn.sigmoid(gz))
    for i in idx:
        s_scr[i] = s[i]


def _gdn_scan(u, w, qg, kd, qk, gc, z32, norm_g, b, t, *, c=GDN_CHUNK, nch=4):
    rows = nch * c
    v3 = lambda a: a.reshape(b, t, a.shape[-1])
    blk = lambda wd, cb=0: pl.BlockSpec((b, rows, wd), lambda i: (0, i, cb))
    out = pl.pallas_call(
        functools.partial(_gdn_scan_body, c=c, nch=nch, nb=b),
        grid=(t // rows,),
        in_specs=[blk(GROUP_W), blk(GROUP_W), blk(GROUP_W), blk(GROUP_W), blk(N_HEADS * c), blk(LANE),
                  blk(GROUP_W, _COL["gdn_z"] // 4), pl.BlockSpec((1, HEAD_DIM), lambda i: (0, 0))],
        out_specs=blk(GROUP_W),
        out_shape=jax.ShapeDtypeStruct((b, t, GROUP_W), F32),
        scratch_shapes=[pltpu.VMEM((b * N_HEADS, HEAD_DIM, HEAD_DIM), F32)],
        compiler_params=_cparams(("arbitrary",)),
        name="gdn_scan",
    )(v3(u), v3(w), v3(qg), v3(kd), v3(qk), v3(gc), v3(z32), norm_g.reshape(1, HEAD_DIM))
    return out.reshape(b * t, GROUP_W)


def _lru_body(x_ref, halo_ref, gate_ref, cw_ref, cb_ref, wa_ref, ba_ref, wx_ref, bx_ref, lam_ref, o_ref, h_scr):
    first = pl.program_id(1) == 0

    @pl.when(first)
    def _():
        h_scr[...] = jnp.zeros_like(h_scr)

    x = x_ref[...]
    tt = x.shape[0]
    halo = jnp.where(first, 0.0, halo_ref[...])
    xc = _causal_conv(x, halo, cw_ref) + cb_ref[...]
    xc16 = xc.astype(BF16)
    r = jax.nn.sigmoid(_dot(xc16, wa_ref[...]) + ba_ref[...])
    i = jax.nn.sigmoid(_dot(xc16, wx_ref[...]) + bx_ref[...])
    log_a = -LRU_C * r * _softplus(-lam_ref[...])
    a = jnp.exp(log_a)
    u = jnp.sqrt(-jnp.tanh(log_a) * (a * a + 1.0)) * (i * xc)
    rows = lax.broadcasted_iota(jnp.int32, a.shape, 0) % 8
    s = 1
    while s < 8:
        keep = rows >= s
        u = a * jnp.where(keep, pltpu.roll(u, s, 0), 0.0) + u
        a = a * jnp.where(keep, pltpu.roll(a, s, 0), 1.0)
        s *= 2
    gate = _gelu_tanh(gate_ref[...])
    h = h_scr[0:1, :]
    for g in range(tt // 8):
        rs = slice(8 * g, 8 * g + 8)
        hg = a[rs, :] * h + u[rs, :]
        o_ref[rs, :] = hg * gate[rs, :]
        h = hg[7:8, :]
    h_scr[0:1, :] = h


def _lru(z32, conv_w, conv_b, wa_bd, b_a, wx_bd, b_x, lam, b, t, *, tt=512):
    tt = min(tt, t)
    nt = t // tt
    w = GROUP_W
    row = lambda v: v.reshape(1, w)
    vec = pl.BlockSpec((1, w), lambda i, j: (0, 0))
    mat = pl.BlockSpec((w, w), lambda i, j: (0, 0))
    cx = _COL["lru_x"] // 4
    return pl.pallas_call(
        _lru_body,
        grid=(b, nt),
        in_specs=[
            pl.BlockSpec((tt, w), lambda i, j: (i * nt + j, cx)),
            pl.BlockSpec((8, w), lambda i, j: (jnp.maximum((i * nt + j) * (tt // 8) - 1, 0), cx)),
            pl.BlockSpec((tt, w), lambda i, j: (i * nt + j, _COL["lru_gate"] // 4)),
            pl.BlockSpec((CONV_W, w), lambda i, j: (0, 0)),
            vec, mat, vec, mat, vec, vec,
        ],
        out_specs=pl.BlockSpec((tt, w), lambda i, j: (i * nt + j, 0)),
        out_shape=jax.ShapeDtypeStruct((b * t, w), F32),
        scratch_shapes=[pltpu.VMEM((8, w), F32)],
        compiler_params=_cparams(("parallel", "arbitrary")),
        name="rg_lru",
    )(z32, z32, z32, conv_w, row(conv_b), wa_bd, row(b_a), wx_bd, row(b_x), row(lam))


def _nsa_cmp_body(rk_ref, rv_ref, pek_ref, pev_ref, w1k_ref, w1v_ref, w2k_ref, w2v_ref, kc_ref, vc_ref):
    for r_ref, pe_ref, w1_ref, w2_ref, o_ref in ((rk_ref, pek_ref, w1k_ref, w2k_ref, kc_ref),
                                                 (rv_ref, pev_ref, w1v_ref, w2v_ref, vc_ref)):
        r = r_ref[...]
        nr = r.shape[0]
        lo = (r + pe_ref[0:1, :]).astype(BF16)
        hi = (pltpu.roll(r, nr - 1, 0) + pe_ref[1:2, :]).astype(BF16)
        hid = _gelu_tanh(_dot(lo, w1_ref[0]) + _dot(hi, w1_ref[1]))
        o_ref[...] = _dot(hid.astype(BF16), w2_ref[...]).astype(BF16)


def _nsa_compress(rk, rv, pe_k, pe_v, w1_k, w1_v, w2_k, w2_v):
    b, nr, wide = rk.shape
    half = CMP_LEN * HEAD_DIM // 2
    assert wide == half
    rspec = pl.BlockSpec((None, nr, wide), lambda i: (i, 0, 0))
    pspec = pl.BlockSpec((2, half), lambda i: (0, 0))
    w1spec = pl.BlockSpec((2, half, HEAD_DIM), lambda i: (0, 0, 0))
    w2spec = pl.BlockSpec((HEAD_DIM, HEAD_DIM), lambda i: (0, 0))
    ospec = pl.BlockSpec((None, nr, HEAD_DIM), lambda i: (i, 0, 0))
    return pl.pallas_call(
        _nsa_cmp_body,
        grid=(b,),
        in_specs=[rspec, rspec, pspec, pspec, w1spec, w1spec, w2spec, w2spec],
        out_specs=[ospec, ospec],
        out_shape=[jax.ShapeDtypeStruct((b, nr, HEAD_DIM), BF16)] * 2,
        compiler_params=_cparams(("parallel",)),
        name="nsa_compress",
    )(rk, rv, pe_k.reshape(2, half), pe_v.reshape(2, half), w1_k.reshape(2, half, HEAD_DIM),
      w1_v.reshape(2, half, HEAD_DIM), w2_k, w2_v)


def _bias_from_dist(dist, th_ref, rb_ref):
    n = jnp.maximum(dist, 0)
    outs = [jnp.full(dist.shape, rb_ref[0, h] * LOG2E, F32) for h in range(N_HEADS)]
    for bkt in range(1, N_BUCKETS):
        ge = n >= th_ref[bkt]
        outs = [jnp.where(ge, rb_ref[bkt, h] * LOG2E, o) for h, o in enumerate(outs)]
    return outs


def _nsa_bias_body(th_ref, rb_ref, sb_ref, wb_ref, cb_ref, *, tk, n_far, ncp):
    qb = Q_BLOCK
    ii = lax.broadcasted_iota(jnp.int32, (qb, tk), 0)
    jj = lax.broadcasted_iota(jnp.int32, (qb, tk), 1)
    for e in range(n_far + 1):
        tiles = _bias_from_dist(e * qb + ii - jj, th_ref, rb_ref)
        for h in range(N_HEADS):
            sb_ref[e, h * qb:(h + 1) * qb, :] = tiles[h]
    nwin = WINDOW + qb
    dw = lax.broadcasted_iota(jnp.int32, (qb, nwin), 0) - lax.broadcasted_iota(jnp.int32, (qb, nwin), 1) + WINDOW
    tiles = _bias_from_dist(dw, th_ref, rb_ref)
    okw = (dw >= 0) & (dw < WINDOW)
    for h in range(N_HEADS):
        wb_ref[h * qb:(h + 1) * qb, :] = jnp.where(okw, tiles[h], NEG)
    mm = lax.broadcasted_iota(jnp.int32, (2 * ncp, qb), 0) - ncp
    dc = lax.broadcasted_iota(jnp.int32, (2 * ncp, qb), 1) - (mm * CMP_STRIDE + CMP_LEN - 1)
    tiles = _bias_from_dist(dc, th_ref, rb_ref)
    for h in range(N_HEADS):
        cb_ref[:, h * qb:(h + 1) * qb] = jnp.where(dc >= 0, tiles[h], NEG)


def _nsa_far_tiles(tk):
    last = MAX_EXACT * (MAX_DIST / MAX_EXACT) ** ((N_BUCKETS - MAX_EXACT - 1) / (N_BUCKETS - MAX_EXACT))
    return -(-(int(math.ceil(last)) + 1 + tk - 1) // Q_BLOCK)


def _nsa_bias(thresholds, rel_bias, t, *, tk):
    n_far = _nsa_far_tiles(tk)
    ncp = t // CMP_STRIDE
    hq = N_HEADS * Q_BLOCK
    smem = pl.BlockSpec(memory_space=pltpu.SMEM)
    return pl.pallas_call(
        functools.partial(_nsa_bias_body, tk=tk, n_far=n_far, ncp=ncp),
        in_specs=[smem, smem],
        out_shape=[jax.ShapeDtypeStruct((n_far + 1, hq, tk), F32), jax.ShapeDtypeStruct((hq, WINDOW + Q_BLOCK), F32),
                   jax.ShapeDtypeStruct((2 * ncp, hq), F32)],
        compiler_params=pltpu.CompilerParams(vmem_limit_bytes=VMEM_LIMIT),
        name="nsa_bias",
    )(thresholds, rel_bias)


def _nsa_body(q_ref, sig_ref, kc_ref, vc_ref, ks_ref, vs_ref, kw_ref, vw_ref, ovt_ref, ex_ref, sb_ref, wb_ref,
              cb_ref, o_ref, *, tk, n_far, n_slc, sc2):
    qi = pl.program_id(1)
    qb = Q_BLOCK
    hq = N_HEADS * qb
    s0 = qi * qb
    nwin = WINDOW + qb
    r = tk // qb
    q4 = jnp.concatenate([q_ref[:, h * HEAD_DIM:(h + 1) * HEAD_DIM] for h in range(N_HEADS)], axis=0)

    heads = range(N_HEADS)
    qh = [q_ref[:, h * HEAD_DIM:(h + 1) * HEAD_DIM] for h in heads]

    ncp = kc_ref.shape[0]
    c0 = pl.multiple_of(ncp - (qb // CMP_STRIDE) * qi, 8)
    w0 = pl.multiple_of(s0, qb)
    kwin = kw_ref[pl.ds(w0, nwin), :]
    vwin = vw_ref[pl.ds(w0, nwin), :]
    s_c = _dot_nt(kc_ref[...], q4) * sc2 + cb_ref[pl.ds(c0, ncp), :]
    before_start = s0 - WINDOW + lax.broadcasted_iota(jnp.int32, (qb, nwin), 1) < 0
    s_w = [jnp.where(before_start, NEG, _dot_nt(qh[h], kwin) * sc2 + wb_ref[h * qb:(h + 1) * qb, :]) for h in heads]
    m_c = jnp.maximum(jnp.max(s_c, axis=0, keepdims=True), 0.5 * NEG)
    e_c = jnp.exp2(s_c - m_c)
    e_w = [jnp.exp2(s_w[h] - jnp.max(s_w[h], axis=1, keepdims=True)) for h in heads]
    p_c = (e_c * (1.0 / jnp.maximum(jnp.sum(e_c, axis=0, keepdims=True), 1e-30))).astype(BF16)
    r_w = [1.0 / jnp.sum(e_w[h], axis=1, keepdims=True) for h in heads]
    o_c = _dot_tn(p_c, vc_ref[...])
    imp = _dot(ovt_ref[...], p_c[:, 0:qb])
    for h in range(1, N_HEADS):
        imp = imp + _dot(ovt_ref[...], p_c[:, h * qb:(h + 1) * qb])
    o_w = [_dot(e_w[h].astype(BF16), vwin) * r_w[h] for h in heads]

    jsl = lax.broadcasted_iota(jnp.int32, (n_slc, qb), 0)
    cur = (s0 + lax.broadcasted_iota(jnp.int32, (n_slc, qb), 1)) // SLC_LEN
    forced = (jsl == 0) | (jsl == cur) | (jsl == cur - 1)
    imp = jnp.where(forced, FORCED_SCORE, imp)
    imp = jnp.where(jsl > cur, -1.0, imp)
    groups = range(n_slc // 8)
    impg = [imp[8 * v:8 * v + 8, :] for v in groups]
    jloc = lax.broadcasted_iota(jnp.int32, (8, qb), 0)
    rank = [jnp.zeros((8, qb), jnp.int32) for _ in groups]
    for j2 in range(n_slc):
        row = impg[j2 // 8][j2 % 8:j2 % 8 + 1, :]
        for v in groups:
            if 8 * v > j2:
                beats = row >= impg[v]
            elif 8 * v + 7 < j2:
                beats = row > impg[v]
            else:
                beats = (row > impg[v]) | ((row >= impg[v]) & (jloc > j2 % 8))
            rank[v] = rank[v] + jnp.where(beats, 1, 0)
    selneg = jnp.where(jnp.concatenate(rank, axis=0) < min(N_SELECT, n_slc), 0.0, NEG).astype(BF16)

    it = lax.broadcasted_iota(jnp.int32, (qb, tk), 0)
    jt = lax.broadcasted_iota(jnp.int32, (qb, tk), 1)

    def sel_step(kt, carry, diag):
        m, l, acc = carry
        k0 = pl.multiple_of(kt * tk, tk)
        k = ks_ref[pl.ds(k0, tk), :]
        v = vs_ref[pl.ds(k0, tk), :]
        add = _dot_tn(selneg, ex_ref[:, pl.ds(k0, tk)])
        if diag:
            add = jnp.where(s0 + it >= k0 + jt, add, NEG)
        eb = [jnp.clip(qi - (kt * r + j), 0, n_far) for j in range(r)]
        bias = lambda h: jnp.concatenate([sb_ref[eb[j], h * qb:(h + 1) * qb, :] for j in range(r)], axis=1)
        s = [_dot_nt(qh[h], k) * sc2 + bias(h) + add for h in heads]
        wide = lambda x: jnp.concatenate([x] * r, axis=1)
        m_new = tuple(jnp.maximum(m[h], jnp.max(s[h], axis=1, keepdims=True)) for h in heads)
        alpha = [jnp.exp2(m[h] - m_new[h]) for h in heads]
        p = [jnp.exp2(s[h] - wide(m_new[h])) for h in heads]
        l = tuple(alpha[h] * l[h] + jnp.sum(p[h], axis=1, keepdims=True) for h in heads)
        pv = [_dot(p[h].astype(BF16), v) for h in heads]
        acc = tuple(alpha[h] * acc[h] + pv[h] for h in heads)
        return m_new, l, acc

    init = (tuple(jnp.full((qb, LANE), 0.5 * NEG, F32) for _ in heads),
            tuple(jnp.zeros((qb, LANE), F32) for _ in heads), tuple(jnp.zeros((qb, HEAD_DIM), F32) for _ in heads))
    kd = qi // r
    carry = lax.fori_loop(0, kd, functools.partial(sel_step, diag=False), init)
    _, l_s, acc_s = sel_step(kd, carry, True)
    o_s = [acc_s[h] * (1.0 / jnp.maximum(l_s[h], 1e-30)) for h in heads]

    sig = sig_ref[...]
    for h in heads:
        g0 = sig[:, _S_NSA_G + 3 * h:_S_NSA_G + 3 * h + 1]
        g1 = sig[:, _S_NSA_G + 3 * h + 1:_S_NSA_G + 3 * h + 2]
        g2 = sig[:, _S_NSA_G + 3 * h + 2:_S_NSA_G + 3 * h + 3]
        o_ref[:, h * HEAD_DIM:(h + 1) * HEAD_DIM] = g0 * o_c[h * qb:(h + 1) * qb] + g1 * o_s[h] + g2 * o_w[h]


def _nsa(zbf, sig, kc, vc, kw_pad, vw_pad, overlap_t, expand, sb, wb, cb, b, t, *, tk):
    qb = Q_BLOCK
    nq = t // qb
    n_slc = t // SLC_LEN
    ncp = kc.shape[1]
    hq = N_HEADS * qb
    n_far = sb.shape[0] - 1
    per_b = lambda rows: pl.BlockSpec((None, rows, HEAD_DIM), lambda i, j: (i, 0, 0))
    whole = lambda a: pl.BlockSpec(a.shape, lambda i, j: (0,) * a.ndim)
    return pl.pallas_call(
        functools.partial(_nsa_body, tk=tk, n_far=n_far, n_slc=n_slc, sc2=HEAD_DIM ** -0.5 * LOG2E),
        grid=(b, nq),
        in_specs=[
            pl.BlockSpec((qb, GROUP_W), lambda i, j: (i * nq + j, _COL["nsa_q"] // 4)),
            pl.BlockSpec((qb, LANE), lambda i, j: (i * nq + j, 0)),
            per_b(ncp), per_b(ncp),
            pl.BlockSpec((t, HEAD_DIM), lambda i, j: (i, _COL["nsa_ks"])),
            pl.BlockSpec((t, HEAD_DIM), lambda i, j: (i, _COL["nsa_vs"])),
            per_b(t + WINDOW), per_b(t + WINDOW),
            whole(overlap_t), whole(expand), whole(sb), whole(wb), whole(cb),
        ],
        out_specs=pl.BlockSpec((qb, GROUP_W), lambda i, j: (i * nq + j, 0)),
        out_shape=jax.ShapeDtypeStruct((b * t, GROUP_W), F32),
        compiler_params=_cparams(("parallel", "arbitrary")),
        name="nsa_attn",
    )(zbf, sig, kc, vc, zbf, zbf, kw_pad, vw_pad, overlap_t, expand, sb, wb, cb)


def _outproj_body(h_ref, oa_ref, ob_ref, oc_ref, od_ref, g_ref, w_ref, o_ref):
    y = jnp.concatenate([
        _rms(oa_ref[...], g_ref[0:1, :]), ob_ref[...], _rms(oc_ref[...], g_ref[1:2, :]),
        _rms(od_ref[...], g_ref[2:3, :])], axis=-1)
    o_ref[...] = h_ref[...] + _dot(y.astype(BF16), w_ref[...])


def _outproj(h, oa, ob, oc, od, g, w, l, *, tm=512):
    nt, d = h.shape
    tm = min(tm, nt)
    grp = pl.BlockSpec((tm, GROUP_W), lambda i: (i, 0))
    return pl.pallas_call(
        _outproj_body,
        grid=(nt // tm,),
        in_specs=[pl.BlockSpec((tm, d), lambda i: (i, 0)), grp, grp, grp, grp,
                  pl.BlockSpec((3, GROUP_W), lambda i: (0, 0)),
                  pl.BlockSpec((None, 4 * GROUP_W, d), lambda i: (l, 0, 0))],
        out_specs=pl.BlockSpec((tm, d), lambda i: (i, 0)),
        out_shape=jax.ShapeDtypeStruct((nt, d), F32),
        compiler_params=_cparams(("parallel",)),
        name="outproj",
    )(h, oa, ob, oc, od, g, w)


_IN_WIDTHS = (("fox_q", GROUP_W), ("fox_k", GROUP_W), ("fox_v", GROUP_W), ("fox_f", N_HEADS),
              ("gdn_q", GROUP_W), ("gdn_k", GROUP_W), ("gdn_v", GROUP_W), ("gdn_a", N_HEADS), ("gdn_b", N_HEADS),
              ("gdn_z", GROUP_W), ("lru_x", GROUP_W), ("lru_gate", GROUP_W), ("nsa_q", GROUP_W),
              ("nsa_kc", HEAD_DIM), ("nsa_vc", HEAD_DIM), ("nsa_ks", HEAD_DIM), ("nsa_vs", HEAD_DIM),
              ("nsa_kw", HEAD_DIM), ("nsa_vw", HEAD_DIM), ("nsa_g", 3 * N_HEADS))
D_IN = sum(w for _, w in _IN_WIDTHS)


def _in_pieces():
    small_lane = {"fox_f": _S_FOX_F, "gdn_a": _S_GDN_A, "gdn_b": _S_GDN_B, "nsa_g": _S_NSA_G}
    out, off = [], 0
    for name, width in _IN_WIDTHS:
        if name in small_lane:
            out.append((off, width, 1, _COL["small"] * LANE + small_lane[name]))
        else:
            out.append((off, width, 0 if name in _COLS16 else 1, _COL[name] * LANE))
        off += width
    return out


def _relayout_body(w_ref, o16_ref, o32_ref):
    outs = (o16_ref, o32_ref)
    tc = w_ref.shape[1]
    tail = _COL["small"] * LANE
    small = []
    for src, width, which, dst in _in_pieces():
        if dst >= tail and which == 1:
            small.append(w_ref[src:src + width, :])
        else:
            outs[which][dst:dst + width, :] = w_ref[src:src + width, :].astype(outs[which].dtype)
    used = sum(s.shape[0] for s in small)
    small.append(jnp.zeros((W32 - tail - used, tc), F32))
    o32_ref[tail:, :] = jnp.concatenate(small, axis=0).astype(o32_ref.dtype)


def _relayout_w_in(w_t, *, tc=256):
    nl, d_in, d = w_t.shape
    return pl.pallas_call(
        _relayout_body,
        grid=(nl, d // tc),
        in_specs=[pl.BlockSpec((None, d_in, tc), lambda l, i: (l, 0, i))],
        out_specs=[pl.BlockSpec((None, W16, tc), lambda l, i: (l, 0, i)),
                   pl.BlockSpec((None, W32, tc), lambda l, i: (l, 0, i))],
        out_shape=[jax.ShapeDtypeStruct((nl, W16, d), BF16), jax.ShapeDtypeStruct((nl, W32, d), BF16)],
        compiler_params=_cparams(("parallel", "parallel")),
        name="w_in_relayout",
    )(w_t)


def _pad_cols(w):
    src, off = {}, 0
    for name, width in _IN_WIDTHS:
        src[name] = w[..., off:off + width]
        off += width
    zeros = lambda n: jnp.zeros(w.shape[:-1] + (n,), w.dtype)
    small = [src["fox_f"], src["gdn_a"], src["gdn_b"], src["nsa_g"]]
    src["small"] = jnp.concatenate(small + [zeros(LANE - sum(s.shape[-1] for s in small))], axis=-1)
    w16 = jnp.concatenate([src[name] for name in _COLS16], axis=-1)
    w32 = jnp.concatenate([src[name] for name in _COLS32] + [zeros(W32 - (_COL["small"] + 1) * LANE)], axis=-1)
    assert w16.shape[-1] == W16 and w32.shape[-1] == W32
    return w16, w32


def _block_diag(w):
    l, nb, bw, _ = w.shape
    eye = jnp.eye(nb, dtype=w.dtype)
    return jnp.einsum("lnde,nm->lndme", w, eye).reshape(l, nb * bw, nb * bw)


def _bucket_thresholds(t):
    n = jnp.arange(max(t, 2 * MAX_DIST), dtype=jnp.int32)
    nf = jnp.maximum(n, 1).astype(F32)
    large = MAX_EXACT + (jnp.log(nf / MAX_EXACT) / math.log(MAX_DIST / MAX_EXACT)
                         * (N_BUCKETS - MAX_EXACT)).astype(jnp.int32)
    large = jnp.minimum(large, N_BUCKETS - 1)
    bucket = jnp.where(n < MAX_EXACT, n, large)
    return jnp.sum(bucket[None, :] < jnp.arange(N_BUCKETS, dtype=jnp.int32)[:, None], axis=1).astype(jnp.int32)


def _nsa_tile(t):
    return min(1024, t)


def _mixer(h, l, p, b, t):
    zbf, z32 = _inproj(h, p["mix_norm_g"][l], p["w16"], p["b16"], p["w32"], p["b32"], l)

    ccol = _fox_gate(z32, b, t)
    crow = jnp.pad(ccol[:, ::LANE].reshape(b, t, N_HEADS).transpose(0, 2, 1), ((0, 0), (0, 8 - N_HEADS), (0, 0)))
    o_a = _fox(zbf, ccol, crow, b, t)

    qkv, gdn_g, sig = _gdn_prep(z32, p["gdn_conv_w"][l], p["gdn_a_log"][l], p["gdn_dt_bias"][l], b, t)
    c = GDN_CHUNK
    grow = gdn_g[:, _S_GDN_A:_S_GDN_A + N_HEADS].reshape(b * t // c, c, N_HEADS).transpose(0, 2, 1)
    grow = jnp.pad(grow, ((0, 0), (0, 8 - N_HEADS), (0, 0)))
    u, w, qg, kd, qk, gc = _gdn_intra(qkv, gdn_g, grow, sig, b, t)
    o_b = _gdn_scan(u, w, qg, kd, qk, gc, z32, p["gdn_norm_g"][l], b, t)

    o_c = _lru(z32, p["lru_conv_w"][l], p["lru_conv_b"][l], p["lru_wa_bd"][l], p["lru_b_a"][l],
               p["lru_wx_bd"][l], p["lru_b_x"][l], p["lru_lambda"][l], b, t)

    col = lambda name: z32[:, _COL[name] * LANE:(_COL[name] + 1) * LANE]
    cmp_rows = lambda a: a.reshape(b, t // CMP_STRIDE, CMP_STRIDE * HEAD_DIM)
    kc, vc = _nsa_compress(cmp_rows(col("nsa_kc")), cmp_rows(col("nsa_vc")), p["nsa_pe_k"][l], p["nsa_pe_v"][l],
                           p["nsa_w1_k16"][l], p["nsa_w1_v16"][l], p["nsa_w2_k16"][l], p["nsa_w2_v16"][l])
    colbf = lambda name: zbf[:, _COL[name] * LANE:(_COL[name] + 1) * LANE].reshape(b, t, HEAD_DIM)
    kw_pad = jnp.pad(colbf("nsa_kw"), ((0, 0), (WINDOW, 0), (0, 0)))
    vw_pad = jnp.pad(colbf("nsa_vw"), ((0, 0), (WINDOW, 0), (0, 0)))
    o_d = _nsa(zbf, sig, kc, vc, kw_pad, vw_pad, p["overlap_t"], p["expand"], p["sb"], p["wb"], p["cb"], b, t,
               tk=_nsa_tile(t))

    return _outproj(h, o_a, o_b, o_c, o_d, p["out_norm_g"][l], p["w_out16"], l)


def _mixer_consts(rel_bias, t):
    n_cmp_rows = t // CMP_STRIDE
    n_slc = t // SLC_LEN
    cn = jnp.arange(n_cmp_rows)[None, :] * CMP_STRIDE
    sj = jnp.arange(n_slc)[:, None] * SLC_LEN
    sb, wb, cb = _nsa_bias(_bucket_thresholds(t), rel_bias, t, tk=Q_BLOCK)
    return dict(
        overlap_t=((cn <= sj + SLC_LEN - 1) & (cn + CMP_LEN - 1 >= sj)).astype(BF16),
        expand=(jnp.arange(t)[None, :] // SLC_LEN == jnp.arange(n_slc)[:, None]).astype(BF16),
        sb=sb, wb=wb, cb=cb)


def kernel(x, ffn1_norm_g, ffn1_w_gate, ffn1_w_up, ffn1_w_down, mix_norm_g, w_in, b_in, gdn_conv_w, gdn_a_log, gdn_dt_bias, gdn_norm_g, lru_conv_w, lru_conv_b, lru_w_a, lru_b_a, lru_w_x, lru_b_x, lru_lambda, nsa_pe_k, nsa_w1_k, nsa_w2_k, nsa_pe_v, nsa_w1_v, nsa_w2_v, rel_bias, out_norm_g, w_out, ffn2_norm_g, ffn2_w_gate, ffn2_w_up, ffn2_w_down, final_norm_g):
    b, t, d = x.shape
    depth = w_in.shape[0]
    assert w_in.shape[-1] == D_IN
    w16, w32 = _relayout_w_in(jnp.swapaxes(w_in, 1, 2))
    b16, b32 = _pad_cols(b_in[:, None, :])
    p = dict(
        mix_norm_g=mix_norm_g, w16=w16, w32=w32, b16=b16, b32=b32,
        gdn_conv_w=gdn_conv_w, gdn_a_log=gdn_a_log, gdn_dt_bias=gdn_dt_bias, gdn_norm_g=gdn_norm_g,
        lru_conv_w=lru_conv_w, lru_conv_b=lru_conv_b, lru_wa_bd=_block_diag(lru_w_a).astype(BF16), lru_b_a=lru_b_a,
        lru_wx_bd=_block_diag(lru_w_x).astype(BF16), lru_b_x=lru_b_x, lru_lambda=lru_lambda,
        nsa_pe_k=nsa_pe_k, nsa_pe_v=nsa_pe_v, nsa_w1_k16=nsa_w1_k.astype(BF16), nsa_w1_v16=nsa_w1_v.astype(BF16),
        nsa_w2_k16=nsa_w2_k.astype(BF16), nsa_w2_v16=nsa_w2_v.astype(BF16),
        out_norm_g=out_norm_g, w_out16=w_out.astype(BF16),
        **_mixer_consts(rel_bias, t),
    )
    f1 = (ffn1_w_gate.astype(BF16), ffn1_w_up.astype(BF16), ffn1_w_down.astype(BF16))
    f2 = (ffn2_w_gate.astype(BF16), ffn2_w_up.astype(BF16), ffn2_w_down.astype(BF16))
    h = x.reshape(b * t, d)
    for l in range(depth):
        h = _ffn(h, ffn1_norm_g[l], *f1, l)
        h = _mixer(h, l, p, b, t)
        h = _ffn(h, ffn2_norm_g[l], *f2, l, final_g=final_norm_g if l == depth - 1 else None)
    return h.reshape(b, t, d)
```

```python
import functools
import math

import jax
import jax.numpy as jnp
from jax import lax
from jax.experimental import pallas as pl
from jax.experimental.pallas import tpu as pltpu

F32 = jnp.float32
BF16 = jnp.bfloat16

EPS = 1e-6
LANE = 128
HEAD_DIM = 128
N_HEADS = 4
GROUP_W = N_HEADS * HEAD_DIM
GDN_CHUNK = 64
CONV_W = 4
LRU_BLOCKS = 8
LRU_C = 8.0
CMP_LEN = 32
CMP_STRIDE = 16
SLC_LEN = 64
N_SELECT = 16
WINDOW = 512
FORCED_SCORE = 1.0e6
N_BUCKETS = 32
MAX_EXACT = 16
MAX_DIST = 1024
Q_BLOCK = 128
NEG = -1.0e30
LOG2E = math.log2(math.e)
VMEM_LIMIT = 56 * 1024 * 1024

_COLS16 = ("fox_q", "fox_k", "fox_v", "nsa_q", "nsa_ks", "nsa_vs", "nsa_kw", "nsa_vw")
_COLS32 = ("gdn_q", "gdn_k", "gdn_v", "gdn_z", "lru_x", "lru_gate", "nsa_kc", "nsa_vc", "small")
_COL = dict(fox_q=0, fox_k=4, fox_v=8, nsa_q=12, nsa_ks=16, nsa_vs=17, nsa_kw=18, nsa_vw=19,
            gdn_q=0, gdn_k=4, gdn_v=8, gdn_z=12, lru_x=16, lru_gate=20, nsa_kc=24, nsa_vc=25, small=26)
W16 = 20 * LANE
W32 = 28 * LANE
_S_FOX_F, _S_GDN_A, _S_GDN_B, _S_NSA_G = 0, 4, 8, 12


def _cparams(sem):
    return pltpu.CompilerParams(dimension_semantics=sem, vmem_limit_bytes=VMEM_LIMIT)


def _rms(x, g):
    return x * lax.rsqrt(jnp.mean(x * x, axis=-1, keepdims=True) + EPS) * g


def _softplus(x):
    return jnp.maximum(x, 0.0) + jnp.log1p(jnp.exp(-jnp.abs(x)))


def _gelu_tanh(x):
    return 0.5 * x * (1.0 + jnp.tanh(math.sqrt(2.0 / math.pi) * (x + 0.044715 * (x * x * x))))


def _dot(a, b):
    return jnp.dot(a, b, preferred_element_type=F32)


def _dot_nt(a, b):
    return lax.dot_general(a, b, (((1,), (1,)), ((), ())), preferred_element_type=F32)


def _dot_tn(a, b):
    return lax.dot_general(a, b, (((0,), (0,)), ((), ())), preferred_element_type=F32)


def _dot_hi(a, b):
    return jnp.dot(a, b, preferred_element_type=F32, precision=lax.Precision.HIGHEST)


def _split(x):
    hi = x.astype(BF16)
    return hi, (x - hi.astype(F32)).astype(BF16)


def _dot3(a, b):
    return _dot(a[0], b[0]) + (_dot(a[0], b[1]) + _dot(a[1], b[0]))


def _ffn_body(*refs, final, convert):
    refs = list(refs)
    x_ref, g_ref, wg_ref, wu_ref, wd_ref = refs[:5]
    del refs[:5]
    fg_ref = refs.pop(0) if final else None
    nxt_in = [refs.pop(0) for _ in range(3)] if convert else []
    o_ref = refs.pop(0)
    nxt_out = [refs.pop(0) for _ in range(3)] if convert else []
    n_scr, = refs
    j = pl.program_id(1)

    @pl.when(j == 0)
    def _():
        n_scr[...] = _rms(x_ref[...], g_ref[...]).astype(BF16)
        o_ref[...] = jnp.zeros_like(o_ref)

    n = n_scr[...]
    gate = _dot(n, wg_ref[...])
    up = _dot(n, wu_ref[...])
    a = (gate * jax.nn.sigmoid(gate) * up).astype(BF16)
    o_ref[...] += _dot(a, wd_ref[...])

    for src, dst in zip(nxt_in, nxt_out):
        dst[...] = src[...].astype(BF16)

    @pl.when(j == pl.num_programs(1) - 1)
    def _():
        y = x_ref[...] + 0.5 * o_ref[...]
        if final:
            y = _rms(y, fg_ref[...])
        o_ref[...] = y


def _ffn(x, g, wg, wu, wd, final_g=None, nxt=None, *, tm=512, tf=512):
    nt, d = x.shape
    f = wg.shape[-1]
    tm = min(tm, nt)
    ni, nj = nt // tm, f // tf
    final = final_g is not None
    convert = nxt is not None
    in_specs = [
        pl.BlockSpec((tm, d), lambda i, j: (i, 0)),
        pl.BlockSpec((1, d), lambda i, j: (0, 0)),
        pl.BlockSpec((d, tf), lambda i, j: (0, j)),
        pl.BlockSpec((d, tf), lambda i, j: (0, j)),
        pl.BlockSpec((tf, d), lambda i, j: (j, 0)),
    ]
    args = [x, g.reshape(1, d), wg, wu, wd]
    if final:
        in_specs.append(pl.BlockSpec((1, d), lambda i, j: (0, 0)))
        args.append(final_g.reshape(1, d))
    out_specs = [pl.BlockSpec((tm, d), lambda i, j: (i, 0))]
    out_shape = [jax.ShapeDtypeStruct((nt, d), F32)]
    if convert:
        wg32, wu32, wd32, l = nxt
        assert d % ni == 0
        dr = d // ni
        in_specs += [pl.BlockSpec((None, dr, tf), lambda i, j: (l, i, j)),
                     pl.BlockSpec((None, dr, tf), lambda i, j: (l, i, j)),
                     pl.BlockSpec((None, tf, dr), lambda i, j: (l, j, i))]
        args += [wg32, wu32, wd32]
        out_specs += [pl.BlockSpec((dr, tf), lambda i, j: (i, j)), pl.BlockSpec((dr, tf), lambda i, j: (i, j)),
                      pl.BlockSpec((tf, dr), lambda i, j: (j, i))]
        out_shape += [jax.ShapeDtypeStruct((d, f), BF16), jax.ShapeDtypeStruct((d, f), BF16),
                      jax.ShapeDtypeStruct((f, d), BF16)]
    res = pl.pallas_call(
        functools.partial(_ffn_body, final=final, convert=convert),
        grid=(ni, nj),
        in_specs=in_specs,
        out_specs=out_specs,
        out_shape=out_shape,
        scratch_shapes=[pltpu.VMEM((tm, d), BF16)],
        compiler_params=_cparams(("parallel", "arbitrary")),
        name="ffn",
    )(*args)
    return res[0], (tuple(res[1:]) if convert else None)


def _inproj_body(x_ref, g_ref, w16_ref, b16_ref, w32_ref, b32_ref, o16_ref, o32_ref, n_scr, *, n16):
    j = pl.program_id(1)

    @pl.when(j == 0)
    def _():
        n_scr[...] = _rms(x_ref[...], g_ref[...]).astype(BF16)

    @pl.when(j < n16)
    def _():
        o16_ref[...] = (_dot_nt(n_scr[...], w16_ref[...]) + b16_ref[...]).astype(o16_ref.dtype)

    @pl.when(j >= n16)
    def _():
        o32_ref[...] = _dot_nt(n_scr[...], w32_ref[...]) + b32_ref[...]


def _inproj(x, g, w16, b16, w32, b32, l, *, tm=1024, n16=2, n32=4):
    nt, d = x.shape
    tm = min(tm, nt)
    t16, t32 = w16.shape[1] // n16, w32.shape[1] // n32
    j16 = lambda j: jnp.minimum(j, n16 - 1)
    j32 = lambda j: jnp.maximum(j - n16, 0)
    return pl.pallas_call(
        functools.partial(_inproj_body, n16=n16),
        grid=(nt // tm, n16 + n32),
        in_specs=[
            pl.BlockSpec((tm, d), lambda i, j: (i, 0)),
            pl.BlockSpec((1, d), lambda i, j: (0, 0)),
            pl.BlockSpec((None, t16, d), lambda i, j: (l, j16(j), 0)),
            pl.BlockSpec((None, 1, t16), lambda i, j: (l, 0, j16(j))),
            pl.BlockSpec((None, t32, d), lambda i, j: (l, j32(j), 0)),
            pl.BlockSpec((None, 1, t32), lambda i, j: (l, 0, j32(j))),
        ],
        out_specs=[pl.BlockSpec((tm, t16), lambda i, j: (i, j16(j))),
                   pl.BlockSpec((tm, t32), lambda i, j: (i, j32(j)))],
        out_shape=[jax.ShapeDtypeStruct((nt, w16.shape[1]), BF16), jax.ShapeDtypeStruct((nt, w32.shape[1]), F32)],
        scratch_shapes=[pltpu.VMEM((tm, d), BF16)],
        compiler_params=_cparams(("parallel", "arbitrary")),
        name="inproj",
    )(x, g.reshape(1, d), w16, b16, w32, b32)


def _cumsum_rows(y):
    rows = lax.broadcasted_iota(jnp.int32, y.shape, 0)
    s = 1
    while s < y.shape[0]:
        y = y + jnp.where(rows >= s, pltpu.roll(y, s, 0), 0.0)
        s *= 2
    return y


def _fox_gate_body(f_ref, c_ref):
    x = f_ref[...]
    log_f = jnp.minimum(x, 0.0) - jnp.log1p(jnp.exp(-jnp.abs(x)))
    c = _cumsum_rows(log_f) * LOG2E
    for h in range(N_HEADS):
        c_ref[:, h * LANE:(h + 1) * LANE] = jnp.broadcast_to(c[:, _S_FOX_F + h:_S_FOX_F + h + 1], (c.shape[0], LANE))


def _fox_gate(z32, b, t):
    return pl.pallas_call(
        _fox_gate_body,
        grid=(b,),
        in_specs=[pl.BlockSpec((t, LANE), lambda i: (i, _COL["small"]))],
        out_specs=pl.BlockSpec((t, N_HEADS * LANE), lambda i: (i, 0)),
        out_shape=jax.ShapeDtypeStruct((b * t, N_HEADS * LANE), F32),
        compiler_params=_cparams(("parallel",)),
        name="fox_gate",
    )(z32)


def _fox_body(q_ref, k_ref, v_ref, ccol_ref, crow_ref, o_ref, m_scr, l_scr, acc_scr, *, tq, tk, sc2):
    qi = pl.program_id(1)
    rows = qi * tq + lax.broadcasted_iota(jnp.int32, (tq, tk), 0)
    cols = lax.broadcasted_iota(jnp.int32, (tq, tk), 1)
    m_scr[...] = jnp.full_like(m_scr, NEG)
    l_scr[...] = jnp.zeros_like(l_scr)
    acc_scr[...] = jnp.zeros_like(acc_scr)

    def step(kt, masked):
        k0 = pl.multiple_of(kt * tk, tk)
        heads = range(N_HEADS)
        hsl = [slice(h * HEAD_DIM, (h + 1) * HEAD_DIM) for h in heads]
        wide = lambda x: jnp.concatenate([x] * (tk // LANE), axis=1)
        s = [_dot_nt(q_ref[:, hsl[h]], k_ref[pl.ds(k0, tk), hsl[h]]) * sc2 - crow_ref[h:h + 1, pl.ds(k0, tk)]
             for h in heads]
        if masked:
            causal = rows >= k0 + cols
            s = [jnp.where(causal, x, NEG) for x in s]
        cq = [ccol_ref[:, hsl[h]] for h in heads]
        m_old = [m_scr[h] for h in heads]
        m_new = [jnp.maximum(m_old[h], jnp.max(s[h], axis=1, keepdims=True) + cq[h]) for h in heads]
        alpha = [jnp.exp2(m_old[h] - m_new[h]) for h in heads]
        p = [jnp.exp2(s[h] + wide(cq[h] - m_new[h])) for h in heads]
        l_new = [alpha[h] * l_scr[h] + jnp.sum(p[h], axis=1, keepdims=True) for h in heads]
        pv = [_dot(p[h].astype(BF16), v_ref[pl.ds(k0, tk), hsl[h]]) for h in heads]
        acc_new = [alpha[h] * acc_scr[h] + pv[h] for h in heads]
        for h in heads:
            m_scr[h] = m_new[h]
            l_scr[h] = l_new[h]
            acc_scr[h] = acc_new[h]

    def body(kt, carry):
        step(kt, False)
        return carry

    kd = (qi * tq) // tk
    lax.fori_loop(0, kd, body, 0)
    step(kd, True)
    for h in range(N_HEADS):
        o_ref[:, h * HEAD_DIM:(h + 1) * HEAD_DIM] = acc_scr[h] / l_scr[h]


def _fox(zbf, ccol, crow, b, t, *, tq=256, tk=1024):
    tq, tk = min(tq, t), min(tk, t)
    assert tk % tq == 0
    nq = t // tq
    return pl.pallas_call(
        functools.partial(_fox_body, tq=tq, tk=tk, sc2=HEAD_DIM ** -0.5 * LOG2E),
        grid=(b, nq),
        in_specs=[
            pl.BlockSpec((tq, GROUP_W), lambda i, j: (i * nq + j, _COL["fox_q"] // 4)),
            pl.BlockSpec((t, GROUP_W), lambda i, j: (i, _COL["fox_k"] // 4)),
            pl.BlockSpec((t, GROUP_W), lambda i, j: (i, _COL["fox_v"] // 4)),
            pl.BlockSpec((tq, N_HEADS * LANE), lambda i, j: (i * nq + j, 0)),
            pl.BlockSpec((None, 8, t), lambda i, j: (i, 0, 0)),
        ],
        out_specs=pl.BlockSpec((tq, GROUP_W), lambda i, j: (i * nq + j, 0)),
        out_shape=jax.ShapeDtypeStruct((b * t, GROUP_W), F32),
        scratch_shapes=[pltpu.VMEM((N_HEADS, tq, LANE), F32), pltpu.VMEM((N_HEADS, tq, LANE), F32),
                        pltpu.VMEM((N_HEADS, tq, HEAD_DIM), F32)],
        compiler_params=_cparams(("parallel", "arbitrary")),
        name="fox_attn",
    )(zbf, zbf, zbf, ccol, crow)


def _causal_conv(x, halo, w_ref):
    tt = x.shape[0]
    xx = jnp.concatenate([halo, x], axis=0)
    y = x * w_ref[CONV_W - 1:CONV_W, :]
    for d in range(1, CONV_W):
        y = y + pltpu.roll(xx, d, 0)[8:8 + tt] * w_ref[CONV_W - 1 - d:CONV_W - d, :]
    return y


def _gdn_prep_body(x_ref, halo_ref, s_ref, w_ref, alog_ref, dt_ref, qkv_ref, g_ref, sig_ref):
    first = pl.program_id(1) == 0
    x = x_ref[...]
    halo = jnp.where(first, 0.0, halo_ref[...])
    y = _causal_conv(x, halo, w_ref)
    y = y * jax.nn.sigmoid(y)
    for h in range(N_HEADS):
        for part, post in ((0, HEAD_DIM ** -0.5), (1, 1.0)):
            cs = slice(part * GROUP_W + h * HEAD_DIM, part * GROUP_W + (h + 1) * HEAD_DIM)
            u = y[:, cs]
            un = u * lax.rsqrt(jnp.sum(u * u, axis=-1, keepdims=True) + EPS)
            qkv_ref[:, cs] = un * post if part == 0 else un
    qkv_ref[:, 2 * GROUP_W:] = y[:, 2 * GROUP_W:]
    s = s_ref[...]
    g_ref[...] = -jnp.exp(alog_ref[...]) * _softplus(s + dt_ref[...])
    sig_ref[...] = jax.nn.sigmoid(s)


def _gdn_prep(z32, conv_w, a_log, dt_bias, b, t, *, tt=512):
    tt = min(tt, t)
    nt = t // tt
    w3 = 3 * GROUP_W
    pad = lambda v: jnp.zeros((1, LANE), F32).at[0, _S_GDN_A:_S_GDN_A + N_HEADS].set(v)
    cb = _COL["gdn_q"] // 12
    return pl.pallas_call(
        _gdn_prep_body,
        grid=(b, nt),
        in_specs=[
            pl.BlockSpec((tt, w3), lambda i, j: (i * nt + j, cb)),
            pl.BlockSpec((8, w3), lambda i, j: (jnp.maximum((i * nt + j) * (tt // 8) - 1, 0), cb)),
            pl.BlockSpec((tt, LANE), lambda i, j: (i * nt + j, _COL["small"])),
            pl.BlockSpec((CONV_W, w3), lambda i, j: (0, 0)),
            pl.BlockSpec((1, LANE), lambda i, j: (0, 0)),
            pl.BlockSpec((1, LANE), lambda i, j: (0, 0)),
        ],
        out_specs=[
            pl.BlockSpec((tt, w3), lambda i, j: (i * nt + j, 0)),
            pl.BlockSpec((tt, LANE), lambda i, j: (i * nt + j, 0)),
            pl.BlockSpec((tt, LANE), lambda i, j: (i * nt + j, 0)),
        ],
        out_shape=[jax.ShapeDtypeStruct((b * t, w3), F32), jax.ShapeDtypeStruct((b * t, LANE), F32),
                   jax.ShapeDtypeStruct((b * t, LANE), F32)],
        compiler_params=_cparams(("parallel", "arbitrary")),
        name="gdn_prep",
    )(z32, z32, z32, conv_w, pad(a_log), pad(dt_bias))


def _gdn_intra_body(qkv_ref, g_ref, grow_ref, sig_ref, u_ref, w_ref, qg_ref, kd_ref, qk_ref, gc_ref, *, c, nch):
    ri = lax.broadcasted_iota(jnp.int32, (c, c), 0)
    ci = lax.broadcasted_iota(jnp.int32, (c, c), 1)
    tril = ri >= ci
    strict = ri > ci
    eye = (ri == ci).astype(F32)
    lower = tril.astype(F32)
    upper = (ri <= ci).astype(F32)
    chains = [(ch, h) for ch in range(nch) for h in range(N_HEADS)]
    rsl = lambda ch: slice(ch * c, (ch + 1) * c)
    hsl = lambda h, part=0: slice(part * GROUP_W + h * HEAD_DIM, part * GROUP_W + (h + 1) * HEAD_DIM)
    gc_cols = [_dot_hi(lower, g_ref[rsl(ch), :]) for ch in range(nch)]
    gc_rows = [_dot_hi(grow_ref[ch], upper) for ch in range(nch)]
    for ch in range(nch):
        gc_ref[rsl(ch), :] = gc_cols[ch]
    q = [qkv_ref[rsl(ch), hsl(h, 0)] for ch, h in chains]
    k = [qkv_ref[rsl(ch), hsl(h, 1)] for ch, h in chains]
    v = [qkv_ref[rsl(ch), hsl(h, 2)] for ch, h in chains]
    beta = [sig_ref[rsl(ch), _S_GDN_B + h:_S_GDN_B + h + 1] for ch, h in chains]
    gc = [gc_cols[ch][:, _S_GDN_A + h:_S_GDN_A + h + 1] for ch, h in chains]
    gr = [gc_rows[ch][h:h + 1, :] for ch, h in chains]
    idx = range(len(chains))
    decay = [jnp.where(tril, jnp.exp(jnp.where(tril, gc[i] - gr[i], 0.0)), 0.0) for i in idx]
    kb = [k[i] * beta[i] for i in idx]
    k16 = [x.astype(BF16) for x in k]
    a = [jnp.where(strict, _dot_nt(kb[i].astype(BF16), k16[i]) * decay[i], 0.0) for i in idx]
    same = lambda s: (ri // s) == (ci // s)
    base = 8
    pw = [-jnp.where(same(base), x, 0.0) for x in a]
    tm = [eye + x for x in pw]
    n = 2
    while n < base:
        pws = [_split(x) for x in pw]
        pw = [_dot3(x, x) for x in pws]
        tm = [tm[i] + _dot3(_split(tm[i]), _split(pw[i])) for i in idx]
        n *= 2
    s = base
    while s < c:
        join = same(2 * s) & jnp.logical_not(same(s))
        tms = [_split(x) for x in tm]
        et = [_dot3(_split(jnp.where(join, a[i], 0.0)), tms[i]) for i in idx]
        tm = [tm[i] - _dot3(tms[i], _split(et[i])) for i in idx]
        s *= 2
    tm16 = [x.astype(BF16) for x in tm]
    u = [_dot(tm16[i], (v[i] * beta[i]).astype(BF16)) for i in idx]
    w = [_dot(tm16[i], (kb[i] * jnp.exp(gc[i])).astype(BF16)) for i in idx]
    qk = [_dot_nt(q[i].astype(BF16), k16[i]) * decay[i] for i in idx]
    for i, (ch, h) in enumerate(chains):
        rs, hs = rsl(ch), hsl(h)
        gl = gc[i][c - 1:c, :]
        u_ref[rs, hs] = u[i]
        w_ref[rs, hs] = w[i].astype(BF16)
        qk_ref[rs, h * c:(h + 1) * c] = qk[i].astype(BF16)
        qg_ref[rs, hs] = (q[i] * jnp.exp(gc[i])).astype(BF16)
        kd_ref[rs, hs] = (k[i] * jnp.exp(gl - gc[i])).astype(BF16)


def _gdn_intra(qkv, g, grow, sig, b, t, *, c=GDN_CHUNK, nch=8):
    nt = b * t
    rows = nch * c
    tok = lambda w: pl.BlockSpec((rows, w), lambda i: (i, 0))
    return pl.pallas_call(
        functools.partial(_gdn_intra_body, c=c, nch=nch),
        grid=(nt // rows,),
        in_specs=[tok(3 * GROUP_W), tok(LANE), pl.BlockSpec((nch, 8, c), lambda i: (i, 0, 0)), tok(LANE)],
        out_specs=[tok(GROUP_W), tok(GROUP_W), tok(GROUP_W), tok(GROUP_W), tok(N_HEADS * c), tok(LANE)],
        out_shape=[jax.ShapeDtypeStruct((nt, GROUP_W), F32), jax.ShapeDtypeStruct((nt, GROUP_W), BF16),
                   jax.ShapeDtypeStruct((nt, GROUP_W), BF16), jax.ShapeDtypeStruct((nt, GROUP_W), BF16),
                   jax.ShapeDtypeStruct((nt, N_HEADS * c), BF16), jax.ShapeDtypeStruct((nt, LANE), F32)],
        compiler_params=_cparams(("parallel",)),
        name="gdn_intra",
    )(qkv, g, grow, sig)


def _gdn_scan_body(u_ref, w_ref, qg_ref, kd_ref, qk_ref, gc_ref, z_ref, ng_ref, o_ref, s_scr, *, c, nch, nb):
    @pl.when(pl.program_id(0) == 0)
    def _():
        s_scr[...] = jnp.zeros_like(s_scr)

    chains = [(b, h) for b in range(nb) for h in range(N_HEADS)]
    idx = range(len(chains))
    hsl = lambda h: slice(h * HEAD_DIM, (h + 1) * HEAD_DIM)
    s = [s_scr[i] for i in idx]
    for ch in range(nch):
        rs = slice(ch * c, (ch + 1) * c)
        egl = [jnp.exp(gc_ref[b, (ch + 1) * c - 1:(ch + 1) * c, :]) for b in range(nb)]
        s16 = [x.astype(BF16) for x in s]
        ws = [_dot(w_ref[b, rs, hsl(h)], s16[i]) for i, (b, h) in enumerate(chains)]
        v_new = [(u_ref[b, rs, hsl(h)] - ws[i]).astype(BF16) for i, (b, h) in enumerate(chains)]
        s = [s[i] * egl[b][:, _S_GDN_A + h:_S_GDN_A + h + 1] + _dot_tn(kd_ref[b, rs, hsl(h)], v_new[i])
             for i, (b, h) in enumerate(chains)]
        o = [_dot(qg_ref[b, rs, hsl(h)], s16[i]) + _dot(qk_ref[b, rs, h * c:(h + 1) * c], v_new[i])
             for i, (b, h) in enumerate(chains)]
        for i, (b, h) in enumerate(chains):
            gz = z_ref[b, rs, hsl(h)]
            o_ref[b, rs, hsl(h)] = _rms(o[i], ng_ref[...]) * (gz * jax.nn.sigmoid(gz))
    for i in idx:
        s_scr[i] = s[i]


def _gdn_scan(u, w, qg, kd, qk, gc, z32, norm_g, b, t, *, c=GDN_CHUNK, nch=4):
    rows = nch * c
    v3 = lambda a: a.reshape(b, t, a.shape[-1])
    blk = lambda wd, cb=0: pl.BlockSpec((b, rows, wd), lambda i: (0, i, cb))
    out = pl.pallas_call(
        functools.partial(_gdn_scan_body, c=c, nch=nch, nb=b),
        grid=(t // rows,),
        in_specs=[blk(GROUP_W), blk(GROUP_W), blk(GROUP_W), blk(GROUP_W), blk(N_HEADS * c), blk(LANE),
                  blk(GROUP_W, _COL["gdn_z"] // 4), pl.BlockSpec((1, HEAD_DIM), lambda i: (0, 0))],
        out_specs=blk(GROUP_W),
        out_shape=jax.ShapeDtypeStruct((b, t, GROUP_W), F32),
        scratch_shapes=[pltpu.VMEM((b * N_HEADS, HEAD_DIM, HEAD_DIM), F32)],
        compiler_params=_cparams(("arbitrary",)),
        name="gdn_scan",
    )(v3(u), v3(w), v3(qg), v3(kd), v3(qk), v3(gc), v3(z32), norm_g.reshape(1, HEAD_DIM))
    return out.reshape(b * t, GROUP_W)


def _lru_body(x_ref, halo_ref, gate_ref, cw_ref, cb_ref, wa_ref, ba_ref, wx_ref, bx_ref, lam_ref, o_ref, h_scr):
    first = pl.program_id(1) == 0

    @pl.when(first)
    def _():
        h_scr[...] = jnp.zeros_like(h_scr)

    x = x_ref[...]
    tt = x.shape[0]
    halo = jnp.where(first, 0.0, halo_ref[...])
    xc = _causal_conv(x, halo, cw_ref) + cb_ref[...]
    xc16 = xc.astype(BF16)
    r = jax.nn.sigmoid(_dot(xc16, wa_ref[...]) + ba_ref[...])
    i = jax.nn.sigmoid(_dot(xc16, wx_ref[...]) + bx_ref[...])
    log_a = -LRU_C * r * _softplus(-lam_ref[...])
    a = jnp.exp(log_a)
    u = jnp.sqrt(-jnp.tanh(log_a) * (a * a + 1.0)) * (i * xc)
    rows = lax.broadcasted_iota(jnp.int32, a.shape, 0) % 8
    s = 1
    while s < 8:
        keep = rows >= s
        u = a * jnp.where(keep, pltpu.roll(u, s, 0), 0.0) + u
        a = a * jnp.where(keep, pltpu.roll(a, s, 0), 1.0)
        s *= 2
    gate = _gelu_tanh(gate_ref[...])
    h = h_scr[0:1, :]
    for g in range(tt // 8):
        rs = slice(8 * g, 8 * g + 8)
        hg = a[rs, :] * h + u[rs, :]
        o_ref[rs, :] = hg * gate[rs, :]
        h = hg[7:8, :]
    h_scr[0:1, :] = h


def _lru(z32, conv_w, conv_b, wa_bd, b_a, wx_bd, b_x, lam, b, t, *, tt=512):
    tt = min(tt, t)
    nt = t // tt
    w = GROUP_W
    row = lambda v: v.reshape(1, w)
    vec = pl.BlockSpec((1, w), lambda i, j: (0, 0))
    mat = pl.BlockSpec((w, w), lambda i, j: (0, 0))
    cx = _COL["lru_x"] // 4
    return pl.pallas_call(
        _lru_body,
        grid=(b, nt),
        in_specs=[
            pl.BlockSpec((tt, w), lambda i, j: (i * nt + j, cx)),
            pl.BlockSpec((8, w), lambda i, j: (jnp.maximum((i * nt + j) * (tt // 8) - 1, 0), cx)),
            pl.BlockSpec((tt, w), lambda i, j: (i * nt + j, _COL["lru_gate"] // 4)),
            pl.BlockSpec((CONV_W, w), lambda i, j: (0, 0)),
            vec, mat, vec, mat, vec, vec,
        ],
        out_specs=pl.BlockSpec((tt, w), lambda i, j: (i * nt + j, 0)),
        out_shape=jax.ShapeDtypeStruct((b * t, w), F32),
        scratch_shapes=[pltpu.VMEM((8, w), F32)],
        compiler_params=_cparams(("parallel", "arbitrary")),
        name="rg_lru",
    )(z32, z32, z32, conv_w, row(conv_b), wa_bd, row(b_a), wx_bd, row(b_x), row(lam))


def _nsa_cmp_body(rk_ref, rv_ref, pek_ref, pev_ref, w1k_ref, w1v_ref, w2k_ref, w2v_ref, kc_ref, vc_ref):
    for r_ref, pe_ref, w1_ref, w2_ref, o_ref in ((rk_ref, pek_ref, w1k_ref, w2k_ref, kc_ref),
                                                 (rv_ref, pev_ref, w1v_ref, w2v_ref, vc_ref)):
        r = r_ref[...]
        nr = r.shape[0]
        lo = (r + pe_ref[0:1, :]).astype(BF16)
        hi = (pltpu.roll(r, nr - 1, 0) + pe_ref[1:2, :]).astype(BF16)
        hid = _gelu_tanh(_dot(lo, w1_ref[0]) + _dot(hi, w1_ref[1]))
        o_ref[...] = _dot(hid.astype(BF16), w2_ref[...]).astype(BF16)


def _nsa_compress(rk, rv, pe_k, pe_v, w1_k, w1_v, w2_k, w2_v):
    b, nr, wide = rk.shape
    half = CMP_LEN * HEAD_DIM // 2
    assert wide == half
    rspec = pl.BlockSpec((None, nr, wide), lambda i: (i, 0, 0))
    pspec = pl.BlockSpec((2, half), lambda i: (0, 0))
    w1spec = pl.BlockSpec((2, half, HEAD_DIM), lambda i: (0, 0, 0))
    w2spec = pl.BlockSpec((HEAD_DIM, HEAD_DIM), lambda i: (0, 0))
    ospec = pl.BlockSpec((None, nr, HEAD_DIM), lambda i: (i, 0, 0))
    return pl.pallas_call(
        _nsa_cmp_body,
        grid=(b,),
        in_specs=[rspec, rspec, pspec, pspec, w1spec, w1spec, w2spec, w2spec],
        out_specs=[ospec, ospec],
        out_shape=[jax.ShapeDtypeStruct((b, nr, HEAD_DIM), BF16)] * 2,
        compiler_params=_cparams(("parallel",)),
        name="nsa_compress",
    )(rk, rv, pe_k.reshape(2, half), pe_v.reshape(2, half), w1_k.reshape(2, half, HEAD_DIM),
      w1_v.reshape(2, half, HEAD_DIM), w2_k, w2_v)


def _bias_from_dist(dist, th_ref, rb_ref):
    n = jnp.maximum(dist, 0)
    outs = [jnp.full(dist.shape, rb_ref[0, h] * LOG2E, F32) for h in range(N_HEADS)]
    for bkt in range(1, N_BUCKETS):
        ge = n >= th_ref[bkt]
        outs = [jnp.where(ge, rb_ref[bkt, h] * LOG2E, o) for h, o in enumerate(outs)]
    return outs


def _nsa_bias_body(th_ref, rb_ref, sb_ref, wb_ref, cb_ref, *, tk, n_far, ncp):
    qb = Q_BLOCK
    ii = lax.broadcasted_iota(jnp.int32, (qb, tk), 0)
    jj = lax.broadcasted_iota(jnp.int32, (qb, tk), 1)
    for e in range(n_far + 1):
        tiles = _bias_from_dist(e * qb + ii - jj, th_ref, rb_ref)
        for h in range(N_HEADS):
            sb_ref[e, h * qb:(h + 1) * qb, :] = tiles[h]
    nwin = WINDOW + qb
    dw = lax.broadcasted_iota(jnp.int32, (qb, nwin), 0) - lax.broadcasted_iota(jnp.int32, (qb, nwin), 1) + WINDOW
    tiles = _bias_from_dist(dw, th_ref, rb_ref)
    okw = (dw >= 0) & (dw < WINDOW)
    for h in range(N_HEADS):
        wb_ref[h * qb:(h + 1) * qb, :] = jnp.where(okw, tiles[h], NEG)
    mm = lax.broadcasted_iota(jnp.int32, (2 * ncp, qb), 0) - ncp
    dc = lax.broadcasted_iota(jnp.int32, (2 * ncp, qb), 1) - (mm * CMP_STRIDE + CMP_LEN - 1)
    tiles = _bias_from_dist(dc, th_ref, rb_ref)
    for h in range(N_HEADS):
        cb_ref[:, h * qb:(h + 1) * qb] = jnp.where(dc >= 0, tiles[h], NEG)


def _nsa_far_tiles(tk):
    last = MAX_EXACT * (MAX_DIST / MAX_EXACT) ** ((N_BUCKETS - MAX_EXACT - 1) / (N_BUCKETS - MAX_EXACT))
    return -(-(int(math.ceil(last)) + 1 + tk - 1) // Q_BLOCK)


def _nsa_bias(thresholds, rel_bias, t, *, tk):
    n_far = _nsa_far_tiles(tk)
    ncp = t // CMP_STRIDE
    hq = N_HEADS * Q_BLOCK
    smem = pl.BlockSpec(memory_space=pltpu.SMEM)
    return pl.pallas_call(
        functools.partial(_nsa_bias_body, tk=tk, n_far=n_far, ncp=ncp),
        in_specs=[smem, smem],
        out_shape=[jax.ShapeDtypeStruct((n_far + 1, hq, tk), F32), jax.ShapeDtypeStruct((hq, WINDOW + Q_BLOCK), F32),
                   jax.ShapeDtypeStruct((2 * ncp, hq), F32)],
        compiler_params=pltpu.CompilerParams(vmem_limit_bytes=VMEM_LIMIT),
        name="nsa_bias",
    )(thresholds, rel_bias)


def _nsa_body(q_ref, sig_ref, kc_ref, vc_ref, ks_ref, vs_ref, kw_ref, vw_ref, ovt_ref, ex_ref, sb_ref, wb_ref,
              cb_ref, o_ref, *, tk, n_far, n_slc, sc2):
    qi = pl.program_id(1)
    qb = Q_BLOCK
    hq = N_HEADS * qb
    s0 = qi * qb
    nwin = WINDOW + qb
    r = tk // qb
    q4 = jnp.concatenate([q_ref[:, h * HEAD_DIM:(h + 1) * HEAD_DIM] for h in range(N_HEADS)], axis=0)

    heads = range(N_HEADS)
    qh = [q_ref[:, h * HEAD_DIM:(h + 1) * HEAD_DIM] for h in heads]

    ncp = kc_ref.shape[0]
    c0 = pl.multiple_of(ncp - (qb // CMP_STRIDE) * qi, 8)
    w0 = pl.multiple_of(s0, qb)
    kwin = kw_ref[pl.ds(w0, nwin), :]
    vwin = vw_ref[pl.ds(w0, nwin), :]
    s_c = _dot_nt(kc_ref[...], q4) * sc2 + cb_ref[pl.ds(c0, ncp), :]
    before_start = s0 - WINDOW + lax.broadcasted_iota(jnp.int32, (qb, nwin), 1) < 0
    s_w = [jnp.where(before_start, NEG, _dot_nt(qh[h], kwin) * sc2 + wb_ref[h * qb:(h + 1) * qb, :]) for h in heads]
    m_c = jnp.maximum(jnp.max(s_c, axis=0, keepdims=True), 0.5 * NEG)
    e_c = jnp.exp2(s_c - m_c)
    e_w = [jnp.exp2(s_w[h] - jnp.max(s_w[h], axis=1, keepdims=True)) for h in heads]
    p_c = (e_c * (1.0 / jnp.maximum(jnp.sum(e_c, axis=0, keepdims=True), 1e-30))).astype(BF16)
    r_w = [1.0 / jnp.sum(e_w[h], axis=1, keepdims=True) for h in heads]
    o_c = _dot_tn(p_c, vc_ref[...])
    imp = _dot(ovt_ref[...], p_c[:, 0:qb])
    for h in range(1, N_HEADS):
        imp = imp + _dot(ovt_ref[...], p_c[:, h * qb:(h + 1) * qb])
    o_w = [_dot(e_w[h].astype(BF16), vwin) * r_w[h] for h in heads]

    jsl = lax.broadcasted_iota(jnp.int32, (n_slc, qb), 0)
    cur = (s0 + lax.broadcasted_iota(jnp.int32, (n_slc, qb), 1)) // SLC_LEN
    forced = (jsl == 0) | (jsl == cur) | (jsl == cur - 1)
    imp = jnp.where(forced, FORCED_SCORE, imp)
    imp = jnp.where(jsl > cur, -1.0, imp)
    groups = range(n_slc // 8)
    impg = [imp[8 * v:8 * v + 8, :] for v in groups]
    jloc = lax.broadcasted_iota(jnp.int32, (8, qb), 0)
    rank = [jnp.zeros((8, qb), jnp.int32) for _ in groups]
    for j2 in range(n_slc):
        row = impg[j2 // 8][j2 % 8:j2 % 8 + 1, :]
        for v in groups:
            if 8 * v > j2:
                beats = row >= impg[v]
            elif 8 * v + 7 < j2:
                beats = row > impg[v]
            else:
                beats = (row > impg[v]) | ((row >= impg[v]) & (jloc > j2 % 8))
            rank[v] = rank[v] + jnp.where(beats, 1, 0)
    selneg = jnp.where(jnp.concatenate(rank, axis=0) < min(N_SELECT, n_slc), 0.0, NEG).astype(BF16)

    it = lax.broadcasted_iota(jnp.int32, (qb, tk), 0)
    jt = lax.broadcasted_iota(jnp.int32, (qb, tk), 1)

    def sel_step(kt, carry, diag):
        m, l, acc = carry
        k0 = pl.multiple_of(kt * tk, tk)
        k = ks_ref[pl.ds(k0, tk), :]
        v = vs_ref[pl.ds(k0, tk), :]
        add = _dot_tn(selneg, ex_ref[:, pl.ds(k0, tk)])
        if diag:
            add = jnp.where(s0 + it >= k0 + jt, add, NEG)
        eb = [jnp.clip(qi - (kt * r + j), 0, n_far) for j in range(r)]
        bias = lambda h: jnp.concatenate([sb_ref[eb[j], h * qb:(h + 1) * qb, :] for j in range(r)], axis=1)
        s = [_dot_nt(qh[h], k) * sc2 + bias(h) + add for h in heads]
        wide = lambda x: jnp.concatenate([x] * r, axis=1)
        m_new = tuple(jnp.maximum(m[h], jnp.max(s[h], axis=1, keepdims=True)) for h in heads)
        alpha = [jnp.exp2(m[h] - m_new[h]) for h in heads]
        p = [jnp.exp2(s[h] - wide(m_new[h])) for h in heads]
        l = tuple(alpha[h] * l[h] + jnp.sum(p[h], axis=1, keepdims=True) for h in heads)
        pv = [_dot(p[h].astype(BF16), v) for h in heads]
        acc = tuple(alpha[h] * acc[h] + pv[h] for h in heads)
        return m_new, l, acc

    init = (tuple(jnp.full((qb, LANE), 0.5 * NEG, F32) for _ in heads),
            tuple(jnp.zeros((qb, LANE), F32) for _ in heads), tuple(jnp.zeros((qb, HEAD_DIM), F32) for _ in heads))
    kd = qi // r
    carry = lax.fori_loop(0, kd, functools.partial(sel_step, diag=False), init)
    _, l_s, acc_s = sel_step(kd, carry, True)
    o_s = [acc_s[h] * (1.0 / jnp.maximum(l_s[h], 1e-30)) for h in heads]

    sig = sig_ref[...]
    for h in heads:
        g0 = sig[:, _S_NSA_G + 3 * h:_S_NSA_G + 3 * h + 1]
        g1 = sig[:, _S_NSA_G + 3 * h + 1:_S_NSA_G + 3 * h + 2]
        g2 = sig[:, _S_NSA_G + 3 * h + 2:_S_NSA_G + 3 * h + 3]
        o_ref[:, h * HEAD_DIM:(h + 1) * HEAD_DIM] = g0 * o_c[h * qb:(h + 1) * qb] + g1 * o_s[h] + g2 * o_w[h]


def _nsa(zbf, sig, kc, vc, kw_pad, vw_pad, overlap_t, expand, sb, wb, cb, b, t, *, tk):
    qb = Q_BLOCK
    nq = t // qb
    n_slc = t // SLC_LEN
    ncp = kc.shape[1]
    hq = N_HEADS * qb
    n_far = sb.shape[0] - 1
    per_b = lambda rows: pl.BlockSpec((None, rows, HEAD_DIM), lambda i, j: (i, 0, 0))
    whole = lambda a: pl.BlockSpec(a.shape, lambda i, j: (0,) * a.ndim)
    return pl.pallas_call(
        functools.partial(_nsa_body, tk=tk, n_far=n_far, n_slc=n_slc, sc2=HEAD_DIM ** -0.5 * LOG2E),
        grid=(b, nq),
        in_specs=[
            pl.BlockSpec((qb, GROUP_W), lambda i, j: (i * nq + j, _COL["nsa_q"] // 4)),
            pl.BlockSpec((qb, LANE), lambda i, j: (i * nq + j, 0)),
            per_b(ncp), per_b(ncp),
            pl.BlockSpec((t, HEAD_DIM), lambda i, j: (i, _COL["nsa_ks"])),
            pl.BlockSpec((t, HEAD_DIM), lambda i, j: (i, _COL["nsa_vs"])),
            per_b(t + WINDOW), per_b(t + WINDOW),
            whole(overlap_t), whole(expand), whole(sb), whole(wb), whole(cb),
        ],
        out_specs=pl.BlockSpec((qb, GROUP_W), lambda i, j: (i * nq + j, 0)),
        out_shape=jax.ShapeDtypeStruct((b * t, GROUP_W), F32),
        compiler_params=_cparams(("parallel", "arbitrary")),
        name="nsa_attn",
    )(zbf, sig, kc, vc, zbf, zbf, kw_pad, vw_pad, overlap_t, expand, sb, wb, cb)


def _outproj_body(h_ref, oa_ref, ob_ref, oc_ref, od_ref, g_ref, w_ref, o_ref):
    y = jnp.concatenate([
        _rms(oa_ref[...], g_ref[0:1, :]), ob_ref[...], _rms(oc_ref[...], g_ref[1:2, :]),
        _rms(od_ref[...], g_ref[2:3, :])], axis=-1)
    o_ref[...] = h_ref[...] + _dot(y.astype(BF16), w_ref[...])


def _outproj(h, oa, ob, oc, od, g, w, l, *, tm=512):
    nt, d = h.shape
    tm = min(tm, nt)
    grp = pl.BlockSpec((tm, GROUP_W), lambda i: (i, 0))
    return pl.pallas_call(
        _outproj_body,
        grid=(nt // tm,),
        in_specs=[pl.BlockSpec((tm, d), lambda i: (i, 0)), grp, grp, grp, grp,
                  pl.BlockSpec((3, GROUP_W), lambda i: (0, 0)),
                  pl.BlockSpec((None, 4 * GROUP_W, d), lambda i: (l, 0, 0))],
        out_specs=pl.BlockSpec((tm, d), lambda i: (i, 0)),
        out_shape=jax.ShapeDtypeStruct((nt, d), F32),
        compiler_params=_cparams(("parallel",)),
        name="outproj",
    )(h, oa, ob, oc, od, g, w)


_IN_WIDTHS = (("fox_q", GROUP_W), ("fox_k", GROUP_W), ("fox_v", GROUP_W), ("fox_f", N_HEADS),
              ("gdn_q", GROUP_W), ("gdn_k", GROUP_W), ("gdn_v", GROUP_W), ("gdn_a", N_HEADS), ("gdn_b", N_HEADS),
              ("gdn_z", GROUP_W), ("lru_x", GROUP_W), ("lru_gate", GROUP_W), ("nsa_q", GROUP_W),
              ("nsa_kc", HEAD_DIM), ("nsa_vc", HEAD_DIM), ("nsa_ks", HEAD_DIM), ("nsa_vs", HEAD_DIM),
              ("nsa_kw", HEAD_DIM), ("nsa_vw", HEAD_DIM), ("nsa_g", 3 * N_HEADS))
D_IN = sum(w for _, w in _IN_WIDTHS)


def _in_pieces():
    small_lane = {"fox_f": _S_FOX_F, "gdn_a": _S_GDN_A, "gdn_b": _S_GDN_B, "nsa_g": _S_NSA_G}
    out, off = [], 0
    for name, width in _IN_WIDTHS:
        if name in small_lane:
            out.append((off, width, 1, _COL["small"] * LANE + small_lane[name]))
        else:
            out.append((off, width, 0 if name in _COLS16 else 1, _COL[name] * LANE))
        off += width
    return out


def _relayout_body(w_ref, o16_ref, o32_ref):
    outs = (o16_ref, o32_ref)
    tc = w_ref.shape[1]
    tail = _COL["small"] * LANE
    small = []
    for src, width, which, dst in _in_pieces():
        if dst >= tail and which == 1:
            small.append(w_ref[src:src + width, :])
        else:
            outs[which][dst:dst + width, :] = w_ref[src:src + width, :].astype(outs[which].dtype)
    used = sum(s.shape[0] for s in small)
    small.append(jnp.zeros((W32 - tail - used, tc), F32))
    o32_ref[tail:, :] = jnp.concatenate(small, axis=0).astype(o32_ref.dtype)


def _relayout_w_in(w_t, *, tc=256):
    nl, d_in, d = w_t.shape
    return pl.pallas_call(
        _relayout_body,
        grid=(nl, d // tc),
        in_specs=[pl.BlockSpec((None, d_in, tc), lambda l, i: (l, 0, i))],
        out_specs=[pl.BlockSpec((None, W16, tc), lambda l, i: (l, 0, i)),
                   pl.BlockSpec((None, W32, tc), lambda l, i: (l, 0, i))],
        out_shape=[jax.ShapeDtypeStruct((nl, W16, d), BF16), jax.ShapeDtypeStruct((nl, W32, d), BF16)],
        compiler_params=_cparams(("parallel", "parallel")),
        name="w_in_relayout",
    )(w_t)


def _pad_cols(w):
    src, off = {}, 0
    for name, width in _IN_WIDTHS:
        src[name] = w[..., off:off + width]
        off += width
    zeros = lambda n: jnp.zeros(w.shape[:-1] + (n,), w.dtype)
    small = [src["fox_f"], src["gdn_a"], src["gdn_b"], src["nsa_g"]]
    src["small"] = jnp.concatenate(small + [zeros(LANE - sum(s.shape[-1] for s in small))], axis=-1)
    w16 = jnp.concatenate([src[name] for name in _COLS16], axis=-1)
    w32 = jnp.concatenate([src[name] for name in _COLS32] + [zeros(W32 - (_COL["small"] + 1) * LANE)], axis=-1)
    assert w16.shape[-1] == W16 and w32.shape[-1] == W32
    return w16, w32


def _block_diag(w):
    l, nb, bw, _ = w.shape
    eye = jnp.eye(nb, dtype=w.dtype)
    return jnp.einsum("lnde,nm->lndme", w, eye).reshape(l, nb * bw, nb * bw)


def _bucket_thresholds(t):
    n = jnp.arange(max(t, 2 * MAX_DIST), dtype=jnp.int32)
    nf = jnp.maximum(n, 1).astype(F32)
    large = MAX_EXACT + (jnp.log(nf / MAX_EXACT) / math.log(MAX_DIST / MAX_EXACT)
                         * (N_BUCKETS - MAX_EXACT)).astype(jnp.int32)
    large = jnp.minimum(large, N_BUCKETS - 1)
    bucket = jnp.where(n < MAX_EXACT, n, large)
    return jnp.sum(bucket[None, :] < jnp.arange(N_BUCKETS, dtype=jnp.int32)[:, None], axis=1).astype(jnp.int32)


def _nsa_tile(t):
    return min(1024, t)


def _mixer(h, l, p, b, t):
    zbf, z32 = _inproj(h, p["mix_norm_g"][l], p["w16"], p["b16"], p["w32"], p["b32"], l)

    ccol = _fox_gate(z32, b, t)
    crow = jnp.pad(ccol[:, ::LANE].reshape(b, t, N_HEADS).transpose(0, 2, 1), ((0, 0), (0, 8 - N_HEADS), (0, 0)))
    o_a = _fox(zbf, ccol, crow, b, t)

    qkv, gdn_g, sig = _gdn_prep(z32, p["gdn_conv_w"][l], p["gdn_a_log"][l], p["gdn_dt_bias"][l], b, t)
    c = GDN_CHUNK
    grow = gdn_g[:, _S_GDN_A:_S_GDN_A + N_HEADS].reshape(b * t // c, c, N_HEADS).transpose(0, 2, 1)
    grow = jnp.pad(grow, ((0, 0), (0, 8 - N_HEADS), (0, 0)))
    u, w, qg, kd, qk, gc = _gdn_intra(qkv, gdn_g, grow, sig, b, t)
    o_b = _gdn_scan(u, w, qg, kd, qk, gc, z32, p["gdn_norm_g"][l], b, t)

    o_c = _lru(z32, p["lru_conv_w"][l], p["lru_conv_b"][l], p["lru_wa_bd"][l], p["lru_b_a"][l],
               p["lru_wx_bd"][l], p["lru_b_x"][l], p["lru_lambda"][l], b, t)

    col = lambda name: z32[:, _COL[name] * LANE:(_COL[name] + 1) * LANE]
    cmp_rows = lambda a: a.reshape(b, t // CMP_STRIDE, CMP_STRIDE * HEAD_DIM)
    kc, vc = _nsa_compress(cmp_rows(col("nsa_kc")), cmp_rows(col("nsa_vc")), p["nsa_pe_k"][l], p["nsa_pe_v"][l],
                           p["nsa_w1_k16"][l], p["nsa_w1_v16"][l], p["nsa_w2_k16"][l], p["nsa_w2_v16"][l])
    colbf = lambda name: zbf[:, _COL[name] * LANE:(_COL[name] + 1) * LANE].reshape(b, t, HEAD_DIM)
    kw_pad = jnp.pad(colbf("nsa_kw"), ((0, 0), (WINDOW, 0), (0, 0)))
    vw_pad = jnp.pad(colbf("nsa_vw"), ((0, 0), (WINDOW, 0), (0, 0)))
    o_d = _nsa(zbf, sig, kc, vc, kw_pad, vw_pad, p["overlap_t"], p["expand"], p["sb"], p["wb"], p["cb"], b, t,
               tk=_nsa_tile(t))

    return _outproj(h, o_a, o_b, o_c, o_d, p["out_norm_g"][l], p["w_out16"], l)


def _mixer_consts(rel_bias, t):
    n_cmp_rows = t // CMP_STRIDE
    n_slc = t // SLC_LEN
    cn = jnp.arange(n_cmp_rows)[None, :] * CMP_STRIDE
    sj = jnp.arange(n_slc)[:, None] * SLC_LEN
    sb, wb, cb = _nsa_bias(_bucket_thresholds(t), rel_bias, t, tk=Q_BLOCK)
    return dict(
        overlap_t=((cn <= sj + SLC_LEN - 1) & (cn + CMP_LEN - 1 >= sj)).astype(BF16),
        expand=(jnp.arange(t)[None, :] // SLC_LEN == jnp.arange(n_slc)[:, None]).astype(BF16),
        sb=sb, wb=wb, cb=cb)


def kernel(x, ffn1_norm_g, ffn1_w_gate, ffn1_w_up, ffn1_w_down, mix_norm_g, w_in, b_in, gdn_conv_w, gdn_a_log, gdn_dt_bias, gdn_norm_g, lru_conv_w, lru_conv_b, lru_w_a, lru_b_a, lru_w_x, lru_b_x, lru_lambda, nsa_pe_k, nsa_w1_k, nsa_w2_k, nsa_pe_v, nsa_w1_v, nsa_w2_v, rel_bias, out_norm_g, w_out, ffn2_norm_g, ffn2_w_gate, ffn2_w_up, ffn2_w_down, final_norm_g):
    b, t, d = x.shape
    depth = w_in.shape[0]
    assert w_in.shape[-1] == D_IN
    w16, w32 = _relayout_w_in(jnp.swapaxes(w_in, 1, 2))
    b16, b32 = _pad_cols(b_in[:, None, :])
    p = dict(
        mix_norm_g=mix_norm_g, w16=w16, w32=w32, b16=b16, b32=b32,
        gdn_conv_w=gdn_conv_w, gdn_a_log=gdn_a_log, gdn_dt_bias=gdn_dt_bias, gdn_norm_g=gdn_norm_g,
        lru_conv_w=lru_conv_w, lru_conv_b=lru_conv_b, lru_wa_bd=_block_diag(lru_w_a).astype(BF16), lru_b_a=lru_b_a,
        lru_wx_bd=_block_diag(lru_w_x).astype(BF16), lru_b_x=lru_b_x, lru_lambda=lru_lambda,
        nsa_pe_k=nsa_pe_k, nsa_pe_v=nsa_pe_v, nsa_w1_k16=nsa_w1_k.astype(BF16), nsa_w1_v16=nsa_w1_v.astype(BF16),
        nsa_w2_k16=nsa_w2_k.astype(BF16), nsa_w2_v16=nsa_w2_v.astype(BF16),
        out_norm_g=out_norm_g, w_out16=w_out.astype(BF16),
        **_mixer_consts(rel_bias, t),
    )
    f1 = (ffn1_w_gate, ffn1_w_up, ffn1_w_down)
    f2 = (ffn2_w_gate, ffn2_w_up, ffn2_w_down)
    w16 = tuple(w[0].astype(BF16) for w in f1)
    h = x.reshape(b * t, d)
    for l in range(depth):
        h, w16 = _ffn(h, ffn1_norm_g[l], *w16, nxt=(*f2, l))
        h = _mixer(h, l, p, b, t)
        last = l == depth - 1
        h, w16 = _ffn(h, ffn2_norm_g[l], *w16, final_g=final_norm_g if last else None,
                      nxt=None if last else (*f1, l + 1))
    return h.reshape(b, t, d)
```

```python
import functools
import math

import jax
import jax.numpy as jnp
from jax import lax
from jax.experimental import pallas as pl
from jax.experimental.pallas import tpu as pltpu

F32 = jnp.float32
BF16 = jnp.bfloat16

EPS = 1e-6
LANE = 128
HEAD_DIM = 128
N_HEADS = 4
GROUP_W = N_HEADS * HEAD_DIM
GDN_CHUNK = 64
CONV_W = 4
LRU_BLOCKS = 8
LRU_C = 8.0
CMP_LEN = 32
CMP_STRIDE = 16
SLC_LEN = 64
N_SELECT = 16
WINDOW = 512
FORCED_SCORE = 1.0e6
N_BUCKETS = 32
MAX_EXACT = 16
MAX_DIST = 1024
Q_BLOCK = 128
NEG = -1.0e30
LOG2E = math.log2(math.e)
VMEM_LIMIT = 56 * 1024 * 1024

_COLS16 = ("fox_q", "fox_k", "fox_v", "nsa_q", "nsa_ks", "nsa_vs", "nsa_kw", "nsa_vw")
_COLS32 = ("gdn_q", "gdn_k", "gdn_v", "gdn_z", "lru_x", "lru_gate", "nsa_kc", "nsa_vc", "small")
_COL = dict(fox_q=0, fox_k=4, fox_v=8, nsa_q=12, nsa_ks=16, nsa_vs=17, nsa_kw=18, nsa_vw=19,
            gdn_q=0, gdn_k=4, gdn_v=8, gdn_z=12, lru_x=16, lru_gate=20, nsa_kc=24, nsa_vc=25, small=26)
W16 = 20 * LANE
W32 = 28 * LANE
_S_FOX_F, _S_GDN_A, _S_GDN_B, _S_NSA_G = 0, 4, 8, 12


def _cparams(sem):
    return pltpu.CompilerParams(dimension_semantics=sem, vmem_limit_bytes=VMEM_LIMIT)


def _rms(x, g):
    return x * lax.rsqrt(jnp.mean(x * x, axis=-1, keepdims=True) + EPS) * g


def _softplus(x):
    return jnp.maximum(x, 0.0) + jnp.log1p(jnp.exp(-jnp.abs(x)))


def _gelu_tanh(x):
    return 0.5 * x * (1.0 + jnp.tanh(math.sqrt(2.0 / math.pi) * (x + 0.044715 * (x * x * x))))


def _dot(a, b):
    return jnp.dot(a, b, preferred_element_type=F32)


def _dot_nt(a, b):
    return lax.dot_general(a, b, (((1,), (1,)), ((), ())), preferred_element_type=F32)


def _dot_tn(a, b):
    return lax.dot_general(a, b, (((0,), (0,)), ((), ())), preferred_element_type=F32)


def _dot_hi(a, b):
    return jnp.dot(a, b, preferred_element_type=F32, precision=lax.Precision.HIGHEST)


def _split(x):
    hi = x.astype(BF16)
    return hi, (x - hi.astype(F32)).astype(BF16)


def _dot3(a, b):
    return _dot(a[0], b[0]) + (_dot(a[0], b[1]) + _dot(a[1], b[0]))


def _ffn_body(*refs, final, convert):
    refs = list(refs)
    x_ref, g_ref, wg_ref, wu_ref, wd_ref = refs[:5]
    del refs[:5]
    fg_ref = refs.pop(0) if final else None
    nxt_in = [refs.pop(0) for _ in range(3)] if convert else []
    o_ref = refs.pop(0)
    nxt_out = [refs.pop(0) for _ in range(3)] if convert else []
    n_scr, = refs
    j = pl.program_id(1)

    @pl.when(j == 0)
    def _():
        n_scr[...] = _rms(x_ref[...], g_ref[...]).astype(BF16)
        o_ref[...] = jnp.zeros_like(o_ref)

    n = n_scr[...]
    gate = _dot(n, wg_ref[...])
    up = _dot(n, wu_ref[...])
    a = (gate * jax.nn.sigmoid(gate) * up).astype(BF16)
    o_ref[...] += _dot(a, wd_ref[...])

    for src, dst in zip(nxt_in, nxt_out):
        dst[...] = src[...].astype(BF16)

    @pl.when(j == pl.num_programs(1) - 1)
    def _():
        y = x_ref[...] + 0.5 * o_ref[...]
        if final:
            y = _rms(y, fg_ref[...])
        o_ref[...] = y


def _ffn(x, g, wg, wu, wd, final_g=None, nxt=None, *, tm=512, tf=512):
    nt, d = x.shape
    f = wg.shape[-1]
    tm = min(tm, nt)
    ni, nj = nt // tm, f // tf
    final = final_g is not None
    convert = nxt is not None
    in_specs = [
        pl.BlockSpec((tm, d), lambda i, j: (i, 0)),
        pl.BlockSpec((1, d), lambda i, j: (0, 0)),
        pl.BlockSpec((d, tf), lambda i, j: (0, j)),
        pl.BlockSpec((d, tf), lambda i, j: (0, j)),
        pl.BlockSpec((tf, d), lambda i, j: (j, 0)),
    ]
    args = [x, g.reshape(1, d), wg, wu, wd]
    if final:
        in_specs.append(pl.BlockSpec((1, d), lambda i, j: (0, 0)))
        args.append(final_g.reshape(1, d))
    out_specs = [pl.BlockSpec((tm, d), lambda i, j: (i, 0))]
    out_shape = [jax.ShapeDtypeStruct((nt, d), F32)]
    if convert:
        wg32, wu32, wd32, l = nxt
        assert d % ni == 0
        dr = d // ni
        in_specs += [pl.BlockSpec((None, dr, tf), lambda i, j: (l, i, j)),
                     pl.BlockSpec((None, dr, tf), lambda i, j: (l, i, j)),
                     pl.BlockSpec((None, tf, dr), lambda i, j: (l, j, i))]
        args += [wg32, wu32, wd32]
        out_specs += [pl.BlockSpec((dr, tf), lambda i, j: (i, j)), pl.BlockSpec((dr, tf), lambda i, j: (i, j)),
                      pl.BlockSpec((tf, dr), lambda i, j: (j, i))]
        out_shape += [jax.ShapeDtypeStruct((d, f), BF16), jax.ShapeDtypeStruct((d, f), BF16),
                      jax.ShapeDtypeStruct((f, d), BF16)]
    res = pl.pallas_call(
        functools.partial(_ffn_body, final=final, convert=convert),
        grid=(ni, nj),
        in_specs=in_specs,
        out_specs=out_specs,
        out_shape=out_shape,
        scratch_shapes=[pltpu.VMEM((tm, d), BF16)],
        compiler_params=_cparams(("parallel", "arbitrary")),
        name="ffn",
    )(*args)
    return res[0], (tuple(res[1:]) if convert else None)


def _inproj_body(x_ref, g_ref, w16_ref, b16_ref, w32_ref, b32_ref, o16_ref, o32_ref, n_scr, *, n16):
    j = pl.program_id(1)

    @pl.when(j == 0)
    def _():
        n_scr[...] = _rms(x_ref[...], g_ref[...]).astype(BF16)

    @pl.when(j < n16)
    def _():
        o16_ref[...] = (_dot_nt(n_scr[...], w16_ref[...]) + b16_ref[...]).astype(o16_ref.dtype)

    @pl.when(j >= n16)
    def _():
        o32_ref[...] = _dot_nt(n_scr[...], w32_ref[...]) + b32_ref[...]


def _inproj(x, g, w16, b16, w32, b32, l, *, tm=1024, n16=2, n32=4):
    nt, d = x.shape
    tm = min(tm, nt)
    t16, t32 = w16.shape[1] // n16, w32.shape[1] // n32
    j16 = lambda j: jnp.minimum(j, n16 - 1)
    j32 = lambda j: jnp.maximum(j - n16, 0)
    return pl.pallas_call(
        functools.partial(_inproj_body, n16=n16),
        grid=(nt // tm, n16 + n32),
        in_specs=[
            pl.BlockSpec((tm, d), lambda i, j: (i, 0)),
            pl.BlockSpec((1, d), lambda i, j: (0, 0)),
            pl.BlockSpec((None, t16, d), lambda i, j: (l, j16(j), 0)),
            pl.BlockSpec((None, 1, t16), lambda i, j: (l, 0, j16(j))),
            pl.BlockSpec((None, t32, d), lambda i, j: (l, j32(j), 0)),
            pl.BlockSpec((None, 1, t32), lambda i, j: (l, 0, j32(j))),
        ],
        out_specs=[pl.BlockSpec((tm, t16), lambda i, j: (i, j16(j))),
                   pl.BlockSpec((tm, t32), lambda i, j: (i, j32(j)))],
        out_shape=[jax.ShapeDtypeStruct((nt, w16.shape[1]), BF16), jax.ShapeDtypeStruct((nt, w32.shape[1]), F32)],
        scratch_shapes=[pltpu.VMEM((tm, d), BF16)],
        compiler_params=_cparams(("parallel", "arbitrary")),
        name="inproj",
    )(x, g.reshape(1, d), w16, b16, w32, b32)


def _cumsum_rows(y):
    rows = lax.broadcasted_iota(jnp.int32, y.shape, 0)
    s = 1
    while s < y.shape[0]:
        y = y + jnp.where(rows >= s, pltpu.roll(y, s, 0), 0.0)
        s *= 2
    return y


def _fox_gate_body(f_ref, c_ref):
    x = f_ref[...]
    log_f = jnp.minimum(x, 0.0) - jnp.log1p(jnp.exp(-jnp.abs(x)))
    c = _cumsum_rows(log_f) * LOG2E
    for h in range(N_HEADS):
        c_ref[:, h * LANE:(h + 1) * LANE] = jnp.broadcast_to(c[:, _S_FOX_F + h:_S_FOX_F + h + 1], (c.shape[0], LANE))


def _fox_gate(z32, b, t):
    return pl.pallas_call(
        _fox_gate_body,
        grid=(b,),
        in_specs=[pl.BlockSpec((t, LANE), lambda i: (i, _COL["small"]))],
        out_specs=pl.BlockSpec((t, N_HEADS * LANE), lambda i: (i, 0)),
        out_shape=jax.ShapeDtypeStruct((b * t, N_HEADS * LANE), F32),
        compiler_params=_cparams(("parallel",)),
        name="fox_gate",
    )(z32)


def _fox_body(q_ref, k_ref, v_ref, ccol_ref, crow_ref, o_ref, m_scr, l_scr, acc_scr, *, tq, tk, sc2):
    qi = pl.program_id(1)
    rows = qi * tq + lax.broadcasted_iota(jnp.int32, (tq, tk), 0)
    cols = lax.broadcasted_iota(jnp.int32, (tq, tk), 1)
    m_scr[...] = jnp.full_like(m_scr, NEG)
    l_scr[...] = jnp.zeros_like(l_scr)
    acc_scr[...] = jnp.zeros_like(acc_scr)

    def step(kt, masked):
        k0 = pl.multiple_of(kt * tk, tk)
        heads = range(N_HEADS)
        hsl = [slice(h * HEAD_DIM, (h + 1) * HEAD_DIM) for h in heads]
        wide = lambda x: jnp.concatenate([x] * (tk // LANE), axis=1)
        s = [_dot_nt(q_ref[:, hsl[h]], k_ref[pl.ds(k0, tk), hsl[h]]) * sc2 - crow_ref[h:h + 1, pl.ds(k0, tk)]
             for h in heads]
        if masked:
            causal = rows >= k0 + cols
            s = [jnp.where(causal, x, NEG) for x in s]
        cq = [ccol_ref[:, hsl[h]] for h in heads]
        m_old = [m_scr[h] for h in heads]
        m_new = [jnp.maximum(m_old[h], jnp.max(s[h], axis=1, keepdims=True) + cq[h]) for h in heads]
        alpha = [jnp.exp2(m_old[h] - m_new[h]) for h in heads]
        p = [jnp.exp2(s[h] + wide(cq[h] - m_new[h])) for h in heads]
        l_new = [alpha[h] * l_scr[h] + jnp.sum(p[h], axis=1, keepdims=True) for h in heads]
        pv = [_dot(p[h].astype(BF16), v_ref[pl.ds(k0, tk), hsl[h]]) for h in heads]
        acc_new = [alpha[h] * acc_scr[h] + pv[h] for h in heads]
        for h in heads:
            m_scr[h] = m_new[h]
            l_scr[h] = l_new[h]
            acc_scr[h] = acc_new[h]

    def body(kt, carry):
        step(kt, False)
        return carry

    kd = (qi * tq) // tk
    lax.fori_loop(0, kd, body, 0)
    step(kd, True)
    for h in range(N_HEADS):
        o_ref[:, h * HEAD_DIM:(h + 1) * HEAD_DIM] = acc_scr[h] / l_scr[h]


def _fox(zbf, ccol, crow, b, t, *, tq=256, tk=1024):
    tq, tk = min(tq, t), min(tk, t)
    assert tk % tq == 0
    nq = t // tq
    return pl.pallas_call(
        functools.partial(_fox_body, tq=tq, tk=tk, sc2=HEAD_DIM ** -0.5 * LOG2E),
        grid=(b, nq),
        in_specs=[
            pl.BlockSpec((tq, GROUP_W), lambda i, j: (i * nq + j, _COL["fox_q"] // 4)),
            pl.BlockSpec((t, GROUP_W), lambda i, j: (i, _COL["fox_k"] // 4)),
            pl.BlockSpec((t, GROUP_W), lambda i, j: (i, _COL["fox_v"] // 4)),
            pl.BlockSpec((tq, N_HEADS * LANE), lambda i, j: (i * nq + j, 0)),
            pl.BlockSpec((None, 8, t), lambda i, j: (i, 0, 0)),
        ],
        out_specs=pl.BlockSpec((tq, GROUP_W), lambda i, j: (i * nq + j, 0)),
        out_shape=jax.ShapeDtypeStruct((b * t, GROUP_W), F32),
        scratch_shapes=[pltpu.VMEM((N_HEADS, tq, LANE), F32), pltpu.VMEM((N_HEADS, tq, LANE), F32),
                        pltpu.VMEM((N_HEADS, tq, HEAD_DIM), F32)],
        compiler_params=_cparams(("parallel", "arbitrary")),
        name="fox_attn",
    )(zbf, zbf, zbf, ccol, crow)


def _causal_conv(x, halo, w_ref):
    tt = x.shape[0]
    xx = jnp.concatenate([halo, x], axis=0)
    y = x * w_ref[CONV_W - 1:CONV_W, :]
    for d in range(1, CONV_W):
        y = y + pltpu.roll(xx, d, 0)[8:8 + tt] * w_ref[CONV_W - 1 - d:CONV_W - d, :]
    return y


def _gdn_prep_body(x_ref, halo_ref, s_ref, w_ref, alog_ref, dt_ref, qkv_ref, g_ref, sig_ref):
    first = pl.program_id(1) == 0
    x = x_ref[...]
    halo = jnp.where(first, 0.0, halo_ref[...])
    y = _causal_conv(x, halo, w_ref)
    y = y * jax.nn.sigmoid(y)
    for h in range(N_HEADS):
        for part, post in ((0, HEAD_DIM ** -0.5), (1, 1.0)):
            cs = slice(part * GROUP_W + h * HEAD_DIM, part * GROUP_W + (h + 1) * HEAD_DIM)
            u = y[:, cs]
            un = u * lax.rsqrt(jnp.sum(u * u, axis=-1, keepdims=True) + EPS)
            qkv_ref[:, cs] = un * post if part == 0 else un
    qkv_ref[:, 2 * GROUP_W:] = y[:, 2 * GROUP_W:]
    s = s_ref[...]
    g_ref[...] = -jnp.exp(alog_ref[...]) * _softplus(s + dt_ref[...])
    sig_ref[...] = jax.nn.sigmoid(s)


def _gdn_prep(z32, conv_w, a_log, dt_bias, b, t, *, tt=512):
    tt = min(tt, t)
    nt = t // tt
    w3 = 3 * GROUP_W
    pad = lambda v: jnp.zeros((1, LANE), F32).at[0, _S_GDN_A:_S_GDN_A + N_HEADS].set(v)
    cb = _COL["gdn_q"] // 12
    return pl.pallas_call(
        _gdn_prep_body,
        grid=(b, nt),
        in_specs=[
            pl.BlockSpec((tt, w3), lambda i, j: (i * nt + j, cb)),
            pl.BlockSpec((8, w3), lambda i, j: (jnp.maximum((i * nt + j) * (tt // 8) - 1, 0), cb)),
            pl.BlockSpec((tt, LANE), lambda i, j: (i * nt + j, _COL["small"])),
            pl.BlockSpec((CONV_W, w3), lambda i, j: (0, 0)),
            pl.BlockSpec((1, LANE), lambda i, j: (0, 0)),
            pl.BlockSpec((1, LANE), lambda i, j: (0, 0)),
        ],
        out_specs=[
            pl.BlockSpec((tt, w3), lambda i, j: (i * nt + j, 0)),
            pl.BlockSpec((tt, LANE), lambda i, j: (i * nt + j, 0)),
            pl.BlockSpec((tt, LANE), lambda i, j: (i * nt + j, 0)),
        ],
        out_shape=[jax.ShapeDtypeStruct((b * t, w3), F32), jax.ShapeDtypeStruct((b * t, LANE), F32),
                   jax.ShapeDtypeStruct((b * t, LANE), F32)],
        compiler_params=_cparams(("parallel", "arbitrary")),
        name="gdn_prep",
    )(z32, z32, z32, conv_w, pad(a_log), pad(dt_bias))


def _gdn_intra_body(qkv_ref, g_ref, grow_ref, sig_ref, u_ref, w_ref, qg_ref, kd_ref, qk_ref, gc_ref, *, c, nch):
    ri = lax.broadcasted_iota(jnp.int32, (c, c), 0)
    ci = lax.broadcasted_iota(jnp.int32, (c, c), 1)
    tril = ri >= ci
    strict = ri > ci
    eye = (ri == ci).astype(F32)
    lower = tril.astype(F32)
    upper = (ri <= ci).astype(F32)
    chains = [(ch, h) for ch in range(nch) for h in range(N_HEADS)]
    rsl = lambda ch: slice(ch * c, (ch + 1) * c)
    hsl = lambda h, part=0: slice(part * GROUP_W + h * HEAD_DIM, part * GROUP_W + (h + 1) * HEAD_DIM)
    gc_cols = [_dot_hi(lower, g_ref[rsl(ch), :]) for ch in range(nch)]
    gc_rows = [_dot_hi(grow_ref[ch], upper) for ch in range(nch)]
    for ch in range(nch):
        gc_ref[rsl(ch), :] = gc_cols[ch]
    q = [qkv_ref[rsl(ch), hsl(h, 0)] for ch, h in chains]
    k = [qkv_ref[rsl(ch), hsl(h, 1)] for ch, h in chains]
    v = [qkv_ref[rsl(ch), hsl(h, 2)] for ch, h in chains]
    beta = [sig_ref[rsl(ch), _S_GDN_B + h:_S_GDN_B + h + 1] for ch, h in chains]
    gc = [gc_cols[ch][:, _S_GDN_A + h:_S_GDN_A + h + 1] for ch, h in chains]
    gr = [gc_rows[ch][h:h + 1, :] for ch, h in chains]
    idx = range(len(chains))
    decay = [jnp.where(tril, jnp.exp(jnp.where(tril, gc[i] - gr[i], 0.0)), 0.0) for i in idx]
    kb = [k[i] * beta[i] for i in idx]
    k16 = [x.astype(BF16) for x in k]
    a = [jnp.where(strict, _dot_nt(kb[i].astype(BF16), k16[i]) * decay[i], 0.0) for i in idx]
    same = lambda s: (ri // s) == (ci // s)
    base = 8
    pw = [-jnp.where(same(base), x, 0.0) for x in a]
    tm = [eye + x for x in pw]
    n = 2
    while n < base:
        pws = [_split(x) for x in pw]
        pw = [_dot3(x, x) for x in pws]
        tm = [tm[i] + _dot3(_split(tm[i]), _split(pw[i])) for i in idx]
        n *= 2
    s = base
    while s < c:
        join = same(2 * s) & jnp.logical_not(same(s))
        tms = [_split(x) for x in tm]
        et = [_dot3(_split(jnp.where(join, a[i], 0.0)), tms[i]) for i in idx]
        tm = [tm[i] - _dot3(tms[i], _split(et[i])) for i in idx]
        s *= 2
    tm16 = [x.astype(BF16) for x in tm]
    u = [_dot(tm16[i], (v[i] * beta[i]).astype(BF16)) for i in idx]
    w = [_dot(tm16[i], (kb[i] * jnp.exp(gc[i])).astype(BF16)) for i in idx]
    qk = [_dot_nt(q[i].astype(BF16), k16[i]) * decay[i] for i in idx]
    for i, (ch, h) in enumerate(chains):
        rs, hs = rsl(ch), hsl(h)
        gl = gc[i][c - 1:c, :]
        u_ref[rs, hs] = u[i]
        w_ref[rs, hs] = w[i].astype(BF16)
        qk_ref[rs, h * c:(h + 1) * c] = qk[i].astype(BF16)
        qg_ref[rs, hs] = (q[i] * jnp.exp(gc[i])).astype(BF16)
        kd_ref[rs, hs] = (k[i] * jnp.exp(gl - gc[i])).astype(BF16)


def _gdn_intra(qkv, g, grow, sig, b, t, *, c=GDN_CHUNK, nch=8):
    nt = b * t
    rows = nch * c
    tok = lambda w: pl.BlockSpec((rows, w), lambda i: (i, 0))
    return pl.pallas_call(
        functools.partial(_gdn_intra_body, c=c, nch=nch),
        grid=(nt // rows,),
        in_specs=[tok(3 * GROUP_W), tok(LANE), pl.BlockSpec((nch, 8, c), lambda i: (i, 0, 0)), tok(LANE)],
        out_specs=[tok(GROUP_W), tok(GROUP_W), tok(GROUP_W), tok(GROUP_W), tok(N_HEADS * c), tok(LANE)],
        out_shape=[jax.ShapeDtypeStruct((nt, GROUP_W), F32), jax.ShapeDtypeStruct((nt, GROUP_W), BF16),
                   jax.ShapeDtypeStruct((nt, GROUP_W), BF16), jax.ShapeDtypeStruct((nt, GROUP_W), BF16),
                   jax.ShapeDtypeStruct((nt, N_HEADS * c), BF16), jax.ShapeDtypeStruct((nt, LANE), F32)],
        compiler_params=_cparams(("parallel",)),
        name="gdn_intra",
    )(qkv, g, grow, sig)


def _gdn_scan_body(u_ref, w_ref, qg_ref, kd_ref, qk_ref, gc_ref, z_ref, ng_ref, o_ref, s_scr, *, c, nch, nb):
    @pl.when(pl.program_id(0) == 0)
    def _():
        s_scr[...] = jnp.zeros_like(s_scr)

    chains = [(b, h) for b in range(nb) for h in range(N_HEADS)]
    idx = range(len(chains))
    hsl = lambda h: slice(h * HEAD_DIM, (h + 1) * HEAD_DIM)
    s = [s_scr[i] for i in idx]
    for ch in range(nch):
        rs = slice(ch * c, (ch + 1) * c)
        egl = [jnp.exp(gc_ref[b, (ch + 1) * c - 1:(ch + 1) * c, :]) for b in range(nb)]
        s16 = [x.astype(BF16) for x in s]
        ws = [_dot(w_ref[b, rs, hsl(h)], s16[i]) for i, (b, h) in enumerate(chains)]
        v_new = [(u_ref[b, rs, hsl(h)] - ws[i]).astype(BF16) for i, (b, h) in enumerate(chains)]
        s = [s[i] * egl[b][:, _S_GDN_A + h:_S_GDN_A + h + 1] + _dot_tn(kd_ref[b, rs, hsl(h)], v_new[i])
             for i, (b, h) in enumerate(chains)]
        o = [_dot(qg_ref[b, rs, hsl(h)], s16[i]) + _dot(qk_ref[b, rs, h * c:(h + 1) * c], v_new[i])
             for i, (b, h) in enumerate(chains)]
        for i, (b, h) in enumerate(chains):
            gz = z_ref[b, rs, hsl(h)]
            o_ref[b, rs, hsl(h)] = _rms(o[i], ng_ref[...]) * (gz * jax.nn.sigmoid(gz))
    for i in idx:
        s_scr[i] = s[i]


def _gdn_scan(u, w, qg, kd, qk, gc, z32, norm_g, b, t, *, c=GDN_CHUNK, nch=4):
    rows = nch * c
    v3 = lambda a: a.reshape(b, t, a.shape[-1])
    blk = lambda wd, cb=0: pl.BlockSpec((b, rows, wd), lambda i: (0, i, cb))
    out = pl.pallas_call(
        functools.partial(_gdn_scan_body, c=c, nch=nch, nb=b),
        grid=(t // rows,),
        in_specs=[blk(GROUP_W), blk(GROUP_W), blk(GROUP_W), blk(GROUP_W), blk(N_HEADS * c), blk(LANE),
                  blk(GROUP_W, _COL["gdn_z"] // 4), pl.BlockSpec((1, HEAD_DIM), lambda i: (0, 0))],
        out_specs=blk(GROUP_W),
        out_shape=jax.ShapeDtypeStruct((b, t, GROUP_W), F32),
        scratch_shapes=[pltpu.VMEM((b * N_HEADS, HEAD_DIM, HEAD_DIM), F32)],
        compiler_params=_cparams(("arbitrary",)),
        name="gdn_scan",
    )(v3(u), v3(w), v3(qg), v3(kd), v3(qk), v3(gc), v3(z32), norm_g.reshape(1, HEAD_DIM))
    return out.reshape(b * t, GROUP_W)


def _lru_body(x_ref, halo_ref, gate_ref, cw_ref, cb_ref, wa_ref, ba_ref, wx_ref, bx_ref, lam_ref, o_ref, h_scr):
    first = pl.program_id(1) == 0

    @pl.when(first)
    def _():
        h_scr[...] = jnp.zeros_like(h_scr)

    x = x_ref[...]
    tt = x.shape[0]
    halo = jnp.where(first, 0.0, halo_ref[...])
    xc = _causal_conv(x, halo, cw_ref) + cb_ref[...]
    xc16 = xc.astype(BF16)
    r = jax.nn.sigmoid(_dot(xc16, wa_ref[...]) + ba_ref[...])
    i = jax.nn.sigmoid(_dot(xc16, wx_ref[...]) + bx_ref[...])
    log_a = -LRU_C * r * _softplus(-lam_ref[...])
    a = jnp.exp(log_a)
    u = jnp.sqrt(-jnp.tanh(log_a) * (a * a + 1.0)) * (i * xc)
    rows = lax.broadcasted_iota(jnp.int32, a.shape, 0) % 8
    s = 1
    while s < 8:
        keep = rows >= s
        u = a * jnp.where(keep, pltpu.roll(u, s, 0), 0.0) + u
        a = a * jnp.where(keep, pltpu.roll(a, s, 0), 1.0)
        s *= 2
    gate = _gelu_tanh(gate_ref[...])
    h = h_scr[0:1, :]
    for g in range(tt // 8):
        rs = slice(8 * g, 8 * g + 8)
        hg = a[rs, :] * h + u[rs, :]
        o_ref[rs, :] = hg * gate[rs, :]
        h = hg[7:8, :]
    h_scr[0:1, :] = h


def _lru(z32, conv_w, conv_b, wa_bd, b_a, wx_bd, b_x, lam, b, t, *, tt=512):
    tt = min(tt, t)
    nt = t // tt
    w = GROUP_W
    row = lambda v: v.reshape(1, w)
    vec = pl.BlockSpec((1, w), lambda i, j: (0, 0))
    mat = pl.BlockSpec((w, w), lambda i, j: (0, 0))
    cx = _COL["lru_x"] // 4
    return pl.pallas_call(
        _lru_body,
        grid=(b, nt),
        in_specs=[
            pl.BlockSpec((tt, w), lambda i, j: (i * nt + j, cx)),
            pl.BlockSpec((8, w), lambda i, j: (jnp.maximum((i * nt + j) * (tt // 8) - 1, 0), cx)),
            pl.BlockSpec((tt, w), lambda i, j: (i * nt + j, _COL["lru_gate"] // 4)),
            pl.BlockSpec((CONV_W, w), lambda i, j: (0, 0)),
            vec, mat, vec, mat, vec, vec,
        ],
        out_specs=pl.BlockSpec((tt, w), lambda i, j: (i * nt + j, 0)),
        out_shape=jax.ShapeDtypeStruct((b * t, w), F32),
        scratch_shapes=[pltpu.VMEM((8, w), F32)],
        compiler_params=_cparams(("parallel", "arbitrary")),
        name="rg_lru",
    )(z32, z32, z32, conv_w, row(conv_b), wa_bd, row(b_a), wx_bd, row(b_x), row(lam))


def _nsa_cmp_body(k_ref, v_ref, pek_ref, pev_ref, w1k_ref, w1v_ref, w2k_ref, w2v_ref, kc_ref, vc_ref):
    nr = kc_ref.shape[0]
    for x_ref, pe_ref, w1_ref, w2_ref, o_ref in ((k_ref, pek_ref, w1k_ref, w2k_ref, kc_ref),
                                                 (v_ref, pev_ref, w1v_ref, w2v_ref, vc_ref)):
        r = jnp.concatenate([x_ref[pl.ds(j, nr, stride=CMP_STRIDE), :] for j in range(CMP_STRIDE)], axis=1)
        lo = (r + pe_ref[0:1, :]).astype(BF16)
        hi = (pltpu.roll(r, nr - 1, 0) + pe_ref[1:2, :]).astype(BF16)
        hid = _gelu_tanh(_dot(lo, w1_ref[0]) + _dot(hi, w1_ref[1]))
        o_ref[...] = _dot(hid.astype(BF16), w2_ref[...]).astype(BF16)


def _nsa_compress(z32, pe_k, pe_v, w1_k, w1_v, w2_k, w2_v, b, t):
    nr = t // CMP_STRIDE
    half = CMP_LEN * HEAD_DIM // 2
    kspec = pl.BlockSpec((t, HEAD_DIM), lambda i: (i, _COL["nsa_kc"]))
    vspec = pl.BlockSpec((t, HEAD_DIM), lambda i: (i, _COL["nsa_vc"]))
    pspec = pl.BlockSpec((2, half), lambda i: (0, 0))
    w1spec = pl.BlockSpec((2, half, HEAD_DIM), lambda i: (0, 0, 0))
    w2spec = pl.BlockSpec((HEAD_DIM, HEAD_DIM), lambda i: (0, 0))
    ospec = pl.BlockSpec((None, nr, HEAD_DIM), lambda i: (i, 0, 0))
    return pl.pallas_call(
        _nsa_cmp_body,
        grid=(b,),
        in_specs=[kspec, vspec, pspec, pspec, w1spec, w1spec, w2spec, w2spec],
        out_specs=[ospec, ospec],
        out_shape=[jax.ShapeDtypeStruct((b, nr, HEAD_DIM), BF16)] * 2,
        compiler_params=_cparams(("parallel",)),
        name="nsa_compress",
    )(z32, z32, pe_k.reshape(2, half), pe_v.reshape(2, half), w1_k.reshape(2, half, HEAD_DIM),
      w1_v.reshape(2, half, HEAD_DIM), w2_k, w2_v)


def _bias_from_dist(dist, th_ref, rb_ref):
    n = jnp.maximum(dist, 0)
    outs = [jnp.full(dist.shape, rb_ref[0, h] * LOG2E, F32) for h in range(N_HEADS)]
    for bkt in range(1, N_BUCKETS):
        ge = n >= th_ref[bkt]
        outs = [jnp.where(ge, rb_ref[bkt, h] * LOG2E, o) for h, o in enumerate(outs)]
    return outs


def _nsa_bias_body(th_ref, rb_ref, sb_ref, wb_ref, cb_ref, *, tk, n_far, ncp):
    qb = Q_BLOCK
    ii = lax.broadcasted_iota(jnp.int32, (qb, tk), 0)
    jj = lax.broadcasted_iota(jnp.int32, (qb, tk), 1)
    for e in range(n_far + 1):
        tiles = _bias_from_dist(e * qb + ii - jj, th_ref, rb_ref)
        for h in range(N_HEADS):
            sb_ref[e, h * qb:(h + 1) * qb, :] = tiles[h]
    nwin = WINDOW + qb
    for e in range(WINDOW // qb + 1):
        dw = (lax.broadcasted_iota(jnp.int32, (qb, nwin), 0) - lax.broadcasted_iota(jnp.int32, (qb, nwin), 1)
              + e * qb)
        tiles = _bias_from_dist(dw, th_ref, rb_ref)
        okw = (dw >= 0) & (dw < WINDOW)
        for h in range(N_HEADS):
            wb_ref[e, h * qb:(h + 1) * qb, :] = jnp.where(okw, tiles[h], NEG)
    mm = lax.broadcasted_iota(jnp.int32, (2 * ncp, qb), 0) - ncp
    dc = lax.broadcasted_iota(jnp.int32, (2 * ncp, qb), 1) - (mm * CMP_STRIDE + CMP_LEN - 1)
    tiles = _bias_from_dist(dc, th_ref, rb_ref)
    for h in range(N_HEADS):
        cb_ref[:, h * qb:(h + 1) * qb] = jnp.where(dc >= 0, tiles[h], NEG)


def _nsa_far_tiles(tk):
    last = MAX_EXACT * (MAX_DIST / MAX_EXACT) ** ((N_BUCKETS - MAX_EXACT - 1) / (N_BUCKETS - MAX_EXACT))
    return -(-(int(math.ceil(last)) + 1 + tk - 1) // Q_BLOCK)


def _nsa_bias(thresholds, rel_bias, t, *, tk):
    n_far = _nsa_far_tiles(tk)
    ncp = t // CMP_STRIDE
    hq = N_HEADS * Q_BLOCK
    smem = pl.BlockSpec(memory_space=pltpu.SMEM)
    return pl.pallas_call(
        functools.partial(_nsa_bias_body, tk=tk, n_far=n_far, ncp=ncp),
        in_specs=[smem, smem],
        out_shape=[jax.ShapeDtypeStruct((n_far + 1, hq, tk), F32),
                   jax.ShapeDtypeStruct((WINDOW // Q_BLOCK + 1, hq, WINDOW + Q_BLOCK), F32),
                   jax.ShapeDtypeStruct((2 * ncp, hq), F32)],
        compiler_params=pltpu.CompilerParams(vmem_limit_bytes=VMEM_LIMIT),
        name="nsa_bias",
    )(thresholds, rel_bias)


def _nsa_body(q_ref, sig_ref, kc_ref, vc_ref, ks_ref, vs_ref, kw_ref, vw_ref, ovt_ref, ex_ref, sb_ref, wb_ref,
              cb_ref, o_ref, *, tk, n_far, n_slc, sc2):
    qi = pl.program_id(1)
    qb = Q_BLOCK
    hq = N_HEADS * qb
    s0 = qi * qb
    nwin = WINDOW + qb
    r = tk // qb
    q4 = jnp.concatenate([q_ref[:, h * HEAD_DIM:(h + 1) * HEAD_DIM] for h in range(N_HEADS)], axis=0)

    heads = range(N_HEADS)
    qh = [q_ref[:, h * HEAD_DIM:(h + 1) * HEAD_DIM] for h in heads]

    ncp = kc_ref.shape[0]
    c0 = pl.multiple_of(ncp - (qb // CMP_STRIDE) * qi, 8)
    we = jnp.minimum(qi, WINDOW // qb)
    w0 = pl.multiple_of((qi - we) * qb, qb)
    kwin = kw_ref[pl.ds(w0, nwin), :]
    vwin = vw_ref[pl.ds(w0, nwin), :]
    s_c = _dot_nt(kc_ref[...], q4) * sc2 + cb_ref[pl.ds(c0, ncp), :]
    s_w = [_dot_nt(qh[h], kwin) * sc2 + wb_ref[we, h * qb:(h + 1) * qb, :] for h in heads]
    m_c = jnp.maximum(jnp.max(s_c, axis=0, keepdims=True), 0.5 * NEG)
    e_c = jnp.exp2(s_c - m_c)
    e_w = [jnp.exp2(s_w[h] - jnp.max(s_w[h], axis=1, keepdims=True)) for h in heads]
    p_c = (e_c * (1.0 / jnp.maximum(jnp.sum(e_c, axis=0, keepdims=True), 1e-30))).astype(BF16)
    r_w = [1.0 / jnp.sum(e_w[h], axis=1, keepdims=True) for h in heads]
    o_c = _dot_tn(p_c, vc_ref[...])
    imp = _dot(ovt_ref[...], p_c[:, 0:qb])
    for h in range(1, N_HEADS):
        imp = imp + _dot(ovt_ref[...], p_c[:, h * qb:(h + 1) * qb])
    o_w = [_dot(e_w[h].astype(BF16), vwin) * r_w[h] for h in heads]

    jsl = lax.broadcasted_iota(jnp.int32, (n_slc, qb), 0)
    cur = (s0 + lax.broadcasted_iota(jnp.int32, (n_slc, qb), 1)) // SLC_LEN
    forced = (jsl == 0) | (jsl == cur) | (jsl == cur - 1)
    imp = jnp.where(forced, FORCED_SCORE, imp)
    imp = jnp.where(jsl > cur, -1.0, imp)
    groups = range(n_slc // 8)
    impg = [imp[8 * v:8 * v + 8, :] for v in groups]
    jloc = lax.broadcasted_iota(jnp.int32, (8, qb), 0)
    rank = [jnp.zeros((8, qb), jnp.int32) for _ in groups]
    for j2 in range(n_slc):
        row = impg[j2 // 8][j2 % 8:j2 % 8 + 1, :]
        for v in groups:
            if 8 * v > j2:
                beats = row >= impg[v]
            elif 8 * v + 7 < j2:
                beats = row > impg[v]
            else:
                beats = (row > impg[v]) | ((row >= impg[v]) & (jloc > j2 % 8))
            rank[v] = rank[v] + jnp.where(beats, 1, 0)
    selneg = jnp.where(jnp.concatenate(rank, axis=0) < min(N_SELECT, n_slc), 0.0, NEG).astype(BF16)

    it = lax.broadcasted_iota(jnp.int32, (qb, tk), 0)
    jt = lax.broadcasted_iota(jnp.int32, (qb, tk), 1)

    def sel_step(kt, carry, diag):
        m, l, acc = carry
        k0 = pl.multiple_of(kt * tk, tk)
        k = ks_ref[pl.ds(k0, tk), :]
        v = vs_ref[pl.ds(k0, tk), :]
        add = _dot_tn(selneg, ex_ref[:, pl.ds(k0, tk)])
        if diag:
            add = jnp.where(s0 + it >= k0 + jt, add, NEG)
        eb = [jnp.clip(qi - (kt * r + j), 0, n_far) for j in range(r)]
        bias = lambda h: jnp.concatenate([sb_ref[eb[j], h * qb:(h + 1) * qb, :] for j in range(r)], axis=1)
        s = [_dot_nt(qh[h], k) * sc2 + bias(h) + add for h in heads]
        wide = lambda x: jnp.concatenate([x] * r, axis=1)
        m_new = tuple(jnp.maximum(m[h], jnp.max(s[h], axis=1, keepdims=True)) for h in heads)
        alpha = [jnp.exp2(m[h] - m_new[h]) for h in heads]
        p = [jnp.exp2(s[h] - wide(m_new[h])) for h in heads]
        l = tuple(alpha[h] * l[h] + jnp.sum(p[h], axis=1, keepdims=True) for h in heads)
        pv = [_dot(p[h].astype(BF16), v) for h in heads]
        acc = tuple(alpha[h] * acc[h] + pv[h] for h in heads)
        return m_new, l, acc

    init = (tuple(jnp.full((qb, LANE), 0.5 * NEG, F32) for _ in heads),
            tuple(jnp.zeros((qb, LANE), F32) for _ in heads), tuple(jnp.zeros((qb, HEAD_DIM), F32) for _ in heads))
    kd = qi // r
    carry = lax.fori_loop(0, kd, functools.partial(sel_step, diag=False), init)
    _, l_s, acc_s = sel_step(kd, carry, True)
    o_s = [acc_s[h] * (1.0 / jnp.maximum(l_s[h], 1e-30)) for h in heads]

    sig = sig_ref[...]
    for h in heads:
        g0 = sig[:, _S_NSA_G + 3 * h:_S_NSA_G + 3 * h + 1]
        g1 = sig[:, _S_NSA_G + 3 * h + 1:_S_NSA_G + 3 * h + 2]
        g2 = sig[:, _S_NSA_G + 3 * h + 2:_S_NSA_G + 3 * h + 3]
        o_ref[:, h * HEAD_DIM:(h + 1) * HEAD_DIM] = g0 * o_c[h * qb:(h + 1) * qb] + g1 * o_s[h] + g2 * o_w[h]


def _nsa(zbf, sig, kc, vc, overlap_t, expand, sb, wb, cb, b, t, *, tk):
    qb = Q_BLOCK
    nq = t // qb
    n_slc = t // SLC_LEN
    ncp = kc.shape[1]
    n_far = sb.shape[0] - 1
    assert t >= WINDOW + qb
    per_b = lambda rows: pl.BlockSpec((None, rows, HEAD_DIM), lambda i, j: (i, 0, 0))
    col = lambda name: pl.BlockSpec((t, HEAD_DIM), lambda i, j: (i, _COL[name]))
    whole = lambda a: pl.BlockSpec(a.shape, lambda i, j: (0,) * a.ndim)
    return pl.pallas_call(
        functools.partial(_nsa_body, tk=tk, n_far=n_far, n_slc=n_slc, sc2=HEAD_DIM ** -0.5 * LOG2E),
        grid=(b, nq),
        in_specs=[
            pl.BlockSpec((qb, GROUP_W), lambda i, j: (i * nq + j, _COL["nsa_q"] // 4)),
            pl.BlockSpec((qb, LANE), lambda i, j: (i * nq + j, 0)),
            per_b(ncp), per_b(ncp),
            col("nsa_ks"), col("nsa_vs"), col("nsa_kw"), col("nsa_vw"),
            whole(overlap_t), whole(expand), whole(sb), whole(wb), whole(cb),
        ],
        out_specs=pl.BlockSpec((qb, GROUP_W), lambda i, j: (i * nq + j, 0)),
        out_shape=jax.ShapeDtypeStruct((b * t, GROUP_W), F32),
        compiler_params=_cparams(("parallel", "arbitrary")),
        name="nsa_attn",
    )(zbf, sig, kc, vc, zbf, zbf, zbf, zbf, overlap_t, expand, sb, wb, cb)


def _outproj_body(h_ref, oa_ref, ob_ref, oc_ref, od_ref, g_ref, w_ref, o_ref):
    y = jnp.concatenate([
        _rms(oa_ref[...], g_ref[0:1, :]), ob_ref[...], _rms(oc_ref[...], g_ref[1:2, :]),
        _rms(od_ref[...], g_ref[2:3, :])], axis=-1)
    o_ref[...] = h_ref[...] + _dot(y.astype(BF16), w_ref[...])


def _outproj(h, oa, ob, oc, od, g, w, l, *, tm=512):
    nt, d = h.shape
    tm = min(tm, nt)
    grp = pl.BlockSpec((tm, GROUP_W), lambda i: (i, 0))
    return pl.pallas_call(
        _outproj_body,
        grid=(nt // tm,),
        in_specs=[pl.BlockSpec((tm, d), lambda i: (i, 0)), grp, grp, grp, grp,
                  pl.BlockSpec((3, GROUP_W), lambda i: (0, 0)),
                  pl.BlockSpec((None, 4 * GROUP_W, d), lambda i: (l, 0, 0))],
        out_specs=pl.BlockSpec((tm, d), lambda i: (i, 0)),
        out_shape=jax.ShapeDtypeStruct((nt, d), F32),
        compiler_params=_cparams(("parallel",)),
        name="outproj",
    )(h, oa, ob, oc, od, g, w)


_IN_WIDTHS = (("fox_q", GROUP_W), ("fox_k", GROUP_W), ("fox_v", GROUP_W), ("fox_f", N_HEADS),
              ("gdn_q", GROUP_W), ("gdn_k", GROUP_W), ("gdn_v", GROUP_W), ("gdn_a", N_HEADS), ("gdn_b", N_HEADS),
              ("gdn_z", GROUP_W), ("lru_x", GROUP_W), ("lru_gate", GROUP_W), ("nsa_q", GROUP_W),
              ("nsa_kc", HEAD_DIM), ("nsa_vc", HEAD_DIM), ("nsa_ks", HEAD_DIM), ("nsa_vs", HEAD_DIM),
              ("nsa_kw", HEAD_DIM), ("nsa_vw", HEAD_DIM), ("nsa_g", 3 * N_HEADS))
D_IN = sum(w for _, w in _IN_WIDTHS)


def _in_pieces():
    small_lane = {"fox_f": _S_FOX_F, "gdn_a": _S_GDN_A, "gdn_b": _S_GDN_B, "nsa_g": _S_NSA_G}
    out, off = [], 0
    for name, width in _IN_WIDTHS:
        if name in small_lane:
            out.append((off, width, 1, _COL["small"] * LANE + small_lane[name]))
        else:
            out.append((off, width, 0 if name in _COLS16 else 1, _COL[name] * LANE))
        off += width
    return out


def _relayout_body(w_ref, o16_ref, o32_ref):
    outs = (o16_ref, o32_ref)
    tc = w_ref.shape[1]
    tail = _COL["small"] * LANE
    small = []
    for src, width, which, dst in _in_pieces():
        if dst >= tail and which == 1:
            small.append(w_ref[src:src + width, :])
        else:
            outs[which][dst:dst + width, :] = w_ref[src:src + width, :].astype(outs[which].dtype)
    used = sum(s.shape[0] for s in small)
    small.append(jnp.zeros((W32 - tail - used, tc), F32))
    o32_ref[tail:, :] = jnp.concatenate(small, axis=0).astype(o32_ref.dtype)


def _relayout_w_in(w_t, *, tc=256):
    nl, d_in, d = w_t.shape
    return pl.pallas_call(
        _relayout_body,
        grid=(nl, d // tc),
        in_specs=[pl.BlockSpec((None, d_in, tc), lambda l, i: (l, 0, i))],
        out_specs=[pl.BlockSpec((None, W16, tc), lambda l, i: (l, 0, i)),
                   pl.BlockSpec((None, W32, tc), lambda l, i: (l, 0, i))],
        out_shape=[jax.ShapeDtypeStruct((nl, W16, d), BF16), jax.ShapeDtypeStruct((nl, W32, d), BF16)],
        compiler_params=_cparams(("parallel", "parallel")),
        name="w_in_relayout",
    )(w_t)


def _pad_cols(w):
    src, off = {}, 0
    for name, width in _IN_WIDTHS:
        src[name] = w[..., off:off + width]
        off += width
    zeros = lambda n: jnp.zeros(w.shape[:-1] + (n,), w.dtype)
    small = [src["fox_f"], src["gdn_a"], src["gdn_b"], src["nsa_g"]]
    src["small"] = jnp.concatenate(small + [zeros(LANE - sum(s.shape[-1] for s in small))], axis=-1)
    w16 = jnp.concatenate([src[name] for name in _COLS16], axis=-1)
    w32 = jnp.concatenate([src[name] for name in _COLS32] + [zeros(W32 - (_COL["small"] + 1) * LANE)], axis=-1)
    assert w16.shape[-1] == W16 and w32.shape[-1] == W32
    return w16, w32


def _block_diag(w):
    l, nb, bw, _ = w.shape
    eye = jnp.eye(nb, dtype=w.dtype)
    return jnp.einsum("lnde,nm->lndme", w, eye).reshape(l, nb * bw, nb * bw)


def _bucket_thresholds(t):
    n = jnp.arange(max(t, 2 * MAX_DIST), dtype=jnp.int32)
    nf = jnp.maximum(n, 1).astype(F32)
    large = MAX_EXACT + (jnp.log(nf / MAX_EXACT) / math.log(MAX_DIST / MAX_EXACT)
                         * (N_BUCKETS - MAX_EXACT)).astype(jnp.int32)
    large = jnp.minimum(large, N_BUCKETS - 1)
    bucket = jnp.where(n < MAX_EXACT, n, large)
    return jnp.sum(bucket[None, :] < jnp.arange(N_BUCKETS, dtype=jnp.int32)[:, None], axis=1).astype(jnp.int32)


def _nsa_tile(t):
    return min(1024, t)


def _mixer(h, l, p, b, t):
    zbf, z32 = _inproj(h, p["mix_norm_g"][l], p["w16"], p["b16"], p["w32"], p["b32"], l)

    ccol = _fox_gate(z32, b, t)
    crow = jnp.pad(ccol[:, ::LANE].reshape(b, t, N_HEADS).transpose(0, 2, 1), ((0, 0), (0, 8 - N_HEADS), (0, 0)))
    o_a = _fox(zbf, ccol, crow, b, t)

    qkv, gdn_g, sig = _gdn_prep(z32, p["gdn_conv_w"][l], p["gdn_a_log"][l], p["gdn_dt_bias"][l], b, t)
    c = GDN_CHUNK
    grow = gdn_g[:, _S_GDN_A:_S_GDN_A + N_HEADS].reshape(b * t // c, c, N_HEADS).transpose(0, 2, 1)
    grow = jnp.pad(grow, ((0, 0), (0, 8 - N_HEADS), (0, 0)))
    u, w, qg, kd, qk, gc = _gdn_intra(qkv, gdn_g, grow, sig, b, t)
    o_b = _gdn_scan(u, w, qg, kd, qk, gc, z32, p["gdn_norm_g"][l], b, t)

    o_c = _lru(z32, p["lru_conv_w"][l], p["lru_conv_b"][l], p["lru_wa_bd"][l], p["lru_b_a"][l],
               p["lru_wx_bd"][l], p["lru_b_x"][l], p["lru_lambda"][l], b, t)

    kc, vc = _nsa_compress(z32, p["nsa_pe_k"][l], p["nsa_pe_v"][l], p["nsa_w1_k16"][l], p["nsa_w1_v16"][l],
                           p["nsa_w2_k16"][l], p["nsa_w2_v16"][l], b, t)
    o_d = _nsa(zbf, sig, kc, vc, p["overlap_t"], p["expand"], p["sb"], p["wb"], p["cb"], b, t, tk=_nsa_tile(t))

    return _outproj(h, o_a, o_b, o_c, o_d, p["out_norm_g"][l], p["w_out16"], l)


def _mixer_consts(rel_bias, t):
    n_cmp_rows = t // CMP_STRIDE
    n_slc = t // SLC_LEN
    cn = jnp.arange(n_cmp_rows)[None, :] * CMP_STRIDE
    sj = jnp.arange(n_slc)[:, None] * SLC_LEN
    sb, wb, cb = _nsa_bias(_bucket_thresholds(t), rel_bias, t, tk=Q_BLOCK)
    return dict(
        overlap_t=((cn <= sj + SLC_LEN - 1) & (cn + CMP_LEN - 1 >= sj)).astype(BF16),
        expand=(jnp.arange(t)[None, :] // SLC_LEN == jnp.arange(n_slc)[:, None]).astype(BF16),
        sb=sb, wb=wb, cb=cb)


def kernel(x, ffn1_norm_g, ffn1_w_gate, ffn1_w_up, ffn1_w_down, mix_norm_g, w_in, b_in, gdn_conv_w, gdn_a_log, gdn_dt_bias, gdn_norm_g, lru_conv_w, lru_conv_b, lru_w_a, lru_b_a, lru_w_x, lru_b_x, lru_lambda, nsa_pe_k, nsa_w1_k, nsa_w2_k, nsa_pe_v, nsa_w1_v, nsa_w2_v, rel_bias, out_norm_g, w_out, ffn2_norm_g, ffn2_w_gate, ffn2_w_up, ffn2_w_down, final_norm_g):
    b, t, d = x.shape
    depth = w_in.shape[0]
    assert w_in.shape[-1] == D_IN
    w16, w32 = _relayout_w_in(jnp.swapaxes(w_in, 1, 2))
    b16, b32 = _pad_cols(b_in[:, None, :])
    p = dict(
        mix_norm_g=mix_norm_g, w16=w16, w32=w32, b16=b16, b32=b32,
        gdn_conv_w=gdn_conv_w, gdn_a_log=gdn_a_log, gdn_dt_bias=gdn_dt_bias, gdn_norm_g=gdn_norm_g,
        lru_conv_w=lru_conv_w, lru_conv_b=lru_conv_b, lru_wa_bd=_block_diag(lru_w_a).astype(BF16), lru_b_a=lru_b_a,
        lru_wx_bd=_block_diag(lru_w_x).astype(BF16), lru_b_x=lru_b_x, lru_lambda=lru_lambda,
        nsa_pe_k=nsa_pe_k, nsa_pe_v=nsa_pe_v, nsa_w1_k16=nsa_w1_k.astype(BF16), nsa_w1_v16=nsa_w1_v.astype(BF16),
        nsa_w2_k16=nsa_w2_k.astype(BF16), nsa_w2_v16=nsa_w2_v.astype(BF16),
        out_norm_g=out_norm_g, w_out16=w_out.astype(BF16),
        **_mixer_consts(rel_bias, t),
    )
    f1 = (ffn1_w_gate, ffn1_w_up, ffn1_w_down)
    f2 = (ffn2_w_gate, ffn2_w_up, ffn2_w_down)
    w16 = tuple(w[0].astype(BF16) for w in f1)
    h = x.reshape(b * t, d)
    for l in range(depth):
        h, w16 = _ffn(h, ffn1_norm_g[l], *w16, nxt=(*f2, l))
        h = _mixer(h, l, p, b, t)
        last = l == depth - 1
        h, w16 = _ffn(h, ffn2_norm_g[l], *w16, final_g=final_norm_g if last else None,
                      nxt=None if last else (*f1, l + 1))
    return h.reshape(b, t, d)
```

```python
import functools
import math

import jax
import jax.numpy as jnp
from jax import lax
from jax.experimental import pallas as pl
from jax.experimental.pallas import tpu as pltpu

F32 = jnp.float32
BF16 = jnp.bfloat16

EPS = 1e-6
LANE = 128
SUBLANE = 8
HEAD_DIM = 128
N_HEADS = 4
GROUP_W = N_HEADS * HEAD_DIM
GDN_CHUNK = 64
CONV_W = 4
LRU_C = 8.0
CMP_LEN = 32
CMP_STRIDE = 16
SLC_LEN = 64
N_SELECT = 16
WINDOW = 512
FORCED_SCORE = 1.0e6
N_BUCKETS = 32
MAX_EXACT = 16
MAX_DIST = 1024
Q_BLOCK = 128
NEG = -1.0e30
LOG2E = math.log2(math.e)
VMEM_LIMIT = 56 * 1024 * 1024

_COLS16 = ("fox_q", "fox_k", "fox_v", "nsa_q", "nsa_ks", "nsa_vs", "nsa_kw", "nsa_vw")
_COLS32 = ("gdn_q", "gdn_k", "gdn_v", "gdn_z", "lru_x", "lru_gate", "nsa_kc", "nsa_vc", "small")
_COL = dict(fox_q=0, fox_k=4, fox_v=8, nsa_q=12, nsa_ks=16, nsa_vs=17, nsa_kw=18, nsa_vw=19,
            gdn_q=0, gdn_k=4, gdn_v=8, gdn_z=12, lru_x=16, lru_gate=20, nsa_kc=24, nsa_vc=25, small=26)
W16 = 20 * LANE
W32 = 28 * LANE
_S_FOX_F, _S_GDN_A, _S_GDN_B, _S_NSA_G = 0, 4, 8, 12


def _cparams(sem):
    return pltpu.CompilerParams(dimension_semantics=sem, vmem_limit_bytes=VMEM_LIMIT)


def _rms(x, g):
    return x * lax.rsqrt(jnp.mean(x * x, axis=-1, keepdims=True) + EPS) * g


def _softplus(x):
    return jnp.maximum(x, 0.0) + jnp.log1p(jnp.exp(-jnp.abs(x)))


def _gelu_tanh(x):
    return 0.5 * x * (1.0 + jnp.tanh(math.sqrt(2.0 / math.pi) * (x + 0.044715 * (x * x * x))))


def _dot(a, b):
    return jnp.dot(a, b, preferred_element_type=F32)


def _dot_nt(a, b):
    return lax.dot_general(a, b, (((1,), (1,)), ((), ())), preferred_element_type=F32)


def _dot_tn(a, b):
    return lax.dot_general(a, b, (((0,), (0,)), ((), ())), preferred_element_type=F32)


def _dot_hi(a, b):
    return jnp.dot(a, b, preferred_element_type=F32, precision=lax.Precision.HIGHEST)


def _split(x):
    hi = x.astype(BF16)
    return hi, (x - hi.astype(F32)).astype(BF16)


def _dot3(a, b):
    return _dot(a[0], b[0]) + (_dot(a[0], b[1]) + _dot(a[1], b[0]))


def _ffn_body(*refs, final, convert):
    refs = list(refs)
    x_ref, g_ref, wg_ref, wu_ref, wd_ref = refs[:5]
    del refs[:5]
    fg_ref = refs.pop(0) if final else None
    nxt_in = [refs.pop(0) for _ in range(3)] if convert else []
    o_ref = refs.pop(0)
    nxt_out = [refs.pop(0) for _ in range(3)] if convert else []
    n_scr, = refs
    j = pl.program_id(1)

    @pl.when(j == 0)
    def _():
        n_scr[...] = _rms(x_ref[...], g_ref[...]).astype(BF16)
        o_ref[...] = jnp.zeros_like(o_ref)

    n = n_scr[...]
    gate = _dot(n, wg_ref[...])
    up = _dot(n, wu_ref[...])
    a = (gate * jax.nn.sigmoid(gate) * up).astype(BF16)
    o_ref[...] += _dot(a, wd_ref[...])

    for src, dst in zip(nxt_in, nxt_out):
        dst[...] = src[...].astype(BF16)

    @pl.when(j == pl.num_programs(1) - 1)
    def _():
        y = x_ref[...] + 0.5 * o_ref[...]
        if final:
            y = _rms(y, fg_ref[...])
        o_ref[...] = y


def _ffn(x, g, wg, wu, wd, final_g=None, nxt=None, *, tm=512, tf=512):
    nt, d = x.shape
    f = wg.shape[-1]
    tm = min(tm, nt)
    ni, nj = nt // tm, f // tf
    final = final_g is not None
    convert = nxt is not None
    in_specs = [
        pl.BlockSpec((tm, d), lambda i, j: (i, 0)),
        pl.BlockSpec((1, d), lambda i, j: (0, 0)),
        pl.BlockSpec((d, tf), lambda i, j: (0, j)),
        pl.BlockSpec((d, tf), lambda i, j: (0, j)),
        pl.BlockSpec((tf, d), lambda i, j: (j, 0)),
    ]
    args = [x, g.reshape(1, d), wg, wu, wd]
    if final:
        in_specs.append(pl.BlockSpec((1, d), lambda i, j: (0, 0)))
        args.append(final_g.reshape(1, d))
    out_specs = [pl.BlockSpec((tm, d), lambda i, j: (i, 0))]
    out_shape = [jax.ShapeDtypeStruct((nt, d), F32)]
    if convert:
        wg32, wu32, wd32, l = nxt
        assert d % ni == 0
        dr = d // ni
        in_specs += [pl.BlockSpec((None, dr, tf), lambda i, j: (l, i, j)),
                     pl.BlockSpec((None, dr, tf), lambda i, j: (l, i, j)),
                     pl.BlockSpec((None, tf, dr), lambda i, j: (l, j, i))]
        args += [wg32, wu32, wd32]
        out_specs += [pl.BlockSpec((dr, tf), lambda i, j: (i, j)), pl.BlockSpec((dr, tf), lambda i, j: (i, j)),
                      pl.BlockSpec((tf, dr), lambda i, j: (j, i))]
        out_shape += [jax.ShapeDtypeStruct((d, f), BF16), jax.ShapeDtypeStruct((d, f), BF16),
                      jax.ShapeDtypeStruct((f, d), BF16)]
    res = pl.pallas_call(
        functools.partial(_ffn_body, final=final, convert=convert),
        grid=(ni, nj),
        in_specs=in_specs,
        out_specs=out_specs,
        out_shape=out_shape,
        scratch_shapes=[pltpu.VMEM((tm, d), BF16)],
        compiler_params=_cparams(("parallel", "arbitrary")),
        name="ffn",
    )(*args)
    return res[0], (tuple(res[1:]) if convert else None)


def _inproj_body(x_ref, g_ref, w_ref, b_ref, o_ref, n_scr):
    @pl.when(pl.program_id(1) == 0)
    def _():
        n_scr[...] = _rms(x_ref[...], g_ref[...]).astype(BF16)

    o_ref[...] = (_dot_nt(n_scr[...], w_ref[...]) + b_ref[...]).astype(o_ref.dtype)


def _inproj(x, g, w, b, l, out_dtype, *, n_tiles, tm=1024):
    nt, d = x.shape
    n = w.shape[1]
    tm = min(tm, nt)
    tn = n // n_tiles
    return pl.pallas_call(
        _inproj_body,
        grid=(nt // tm, n_tiles),
        in_specs=[
            pl.BlockSpec((tm, d), lambda i, j: (i, 0)),
            pl.BlockSpec((1, d), lambda i, j: (0, 0)),
            pl.BlockSpec((None, tn, d), lambda i, j: (l, j, 0)),
            pl.BlockSpec((None, 1, tn), lambda i, j: (l, 0, j)),
        ],
        out_specs=pl.BlockSpec((tm, tn), lambda i, j: (i, j)),
        out_shape=jax.ShapeDtypeStruct((nt, n), out_dtype),
        scratch_shapes=[pltpu.VMEM((tm, d), BF16)],
        compiler_params=_cparams(("parallel", "arbitrary")),
        name="inproj",
    )(x, g.reshape(1, d), w, b)


def _cumsum_rows(y):
    rows = lax.broadcasted_iota(jnp.int32, y.shape, 0)
    s = 1
    while s < y.shape[0]:
        y = y + jnp.where(rows >= s, pltpu.roll(y, s, 0), 0.0)
        s *= 2
    return y


def _fox_gate_body(f_ref, c_ref, crow_ref):
    x = f_ref[...]
    log_f = jnp.minimum(x, 0.0) - jnp.log1p(jnp.exp(-jnp.abs(x)))
    c = _cumsum_rows(log_f) * LOG2E
    for h in range(N_HEADS):
        c_ref[:, h * LANE:(h + 1) * LANE] = jnp.broadcast_to(c[:, _S_FOX_F + h:_S_FOX_F + h + 1], (c.shape[0], LANE))
    crow_ref[...] = c.T[_S_FOX_F:_S_FOX_F + SUBLANE, :]


def _fox_gate(z32, b, t):
    return pl.pallas_call(
        _fox_gate_body,
        grid=(b,),
        in_specs=[pl.BlockSpec((t, LANE), lambda i: (i, _COL["small"]))],
        out_specs=[pl.BlockSpec((t, N_HEADS * LANE), lambda i: (i, 0)), pl.BlockSpec((None, SUBLANE, t), lambda i: (i, 0, 0))],
        out_shape=[jax.ShapeDtypeStruct((b * t, N_HEADS * LANE), F32), jax.ShapeDtypeStruct((b, SUBLANE, t), F32)],
        compiler_params=_cparams(("parallel",)),
        name="fox_gate",
    )(z32)


def _fox_body(q_ref, k_ref, v_ref, ccol_ref, crow_ref, o_ref, m_scr, l_scr, acc_scr, *, tq, tk, sc2):
    qi = pl.program_id(1)
    q0 = qi * tq
    m_scr[...] = jnp.full_like(m_scr, NEG)
    l_scr[...] = jnp.zeros_like(l_scr)
    acc_scr[...] = jnp.zeros_like(acc_scr)

    def step(k0, width, masked):
        heads = range(N_HEADS)
        hsl = [slice(h * HEAD_DIM, (h + 1) * HEAD_DIM) for h in heads]
        wide = lambda x: jnp.concatenate([x] * (width // LANE), axis=1)
        s = [_dot_nt(q_ref[:, hsl[h]], k_ref[pl.ds(k0, width), hsl[h]]) * sc2 - crow_ref[h:h + 1, pl.ds(k0, width)]
             for h in heads]
        if masked:
            causal = (q0 + lax.broadcasted_iota(jnp.int32, (tq, width), 0)
                      >= k0 + lax.broadcasted_iota(jnp.int32, (tq, width), 1))
            s = [jnp.where(causal, x, NEG) for x in s]
        cq = [ccol_ref[:, hsl[h]] for h in heads]
        m_old = [m_scr[h] for h in heads]
        m_new = [jnp.maximum(m_old[h], jnp.max(s[h], axis=1, keepdims=True) + cq[h]) for h in heads]
        alpha = [jnp.exp2(m_old[h] - m_new[h]) for h in heads]
        p = [jnp.exp2(s[h] + wide(cq[h] - m_new[h])) for h in heads]
        l_new = [alpha[h] * l_scr[h] + jnp.sum(p[h], axis=1, keepdims=True) for h in heads]
        pv = [_dot(p[h].astype(BF16), v_ref[pl.ds(k0, width), hsl[h]]) for h in heads]
        acc_new = [alpha[h] * acc_scr[h] + pv[h] for h in heads]
        for h in heads:
            m_scr[h] = m_new[h]
            l_scr[h] = l_new[h]
            acc_scr[h] = acc_new[h]

    def wide_tile(kt, carry):
        step(pl.multiple_of(kt * tk, tk), tk, False)
        return carry

    def narrow_tile(kt, carry):
        step(pl.multiple_of(kt * tq, tq), tq, False)
        return carry

    n_wide = q0 // tk
    lax.fori_loop(0, n_wide, wide_tile, 0)
    lax.fori_loop(n_wide * (tk // tq), qi, narrow_tile, 0)
    step(pl.multiple_of(q0, tq), tq, True)
    for h in range(N_HEADS):
        o_ref[:, h * HEAD_DIM:(h + 1) * HEAD_DIM] = acc_scr[h] / l_scr[h]


def _fox(zbf, ccol, crow, b, t, *, tq=256, tk=1024):
    tq, tk = min(tq, t), min(tk, t)
    assert tk % tq == 0
    nq = t // tq
    return pl.pallas_call(
        functools.partial(_fox_body, tq=tq, tk=tk, sc2=HEAD_DIM ** -0.5 * LOG2E),
        grid=(b, nq),
        in_specs=[
            pl.BlockSpec((tq, GROUP_W), lambda i, j: (i * nq + j, _COL["fox_q"] // 4)),
            pl.BlockSpec((t, GROUP_W), lambda i, j: (i, _COL["fox_k"] // 4)),
            pl.BlockSpec((t, GROUP_W), lambda i, j: (i, _COL["fox_v"] // 4)),
            pl.BlockSpec((tq, N_HEADS * LANE), lambda i, j: (i * nq + j, 0)),
            pl.BlockSpec((None, SUBLANE, t), lambda i, j: (i, 0, 0)),
        ],
        out_specs=pl.BlockSpec((tq, GROUP_W), lambda i, j: (i * nq + j, 0)),
        out_shape=jax.ShapeDtypeStruct((b * t, GROUP_W), F32),
        scratch_shapes=[pltpu.VMEM((N_HEADS, tq, LANE), F32), pltpu.VMEM((N_HEADS, tq, LANE), F32),
                        pltpu.VMEM((N_HEADS, tq, HEAD_DIM), F32)],
        compiler_params=_cparams(("parallel", "arbitrary")),
        name="fox_attn",
    )(zbf, zbf, zbf, ccol, crow)


def _causal_conv(x, halo, w_ref):
    tt = x.shape[0]
    xx = jnp.concatenate([halo, x], axis=0)
    y = x * w_ref[CONV_W - 1:CONV_W, :]
    for d in range(1, CONV_W):
        y = y + pltpu.roll(xx, d, 0)[SUBLANE:SUBLANE + tt] * w_ref[CONV_W - 1 - d:CONV_W - d, :]
    return y


def _gdn_prep_body(x_ref, halo_ref, s_ref, w_ref, alog_ref, dt_ref, qkv_ref, g_ref, sig_ref):
    first = pl.program_id(1) == 0
    x = x_ref[...]
    halo = jnp.where(first, 0.0, halo_ref[...])
    y = _causal_conv(x, halo, w_ref)
    y = y * jax.nn.sigmoid(y)
    for h in range(N_HEADS):
        for part, post in ((0, HEAD_DIM ** -0.5), (1, 1.0)):
            cs = slice(part * GROUP_W + h * HEAD_DIM, part * GROUP_W + (h + 1) * HEAD_DIM)
            u = y[:, cs]
            un = u * lax.rsqrt(jnp.sum(u * u, axis=-1, keepdims=True) + EPS)
            qkv_ref[:, cs] = un * post if part == 0 else un
    qkv_ref[:, 2 * GROUP_W:] = y[:, 2 * GROUP_W:]
    s = s_ref[...]
    g_ref[...] = -jnp.exp(alog_ref[...]) * _softplus(s + dt_ref[...])
    sig_ref[...] = jax.nn.sigmoid(s)


def _gdn_prep(z32, conv_w, a_log, dt_bias, b, t, *, tt=512):
    tt = min(tt, t)
    nt = t // tt
    w3 = 3 * GROUP_W
    pad = lambda v: jnp.zeros((1, LANE), F32).at[0, _S_GDN_A:_S_GDN_A + N_HEADS].set(v)
    cb = _COL["gdn_q"] // 12
    return pl.pallas_call(
        _gdn_prep_body,
        grid=(b, nt),
        in_specs=[
            pl.BlockSpec((tt, w3), lambda i, j: (i * nt + j, cb)),
            pl.BlockSpec((SUBLANE, w3), lambda i, j: (jnp.maximum((i * nt + j) * (tt // SUBLANE) - 1, 0), cb)),
            pl.BlockSpec((tt, LANE), lambda i, j: (i * nt + j, _COL["small"])),
            pl.BlockSpec((CONV_W, w3), lambda i, j: (0, 0)),
            pl.BlockSpec((1, LANE), lambda i, j: (0, 0)),
            pl.BlockSpec((1, LANE), lambda i, j: (0, 0)),
        ],
        out_specs=[
            pl.BlockSpec((tt, w3), lambda i, j: (i * nt + j, 0)),
            pl.BlockSpec((tt, LANE), lambda i, j: (i * nt + j, 0)),
            pl.BlockSpec((tt, LANE), lambda i, j: (i * nt + j, 0)),
        ],
        out_shape=[jax.ShapeDtypeStruct((b * t, w3), F32), jax.ShapeDtypeStruct((b * t, LANE), F32),
                   jax.ShapeDtypeStruct((b * t, LANE), F32)],
        compiler_params=_cparams(("parallel", "arbitrary")),
        name="gdn_prep",
    )(z32, z32, z32, conv_w, pad(a_log), pad(dt_bias))


def _gdn_intra_body(qkv_ref, g_ref, grow_ref, sig_ref, u_ref, w_ref, qg_ref, kd_ref, qk_ref, gc_ref, *, c, nch):
    ri = lax.broadcasted_iota(jnp.int32, (c, c), 0)
    ci = lax.broadcasted_iota(jnp.int32, (c, c), 1)
    tril = ri >= ci
    strict = ri > ci
    eye = (ri == ci).astype(F32)
    lower = tril.astype(F32)
    upper = (ri <= ci).astype(F32)
    chains = [(ch, h) for ch in range(nch) for h in range(N_HEADS)]
    rsl = lambda ch: slice(ch * c, (ch + 1) * c)
    hsl = lambda h, part=0: slice(part * GROUP_W + h * HEAD_DIM, part * GROUP_W + (h + 1) * HEAD_DIM)
    gc_cols = [_dot_hi(lower, g_ref[rsl(ch), :]) for ch in range(nch)]
    gc_rows = [_dot_hi(grow_ref[ch], upper) for ch in range(nch)]
    for ch in range(nch):
        gc_ref[rsl(ch), :] = gc_cols[ch]
    q = [qkv_ref[rsl(ch), hsl(h, 0)] for ch, h in chains]
    k = [qkv_ref[rsl(ch), hsl(h, 1)] for ch, h in chains]
    v = [qkv_ref[rsl(ch), hsl(h, 2)] for ch, h in chains]
    beta = [sig_ref[rsl(ch), _S_GDN_B + h:_S_GDN_B + h + 1] for ch, h in chains]
    gc = [gc_cols[ch][:, _S_GDN_A + h:_S_GDN_A + h + 1] for ch, h in chains]
    gr = [gc_rows[ch][h:h + 1, :] for ch, h in chains]
    idx = range(len(chains))
    decay = [jnp.where(tril, jnp.exp(jnp.where(tril, gc[i] - gr[i], 0.0)), 0.0) for i in idx]
    kb = [k[i] * beta[i] for i in idx]
    k16 = [x.astype(BF16) for x in k]
    a = [jnp.where(strict, _dot_nt(kb[i].astype(BF16), k16[i]) * decay[i], 0.0) for i in idx]
    same = lambda s: (ri // s) == (ci // s)
    base = 8
    pw = [-jnp.where(same(base), x, 0.0) for x in a]
    tm = [eye + x for x in pw]
    n = 2
    while n < base:
        pws = [_split(x) for x in pw]
        pw = [_dot3(x, x) for x in pws]
        tm = [tm[i] + _dot3(_split(tm[i]), _split(pw[i])) for i in idx]
        n *= 2
    s = base
    while s < c:
        join = same(2 * s) & jnp.logical_not(same(s))
        tms = [_split(x) for x in tm]
        et = [_dot3(_split(jnp.where(join, a[i], 0.0)), tms[i]) for i in idx]
        tm = [tm[i] - _dot3(tms[i], _split(et[i])) for i in idx]
        s *= 2
    tm16 = [x.astype(BF16) for x in tm]
    u = [_dot(tm16[i], (v[i] * beta[i]).astype(BF16)) for i in idx]
    w = [_dot(tm16[i], (kb[i] * jnp.exp(gc[i])).astype(BF16)) for i in idx]
    qk = [_dot_nt(q[i].astype(BF16), k16[i]) * decay[i] for i in idx]
    for i, (ch, h) in enumerate(chains):
        rs, hs = rsl(ch), hsl(h)
        gl = gc[i][c - 1:c, :]
        u_ref[rs, hs] = u[i]
        w_ref[rs, hs] = w[i].astype(BF16)
        qk_ref[rs, h * c:(h + 1) * c] = qk[i].astype(BF16)
        qg_ref[rs, hs] = (q[i] * jnp.exp(gc[i])).astype(BF16)
        kd_ref[rs, hs] = (k[i] * jnp.exp(gl - gc[i])).astype(BF16)


def _gdn_intra(qkv, g, grow, sig, b, t, *, c=GDN_CHUNK, nch=8):
    nt = b * t
    rows = nch * c
    tok = lambda w: pl.BlockSpec((rows, w), lambda i: (i, 0))
    return pl.pallas_call(
        functools.partial(_gdn_intra_body, c=c, nch=nch),
        grid=(nt // rows,),
        in_specs=[tok(3 * GROUP_W), tok(LANE), pl.BlockSpec((nch, SUBLANE, c), lambda i: (i, 0, 0)), tok(LANE)],
        out_specs=[tok(GROUP_W), tok(GROUP_W), tok(GROUP_W), tok(GROUP_W), tok(N_HEADS * c), tok(LANE)],
        out_shape=[jax.ShapeDtypeStruct((nt, GROUP_W), F32), jax.ShapeDtypeStruct((nt, GROUP_W), BF16),
                   jax.ShapeDtypeStruct((nt, GROUP_W), BF16), jax.ShapeDtypeStruct((nt, GROUP_W), BF16),
                   jax.ShapeDtypeStruct((nt, N_HEADS * c), BF16), jax.ShapeDtypeStruct((nt, LANE), F32)],
        compiler_params=_cparams(("parallel",)),
        name="gdn_intra",
    )(qkv, g, grow, sig)


def _gdn_scan_body(u_ref, w_ref, qg_ref, kd_ref, qk_ref, gc_ref, z_ref, ng_ref, o_ref, s_scr, *, c, nch, nb):
    @pl.when(pl.program_id(0) == 0)
    def _():
        s_scr[...] = jnp.zeros_like(s_scr)

    chains = [(b, h) for b in range(nb) for h in range(N_HEADS)]
    idx = range(len(chains))
    hsl = lambda h: slice(h * HEAD_DIM, (h + 1) * HEAD_DIM)
    s = [s_scr[i] for i in idx]
    for ch in range(nch):
        rs = slice(ch * c, (ch + 1) * c)
        egl = [jnp.exp(gc_ref[b, (ch + 1) * c - 1:(ch + 1) * c, :]) for b in range(nb)]
        s16 = [x.astype(BF16) for x in s]
        ws = [_dot(w_ref[b, rs, hsl(h)], s16[i]) for i, (b, h) in enumerate(chains)]
        v_new = [(u_ref[b, rs, hsl(h)] - ws[i]).astype(BF16) for i, (b, h) in enumerate(chains)]
        s = [s[i] * egl[b][:, _S_GDN_A + h:_S_GDN_A + h + 1] + _dot_tn(kd_ref[b, rs, hsl(h)], v_new[i])
             for i, (b, h) in enumerate(chains)]
        o = [_dot(qg_ref[b, rs, hsl(h)], s16[i]) + _dot(qk_ref[b, rs, h * c:(h + 1) * c], v_new[i])
             for i, (b, h) in enumerate(chains)]
        for i, (b, h) in enumerate(chains):
            gz = z_ref[b, rs, hsl(h)]
            o_ref[b, rs, hsl(h)] = _rms(o[i], ng_ref[...]) * (gz * jax.nn.sigmoid(gz))
    for i in idx:
        s_scr[i] = s[i]


def _gdn_scan(u, w, qg, kd, qk, gc, z32, norm_g, b, t, *, c=GDN_CHUNK, nch=4):
    rows = nch * c
    v3 = lambda a: a.reshape(b, t, a.shape[-1])
    blk = lambda wd, cb=0: pl.BlockSpec((b, rows, wd), lambda i: (0, i, cb))
    out = pl.pallas_call(
        functools.partial(_gdn_scan_body, c=c, nch=nch, nb=b),
        grid=(t // rows,),
        in_specs=[blk(GROUP_W), blk(GROUP_W), blk(GROUP_W), blk(GROUP_W), blk(N_HEADS * c), blk(LANE),
                  blk(GROUP_W, _COL["gdn_z"] // 4), pl.BlockSpec((1, HEAD_DIM), lambda i: (0, 0))],
        out_specs=blk(GROUP_W),
        out_shape=jax.ShapeDtypeStruct((b, t, GROUP_W), F32),
        scratch_shapes=[pltpu.VMEM((b * N_HEADS, HEAD_DIM, HEAD_DIM), F32)],
        compiler_params=_cparams(("arbitrary",)),
        name="gdn_scan",
    )(v3(u), v3(w), v3(qg), v3(kd), v3(qk), v3(gc), v3(z32), norm_g.reshape(1, HEAD_DIM))
    return out.reshape(b * t, GROUP_W)


def _lru_body(x_ref, halo_ref, gate_ref, cw_ref, cb_ref, wa_ref, ba_ref, wx_ref, bx_ref, lam_ref, o_ref, h_scr):
    first = pl.program_id(1) == 0

    @pl.when(first)
    def _():
        h_scr[...] = jnp.zeros_like(h_scr)

    x = x_ref[...]
    tt = x.shape[0]
    halo = jnp.where(first, 0.0, halo_ref[...])
    xc = _causal_conv(x, halo, cw_ref) + cb_ref[...]
    xc16 = xc.astype(BF16)
    r = jax.nn.sigmoid(_dot(xc16, wa_ref[...]) + ba_ref[...])
    i = jax.nn.sigmoid(_dot(xc16, wx_ref[...]) + bx_ref[...])
    log_a = -LRU_C * r * _softplus(-lam_ref[...])
    a = jnp.exp(log_a)
    u = jnp.sqrt(-jnp.tanh(log_a) * (a * a + 1.0)) * (i * xc)
    rows = lax.broadcasted_iota(jnp.int32, a.shape, 0) % SUBLANE
    s = 1
    while s < SUBLANE:
        keep = rows >= s
        u = a * jnp.where(keep, pltpu.roll(u, s, 0), 0.0) + u
        a = a * jnp.where(keep, pltpu.roll(a, s, 0), 1.0)
        s *= 2
    gate = _gelu_tanh(gate_ref[...])
    h = h_scr[0:1, :]
    for g in range(tt // SUBLANE):
        rs = slice(SUBLANE * g, SUBLANE * (g + 1))
        hg = a[rs, :] * h + u[rs, :]
        o_ref[rs, :] = hg * gate[rs, :]
        h = hg[SUBLANE - 1:SUBLANE, :]
    h_scr[0:1, :] = h


def _lru(z32, conv_w, conv_b, wa_bd, b_a, wx_bd, b_x, lam, b, t, *, tt=512):
    tt = min(tt, t)
    nt = t // tt
    w = GROUP_W
    row = lambda v: v.reshape(1, w)
    vec = pl.BlockSpec((1, w), lambda i, j: (0, 0))
    mat = pl.BlockSpec((w, w), lambda i, j: (0, 0))
    cx = _COL["lru_x"] // 4
    return pl.pallas_call(
        _lru_body,
        grid=(b, nt),
        in_specs=[
            pl.BlockSpec((tt, w), lambda i, j: (i * nt + j, cx)),
            pl.BlockSpec((SUBLANE, w), lambda i, j: (jnp.maximum((i * nt + j) * (tt // SUBLANE) - 1, 0), cx)),
            pl.BlockSpec((tt, w), lambda i, j: (i * nt + j, _COL["lru_gate"] // 4)),
            pl.BlockSpec((CONV_W, w), lambda i, j: (0, 0)),
            vec, mat, vec, mat, vec, vec,
        ],
        out_specs=pl.BlockSpec((tt, w), lambda i, j: (i * nt + j, 0)),
        out_shape=jax.ShapeDtypeStruct((b * t, w), F32),
        scratch_shapes=[pltpu.VMEM((SUBLANE, w), F32)],
        compiler_params=_cparams(("parallel", "arbitrary")),
        name="rg_lru",
    )(z32, z32, z32, conv_w, row(conv_b), wa_bd, row(b_a), wx_bd, row(b_x), row(lam))


def _nsa_cmp_body(k_ref, v_ref, pek_ref, pev_ref, w1k_ref, w1v_ref, w2k_ref, w2v_ref, kc_ref, vc_ref):
    nr = kc_ref.shape[0]
    for x_ref, pe_ref, w1_ref, w2_ref, o_ref in ((k_ref, pek_ref, w1k_ref, w2k_ref, kc_ref),
                                                 (v_ref, pev_ref, w1v_ref, w2v_ref, vc_ref)):
        r = jnp.concatenate([x_ref[pl.ds(j, nr, stride=CMP_STRIDE), :] for j in range(CMP_STRIDE)], axis=1)
        lo = (r + pe_ref[0:1, :]).astype(BF16)
        hi = (pltpu.roll(r, nr - 1, 0) + pe_ref[1:2, :]).astype(BF16)
        hid = _gelu_tanh(_dot(lo, w1_ref[0]) + _dot(hi, w1_ref[1]))
        o_ref[...] = _dot(hid.astype(BF16), w2_ref[...]).astype(BF16)


def _nsa_compress(z32, pe_k, pe_v, w1_k, w1_v, w2_k, w2_v, b, t):
    nr = t // CMP_STRIDE
    half = CMP_LEN * HEAD_DIM // 2
    kspec = pl.BlockSpec((t, HEAD_DIM), lambda i: (i, _COL["nsa_kc"]))
    vspec = pl.BlockSpec((t, HEAD_DIM), lambda i: (i, _COL["nsa_vc"]))
    pspec = pl.BlockSpec((2, half), lambda i: (0, 0))
    w1spec = pl.BlockSpec((2, half, HEAD_DIM), lambda i: (0, 0, 0))
    w2spec = pl.BlockSpec((HEAD_DIM, HEAD_DIM), lambda i: (0, 0))
    ospec = pl.BlockSpec((None, nr, HEAD_DIM), lambda i: (i, 0, 0))
    return pl.pallas_call(
        _nsa_cmp_body,
        grid=(b,),
        in_specs=[kspec, vspec, pspec, pspec, w1spec, w1spec, w2spec, w2spec],
        out_specs=[ospec, ospec],
        out_shape=[jax.ShapeDtypeStruct((b, nr, HEAD_DIM), BF16)] * 2,
        compiler_params=_cparams(("parallel",)),
        name="nsa_compress",
    )(z32, z32, pe_k.reshape(2, half), pe_v.reshape(2, half), w1_k.reshape(2, half, HEAD_DIM),
      w1_v.reshape(2, half, HEAD_DIM), w2_k, w2_v)


def _bias_from_dist(dist, th_ref, rb_ref):
    n = jnp.maximum(dist, 0)
    outs = [jnp.full(dist.shape, rb_ref[0, h] * LOG2E, F32) for h in range(N_HEADS)]
    for bkt in range(1, N_BUCKETS):
        ge = n >= th_ref[bkt]
        outs = [jnp.where(ge, rb_ref[bkt, h] * LOG2E, o) for h, o in enumerate(outs)]
    return outs


def _nsa_bias_body(th_ref, rb_ref, sb_ref, wb_ref, cb_ref, *, tk, n_far, ncp):
    qb = Q_BLOCK
    ii = lax.broadcasted_iota(jnp.int32, (qb, tk), 0)
    jj = lax.broadcasted_iota(jnp.int32, (qb, tk), 1)
    for e in range(n_far + 1):
        tiles = _bias_from_dist(e * qb + ii - jj, th_ref, rb_ref)
        for h in range(N_HEADS):
            sb_ref[e, h * qb:(h + 1) * qb, :] = tiles[h]
    nwin = WINDOW + qb
    for e in range(WINDOW // qb + 1):
        dw = (lax.broadcasted_iota(jnp.int32, (qb, nwin), 0) - lax.broadcasted_iota(jnp.int32, (qb, nwin), 1)
              + e * qb)
        tiles = _bias_from_dist(dw, th_ref, rb_ref)
        okw = (dw >= 0) & (dw < WINDOW)
        for h in range(N_HEADS):
            wb_ref[e, h * qb:(h + 1) * qb, :] = jnp.where(okw, tiles[h], NEG)
    mm = lax.broadcasted_iota(jnp.int32, (2 * ncp, qb), 0) - ncp
    dc = lax.broadcasted_iota(jnp.int32, (2 * ncp, qb), 1) - (mm * CMP_STRIDE + CMP_LEN - 1)
    tiles = _bias_from_dist(dc, th_ref, rb_ref)
    for h in range(N_HEADS):
        cb_ref[:, h * qb:(h + 1) * qb] = jnp.where(dc >= 0, tiles[h], NEG)


def _nsa_far_tiles(tk):
    last = MAX_EXACT * (MAX_DIST / MAX_EXACT) ** ((N_BUCKETS - MAX_EXACT - 1) / (N_BUCKETS - MAX_EXACT))
    return -(-(int(math.ceil(last)) + 1 + tk - 1) // Q_BLOCK)


def _nsa_bias(thresholds, rel_bias, t, *, tk):
    n_far = _nsa_far_tiles(tk)
    ncp = t // CMP_STRIDE
    hq = N_HEADS * Q_BLOCK
    smem = pl.BlockSpec(memory_space=pltpu.SMEM)
    return pl.pallas_call(
        functools.partial(_nsa_bias_body, tk=tk, n_far=n_far, ncp=ncp),
        in_specs=[smem, smem],
        out_shape=[jax.ShapeDtypeStruct((n_far + 1, hq, tk), F32),
                   jax.ShapeDtypeStruct((WINDOW // Q_BLOCK + 1, hq, WINDOW + Q_BLOCK), F32),
                   jax.ShapeDtypeStruct((2 * ncp, hq), F32)],
        compiler_params=pltpu.CompilerParams(vmem_limit_bytes=VMEM_LIMIT),
        name="nsa_bias",
    )(thresholds, rel_bias)


def _nsa_body(q_ref, sig_ref, kc_ref, vc_ref, ks_ref, vs_ref, kw_ref, vw_ref, ovt_ref, ex_ref, sb_ref, wb_ref,
              cb_ref, o_ref, *, tk, n_far, n_slc, sc2):
    qi = pl.program_id(1)
    qb = Q_BLOCK
    s0 = qi * qb
    nwin = WINDOW + qb
    heads = range(N_HEADS)
    qh = [q_ref[:, h * HEAD_DIM:(h + 1) * HEAD_DIM] for h in heads]
    q4 = jnp.concatenate(qh, axis=0)

    ncp = kc_ref.shape[0]
    c0 = pl.multiple_of(ncp - (qb // CMP_STRIDE) * qi, SUBLANE)
    we = jnp.minimum(qi, WINDOW // qb)
    w0 = pl.multiple_of((qi - we) * qb, qb)
    kwin = kw_ref[pl.ds(w0, nwin), :]
    vwin = vw_ref[pl.ds(w0, nwin), :]
    s_c = _dot_nt(kc_ref[...], q4) * sc2 + cb_ref[pl.ds(c0, ncp), :]
    s_w = [_dot_nt(qh[h], kwin) * sc2 + wb_ref[we, h * qb:(h + 1) * qb, :] for h in heads]
    m_c = jnp.maximum(jnp.max(s_c, axis=0, keepdims=True), 0.5 * NEG)
    e_c = jnp.exp2(s_c - m_c)
    e_w = [jnp.exp2(s_w[h] - jnp.max(s_w[h], axis=1, keepdims=True)) for h in heads]
    p_c = (e_c * (1.0 / jnp.maximum(jnp.sum(e_c, axis=0, keepdims=True), 1e-30))).astype(BF16)
    r_w = [1.0 / jnp.sum(e_w[h], axis=1, keepdims=True) for h in heads]
    o_c = _dot_tn(p_c, vc_ref[...])
    imp = _dot(ovt_ref[...], p_c[:, 0:qb])
    for h in range(1, N_HEADS):
        imp = imp + _dot(ovt_ref[...], p_c[:, h * qb:(h + 1) * qb])
    o_w = [_dot(e_w[h].astype(BF16), vwin) * r_w[h] for h in heads]

    jsl = lax.broadcasted_iota(jnp.int32, (n_slc, qb), 0)
    cur = (s0 + lax.broadcasted_iota(jnp.int32, (n_slc, qb), 1)) // SLC_LEN
    forced = (jsl == 0) | (jsl == cur) | (jsl == cur - 1)
    imp = jnp.where(forced, FORCED_SCORE, imp)
    imp = jnp.where(jsl > cur, -1.0, imp)
    sg = SUBLANE
    groups = range(n_slc // sg)
    impg = [imp[sg * v:sg * (v + 1), :] for v in groups]
    jloc = lax.broadcasted_iota(jnp.int32, (sg, qb), 0)
    rank = [jnp.zeros((sg, qb), jnp.int32) for _ in groups]
    for j2 in range(n_slc):
        row = impg[j2 // sg][j2 % sg:j2 % sg + 1, :]
        for v in groups:
            if sg * v > j2:
                beats = row >= impg[v]
            elif sg * (v + 1) <= j2:
                beats = row > impg[v]
            else:
                beats = (row > impg[v]) | ((row >= impg[v]) & (jloc > j2 % sg))
            rank[v] = rank[v] + jnp.where(beats, 1, 0)
    selneg = jnp.where(jnp.concatenate(rank, axis=0) < min(N_SELECT, n_slc), 0.0, NEG).astype(BF16)

    def sel_step(kt, carry, width, diag):
        m, l, acc = carry
        nb = width // qb
        k0 = pl.multiple_of(kt * width, width)
        k = ks_ref[pl.ds(k0, width), :]
        v = vs_ref[pl.ds(k0, width), :]
        add = _dot_tn(selneg, ex_ref[:, pl.ds(k0, width)])
        if diag:
            causal = (s0 + lax.broadcasted_iota(jnp.int32, (qb, width), 0)
                      >= k0 + lax.broadcasted_iota(jnp.int32, (qb, width), 1))
            add = jnp.where(causal, add, NEG)
        eb = [jnp.clip(qi - (kt * nb + j), 0, n_far) for j in range(nb)]
        bias = lambda h: jnp.concatenate([sb_ref[eb[j], h * qb:(h + 1) * qb, :] for j in range(nb)], axis=1)
        s = [_dot_nt(qh[h], k) * sc2 + bias(h) + add for h in heads]
        wide = lambda x: jnp.concatenate([x] * nb, axis=1)
        m_new = tuple(jnp.maximum(m[h], jnp.max(s[h], axis=1, keepdims=True)) for h in heads)
        alpha = [jnp.exp2(m[h] - m_new[h]) for h in heads]
        p = [jnp.exp2(s[h] - wide(m_new[h])) for h in heads]
        l = tuple(alpha[h] * l[h] + jnp.sum(p[h], axis=1, keepdims=True) for h in heads)
        pv = [_dot(p[h].astype(BF16), v) for h in heads]
        acc = tuple(alpha[h] * acc[h] + pv[h] for h in heads)
        return m_new, l, acc

    init = (tuple(jnp.full((qb, LANE), 0.5 * NEG, F32) for _ in heads),
            tuple(jnp.zeros((qb, LANE), F32) for _ in heads), tuple(jnp.zeros((qb, HEAD_DIM), F32) for _ in heads))
    kd = s0 // tk
    carry = lax.fori_loop(0, kd, functools.partial(sel_step, width=tk, diag=False), init)
    _, l_s, acc_s = sel_step(kd, carry, tk, True)
    o_s = [acc_s[h] * (1.0 / jnp.maximum(l_s[h], 1e-30)) for h in heads]

    sig = sig_ref[...]
    for h in heads:
        g0 = sig[:, _S_NSA_G + 3 * h:_S_NSA_G + 3 * h + 1]
        g1 = sig[:, _S_NSA_G + 3 * h + 1:_S_NSA_G + 3 * h + 2]
        g2 = sig[:, _S_NSA_G + 3 * h + 2:_S_NSA_G + 3 * h + 3]
        o_ref[:, h * HEAD_DIM:(h + 1) * HEAD_DIM] = g0 * o_c[h * qb:(h + 1) * qb] + g1 * o_s[h] + g2 * o_w[h]


def _nsa(zbf, sig, kc, vc, overlap_t, expand, sb, wb, cb, b, t, *, tk):
    qb = Q_BLOCK
    nq = t // qb
    n_slc = t // SLC_LEN
    ncp = kc.shape[1]
    n_far = sb.shape[0] - 1
    assert t >= WINDOW + qb
    per_b = lambda rows: pl.BlockSpec((None, rows, HEAD_DIM), lambda i, j: (i, 0, 0))
    col = lambda name: pl.BlockSpec((t, HEAD_DIM), lambda i, j: (i, _COL[name]))
    whole = lambda a: pl.BlockSpec(a.shape, lambda i, j: (0,) * a.ndim)
    return pl.pallas_call(
        functools.partial(_nsa_body, tk=tk, n_far=n_far, n_slc=n_slc, sc2=HEAD_DIM ** -0.5 * LOG2E),
        grid=(b, nq),
        in_specs=[
            pl.BlockSpec((qb, GROUP_W), lambda i, j: (i * nq + j, _COL["nsa_q"] // 4)),
            pl.BlockSpec((qb, LANE), lambda i, j: (i * nq + j, 0)),
            per_b(ncp), per_b(ncp),
            col("nsa_ks"), col("nsa_vs"), col("nsa_kw"), col("nsa_vw"),
            whole(overlap_t), whole(expand), whole(sb), whole(wb), whole(cb),
        ],
        out_specs=pl.BlockSpec((qb, GROUP_W), lambda i, j: (i * nq + j, 0)),
        out_shape=jax.ShapeDtypeStruct((b * t, GROUP_W), F32),
        compiler_params=_cparams(("parallel", "arbitrary")),
        name="nsa_attn",
    )(zbf, sig, kc, vc, zbf, zbf, zbf, zbf, overlap_t, expand, sb, wb, cb)


def _outproj_body(h_ref, oa_ref, ob_ref, oc_ref, od_ref, g_ref, w_ref, o_ref):
    y = jnp.concatenate([
        _rms(oa_ref[...], g_ref[0:1, :]), ob_ref[...], _rms(oc_ref[...], g_ref[1:2, :]),
        _rms(od_ref[...], g_ref[2:3, :])], axis=-1)
    o_ref[...] = h_ref[...] + _dot(y.astype(BF16), w_ref[...])


def _outproj(h, oa, ob, oc, od, g, w, l, *, tm=512):
    nt, d = h.shape
    tm = min(tm, nt)
    grp = pl.BlockSpec((tm, GROUP_W), lambda i: (i, 0))
    return pl.pallas_call(
        _outproj_body,
        grid=(nt // tm,),
        in_specs=[pl.BlockSpec((tm, d), lambda i: (i, 0)), grp, grp, grp, grp,
                  pl.BlockSpec((3, GROUP_W), lambda i: (0, 0)),
                  pl.BlockSpec((None, 4 * GROUP_W, d), lambda i: (l, 0, 0))],
        out_specs=pl.BlockSpec((tm, d), lambda i: (i, 0)),
        out_shape=jax.ShapeDtypeStruct((nt, d), F32),
        compiler_params=_cparams(("parallel",)),
        name="outproj",
    )(h, oa, ob, oc, od, g, w)


_IN_WIDTHS = (("fox_q", GROUP_W), ("fox_k", GROUP_W), ("fox_v", GROUP_W), ("fox_f", N_HEADS),
              ("gdn_q", GROUP_W), ("gdn_k", GROUP_W), ("gdn_v", GROUP_W), ("gdn_a", N_HEADS), ("gdn_b", N_HEADS),
              ("gdn_z", GROUP_W), ("lru_x", GROUP_W), ("lru_gate", GROUP_W), ("nsa_q", GROUP_W),
              ("nsa_kc", HEAD_DIM), ("nsa_vc", HEAD_DIM), ("nsa_ks", HEAD_DIM), ("nsa_vs", HEAD_DIM),
              ("nsa_kw", HEAD_DIM), ("nsa_vw", HEAD_DIM), ("nsa_g", 3 * N_HEADS))
D_IN = sum(w for _, w in _IN_WIDTHS)


def _in_pieces():
    small_lane = {"fox_f": _S_FOX_F, "gdn_a": _S_GDN_A, "gdn_b": _S_GDN_B, "nsa_g": _S_NSA_G}
    out, off = [], 0
    for name, width in _IN_WIDTHS:
        if name in small_lane:
            out.append((off, width, 1, _COL["small"] * LANE + small_lane[name]))
        else:
            out.append((off, width, 0 if name in _COLS16 else 1, _COL[name] * LANE))
        off += width
    return out


def _relayout_body(w_ref, o16_ref, o32_ref):
    outs = (o16_ref, o32_ref)
    tc = w_ref.shape[1]
    tail = _COL["small"] * LANE
    small = []
    for src, width, which, dst in _in_pieces():
        if dst >= tail and which == 1:
            small.append(w_ref[src:src + width, :])
        else:
            outs[which][dst:dst + width, :] = w_ref[src:src + width, :].astype(outs[which].dtype)
    used = sum(s.shape[0] for s in small)
    small.append(jnp.zeros((W32 - tail - used, tc), F32))
    o32_ref[tail:, :] = jnp.concatenate(small, axis=0).astype(o32_ref.dtype)


def _relayout_w_in(w_t, *, tc=256):
    nl, d_in, d = w_t.shape
    return pl.pallas_call(
        _relayout_body,
        grid=(nl, d // tc),
        in_specs=[pl.BlockSpec((None, d_in, tc), lambda l, i: (l, 0, i))],
        out_specs=[pl.BlockSpec((None, W16, tc), lambda l, i: (l, 0, i)),
                   pl.BlockSpec((None, W32, tc), lambda l, i: (l, 0, i))],
        out_shape=[jax.ShapeDtypeStruct((nl, W16, d), BF16), jax.ShapeDtypeStruct((nl, W32, d), BF16)],
        compiler_params=_cparams(("parallel", "parallel")),
        name="w_in_relayout",
    )(w_t)


def _pad_cols(w):
    src, off = {}, 0
    for name, width in _IN_WIDTHS:
        src[name] = w[..., off:off + width]
        off += width
    zeros = lambda n: jnp.zeros(w.shape[:-1] + (n,), w.dtype)
    small = [src["fox_f"], src["gdn_a"], src["gdn_b"], src["nsa_g"]]
    src["small"] = jnp.concatenate(small + [zeros(LANE - sum(s.shape[-1] for s in small))], axis=-1)
    w16 = jnp.concatenate([src[name] for name in _COLS16], axis=-1)
    w32 = jnp.concatenate([src[name] for name in _COLS32] + [zeros(W32 - (_COL["small"] + 1) * LANE)], axis=-1)
    assert w16.shape[-1] == W16 and w32.shape[-1] == W32
    return w16, w32


def _block_diag(w):
    l, nb, bw, _ = w.shape
    eye = jnp.eye(nb, dtype=w.dtype)
    return jnp.einsum("lnde,nm->lndme", w, eye).reshape(l, nb * bw, nb * bw)


def _bucket_thresholds(t):
    n = jnp.arange(max(t, 2 * MAX_DIST), dtype=jnp.int32)
    nf = jnp.maximum(n, 1).astype(F32)
    large = MAX_EXACT + (jnp.log(nf / MAX_EXACT) / math.log(MAX_DIST / MAX_EXACT)
                         * (N_BUCKETS - MAX_EXACT)).astype(jnp.int32)
    large = jnp.minimum(large, N_BUCKETS - 1)
    bucket = jnp.where(n < MAX_EXACT, n, large)
    return jnp.sum(bucket[None, :] < jnp.arange(N_BUCKETS, dtype=jnp.int32)[:, None], axis=1).astype(jnp.int32)


def _nsa_tile(t):
    return min(1024, t)


def _mixer(h, l, p, b, t):
    zbf = _inproj(h, p["mix_norm_g"][l], p["w16"], p["b16"], l, BF16, n_tiles=2)
    z32 = _inproj(h, p["mix_norm_g"][l], p["w32"], p["b32"], l, F32, n_tiles=4)

    ccol, crow = _fox_gate(z32, b, t)
    o_a = _fox(zbf, ccol, crow, b, t)

    qkv, gdn_g, sig = _gdn_prep(z32, p["gdn_conv_w"][l], p["gdn_a_log"][l], p["gdn_dt_bias"][l], b, t)
    c = GDN_CHUNK
    grow = gdn_g[:, _S_GDN_A:_S_GDN_A + N_HEADS].reshape(b * t // c, c, N_HEADS).transpose(0, 2, 1)
    grow = jnp.pad(grow, ((0, 0), (0, SUBLANE - N_HEADS), (0, 0)))
    u, w, qg, kd, qk, gc = _gdn_intra(qkv, gdn_g, grow, sig, b, t)
    o_b = _gdn_scan(u, w, qg, kd, qk, gc, z32, p["gdn_norm_g"][l], b, t)

    o_c = _lru(z32, p["lru_conv_w"][l], p["lru_conv_b"][l], p["lru_wa_bd"][l], p["lru_b_a"][l],
               p["lru_wx_bd"][l], p["lru_b_x"][l], p["lru_lambda"][l], b, t)

    kc, vc = _nsa_compress(z32, p["nsa_pe_k"][l], p["nsa_pe_v"][l], p["nsa_w1_k16"][l], p["nsa_w1_v16"][l],
                           p["nsa_w2_k16"][l], p["nsa_w2_v16"][l], b, t)
    o_d = _nsa(zbf, sig, kc, vc, p["overlap_t"], p["expand"], p["sb"], p["wb"], p["cb"], b, t, tk=_nsa_tile(t))

    return _outproj(h, o_a, o_b, o_c, o_d, p["out_norm_g"][l], p["w_out16"], l)


def _mixer_consts(rel_bias, t):
    n_cmp_rows = t // CMP_STRIDE
    n_slc = t // SLC_LEN
    cn = jnp.arange(n_cmp_rows)[None, :] * CMP_STRIDE
    sj = jnp.arange(n_slc)[:, None] * SLC_LEN
    sb, wb, cb = _nsa_bias(_bucket_thresholds(t), rel_bias, t, tk=Q_BLOCK)
    return dict(
        overlap_t=((cn <= sj + SLC_LEN - 1) & (cn + CMP_LEN - 1 >= sj)).astype(BF16),
        expand=(jnp.arange(t)[None, :] // SLC_LEN == jnp.arange(n_slc)[:, None]).astype(BF16),
        sb=sb, wb=wb, cb=cb)


def kernel(x, ffn1_norm_g, ffn1_w_gate, ffn1_w_up, ffn1_w_down, mix_norm_g, w_in, b_in, gdn_conv_w, gdn_a_log, gdn_dt_bias, gdn_norm_g, lru_conv_w, lru_conv_b, lru_w_a, lru_b_a, lru_w_x, lru_b_x, lru_lambda, nsa_pe_k, nsa_w1_k, nsa_w2_k, nsa_pe_v, nsa_w1_v, nsa_w2_v, rel_bias, out_norm_g, w_out, ffn2_norm_g, ffn2_w_gate, ffn2_w_up, ffn2_w_down, final_norm_g):
    b, t, d = x.shape
    depth = w_in.shape[0]
    assert w_in.shape[-1] == D_IN
    w16, w32 = _relayout_w_in(jnp.swapaxes(w_in, 1, 2))
    b16, b32 = _pad_cols(b_in[:, None, :])
    p = dict(
        mix_norm_g=mix_norm_g, w16=w16, w32=w32, b16=b16, b32=b32,
        gdn_conv_w=gdn_conv_w, gdn_a_log=gdn_a_log, gdn_dt_bias=gdn_dt_bias, gdn_norm_g=gdn_norm_g,
        lru_conv_w=lru_conv_w, lru_conv_b=lru_conv_b, lru_wa_bd=_block_diag(lru_w_a).astype(BF16), lru_b_a=lru_b_a,
        lru_wx_bd=_block_diag(lru_w_x).astype(BF16), lru_b_x=lru_b_x, lru_lambda=lru_lambda,
        nsa_pe_k=nsa_pe_k, nsa_pe_v=nsa_pe_v, nsa_w1_k16=nsa_w1_k.astype(BF16), nsa_w1_v16=nsa_w1_v.astype(BF16),
        nsa_w2_k16=nsa_w2_k.astype(BF16), nsa_w2_v16=nsa_w2_v.astype(BF16),
        out_norm_g=out_norm_g, w_out16=w_out.astype(BF16),
        **_mixer_consts(rel_bias, t),
    )
    f1 = (ffn1_w_gate, ffn1_w_up, ffn1_w_down)
    f2 = (ffn2_w_gate, ffn2_w_up, ffn2_w_down)
    w16 = tuple(w[0].astype(BF16) for w in f1)
    h = x.reshape(b * t, d)
    for l in range(depth):
        h, w16 = _ffn(h, ffn1_norm_g[l], *w16, nxt=(*f2, l))
        h = _mixer(h, l, p, b, t)
        last = l == depth - 1
        h, w16 = _ffn(h, ffn2_norm_g[l], *w16, final_g=final_norm_g if last else None,
                      nxt=None if last else (*f1, l + 1))
    return h.reshape(b, t, d)
```

```python
import functools
import math

import jax
import jax.numpy as jnp
from jax import lax
from jax.experimental import pallas as pl
from jax.experimental.pallas import tpu as pltpu

F32 = jnp.float32
BF16 = jnp.bfloat16

EPS = 1e-6
LANE = 128
SUBLANE = 8
HEAD_DIM = 128
N_HEADS = 4
GROUP_W = N_HEADS * HEAD_DIM
GDN_CHUNK = 64
CONV_W = 4
LRU_C = 8.0
CMP_LEN = 32
CMP_STRIDE = 16
SLC_LEN = 64
N_SELECT = 16
WINDOW = 512
FORCED_SCORE = 1.0e6
N_BUCKETS = 32
MAX_EXACT = 16
MAX_DIST = 1024
Q_BLOCK = 128
NEG = -1.0e30
LOG2E = math.log2(math.e)
VMEM_LIMIT = 56 * 1024 * 1024

_COLS16 = ("fox_q", "fox_k", "fox_v", "nsa_q", "nsa_ks", "nsa_vs", "nsa_kw", "nsa_vw")
_COLS32 = ("gdn_q", "gdn_k", "gdn_v", "gdn_z", "lru_x", "lru_gate", "nsa_kc", "nsa_vc", "small")
_COL = dict(fox_q=0, fox_k=4, fox_v=8, nsa_q=12, nsa_ks=16, nsa_vs=17, nsa_kw=18, nsa_vw=19,
            gdn_q=0, gdn_k=4, gdn_v=8, gdn_z=12, lru_x=16, lru_gate=20, nsa_kc=24, nsa_vc=25, small=26)
W16 = 20 * LANE
W32 = 28 * LANE
_S_FOX_F, _S_GDN_A, _S_GDN_B, _S_NSA_G = 0, 4, 8, 12


def _cparams(sem):
    return pltpu.CompilerParams(dimension_semantics=sem, vmem_limit_bytes=VMEM_LIMIT)


def _rms(x, g):
    return x * lax.rsqrt(jnp.mean(x * x, axis=-1, keepdims=True) + EPS) * g


def _softplus(x):
    return jnp.maximum(x, 0.0) + jnp.log1p(jnp.exp(-jnp.abs(x)))


def _gelu_tanh(x):
    return 0.5 * x * (1.0 + jnp.tanh(math.sqrt(2.0 / math.pi) * (x + 0.044715 * (x * x * x))))


def _dot(a, b):
    return jnp.dot(a, b, preferred_element_type=F32)


def _dot_nt(a, b):
    return lax.dot_general(a, b, (((1,), (1,)), ((), ())), preferred_element_type=F32)


def _dot_tn(a, b):
    return lax.dot_general(a, b, (((0,), (0,)), ((), ())), preferred_element_type=F32)


def _dot_hi(a, b):
    return jnp.dot(a, b, preferred_element_type=F32, precision=lax.Precision.HIGHEST)


def _split(x):
    hi = x.astype(BF16)
    return hi, (x - hi.astype(F32)).astype(BF16)


def _dot3(a, b):
    return _dot(a[0], b[0]) + (_dot(a[0], b[1]) + _dot(a[1], b[0]))


def _ffn_body(*refs, final, convert):
    refs = list(refs)
    x_ref, g_ref, wg_ref, wu_ref, wd_ref = refs[:5]
    del refs[:5]
    fg_ref = refs.pop(0) if final else None
    nxt_in = [refs.pop(0) for _ in range(3)] if convert else []
    o_ref = refs.pop(0)
    nxt_out = [refs.pop(0) for _ in range(3)] if convert else []
    n_scr, = refs
    j = pl.program_id(1)

    @pl.when(j == 0)
    def _():
        n_scr[...] = _rms(x_ref[...], g_ref[...]).astype(BF16)
        o_ref[...] = jnp.zeros_like(o_ref)

    n = n_scr[...]
    gate = _dot(n, wg_ref[...])
    up = _dot(n, wu_ref[...])
    a = (gate * jax.nn.sigmoid(gate) * up).astype(BF16)
    o_ref[...] += _dot(a, wd_ref[...])

    for src, dst in zip(nxt_in, nxt_out):
        dst[...] = src[...].astype(BF16)

    @pl.when(j == pl.num_programs(1) - 1)
    def _():
        y = x_ref[...] + 0.5 * o_ref[...]
        if final:
            y = _rms(y, fg_ref[...])
        o_ref[...] = y


def _ffn(x, g, wg, wu, wd, final_g=None, nxt=None, *, tm=512, tf=512):
    nt, d = x.shape
    f = wg.shape[-1]
    tm = min(tm, nt)
    ni, nj = nt // tm, f // tf
    final = final_g is not None
    convert = nxt is not None
    in_specs = [
        pl.BlockSpec((tm, d), lambda i, j: (i, 0)),
        pl.BlockSpec((1, d), lambda i, j: (0, 0)),
        pl.BlockSpec((d, tf), lambda i, j: (0, j)),
        pl.BlockSpec((d, tf), lambda i, j: (0, j)),
        pl.BlockSpec((tf, d), lambda i, j: (j, 0)),
    ]
    args = [x, g.reshape(1, d), wg, wu, wd]
    if final:
        in_specs.append(pl.BlockSpec((1, d), lambda i, j: (0, 0)))
        args.append(final_g.reshape(1, d))
    out_specs = [pl.BlockSpec((tm, d), lambda i, j: (i, 0))]
    out_shape = [jax.ShapeDtypeStruct((nt, d), F32)]
    if convert:
        wg32, wu32, wd32, l = nxt
        assert d % ni == 0
        dr = d // ni
        in_specs += [pl.BlockSpec((None, dr, tf), lambda i, j: (l, i, j)),
                     pl.BlockSpec((None, dr, tf), lambda i, j: (l, i, j)),
                     pl.BlockSpec((None, tf, dr), lambda i, j: (l, j, i))]
        args += [wg32, wu32, wd32]
        out_specs += [pl.BlockSpec((dr, tf), lambda i, j: (i, j)), pl.BlockSpec((dr, tf), lambda i, j: (i, j)),
                      pl.BlockSpec((tf, dr), lambda i, j: (j, i))]
        out_shape += [jax.ShapeDtypeStruct((d, f), BF16), jax.ShapeDtypeStruct((d, f), BF16),
                      jax.ShapeDtypeStruct((f, d), BF16)]
    res = pl.pallas_call(
        functools.partial(_ffn_body, final=final, convert=convert),
        grid=(ni, nj),
        in_specs=in_specs,
        out_specs=out_specs,
        out_shape=out_shape,
        scratch_shapes=[pltpu.VMEM((tm, d), BF16)],
        compiler_params=_cparams(("parallel", "arbitrary")),
        name="ffn",
    )(*args)
    return res[0], (tuple(res[1:]) if convert else None)


def _inproj_body(x_ref, g_ref, w_ref, b_ref, o_ref, n_scr):
    @pl.when(pl.program_id(1) == 0)
    def _():
        n_scr[...] = _rms(x_ref[...], g_ref[...]).astype(BF16)

    o_ref[...] = (_dot_nt(n_scr[...], w_ref[...]) + b_ref[...]).astype(o_ref.dtype)


def _inproj(x, g, w, b, l, out_dtype, *, n_tiles, tm=1024):
    nt, d = x.shape
    n = w.shape[1]
    tm = min(tm, nt)
    tn = n // n_tiles
    return pl.pallas_call(
        _inproj_body,
        grid=(nt // tm, n_tiles),
        in_specs=[
            pl.BlockSpec((tm, d), lambda i, j: (i, 0)),
            pl.BlockSpec((1, d), lambda i, j: (0, 0)),
            pl.BlockSpec((None, tn, d), lambda i, j: (l, j, 0)),
            pl.BlockSpec((None, 1, tn), lambda i, j: (l, 0, j)),
        ],
        out_specs=pl.BlockSpec((tm, tn), lambda i, j: (i, j)),
        out_shape=jax.ShapeDtypeStruct((nt, n), out_dtype),
        scratch_shapes=[pltpu.VMEM((tm, d), BF16)],
        compiler_params=_cparams(("parallel", "arbitrary")),
        name="inproj",
    )(x, g.reshape(1, d), w, b)


def _cumsum_rows(y):
    rows = lax.broadcasted_iota(jnp.int32, y.shape, 0)
    s = 1
    while s < y.shape[0]:
        y = y + jnp.where(rows >= s, pltpu.roll(y, s, 0), 0.0)
        s *= 2
    return y


def _fox_gate_body(f_ref, c_ref, crow_ref):
    x = f_ref[...]
    log_f = jnp.minimum(x, 0.0) - jnp.log1p(jnp.exp(-jnp.abs(x)))
    c = _cumsum_rows(log_f) * LOG2E
    for h in range(N_HEADS):
        c_ref[:, h * LANE:(h + 1) * LANE] = jnp.broadcast_to(c[:, _S_FOX_F + h:_S_FOX_F + h + 1], (c.shape[0], LANE))
    crow_ref[...] = c.T[_S_FOX_F:_S_FOX_F + SUBLANE, :]


def _fox_gate(z32, b, t):
    return pl.pallas_call(
        _fox_gate_body,
        grid=(b,),
        in_specs=[pl.BlockSpec((t, LANE), lambda i: (i, _COL["small"]))],
        out_specs=[pl.BlockSpec((t, N_HEADS * LANE), lambda i: (i, 0)), pl.BlockSpec((None, SUBLANE, t), lambda i: (i, 0, 0))],
        out_shape=[jax.ShapeDtypeStruct((b * t, N_HEADS * LANE), F32), jax.ShapeDtypeStruct((b, SUBLANE, t), F32)],
        compiler_params=_cparams(("parallel",)),
        name="fox_gate",
    )(z32)


def _fox_body(q_ref, k_ref, v_ref, ccol_ref, crow_ref, o_ref, m_scr, l_scr, acc_scr, *, tq, tk, sc2):
    qi = pl.program_id(1)
    q0 = qi * tq
    m_scr[...] = jnp.full_like(m_scr, NEG)
    l_scr[...] = jnp.zeros_like(l_scr)
    acc_scr[...] = jnp.zeros_like(acc_scr)

    def step(k0, width, masked):
        heads = range(N_HEADS)
        hsl = [slice(h * HEAD_DIM, (h + 1) * HEAD_DIM) for h in heads]
        wide = lambda x: jnp.concatenate([x] * (width // LANE), axis=1)
        s = [_dot_nt(q_ref[:, hsl[h]], k_ref[pl.ds(k0, width), hsl[h]]) * sc2 - crow_ref[h:h + 1, pl.ds(k0, width)]
             for h in heads]
        if masked:
            causal = (q0 + lax.broadcasted_iota(jnp.int32, (tq, width), 0)
                      >= k0 + lax.broadcasted_iota(jnp.int32, (tq, width), 1))
            s = [jnp.where(causal, x, NEG) for x in s]
        cq = [ccol_ref[:, hsl[h]] for h in heads]
        m_old = [m_scr[h] for h in heads]
        m_new = [jnp.maximum(m_old[h], jnp.max(s[h], axis=1, keepdims=True) + cq[h]) for h in heads]
        alpha = [jnp.exp2(m_old[h] - m_new[h]) for h in heads]
        p = [jnp.exp2(s[h] + wide(cq[h] - m_new[h])) for h in heads]
        l_new = [alpha[h] * l_scr[h] + jnp.sum(p[h], axis=1, keepdims=True) for h in heads]
        pv = [_dot(p[h].astype(BF16), v_ref[pl.ds(k0, width), hsl[h]]) for h in heads]
        acc_new = [alpha[h] * acc_scr[h] + pv[h] for h in heads]
        for h in heads:
            m_scr[h] = m_new[h]
            l_scr[h] = l_new[h]
            acc_scr[h] = acc_new[h]

    def wide_tile(kt, carry):
        step(pl.multiple_of(kt * tk, tk), tk, False)
        return carry

    def narrow_tile(kt, carry):
        step(pl.multiple_of(kt * tq, tq), tq, False)
        return carry

    n_wide = q0 // tk
    lax.fori_loop(0, n_wide, wide_tile, 0)
    lax.fori_loop(n_wide * (tk // tq), qi, narrow_tile, 0)
    step(pl.multiple_of(q0, tq), tq, True)
    for h in range(N_HEADS):
        o_ref[:, h * HEAD_DIM:(h + 1) * HEAD_DIM] = acc_scr[h] / l_scr[h]


def _fox(zbf, ccol, crow, b, t, *, tq=256, tk=1024):
    tq, tk = min(tq, t), min(tk, t)
    assert tk % tq == 0
    nq = t // tq
    return pl.pallas_call(
        functools.partial(_fox_body, tq=tq, tk=tk, sc2=HEAD_DIM ** -0.5 * LOG2E),
        grid=(b, nq),
        in_specs=[
            pl.BlockSpec((tq, GROUP_W), lambda i, j: (i * nq + j, _COL["fox_q"] // 4)),
            pl.BlockSpec((t, GROUP_W), lambda i, j: (i, _COL["fox_k"] // 4)),
            pl.BlockSpec((t, GROUP_W), lambda i, j: (i, _COL["fox_v"] // 4)),
            pl.BlockSpec((tq, N_HEADS * LANE), lambda i, j: (i * nq + j, 0)),
            pl.BlockSpec((None, SUBLANE, t), lambda i, j: (i, 0, 0)),
        ],
        out_specs=pl.BlockSpec((tq, GROUP_W), lambda i, j: (i * nq + j, 0)),
        out_shape=jax.ShapeDtypeStruct((b * t, GROUP_W), F32),
        scratch_shapes=[pltpu.VMEM((N_HEADS, tq, LANE), F32), pltpu.VMEM((N_HEADS, tq, LANE), F32),
                        pltpu.VMEM((N_HEADS, tq, HEAD_DIM), F32)],
        compiler_params=_cparams(("parallel", "arbitrary")),
        name="fox_attn",
    )(zbf, zbf, zbf, ccol, crow)


def _causal_conv(x, halo, w_ref):
    tt = x.shape[0]
    xx = jnp.concatenate([halo, x], axis=0)
    y = x * w_ref[CONV_W - 1:CONV_W, :]
    for d in range(1, CONV_W):
        y = y + pltpu.roll(xx, d, 0)[SUBLANE:SUBLANE + tt] * w_ref[CONV_W - 1 - d:CONV_W - d, :]
    return y


def _gdn_prep_body(x_ref, halo_ref, s_ref, w_ref, alog_ref, dt_ref, qkv_ref, g_ref, sig_ref):
    first = pl.program_id(1) == 0
    x = x_ref[...]
    halo = jnp.where(first, 0.0, halo_ref[...])
    y = _causal_conv(x, halo, w_ref)
    y = y * jax.nn.sigmoid(y)
    for h in range(N_HEADS):
        for part, post in ((0, HEAD_DIM ** -0.5), (1, 1.0)):
            cs = slice(part * GROUP_W + h * HEAD_DIM, part * GROUP_W + (h + 1) * HEAD_DIM)
            u = y[:, cs]
            un = u * lax.rsqrt(jnp.sum(u * u, axis=-1, keepdims=True) + EPS)
            qkv_ref[:, cs] = un * post if part == 0 else un
    qkv_ref[:, 2 * GROUP_W:] = y[:, 2 * GROUP_W:]
    s = s_ref[...]
    g_ref[...] = -jnp.exp(alog_ref[...]) * _softplus(s + dt_ref[...])
    sig_ref[...] = jax.nn.sigmoid(s)


def _gdn_prep(z32, conv_w, a_log, dt_bias, b, t, *, tt=1024):
    tt = min(tt, t)
    nt = t // tt
    w3 = 3 * GROUP_W
    pad = lambda v: jnp.zeros((1, LANE), F32).at[0, _S_GDN_A:_S_GDN_A + N_HEADS].set(v)
    cb = _COL["gdn_q"] // 12
    return pl.pallas_call(
        _gdn_prep_body,
        grid=(b, nt),
        in_specs=[
            pl.BlockSpec((tt, w3), lambda i, j: (i * nt + j, cb)),
            pl.BlockSpec((SUBLANE, w3), lambda i, j: (jnp.maximum((i * nt + j) * (tt // SUBLANE) - 1, 0), cb)),
            pl.BlockSpec((tt, LANE), lambda i, j: (i * nt + j, _COL["small"])),
            pl.BlockSpec((CONV_W, w3), lambda i, j: (0, 0)),
            pl.BlockSpec((1, LANE), lambda i, j: (0, 0)),
            pl.BlockSpec((1, LANE), lambda i, j: (0, 0)),
        ],
        out_specs=[
            pl.BlockSpec((tt, w3), lambda i, j: (i * nt + j, 0)),
            pl.BlockSpec((tt, LANE), lambda i, j: (i * nt + j, 0)),
            pl.BlockSpec((tt, LANE), lambda i, j: (i * nt + j, 0)),
        ],
        out_shape=[jax.ShapeDtypeStruct((b * t, w3), F32), jax.ShapeDtypeStruct((b * t, LANE), F32),
                   jax.ShapeDtypeStruct((b * t, LANE), F32)],
        compiler_params=_cparams(("parallel", "arbitrary")),
        name="gdn_prep",
    )(z32, z32, z32, conv_w, pad(a_log), pad(dt_bias))


def _gdn_intra_body(qkv_ref, g_ref, grow_ref, sig_ref, u_ref, w_ref, qg_ref, kd_ref, qk_ref, gc_ref, *, c, nch):
    ri = lax.broadcasted_iota(jnp.int32, (c, c), 0)
    ci = lax.broadcasted_iota(jnp.int32, (c, c), 1)
    tril = ri >= ci
    strict = ri > ci
    eye = (ri == ci).astype(F32)
    lower = tril.astype(F32)
    upper = (ri <= ci).astype(F32)
    chains = [(ch, h) for ch in range(nch) for h in range(N_HEADS)]
    rsl = lambda ch: slice(ch * c, (ch + 1) * c)
    hsl = lambda h, part=0: slice(part * GROUP_W + h * HEAD_DIM, part * GROUP_W + (h + 1) * HEAD_DIM)
    gc_cols = [_dot_hi(lower, g_ref[rsl(ch), :]) for ch in range(nch)]
    gc_rows = [_dot_hi(grow_ref[ch], upper) for ch in range(nch)]
    for ch in range(nch):
        gc_ref[rsl(ch), :] = gc_cols[ch]
    q = [qkv_ref[rsl(ch), hsl(h, 0)] for ch, h in chains]
    k = [qkv_ref[rsl(ch), hsl(h, 1)] for ch, h in chains]
    v = [qkv_ref[rsl(ch), hsl(h, 2)] for ch, h in chains]
    beta = [sig_ref[rsl(ch), _S_GDN_B + h:_S_GDN_B + h + 1] for ch, h in chains]
    gc = [gc_cols[ch][:, _S_GDN_A + h:_S_GDN_A + h + 1] for ch, h in chains]
    gr = [gc_rows[ch][h:h + 1, :] for ch, h in chains]
    idx = range(len(chains))
    decay = [jnp.where(tril, jnp.exp(jnp.where(tril, gc[i] - gr[i], 0.0)), 0.0) for i in idx]
    kb = [k[i] * beta[i] for i in idx]
    k16 = [x.astype(BF16) for x in k]
    a = [jnp.where(strict, _dot_nt(kb[i].astype(BF16), k16[i]) * decay[i], 0.0) for i in idx]
    same = lambda s: (ri // s) == (ci // s)
    base = 8
    pw = [-jnp.where(same(base), x, 0.0) for x in a]
    tm = [eye + x for x in pw]
    n = 2
    while n < base:
        pws = [_split(x) for x in pw]
        pw = [_dot3(x, x) for x in pws]
        tm = [tm[i] + _dot3(_split(tm[i]), _split(pw[i])) for i in idx]
        n *= 2
    s = base
    while s < c:
        join = same(2 * s) & jnp.logical_not(same(s))
        tms = [_split(x) for x in tm]
        et = [_dot3(_split(jnp.where(join, a[i], 0.0)), tms[i]) for i in idx]
        tm = [tm[i] - _dot3(tms[i], _split(et[i])) for i in idx]
        s *= 2
    tm16 = [x.astype(BF16) for x in tm]
    u = [_dot(tm16[i], (v[i] * beta[i]).astype(BF16)) for i in idx]
    w = [_dot(tm16[i], (kb[i] * jnp.exp(gc[i])).astype(BF16)) for i in idx]
    qk = [_dot_nt(q[i].astype(BF16), k16[i]) * decay[i] for i in idx]
    for i, (ch, h) in enumerate(chains):
        rs, hs = rsl(ch), hsl(h)
        gl = gc[i][c - 1:c, :]
        u_ref[rs, hs] = u[i]
        w_ref[rs, hs] = w[i].astype(BF16)
        qk_ref[rs, h * c:(h + 1) * c] = qk[i].astype(BF16)
        qg_ref[rs, hs] = (q[i] * jnp.exp(gc[i])).astype(BF16)
        kd_ref[rs, hs] = (k[i] * jnp.exp(gl - gc[i])).astype(BF16)


def _gdn_intra(qkv, g, grow, sig, b, t, *, c=GDN_CHUNK, nch=8):
    nt = b * t
    rows = nch * c
    tok = lambda w: pl.BlockSpec((rows, w), lambda i: (i, 0))
    return pl.pallas_call(
        functools.partial(_gdn_intra_body, c=c, nch=nch),
        grid=(nt // rows,),
        in_specs=[tok(3 * GROUP_W), tok(LANE), pl.BlockSpec((nch, SUBLANE, c), lambda i: (i, 0, 0)), tok(LANE)],
        out_specs=[tok(GROUP_W), tok(GROUP_W), tok(GROUP_W), tok(GROUP_W), tok(N_HEADS * c), tok(LANE)],
        out_shape=[jax.ShapeDtypeStruct((nt, GROUP_W), F32), jax.ShapeDtypeStruct((nt, GROUP_W), BF16),
                   jax.ShapeDtypeStruct((nt, GROUP_W), BF16), jax.ShapeDtypeStruct((nt, GROUP_W), BF16),
                   jax.ShapeDtypeStruct((nt, N_HEADS * c), BF16), jax.ShapeDtypeStruct((nt, LANE), F32)],
        compiler_params=_cparams(("parallel",)),
        name="gdn_intra",
    )(qkv, g, grow, sig)


def _gdn_scan_body(u_ref, w_ref, qg_ref, kd_ref, qk_ref, gc_ref, z_ref, ng_ref, o_ref, s_scr, *, c, nch, nb):
    @pl.when(pl.program_id(0) == 0)
    def _():
        s_scr[...] = jnp.zeros_like(s_scr)

    chains = [(b, h) for b in range(nb) for h in range(N_HEADS)]
    idx = range(len(chains))
    hsl = lambda h: slice(h * HEAD_DIM, (h + 1) * HEAD_DIM)
    s = [s_scr[i] for i in idx]
    for ch in range(nch):
        rs = slice(ch * c, (ch + 1) * c)
        egl = [jnp.exp(gc_ref[b, (ch + 1) * c - 1:(ch + 1) * c, :]) for b in range(nb)]
        s16 = [x.astype(BF16) for x in s]
        ws = [_dot(w_ref[b, rs, hsl(h)], s16[i]) for i, (b, h) in enumerate(chains)]
        v_new = [(u_ref[b, rs, hsl(h)] - ws[i]).astype(BF16) for i, (b, h) in enumerate(chains)]
        s = [s[i] * egl[b][:, _S_GDN_A + h:_S_GDN_A + h + 1] + _dot_tn(kd_ref[b, rs, hsl(h)], v_new[i])
             for i, (b, h) in enumerate(chains)]
        o = [_dot(qg_ref[b, rs, hsl(h)], s16[i]) + _dot(qk_ref[b, rs, h * c:(h + 1) * c], v_new[i])
             for i, (b, h) in enumerate(chains)]
        for i, (b, h) in enumerate(chains):
            gz = z_ref[b, rs, hsl(h)]
            o_ref[b, rs, hsl(h)] = _rms(o[i], ng_ref[...]) * (gz * jax.nn.sigmoid(gz))
    for i in idx:
        s_scr[i] = s[i]


def _gdn_scan(u, w, qg, kd, qk, gc, z32, norm_g, b, t, *, c=GDN_CHUNK, nch=8):
    rows = nch * c
    v3 = lambda a: a.reshape(b, t, a.shape[-1])
    blk = lambda wd, cb=0: pl.BlockSpec((b, rows, wd), lambda i: (0, i, cb))
    out = pl.pallas_call(
        functools.partial(_gdn_scan_body, c=c, nch=nch, nb=b),
        grid=(t // rows,),
        in_specs=[blk(GROUP_W), blk(GROUP_W), blk(GROUP_W), blk(GROUP_W), blk(N_HEADS * c), blk(LANE),
                  blk(GROUP_W, _COL["gdn_z"] // 4), pl.BlockSpec((1, HEAD_DIM), lambda i: (0, 0))],
        out_specs=blk(GROUP_W),
        out_shape=jax.ShapeDtypeStruct((b, t, GROUP_W), F32),
        scratch_shapes=[pltpu.VMEM((b * N_HEADS, HEAD_DIM, HEAD_DIM), F32)],
        compiler_params=_cparams(("arbitrary",)),
        name="gdn_scan",
    )(v3(u), v3(w), v3(qg), v3(kd), v3(qk), v3(gc), v3(z32), norm_g.reshape(1, HEAD_DIM))
    return out.reshape(b * t, GROUP_W)


def _lru_body(x_ref, halo_ref, gate_ref, cw_ref, cb_ref, wa_ref, ba_ref, wx_ref, bx_ref, lam_ref, o_ref, h_scr):
    first = pl.program_id(1) == 0

    @pl.when(first)
    def _():
        h_scr[...] = jnp.zeros_like(h_scr)

    x = x_ref[...]
    tt = x.shape[0]
    halo = jnp.where(first, 0.0, halo_ref[...])
    xc = _causal_conv(x, halo, cw_ref) + cb_ref[...]
    xc16 = xc.astype(BF16)
    r = jax.nn.sigmoid(_dot(xc16, wa_ref[...]) + ba_ref[...])
    i = jax.nn.sigmoid(_dot(xc16, wx_ref[...]) + bx_ref[...])
    log_a = -LRU_C * r * _softplus(-lam_ref[...])
    a = jnp.exp(log_a)
    u = jnp.sqrt(-jnp.tanh(log_a) * (a * a + 1.0)) * (i * xc)
    rows = lax.broadcasted_iota(jnp.int32, a.shape, 0) % SUBLANE
    s = 1
    while s < SUBLANE:
        keep = rows >= s
        u = a * jnp.where(keep, pltpu.roll(u, s, 0), 0.0) + u
        a = a * jnp.where(keep, pltpu.roll(a, s, 0), 1.0)
        s *= 2
    gate = _gelu_tanh(gate_ref[...])
    h = h_scr[0:1, :]
    for g in range(tt // SUBLANE):
        rs = slice(SUBLANE * g, SUBLANE * (g + 1))
        hg = a[rs, :] * h + u[rs, :]
        o_ref[rs, :] = hg * gate[rs, :]
        h = hg[SUBLANE - 1:SUBLANE, :]
    h_scr[0:1, :] = h


def _lru(z32, conv_w, conv_b, wa_bd, b_a, wx_bd, b_x, lam, b, t, *, tt=1024):
    tt = min(tt, t)
    nt = t // tt
    w = GROUP_W
    row = lambda v: v.reshape(1, w)
    vec = pl.BlockSpec((1, w), lambda i, j: (0, 0))
    mat = pl.BlockSpec((w, w), lambda i, j: (0, 0))
    cx = _COL["lru_x"] // 4
    return pl.pallas_call(
        _lru_body,
        grid=(b, nt),
        in_specs=[
            pl.BlockSpec((tt, w), lambda i, j: (i * nt + j, cx)),
            pl.BlockSpec((SUBLANE, w), lambda i, j: (jnp.maximum((i * nt + j) * (tt // SUBLANE) - 1, 0), cx)),
            pl.BlockSpec((tt, w), lambda i, j: (i * nt + j, _COL["lru_gate"] // 4)),
            pl.BlockSpec((CONV_W, w), lambda i, j: (0, 0)),
            vec, mat, vec, mat, vec, vec,
        ],
        out_specs=pl.BlockSpec((tt, w), lambda i, j: (i * nt + j, 0)),
        out_shape=jax.ShapeDtypeStruct((b * t, w), F32),
        scratch_shapes=[pltpu.VMEM((SUBLANE, w), F32)],
        compiler_params=_cparams(("parallel", "arbitrary")),
        name="rg_lru",
    )(z32, z32, z32, conv_w, row(conv_b), wa_bd, row(b_a), wx_bd, row(b_x), row(lam))


def _nsa_cmp_body(k_ref, v_ref, pek_ref, pev_ref, w1k_ref, w1v_ref, w2k_ref, w2v_ref, kc_ref, vc_ref):
    nr = kc_ref.shape[0]
    for x_ref, pe_ref, w1_ref, w2_ref, o_ref in ((k_ref, pek_ref, w1k_ref, w2k_ref, kc_ref),
                                                 (v_ref, pev_ref, w1v_ref, w2v_ref, vc_ref)):
        r = jnp.concatenate([x_ref[pl.ds(j, nr, stride=CMP_STRIDE), :] for j in range(CMP_STRIDE)], axis=1)
        lo = (r + pe_ref[0:1, :]).astype(BF16)
        hi = (pltpu.roll(r, nr - 1, 0) + pe_ref[1:2, :]).astype(BF16)
        hid = _gelu_tanh(_dot(lo, w1_ref[0]) + _dot(hi, w1_ref[1]))
        o_ref[...] = _dot(hid.astype(BF16), w2_ref[...]).astype(BF16)


def _nsa_compress(z32, pe_k, pe_v, w1_k, w1_v, w2_k, w2_v, b, t):
    nr = t // CMP_STRIDE
    half = CMP_LEN * HEAD_DIM // 2
    kspec = pl.BlockSpec((t, HEAD_DIM), lambda i: (i, _COL["nsa_kc"]))
    vspec = pl.BlockSpec((t, HEAD_DIM), lambda i: (i, _COL["nsa_vc"]))
    pspec = pl.BlockSpec((2, half), lambda i: (0, 0))
    w1spec = pl.BlockSpec((2, half, HEAD_DIM), lambda i: (0, 0, 0))
    w2spec = pl.BlockSpec((HEAD_DIM, HEAD_DIM), lambda i: (0, 0))
    ospec = pl.BlockSpec((None, nr, HEAD_DIM), lambda i: (i, 0, 0))
    return pl.pallas_call(
        _nsa_cmp_body,
        grid=(b,),
        in_specs=[kspec, vspec, pspec, pspec, w1spec, w1spec, w2spec, w2spec],
        out_specs=[ospec, ospec],
        out_shape=[jax.ShapeDtypeStruct((b, nr, HEAD_DIM), BF16)] * 2,
        compiler_params=_cparams(("parallel",)),
        name="nsa_compress",
    )(z32, z32, pe_k.reshape(2, half), pe_v.reshape(2, half), w1_k.reshape(2, half, HEAD_DIM),
      w1_v.reshape(2, half, HEAD_DIM), w2_k, w2_v)


def _bias_from_dist(dist, th_ref, rb_ref):
    n = jnp.maximum(dist, 0)
    outs = [jnp.full(dist.shape, rb_ref[0, h] * LOG2E, F32) for h in range(N_HEADS)]
    for bkt in range(1, N_BUCKETS):
        ge = n >= th_ref[bkt]
        outs = [jnp.where(ge, rb_ref[bkt, h] * LOG2E, o) for h, o in enumerate(outs)]
    return outs


def _nsa_bias_body(th_ref, rb_ref, sb_ref, wb_ref, cb_ref, *, tk, n_far, ncp):
    qb = Q_BLOCK
    ii = lax.broadcasted_iota(jnp.int32, (qb, tk), 0)
    jj = lax.broadcasted_iota(jnp.int32, (qb, tk), 1)
    for e in range(n_far + 1):
        tiles = _bias_from_dist(e * qb + ii - jj, th_ref, rb_ref)
        for h in range(N_HEADS):
            sb_ref[e, h * qb:(h + 1) * qb, :] = tiles[h]
    nwin = WINDOW + qb
    for e in range(WINDOW // qb + 1):
        dw = (lax.broadcasted_iota(jnp.int32, (qb, nwin), 0) - lax.broadcasted_iota(jnp.int32, (qb, nwin), 1)
              + e * qb)
        tiles = _bias_from_dist(dw, th_ref, rb_ref)
        okw = (dw >= 0) & (dw < WINDOW)
        for h in range(N_HEADS):
            wb_ref[e, h * qb:(h + 1) * qb, :] = jnp.where(okw, tiles[h], NEG)
    mm = lax.broadcasted_iota(jnp.int32, (2 * ncp, qb), 0) - ncp
    dc = lax.broadcasted_iota(jnp.int32, (2 * ncp, qb), 1) - (mm * CMP_STRIDE + CMP_LEN - 1)
    tiles = _bias_from_dist(dc, th_ref, rb_ref)
    for h in range(N_HEADS):
        cb_ref[:, h * qb:(h + 1) * qb] = jnp.where(dc >= 0, tiles[h], NEG)


def _nsa_far_tiles(tk):
    last = MAX_EXACT * (MAX_DIST / MAX_EXACT) ** ((N_BUCKETS - MAX_EXACT - 1) / (N_BUCKETS - MAX_EXACT))
    return -(-(int(math.ceil(last)) + 1 + tk - 1) // Q_BLOCK)


def _nsa_bias(thresholds, rel_bias, t, *, tk):
    n_far = _nsa_far_tiles(tk)
    ncp = t // CMP_STRIDE
    hq = N_HEADS * Q_BLOCK
    smem = pl.BlockSpec(memory_space=pltpu.SMEM)
    return pl.pallas_call(
        functools.partial(_nsa_bias_body, tk=tk, n_far=n_far, ncp=ncp),
        in_specs=[smem, smem],
        out_shape=[jax.ShapeDtypeStruct((n_far + 1, hq, tk), F32),
                   jax.ShapeDtypeStruct((WINDOW // Q_BLOCK + 1, hq, WINDOW + Q_BLOCK), F32),
                   jax.ShapeDtypeStruct((2 * ncp, hq), F32)],
        compiler_params=pltpu.CompilerParams(vmem_limit_bytes=VMEM_LIMIT),
        name="nsa_bias",
    )(thresholds, rel_bias)


def _nsa_body(q_ref, sig_ref, kc_ref, vc_ref, ks_ref, vs_ref, kw_ref, vw_ref, ovt_ref, ex_ref, sb_ref, wb_ref,
              cb_ref, o_ref, *, tk, n_far, n_slc, sc2):
    qi = pl.program_id(1)
    qb = Q_BLOCK
    s0 = qi * qb
    nwin = WINDOW + qb
    heads = range(N_HEADS)
    qh = [q_ref[:, h * HEAD_DIM:(h + 1) * HEAD_DIM] for h in heads]
    q4 = jnp.concatenate(qh, axis=0)

    ncp = kc_ref.shape[0]
    c0 = pl.multiple_of(ncp - (qb // CMP_STRIDE) * qi, SUBLANE)
    we = jnp.minimum(qi, WINDOW // qb)
    w0 = pl.multiple_of((qi - we) * qb, qb)
    kwin = kw_ref[pl.ds(w0, nwin), :]
    vwin = vw_ref[pl.ds(w0, nwin), :]
    s_c = _dot_nt(kc_ref[...], q4) * sc2 + cb_ref[pl.ds(c0, ncp), :]
    s_w = [_dot_nt(qh[h], kwin) * sc2 + wb_ref[we, h * qb:(h + 1) * qb, :] for h in heads]
    m_c = jnp.maximum(jnp.max(s_c, axis=0, keepdims=True), 0.5 * NEG)
    e_c = jnp.exp2(s_c - m_c)
    e_w = [jnp.exp2(s_w[h] - jnp.max(s_w[h], axis=1, keepdims=True)) for h in heads]
    p_c = (e_c * (1.0 / jnp.maximum(jnp.sum(e_c, axis=0, keepdims=True), 1e-30))).astype(BF16)
    r_w = [1.0 / jnp.sum(e_w[h], axis=1, keepdims=True) for h in heads]
    o_c = _dot_tn(p_c, vc_ref[...])
    imp = _dot(ovt_ref[...], p_c[:, 0:qb])
    for h in range(1, N_HEADS):
        imp = imp + _dot(ovt_ref[...], p_c[:, h * qb:(h + 1) * qb])
    o_w = [_dot(e_w[h].astype(BF16), vwin) * r_w[h] for h in heads]

    jsl = lax.broadcasted_iota(jnp.int32, (n_slc, qb), 0)
    cur = (s0 + lax.broadcasted_iota(jnp.int32, (n_slc, qb), 1)) // SLC_LEN
    forced = (jsl == 0) | (jsl == cur) | (jsl == cur - 1)
    imp = jnp.where(forced, FORCED_SCORE, imp)
    imp = jnp.where(jsl > cur, -1.0, imp)
    sg = SUBLANE
    groups = range(n_slc // sg)
    impg = [imp[sg * v:sg * (v + 1), :] for v in groups]
    jloc = lax.broadcasted_iota(jnp.int32, (sg, qb), 0)
    rank = [jnp.zeros((sg, qb), jnp.int32) for _ in groups]
    for j2 in range(n_slc):
        row = impg[j2 // sg][j2 % sg:j2 % sg + 1, :]
        for v in groups:
            if sg * v > j2:
                beats = row >= impg[v]
            elif sg * (v + 1) <= j2:
                beats = row > impg[v]
            else:
                beats = (row > impg[v]) | ((row >= impg[v]) & (jloc > j2 % sg))
            rank[v] = rank[v] + jnp.where(beats, 1, 0)
    selneg = jnp.where(jnp.concatenate(rank, axis=0) < min(N_SELECT, n_slc), 0.0, NEG).astype(BF16)

    def sel_step(kt, carry, width, diag):
        m, l, acc = carry
        nb = width // qb
        k0 = pl.multiple_of(kt * width, width)
        k = ks_ref[pl.ds(k0, width), :]
        v = vs_ref[pl.ds(k0, width), :]
        add = _dot_tn(selneg, ex_ref[:, pl.ds(k0, width)])
        if diag:
            causal = (s0 + lax.broadcasted_iota(jnp.int32, (qb, width), 0)
                      >= k0 + lax.broadcasted_iota(jnp.int32, (qb, width), 1))
            add = jnp.where(causal, add, NEG)
        eb = [jnp.clip(qi - (kt * nb + j), 0, n_far) for j in range(nb)]
        bias = lambda h: jnp.concatenate([sb_ref[eb[j], h * qb:(h + 1) * qb, :] for j in range(nb)], axis=1)
        s = [_dot_nt(qh[h], k) * sc2 + bias(h) + add for h in heads]
        wide = lambda x: jnp.concatenate([x] * nb, axis=1)
        m_new = tuple(jnp.maximum(m[h], jnp.max(s[h], axis=1, keepdims=True)) for h in heads)
        alpha = [jnp.exp2(m[h] - m_new[h]) for h in heads]
        p = [jnp.exp2(s[h] - wide(m_new[h])) for h in heads]
        l = tuple(alpha[h] * l[h] + jnp.sum(p[h], axis=1, keepdims=True) for h in heads)
        pv = [_dot(p[h].astype(BF16), v) for h in heads]
        acc = tuple(alpha[h] * acc[h] + pv[h] for h in heads)
        return m_new, l, acc

    init = (tuple(jnp.full((qb, LANE), 0.5 * NEG, F32) for _ in heads),
            tuple(jnp.zeros((qb, LANE), F32) for _ in heads), tuple(jnp.zeros((qb, HEAD_DIM), F32) for _ in heads))
    kd = s0 // tk
    carry = lax.fori_loop(0, kd, functools.partial(sel_step, width=tk, diag=False), init)
    _, l_s, acc_s = sel_step(kd, carry, tk, True)
    o_s = [acc_s[h] * (1.0 / jnp.maximum(l_s[h], 1e-30)) for h in heads]

    sig = sig_ref[...]
    for h in heads:
        g0 = sig[:, _S_NSA_G + 3 * h:_S_NSA_G + 3 * h + 1]
        g1 = sig[:, _S_NSA_G + 3 * h + 1:_S_NSA_G + 3 * h + 2]
        g2 = sig[:, _S_NSA_G + 3 * h + 2:_S_NSA_G + 3 * h + 3]
        o_ref[:, h * HEAD_DIM:(h + 1) * HEAD_DIM] = g0 * o_c[h * qb:(h + 1) * qb] + g1 * o_s[h] + g2 * o_w[h]


def _nsa(zbf, sig, kc, vc, overlap_t, expand, sb, wb, cb, b, t, *, tk):
    qb = Q_BLOCK
    nq = t // qb
    n_slc = t // SLC_LEN
    ncp = kc.shape[1]
    n_far = sb.shape[0] - 1
    assert t >= WINDOW + qb
    per_b = lambda rows: pl.BlockSpec((None, rows, HEAD_DIM), lambda i, j: (i, 0, 0))
    col = lambda name: pl.BlockSpec((t, HEAD_DIM), lambda i, j: (i, _COL[name]))
    whole = lambda a: pl.BlockSpec(a.shape, lambda i, j: (0,) * a.ndim)
    return pl.pallas_call(
        functools.partial(_nsa_body, tk=tk, n_far=n_far, n_slc=n_slc, sc2=HEAD_DIM ** -0.5 * LOG2E),
        grid=(b, nq),
        in_specs=[
            pl.BlockSpec((qb, GROUP_W), lambda i, j: (i * nq + j, _COL["nsa_q"] // 4)),
            pl.BlockSpec((qb, LANE), lambda i, j: (i * nq + j, 0)),
            per_b(ncp), per_b(ncp),
            col("nsa_ks"), col("nsa_vs"), col("nsa_kw"), col("nsa_vw"),
            whole(overlap_t), whole(expand), whole(sb), whole(wb), whole(cb),
        ],
        out_specs=pl.BlockSpec((qb, GROUP_W), lambda i, j: (i * nq + j, 0)),
        out_shape=jax.ShapeDtypeStruct((b * t, GROUP_W), F32),
        compiler_params=_cparams(("parallel", "arbitrary")),
        name="nsa_attn",
    )(zbf, sig, kc, vc, zbf, zbf, zbf, zbf, overlap_t, expand, sb, wb, cb)


def _outproj_body(h_ref, oa_ref, ob_ref, oc_ref, od_ref, g_ref, w_ref, o_ref):
    y = jnp.concatenate([
        _rms(oa_ref[...], g_ref[0:1, :]), ob_ref[...], _rms(oc_ref[...], g_ref[1:2, :]),
        _rms(od_ref[...], g_ref[2:3, :])], axis=-1)
    o_ref[...] = h_ref[...] + _dot(y.astype(BF16), w_ref[...])


def _outproj(h, oa, ob, oc, od, g, w, l, *, tm=512):
    nt, d = h.shape
    tm = min(tm, nt)
    grp = pl.BlockSpec((tm, GROUP_W), lambda i: (i, 0))
    return pl.pallas_call(
        _outproj_body,
        grid=(nt // tm,),
        in_specs=[pl.BlockSpec((tm, d), lambda i: (i, 0)), grp, grp, grp, grp,
                  pl.BlockSpec((3, GROUP_W), lambda i: (0, 0)),
                  pl.BlockSpec((None, 4 * GROUP_W, d), lambda i: (l, 0, 0))],
        out_specs=pl.BlockSpec((tm, d), lambda i: (i, 0)),
        out_shape=jax.ShapeDtypeStruct((nt, d), F32),
        compiler_params=_cparams(("parallel",)),
        name="outproj",
    )(h, oa, ob, oc, od, g, w)


_IN_WIDTHS = (("fox_q", GROUP_W), ("fox_k", GROUP_W), ("fox_v", GROUP_W), ("fox_f", N_HEADS),
              ("gdn_q", GROUP_W), ("gdn_k", GROUP_W), ("gdn_v", GROUP_W), ("gdn_a", N_HEADS), ("gdn_b", N_HEADS),
              ("gdn_z", GROUP_W), ("lru_x", GROUP_W), ("lru_gate", GROUP_W), ("nsa_q", GROUP_W),
              ("nsa_kc", HEAD_DIM), ("nsa_vc", HEAD_DIM), ("nsa_ks", HEAD_DIM), ("nsa_vs", HEAD_DIM),
              ("nsa_kw", HEAD_DIM), ("nsa_vw", HEAD_DIM), ("nsa_g", 3 * N_HEADS))
D_IN = sum(w for _, w in _IN_WIDTHS)


def _in_pieces():
    small_lane = {"fox_f": _S_FOX_F, "gdn_a": _S_GDN_A, "gdn_b": _S_GDN_B, "nsa_g": _S_NSA_G}
    out, off = [], 0
    for name, width in _IN_WIDTHS:
        if name in small_lane:
            out.append((off, width, 1, _COL["small"] * LANE + small_lane[name]))
        else:
            out.append((off, width, 0 if name in _COLS16 else 1, _COL[name] * LANE))
        off += width
    return out


def _relayout_body(w_ref, o16_ref, o32_ref):
    outs = (o16_ref, o32_ref)
    tc = w_ref.shape[1]
    tail = _COL["small"] * LANE
    small = []
    for src, width, which, dst in _in_pieces():
        if dst >= tail and which == 1:
            small.append(w_ref[src:src + width, :])
        else:
            outs[which][dst:dst + width, :] = w_ref[src:src + width, :].astype(outs[which].dtype)
    used = sum(s.shape[0] for s in small)
    small.append(jnp.zeros((W32 - tail - used, tc), F32))
    o32_ref[tail:, :] = jnp.concatenate(small, axis=0).astype(o32_ref.dtype)


def _relayout_w_in(w_t, *, tc=256):
    nl, d_in, d = w_t.shape
    return pl.pallas_call(
        _relayout_body,
        grid=(nl, d // tc),
        in_specs=[pl.BlockSpec((None, d_in, tc), lambda l, i: (l, 0, i))],
        out_specs=[pl.BlockSpec((None, W16, tc), lambda l, i: (l, 0, i)),
                   pl.BlockSpec((None, W32, tc), lambda l, i: (l, 0, i))],
        out_shape=[jax.ShapeDtypeStruct((nl, W16, d), BF16), jax.ShapeDtypeStruct((nl, W32, d), BF16)],
        compiler_params=_cparams(("parallel", "parallel")),
        name="w_in_relayout",
    )(w_t)


def _pad_cols(w):
    src, off = {}, 0
    for name, width in _IN_WIDTHS:
        src[name] = w[..., off:off + width]
        off += width
    zeros = lambda n: jnp.zeros(w.shape[:-1] + (n,), w.dtype)
    small = [src["fox_f"], src["gdn_a"], src["gdn_b"], src["nsa_g"]]
    src["small"] = jnp.concatenate(small + [zeros(LANE - sum(s.shape[-1] for s in small))], axis=-1)
    w16 = jnp.concatenate([src[name] for name in _COLS16], axis=-1)
    w32 = jnp.concatenate([src[name] for name in _COLS32] + [zeros(W32 - (_COL["small"] + 1) * LANE)], axis=-1)
    assert w16.shape[-1] == W16 and w32.shape[-1] == W32
    return w16, w32


def _block_diag(w):
    l, nb, bw, _ = w.shape
    eye = jnp.eye(nb, dtype=w.dtype)
    return jnp.einsum("lnde,nm->lndme", w, eye).reshape(l, nb * bw, nb * bw)


def _bucket_thresholds(t):
    n = jnp.arange(max(t, 2 * MAX_DIST), dtype=jnp.int32)
    nf = jnp.maximum(n, 1).astype(F32)
    large = MAX_EXACT + (jnp.log(nf / MAX_EXACT) / math.log(MAX_DIST / MAX_EXACT)
                         * (N_BUCKETS - MAX_EXACT)).astype(jnp.int32)
    large = jnp.minimum(large, N_BUCKETS - 1)
    bucket = jnp.where(n < MAX_EXACT, n, large)
    return jnp.sum(bucket[None, :] < jnp.arange(N_BUCKETS, dtype=jnp.int32)[:, None], axis=1).astype(jnp.int32)


def _nsa_tile(t):
    return min(1024, t)


def _mixer(h, l, p, b, t):
    zbf = _inproj(h, p["mix_norm_g"][l], p["w16"], p["b16"], l, BF16, n_tiles=2)
    z32 = _inproj(h, p["mix_norm_g"][l], p["w32"], p["b32"], l, F32, n_tiles=4)

    ccol, crow = _fox_gate(z32, b, t)
    o_a = _fox(zbf, ccol, crow, b, t)

    qkv, gdn_g, sig = _gdn_prep(z32, p["gdn_conv_w"][l], p["gdn_a_log"][l], p["gdn_dt_bias"][l], b, t)
    c = GDN_CHUNK
    grow = gdn_g[:, _S_GDN_A:_S_GDN_A + N_HEADS].reshape(b * t // c, c, N_HEADS).transpose(0, 2, 1)
    grow = jnp.pad(grow, ((0, 0), (0, SUBLANE - N_HEADS), (0, 0)))
    u, w, qg, kd, qk, gc = _gdn_intra(qkv, gdn_g, grow, sig, b, t)
    o_b = _gdn_scan(u, w, qg, kd, qk, gc, z32, p["gdn_norm_g"][l], b, t)

    o_c = _lru(z32, p["lru_conv_w"][l], p["lru_conv_b"][l], p["lru_wa_bd"][l], p["lru_b_a"][l],
               p["lru_wx_bd"][l], p["lru_b_x"][l], p["lru_lambda"][l], b, t)

    kc, vc = _nsa_compress(z32, p["nsa_pe_k"][l], p["nsa_pe_v"][l], p["nsa_w1_k16"][l], p["nsa_w1_v16"][l],
                           p["nsa_w2_k16"][l], p["nsa_w2_v16"][l], b, t)
    o_d = _nsa(zbf, sig, kc, vc, p["overlap_t"], p["expand"], p["sb"], p["wb"], p["cb"], b, t, tk=_nsa_tile(t))

    return _outproj(h, o_a, o_b, o_c, o_d, p["out_norm_g"][l], p["w_out16"], l)


def _mixer_consts(rel_bias, t):
    n_cmp_rows = t // CMP_STRIDE
    n_slc = t // SLC_LEN
    cn = jnp.arange(n_cmp_rows)[None, :] * CMP_STRIDE
    sj = jnp.arange(n_slc)[:, None] * SLC_LEN
    sb, wb, cb = _nsa_bias(_bucket_thresholds(t), rel_bias, t, tk=Q_BLOCK)
    return dict(
        overlap_t=((cn <= sj + SLC_LEN - 1) & (cn + CMP_LEN - 1 >= sj)).astype(BF16),
        expand=(jnp.arange(t)[None, :] // SLC_LEN == jnp.arange(n_slc)[:, None]).astype(BF16),
        sb=sb, wb=wb, cb=cb)


def kernel(x, ffn1_norm_g, ffn1_w_gate, ffn1_w_up, ffn1_w_down, mix_norm_g, w_in, b_in, gdn_conv_w, gdn_a_log, gdn_dt_bias, gdn_norm_g, lru_conv_w, lru_conv_b, lru_w_a, lru_b_a, lru_w_x, lru_b_x, lru_lambda, nsa_pe_k, nsa_w1_k, nsa_w2_k, nsa_pe_v, nsa_w1_v, nsa_w2_v, rel_bias, out_norm_g, w_out, ffn2_norm_g, ffn2_w_gate, ffn2_w_up, ffn2_w_down, final_norm_g):
    b, t, d = x.shape
    depth = w_in.shape[0]
    assert w_in.shape[-1] == D_IN
    w16, w32 = _relayout_w_in(jnp.swapaxes(w_in, 1, 2))
    b16, b32 = _pad_cols(b_in[:, None, :])
    p = dict(
        mix_norm_g=mix_norm_g, w16=w16, w32=w32, b16=b16, b32=b32,
        gdn_conv_w=gdn_conv_w, gdn_a_log=gdn_a_log, gdn_dt_bias=gdn_dt_bias, gdn_norm_g=gdn_norm_g,
        lru_conv_w=lru_conv_w, lru_conv_b=lru_conv_b, lru_wa_bd=_block_diag(lru_w_a).astype(BF16), lru_b_a=lru_b_a,
        lru_wx_bd=_block_diag(lru_w_x).astype(BF16), lru_b_x=lru_b_x, lru_lambda=lru_lambda,
        nsa_pe_k=nsa_pe_k, nsa_pe_v=nsa_pe_v, nsa_w1_k16=nsa_w1_k.astype(BF16), nsa_w1_v16=nsa_w1_v.astype(BF16),
        nsa_w2_k16=nsa_w2_k.astype(BF16), nsa_w2_v16=nsa_w2_v.astype(BF16),
        out_norm_g=out_norm_g, w_out16=w_out.astype(BF16),
        **_mixer_consts(rel_bias, t),
    )
    f1 = (ffn1_w_gate, ffn1_w_up, ffn1_w_down)
    f2 = (ffn2_w_gate, ffn2_w_up, ffn2_w_down)
    w16 = tuple(w[0].astype(BF16) for w in f1)
    h = x.reshape(b * t, d)
    for l in range(depth):
        h, w16 = _ffn(h, ffn1_norm_g[l], *w16, nxt=(*f2, l))
        h = _mixer(h, l, p, b, t)
        last = l == depth - 1
        h, w16 = _ffn(h, ffn2_norm_g[l], *w16, final_g=final_norm_g if last else None,
                      nxt=None if last else (*f1, l + 1))
    return h.reshape(b, t, d)
```
